```python
import math
import jax
import jax.numpy as jnp
from jax import lax
import numpy as np

D_MODEL = 2048
BATCH = 2
SEQ = 8192
DEPTH = 2

GRID_W = 64
CTX_LEN = 256
N_BRANCH = 4
BRANCH_WIDTH = D_MODEL // 2

MLA_Q_LORA = D_MODEL // 4
MLA_KV_LORA = D_MODEL // 4
QK_NOPE = 128
QK_ROPE = 64
V_HEAD = 128
MLA_HEADS = BRANCH_WIDTH // V_HEAD
ROPE_BASE = 10000.0
Q_BLOCK = 128

SSM_HEAD_DIM = 64
SSM_HEADS = BRANCH_WIDTH // SSM_HEAD_DIM
SSM_INNER = SSM_HEADS * SSM_HEAD_DIM
SSM_GROUPS = 2
SSM_STATE = 128
SSM_CHUNK = 128
SSM_XBC = SSM_INNER + 2 * SSM_GROUPS * SSM_STATE

HY_WIDTH = BRANCH_WIDTH
HY_EMB = 33
HY_BANDS = (HY_EMB - 1) // 2
HY_FILTER_HIDDEN = 64
HY_TARGET = 1e-2
HY_SLOW_FRAC = 1.5
HY_QUICK_FRAC = 0.3
HY_MIN_DECAY = math.log(HY_TARGET) / HY_SLOW_FRAC
HY_MAX_DECAY = math.log(HY_TARGET) / HY_QUICK_FRAC

RW_HEAD_DIM = 64
RW_WIDTH = BRANCH_WIDTH
RW_HEADS = RW_WIDTH // RW_HEAD_DIM
RW_DECAY_LORA = 64
RW_ICLR_LORA = 64
RW_GATE_LORA = 160
RW_GN_EPS = 64e-5

N_EXPERTS = 16
EXPERT_FF = 1408
EC_CAPACITY = 2

ALPHA = (2 * DEPTH) ** 0.25
BETA = (8 * DEPTH) ** -0.25

MLA_COLS = MLA_Q_LORA + MLA_KV_LORA + QK_ROPE
SSM_COLS = SSM_INNER + SSM_XBC + 2 * SSM_HEADS
HY_COLS = 3 * HY_WIDTH
RW_COLS = 3 * RW_WIDTH + 2 * RW_DECAY_LORA + 2 * RW_ICLR_LORA + RW_GATE_LORA
GATE_COLS = N_BRANCH * D_MODEL
N_IN = MLA_COLS + SSM_COLS + HY_COLS + RW_COLS + GATE_COLS

kernel_name = 'hybrid_mla_ssd_hyena_rwkv7_ecmoe_dit'

F32 = jnp.float32


def layer_norm(x, eps=1e-6):
    xf = x.astype(F32)
    mu = jnp.mean(xf, -1, keepdims=True)
    var = jnp.mean(jnp.square(xf - mu), -1, keepdims=True)
    return (xf - mu) * lax.rsqrt(var + eps)


def rms_norm(x, g, eps=1e-6):
    xf = x.astype(F32)
    return (xf * lax.rsqrt(jnp.mean(xf * xf, -1, keepdims=True) + eps) * g).astype(x.dtype)


def modulate(x, shift, scale):
    return (layer_norm(x) * (1.0 + scale) + shift).astype(x.dtype)


def post_norm(x, y, g, b):
    return (layer_norm(ALPHA * x + y) * g + b).astype(x.dtype)


def dwconv3(u, w, b):
    up = jnp.pad(u, ((0, 0), (1, 1), (0, 0)))
    return up[:, :-2] * w[0] + up[:, 1:-1] * w[1] + up[:, 2:] * w[2] + b


def token_shift(u, mu):
    up = jnp.pad(u, ((0, 0), (1, 1), (0, 0)))
    return u + mu[0] * (up[:, :-2] - u) + mu[1] * (up[:, 2:] - u)


def axial_rope_tables(n_tokens):
    rows = n_tokens // GRID_W
    row = jnp.repeat(jnp.arange(rows), GRID_W).astype(F32)
    col = jnp.tile(jnp.arange(GRID_W), rows).astype(F32)
    half = QK_ROPE // 2
    inv = ROPE_BASE ** (-jnp.arange(0, half, 2, dtype=F32) / half)
    ang = jnp.stack([row[:, None] * inv, col[:, None] * inv], 1)
    return jnp.cos(ang), jnp.sin(ang)


def apply_axial_rope(x, cos, sin):
    xs = x.reshape(x.shape[:-1] + (2, 2, QK_ROPE // 4)).astype(F32)
    x1, x2 = xs[..., 0, :], xs[..., 1, :]
    out = jnp.stack([x1 * cos - x2 * sin, x2 * cos + x1 * sin], -2)
    return out.reshape(x.shape).astype(x.dtype)


def mla_project(u, p, rope):
    b, n, _ = u.shape
    cq, ckv, kpe = jnp.split(u, [MLA_Q_LORA, MLA_Q_LORA + MLA_KV_LORA], -1)
    q = (rms_norm(cq, p['mla_q_norm']) @ p['mla_w_q_up']).reshape(b, n, MLA_HEADS, QK_NOPE + QK_ROPE)
    kv = (rms_norm(ckv, p['mla_kv_norm']) @ p['mla_w_kv_up']).reshape(b, n, MLA_HEADS, QK_NOPE + V_HEAD)
    q_nope, q_pe = jnp.split(q, [QK_NOPE], -1)
    k_nope, v = jnp.split(kv, [QK_NOPE], -1)
    if rope is not None:
        cos, sin = rope
        q_pe = apply_axial_rope(q_pe, cos[:, None], sin[:, None])
        kpe = apply_axial_rope(kpe, cos, sin)
    q = jnp.concatenate([q_nope, q_pe], -1)
    k = jnp.concatenate([k_nope, jnp.broadcast_to(kpe[:, :, None], (b, n, MLA_HEADS, QK_ROPE))], -1)
    return q, k, v


def block_attention(q, k, v):
    b, n, h, dk = q.shape
    scale = dk ** -0.5
    qb = jnp.moveaxis(q.reshape(b, n // Q_BLOCK, Q_BLOCK, h, dk), 1, 0)

    def attend(qi):
        s = jnp.einsum('bqhd,bkhd->bhqk', qi, k).astype(F32) * scale
        pr = jax.nn.softmax(s, -1).astype(v.dtype)
        return jnp.einsum('bhqk,bkhd->bqhd', pr, v)

    o = lax.map(attend, qb)
    return jnp.moveaxis(o, 0, 1).reshape(b, n, h * v.shape[-1])


def mla_branch(uc, ul, rope, p, need_ctx):
    qc, kc, vc = mla_project(uc, p, None)
    ql, kl, vl = mla_project(ul, p, rope)
    yl = block_attention(ql, jnp.concatenate([kc, kl], 1), jnp.concatenate([vc, vl], 1))
    yc = block_attention(qc, kc, vc) if need_ctx else None
    return yc, yl


def segsum(x):
    t = x.shape[-1]
    xr = jnp.broadcast_to(x[..., :, None], x.shape + (t,))
    xr = jnp.where(jnp.tril(jnp.ones((t, t), bool), -1), xr, 0.0)
    cs = jnp.cumsum(xr, axis=-2)
    return jnp.where(jnp.tril(jnp.ones((t, t), bool)), cs, -jnp.inf)


def ssd_chunked(X, dA, Bm, Cm, init):
    b, L, g, j, p_ = X.shape
    c = L // SSM_CHUNK
    X = X.reshape(b, c, SSM_CHUNK, g, j, p_)
    Bm = Bm.reshape(b, c, SSM_CHUNK, g, -1)
    Cm = Cm.reshape(b, c, SSM_CHUNK, g, -1)
    A = jnp.moveaxis(dA.reshape(b, c, SSM_CHUNK, g, j), (1, 2), (3, 4))
    A_cs = jnp.cumsum(A, -1)
    CB = jnp.einsum('bclgn,bcsgn->bgcls', Cm, Bm)
    M = CB[:, :, None] * jnp.exp(segsum(A))
    y_diag = jnp.einsum('bgjcls,bcsgjp->bclgjp', M, X)
    ds = jnp.moveaxis(jnp.exp(A_cs[..., -1:] - A_cs), (3, 4), (1, 2))
    states = jnp.einsum('bclgn,bclgjp->bcgjpn', Bm, X * ds[..., None])
    states = jnp.concatenate([init[:, None], states], 1)
    chunk_A = jnp.pad(A_cs[..., -1], ((0, 0), (0, 0), (0, 0), (1, 0)))
    new_states = jnp.einsum('bgjzc,bcgjpn->bzgjpn', jnp.exp(segsum(chunk_A)), states)
    prev_states, final = new_states[:, :-1], new_states[:, -1]
    sdo = jnp.moveaxis(jnp.exp(A_cs), (3, 4), (1, 2))
    y_off = jnp.einsum('bclgn,bcgjpn->bclgjp', Cm, prev_states) * sdo[..., None]
    return (y_diag + y_off).reshape(b, L, g, j, p_), final


def mamba_prep(u, p):
    b, n, _ = u.shape
    z, xbc, dt = jnp.split(u, [SSM_INNER, SSM_INNER + SSM_XBC], -1)
    xbc = jax.nn.silu(dwconv3(xbc, p['ssm_conv_w'], p['ssm_conv_b']).astype(F32))
    xs, bm, cm = jnp.split(xbc, [SSM_INNER, SSM_INNER + SSM_GROUPS * SSM_STATE], -1)
    xs = xs.reshape(b, n, SSM_HEADS, SSM_HEAD_DIM)
    bm = bm.reshape(b, n, SSM_GROUPS, SSM_STATE)
    cm = cm.reshape(b, n, SSM_GROUPS, SSM_STATE)
    dt = jax.nn.softplus(dt.astype(F32).reshape(b, n, 2, SSM_HEADS) + p['ssm_dt_bias'])
    return z, xs, bm, cm, dt


def ssd_direction(xs, dt, a, bm, cm, init, reverse):
    if reverse:
        xs, dt, bm, cm = (jnp.flip(t, 1) for t in (xs, dt, bm, cm))
    b, n = dt.shape[:2]
    hg = SSM_HEADS // SSM_GROUPS
    X = (xs * dt[..., None]).reshape(b, n, SSM_GROUPS, hg, SSM_HEAD_DIM)
    dA = (dt * a).reshape(b, n, SSM_GROUPS, hg)
    y, final = ssd_chunked(X, dA, bm, cm, init)
    y = y.reshape(b, n, SSM_HEADS, SSM_HEAD_DIM)
    if reverse:
        y = jnp.flip(y, 1)
    return y, final


def mamba_out(y, xs, z, p):
    b, n = y.shape[:2]
    y = (y + xs * p['ssm_d'][:, None]).reshape(b, n, SSM_INNER) * jax.nn.silu(z.astype(F32))
    yg = y.reshape(b, n, SSM_GROUPS, SSM_INNER // SSM_GROUPS)
    yg = yg * lax.rsqrt(jnp.mean(yg * yg, -1, keepdims=True) + 1e-5)
    return (yg.reshape(b, n, SSM_INNER) * p['ssm_norm']).astype(z.dtype)


def mamba_branch(uc, ul, p, need_ctx):
    a = -jnp.exp(p['ssm_a_log'].astype(F32))
    zc, xc, bc, cc, dtc = mamba_prep(uc, p)
    zl, xl, bl, cl, dtl = mamba_prep(ul, p)
    zero = jnp.zeros((ul.shape[0], SSM_GROUPS, SSM_HEADS // SSM_GROUPS, SSM_HEAD_DIM, SSM_STATE), F32)
    yc = yl = 0.0
    for d, rev in enumerate((False, True)):
        yc_d, sc = ssd_direction(xc, dtc[:, :, d], a[d], bc, cc, zero, rev)
        yl_d, _ = ssd_direction(xl, dtl[:, :, d], a[d], bl, cl, sc, rev)
        yc = yc + yc_d
        yl = yl + yl_d
    return (mamba_out(yc, xc, zc, p) if need_ctx else None), mamba_out(yl, xl, zl, p)


def hyena_filters(n, p):
    t = jnp.linspace(0.0, 1.0, n, dtype=F32)[:, None]
    wpos = 2.0 * math.pi * jnp.arange(n, dtype=F32)[:, None] / n
    f = jnp.linspace(1e-4, HY_BANDS - 1, HY_BANDS, dtype=F32)[None]
    z = jnp.concatenate([t, jnp.cos(f * wpos), -jnp.sin(f * wpos)], -1)
    freq = p['hy_freq']
    hdn = jnp.sin(freq * (z @ p['hy_w1'] + p['hy_b1']))
    hdn = jnp.sin(freq * (hdn @ p['hy_w2'] + p['hy_b2']))
    filt = (hdn @ p['hy_w3']).astype(F32).reshape(n, 2, HY_WIDTH)
    deltas = jnp.abs(jnp.linspace(HY_MIN_DECAY, HY_MAX_DECAY, HY_WIDTH, dtype=F32))
    return filt * jnp.exp(-t * deltas)[:, None]


def bidir_long_conv(v, filt):
    n, ch = v.shape[1], v.shape[2]
    k_full = jnp.concatenate([filt[:, 0], jnp.zeros((1, ch), F32), filt[:0:-1, 1]], 0)
    kf = jnp.fft.rfft(k_full, axis=0)
    vf = jnp.fft.rfft(v, n=2 * n, axis=1)
    return jnp.fft.irfft(vf * kf, n=2 * n, axis=1)[:, :n]


def hyena_seq(u, p):
    n = u.shape[1]
    uc = dwconv3(u, p['hy_conv_w'], p['hy_conv_b']).astype(F32)
    x0, x1, v = jnp.split(uc, 3, -1)
    v = v * x1
    y = bidir_long_conv(v, hyena_filters(n, p)) + v * p['hy_d']
    return (y * x0).astype(u.dtype)


def hyena_branch(uc, ul, p, need_ctx):
    return (hyena_seq(uc, p) if need_ctx else None), hyena_seq(ul, p)


def rwkv_prep(u, p):
    b, n, _ = u.shape
    heads = lambda t: t.reshape(b, n, RW_HEADS, RW_HEAD_DIM)
    us = token_shift(u, p['rw_mu']).astype(F32)
    cuts = np.cumsum([RW_WIDTH, RW_WIDTH, RW_WIDTH, 2 * RW_DECAY_LORA, 2 * RW_ICLR_LORA]).tolist()
    r, k, v, wd, ad, gd = jnp.split(us, cuts, -1)
    g = jax.nn.sigmoid(gd) @ p['rw_g_up']
    kk = heads(k * p['rw_kk'])
    kk = kk / jnp.maximum(jnp.sqrt(jnp.sum(kk * kk, -1, keepdims=True)), 1e-12)
    wd = wd.reshape(b, n, 2, RW_DECAY_LORA)
    ad = ad.reshape(b, n, 2, RW_ICLR_LORA)
    dirs = []
    for d in range(2):
        w_log = -jax.nn.softplus(-(p['rw_w0'][d] + jnp.tanh(wd[:, :, d]) @ p['rw_w_up'][d])) - 0.5
        iclr = jax.nn.sigmoid(p['rw_a0'][d] + ad[:, :, d] @ p['rw_a_up'][d])
        k_d = heads(k * (1.0 + (iclr - 1.0) * p['rw_ka']))
        dirs.append((heads(jnp.exp(-jnp.exp(w_log))), k_d, kk * heads(iclr)))
    return heads(r), heads(v), kk, g, dirs


def rwkv_scan(r, decay, k, v, a_vec, b_vec, s0, reverse):
    def step(s, inp):
        r_t, w_t, k_t, v_t, a_t, b_t = inp
        sa = jnp.einsum('bhvk,bhk->bhv', s, a_t)
        s = s * w_t[:, :, None, :] + sa[..., None] * b_t[:, :, None, :] + v_t[..., None] * k_t[:, :, None, :]
        return s, jnp.einsum('bhvk,bhk->bhv', s, r_t)

    xs = tuple(jnp.moveaxis(t, 1, 0) for t in (r, decay, k, v, a_vec, b_vec))
    s, ys = lax.scan(step, s0, xs, reverse=reverse)
    return jnp.moveaxis(ys, 0, 1), s


def rwkv_out(y, r, v, g, dirs, p, dtype):
    b, n = y.shape[:2]
    mu = jnp.mean(y, -1, keepdims=True)
    var = jnp.mean(jnp.square(y - mu), -1, keepdims=True)
    yn = ((y - mu) * lax.rsqrt(var + RW_GN_EPS)).reshape(b, n, RW_WIDTH) * p['rw_ln_g'] + p['rw_ln_b']
    bonus = sum(jnp.sum(r * k_d * p['rw_rk'], -1, keepdims=True) * v for _, k_d, _ in dirs)
    return ((yn + bonus.reshape(b, n, RW_WIDTH)) * g).astype(dtype)


def rwkv_branch(uc, ul, p, need_ctx):
    rc, vc, kkc, gc, dc = rwkv_prep(uc, p)
    rl, vl, kkl, gl, dl = rwkv_prep(ul, p)
    s0 = jnp.zeros((ul.shape[0], RW_HEADS, RW_HEAD_DIM, RW_HEAD_DIM), F32)
    yc = yl = 0.0
    for d, rev in enumerate((False, True)):
        wc_, kc_, bc_ = dc[d]
        yc_d, sc = rwkv_scan(rc, wc_, kc_, vc, -kkc, bc_, s0, rev)
        wl_, kl_, bl_ = dl[d]
        yl_d, _ = rwkv_scan(rl, wl_, kl_, vl, -kkl, bl_, sc, rev)
        yc = yc + yc_d
        yl = yl + yl_d
    out_c = rwkv_out(yc, rc, vc, gc, dc, p, uc.dtype) if need_ctx else None
    return out_c, rwkv_out(yl, rl, vl, gl, dl, p, ul.dtype)


def merge_branches(branches, gate_logits, p):
    b, n, _ = gate_logits.shape
    gates = jax.nn.sigmoid(gate_logits.reshape(b, n, N_BRANCH, D_MODEL))
    merged = sum(gates[:, :, i] * (y @ p['w_branch'][i]) for i, y in enumerate(branches))
    return merged @ p['w_out']


def hybrid_mixer(hc, hl, p, rope, need_ctx):
    cuts = np.cumsum([MLA_COLS, SSM_COLS, HY_COLS, RW_COLS]).tolist()
    pc = jnp.split(hc @ p['w_in'], cuts, -1)
    pl = jnp.split(hl @ p['w_in'], cuts, -1)
    branches = (mla_branch(pc[0], pl[0], rope, p, need_ctx),
                mamba_branch(pc[1], pl[1], p, need_ctx),
                hyena_branch(pc[2], pl[2], p, need_ctx),
                rwkv_branch(pc[3], pl[3], p, need_ctx))
    yl = merge_branches([br[1] for br in branches], pl[4], p)
    yc = merge_branches([br[0] for br in branches], pc[4], p) if need_ctx else None
    return yc, yl


def expert_choice_ffn(h, p):
    b, n, _ = h.shape
    cap = EC_CAPACITY * n // N_EXPERTS
    aff = jax.nn.softmax((h @ p['w_router']).astype(F32), -1)
    gate, idx = lax.top_k(jnp.swapaxes(aff, 1, 2), cap)
    bidx = jnp.arange(b)[:, None, None]
    xs = h[bidx, idx]
    hid = jax.nn.silu(jnp.einsum('becd,edf->becf', xs, p['w_gate_e'])) * jnp.einsum('becd,edf->becf', xs, p['w_up_e'])
    ye = jnp.einsum('becf,efd->becd', hid, p['w_down_e']) * gate[..., None].astype(h.dtype)
    return jnp.zeros_like(h).at[bidx, idx].add(ye.astype(h.dtype))


def trunk_layer(xc, xl, mod_c, mod_l, p, rope, need_ctx):
    sh1c, sc1c, g1c, sh2c, sc2c, g2c = jnp.split(mod_c, 6, -1)
    sh1, sc1, g1, sh2, sc2, g2 = (m[:, None] for m in jnp.split(mod_l, 6, -1))
    yc, yl = hybrid_mixer(modulate(xc, sh1c, sc1c), modulate(xl, sh1, sc1), p, rope, need_ctx)
    xl = post_norm(xl, g1 * yl, p['ln1_g'], p['ln1_b'])
    xl = post_norm(xl, g2 * expert_choice_ffn(modulate(xl, sh2, sc2), p), p['ln2_g'], p['ln2_b'])
    if need_ctx:
        xc = post_norm(xc, g1c * yc, p['ln1_g'], p['ln1_b'])
        xc = post_norm(xc, g2c * expert_choice_ffn(modulate(xc, sh2c, sc2c), p), p['ln2_g'], p['ln2_b'])
    return xc, xl


def setup_inputs(seed: int = 0) -> dict:
    key = jax.random.key(seed)
    keys = iter(jax.random.split(key, 64))
    L = DEPTH

    def normal(shape, scale=1.0):
        return scale * jax.random.normal(next(keys), shape, F32)

    def gain(shape):
        return 1.0 + 0.05 * jax.random.normal(next(keys), shape, F32)

    def uniform(shape, lo, hi):
        return jax.random.uniform(next(keys), shape, F32, lo, hi)

    dt0 = jnp.exp(uniform((L, 2, SSM_HEADS), math.log(1e-3), math.log(1e-1)))
    return {
        'x': normal((BATCH, SEQ, D_MODEL)),
        'c': normal((BATCH, D_MODEL)),
        'ctx': normal((BATCH, CTX_LEN, D_MODEL)),
        'c_ctx': normal((D_MODEL,)),
        'w_ada': normal((L, D_MODEL, 6 * D_MODEL), D_MODEL ** -0.5),
        'b_ada': normal((L, 6 * D_MODEL), 0.01),
        'w_in': normal((L, D_MODEL, N_IN), D_MODEL ** -0.5),
        'mla_q_norm': gain((L, MLA_Q_LORA)),
        'mla_w_q_up': normal((L, MLA_Q_LORA, MLA_HEADS * (QK_NOPE + QK_ROPE)), MLA_Q_LORA ** -0.5),
        'mla_kv_norm': gain((L, MLA_KV_LORA)),
        'mla_w_kv_up': normal((L, MLA_KV_LORA, MLA_HEADS * (QK_NOPE + V_HEAD)), MLA_KV_LORA ** -0.5),
        'ssm_conv_w': normal((L, 3, SSM_XBC), 3 ** -0.5),
        'ssm_conv_b': normal((L, SSM_XBC), 0.01),
        'ssm_dt_bias': dt0 + jnp.log(-jnp.expm1(-dt0)),
        'ssm_a_log': jnp.log(uniform((L, 2, SSM_HEADS), 1.0, 16.0)),
        'ssm_d': gain((L, SSM_HEADS)),
        'ssm_norm': gain((L, SSM_INNER)),
        'hy_conv_w': normal((L, 3, HY_COLS), 3 ** -0.5),
        'hy_conv_b': normal((L, HY_COLS), 0.01),
        'hy_w1': normal((L, HY_EMB, HY_FILTER_HIDDEN), HY_EMB ** -0.5),
        'hy_b1': normal((L, HY_FILTER_HIDDEN), 0.1),
        'hy_w2': normal((L, HY_FILTER_HIDDEN, HY_FILTER_HIDDEN), HY_FILTER_HIDDEN ** -0.5),
        'hy_b2': normal((L, HY_FILTER_HIDDEN), 0.1),
        'hy_w3': normal((L, HY_FILTER_HIDDEN, 2 * HY_WIDTH), 0.02),
        'hy_freq': gain((L, HY_FILTER_HIDDEN)),
        'hy_d': normal((L, HY_WIDTH), 0.1),
        'rw_mu': uniform((L, 2, RW_COLS), 0.0, 0.5),
        'rw_w0': uniform((L, 2, RW_WIDTH), -5.0, -0.5),
        'rw_w_up': normal((L, 2, RW_DECAY_LORA, RW_WIDTH), 0.1),
        'rw_a0': normal((L, 2, RW_WIDTH), 0.1),
        'rw_a_up': normal((L, 2, RW_ICLR_LORA, RW_WIDTH), 0.1),
        'rw_g_up': normal((L, RW_GATE_LORA, RW_WIDTH), RW_GATE_LORA ** -0.5),
        'rw_kk': 0.85 + normal((L, RW_WIDTH), 0.05),
        'rw_ka': gain((L, RW_WIDTH)),
        'rw_rk': normal((L, RW_HEADS, RW_HEAD_DIM), 0.1),
        'rw_ln_g': gain((L, RW_WIDTH)),
        'rw_ln_b': normal((L, RW_WIDTH), 0.01),
        'w_branch': normal((L, N_BRANCH, BRANCH_WIDTH, D_MODEL), BRANCH_WIDTH ** -0.5),
        'w_out': normal((L, D_MODEL, D_MODEL), BETA * D_MODEL ** -0.5),
        'ln1_g': gain((L, D_MODEL)),
        'ln1_b': normal((L, D_MODEL), 0.01),
        'w_router': normal((L, D_MODEL, N_EXPERTS), D_MODEL ** -0.5),
        'w_gate_e': normal((L, N_EXPERTS, D_MODEL, EXPERT_FF), D_MODEL ** -0.5),
        'w_up_e': normal((L, N_EXPERTS, D_MODEL, EXPERT_FF), D_MODEL ** -0.5),
        'w_down_e': normal((L, N_EXPERTS, EXPERT_FF, D_MODEL), BETA * EXPERT_FF ** -0.5),
        'ln2_g': gain((L, D_MODEL)),
        'ln2_b': normal((L, D_MODEL), 0.01),
    }


def reference(x, c, ctx, c_ctx, w_ada, b_ada, w_in, mla_q_norm, mla_w_q_up, mla_kv_norm, mla_w_kv_up,
              ssm_conv_w, ssm_conv_b, ssm_dt_bias, ssm_a_log, ssm_d, ssm_norm,
              hy_conv_w, hy_conv_b, hy_w1, hy_b1, hy_w2, hy_b2, hy_w3, hy_freq, hy_d,
              rw_mu, rw_w0, rw_w_up, rw_a0, rw_a_up, rw_g_up, rw_kk, rw_ka, rw_rk, rw_ln_g, rw_ln_b,
              w_branch, w_out, ln1_g, ln1_b, w_router, w_gate_e, w_up_e, w_down_e, ln2_g, ln2_b):
    stacked = dict(
        w_in=w_in, mla_q_norm=mla_q_norm, mla_w_q_up=mla_w_q_up, mla_kv_norm=mla_kv_norm, mla_w_kv_up=mla_w_kv_up,
        ssm_conv_w=ssm_conv_w, ssm_conv_b=ssm_conv_b, ssm_dt_bias=ssm_dt_bias, ssm_a_log=ssm_a_log,
        ssm_d=ssm_d, ssm_norm=ssm_norm,
        hy_conv_w=hy_conv_w, hy_conv_b=hy_conv_b, hy_w1=hy_w1, hy_b1=hy_b1, hy_w2=hy_w2, hy_b2=hy_b2,
        hy_w3=hy_w3, hy_freq=hy_freq, hy_d=hy_d,
        rw_mu=rw_mu, rw_w0=rw_w0, rw_w_up=rw_w_up, rw_a0=rw_a0, rw_a_up=rw_a_up, rw_g_up=rw_g_up,
        rw_kk=rw_kk, rw_ka=rw_ka, rw_rk=rw_rk, rw_ln_g=rw_ln_g, rw_ln_b=rw_ln_b,
        w_branch=w_branch, w_out=w_out, ln1_g=ln1_g, ln1_b=ln1_b,
        w_router=w_router, w_gate_e=w_gate_e, w_up_e=w_up_e, w_down_e=w_down_e, ln2_g=ln2_g, ln2_b=ln2_b)
    rope = axial_rope_tables(x.shape[1])
    xc, xl = ctx, x
    for i in range(DEPTH):
        p = {name: arr[i] for name, arr in stacked.items()}
        mod_l = jax.nn.silu(c) @ w_ada[i] + b_ada[i]
        mod_c = jax.nn.silu(c_ctx) @ w_ada[i] + b_ada[i]
        xc, xl = trunk_layer(xc, xl, mod_c, mod_l, p, rope, i < DEPTH - 1)
    return xl
```

```python
import functools
import math

import jax
import jax.numpy as jnp
from jax import lax
import numpy as np
from jax.experimental import pallas as pl
from jax.experimental.pallas import tpu as pltpu

D_MODEL = 2048
DEPTH = 2

GRID_W = 64
N_BRANCH = 4
BRANCH_WIDTH = D_MODEL // 2

MLA_Q_LORA = D_MODEL // 4
MLA_KV_LORA = D_MODEL // 4
QK_NOPE = 128
QK_ROPE = 64
V_HEAD = 128
MLA_HEADS = BRANCH_WIDTH // V_HEAD
ROPE_BASE = 10000.0
Q_BLOCK = 128

SSM_HEAD_DIM = 64
SSM_HEADS = BRANCH_WIDTH // SSM_HEAD_DIM
SSM_INNER = SSM_HEADS * SSM_HEAD_DIM
SSM_GROUPS = 2
SSM_STATE = 128
SSM_CHUNK = 128
SSM_XBC = SSM_INNER + 2 * SSM_GROUPS * SSM_STATE

HY_WIDTH = BRANCH_WIDTH
HY_EMB = 33
HY_BANDS = (HY_EMB - 1) // 2
HY_FILTER_HIDDEN = 64
HY_TARGET = 1e-2
HY_SLOW_FRAC = 1.5
HY_QUICK_FRAC = 0.3
HY_MIN_DECAY = math.log(HY_TARGET) / HY_SLOW_FRAC
HY_MAX_DECAY = math.log(HY_TARGET) / HY_QUICK_FRAC

RW_HEAD_DIM = 64
RW_WIDTH = BRANCH_WIDTH
RW_HEADS = RW_WIDTH // RW_HEAD_DIM
RW_DECAY_LORA = 64
RW_ICLR_LORA = 64
RW_GATE_LORA = 160
RW_GN_EPS = 64e-5

N_EXPERTS = 16
EXPERT_FF = 1408
EC_CAPACITY = 2

ALPHA = (2 * DEPTH) ** 0.25

MLA_COLS = MLA_Q_LORA + MLA_KV_LORA + QK_ROPE
SSM_COLS = SSM_INNER + SSM_XBC + 2 * SSM_HEADS
HY_COLS = 3 * HY_WIDTH
RW_COLS = 3 * RW_WIDTH + 2 * RW_DECAY_LORA + 2 * RW_ICLR_LORA + RW_GATE_LORA

F32 = jnp.float32
HI = lax.Precision.HIGHEST

RW_T = 64


def _dot(a, b, prec=HI):
    return jnp.dot(a, b, precision=prec, preferred_element_type=F32)


def _dot_nt(a, b, prec=HI):
    return lax.dot_general(a, b, (((1,), (1,)), ((), ())), precision=prec, preferred_element_type=F32)


def _dot_tn(a, b, prec=HI):
    return lax.dot_general(a, b, (((0,), (0,)), ((), ())), precision=prec, preferred_element_type=F32)


def _rwkv_kernel(r_ref, lw_ref, k_ref, v_ref, a_ref, b_ref, s0_ref, y_ref, sT_ref, s_scr, *, nsub):
    j = pl.program_id(1)

    @pl.when(j == 0)
    def _():
        s_scr[...] = s0_ref[0]

    T = RW_T
    row = lax.broadcasted_iota(jnp.int32, (T, T), 0)
    col = lax.broadcasted_iota(jnp.int32, (T, T), 1)
    incl = row >= col
    strict = row > col
    lcum = incl.astype(F32)
    eye = (row == col).astype(F32)

    def chunk(ci, S):
        sl = pl.ds(pl.multiple_of(ci * T, T), T)
        r = r_ref[0, sl, :]
        lw = lw_ref[0, sl, :]
        k = k_ref[0, sl, :]
        v = v_ref[0, sl, :]
        a = a_ref[0, sl, :]
        b = b_ref[0, sl, :]
        c = _dot(lcum, lw)
        cT = c[T - 1:T, :]
        e_neg = jnp.exp(-c)
        a_t = a * jnp.exp(c - lw)
        b_t = b * e_neg
        k_t = k * e_neg
        r_t = r * jnp.exp(c)
        e_end = jnp.exp(cT - c)
        b_h = b * e_end
        k_h = k * e_end
        g_T = jnp.exp(cT)
        x = jnp.concatenate([a_t, r_t], axis=0)
        gb = _dot_nt(x, b_t)
        gk = _dot_nt(x, k_t)
        n = jnp.where(strict, gb[:T], 0.0)
        l_ak = jnp.where(strict, gk[:T], 0.0)
        m_rb = jnp.where(incl, gb[T:], 0.0)
        m_rk = jnp.where(incl, gk[T:], 0.0)
        p = eye + n
        m = n
        for _ in range(5):
            m = _dot(m, m)
            p = p + _dot(m, p)
        w = _dot(p, a_t)
        u0 = _dot(p, _dot(l_ak, v))
        u = _dot_nt(w, S) + u0
        y = _dot_nt(r_t, S) + _dot(m_rb, u) + _dot(m_rk, v)
        y_ref[0, sl, :] = y
        return S * g_T + _dot_tn(u, b_h) + _dot_tn(v, k_h)

    S = lax.fori_loop(0, nsub, chunk, s_scr[...])
    s_scr[...] = S

    @pl.when(j == pl.num_programs(1) - 1)
    def _():
        sT_ref[0] = S


def rwkv_scan_pallas(r, lw, k, v, a, b, s0, *, tb=512):
    G, n, D = r.shape
    tb = min(tb, n)
    assert n % tb == 0 and tb % RW_T == 0
    seq = pl.BlockSpec((1, tb, D), lambda g, j: (g, j, 0))
    st = pl.BlockSpec((1, D, D), lambda g, j: (g, 0, 0))
    return pl.pallas_call(
        functools.partial(_rwkv_kernel, nsub=tb // RW_T),
        grid=(G, n // tb),
        in_specs=[seq] * 6 + [st],
        out_specs=[seq, st],
        out_shape=[jax.ShapeDtypeStruct((G, n, D), F32), jax.ShapeDtypeStruct((G, D, D), F32)],
        scratch_shapes=[pltpu.VMEM((D, D), F32)],
        compiler_params=pltpu.CompilerParams(dimension_semantics=("parallel", "arbitrary")),
        name="rwkv_scan",
    )(r, lw, k, v, a, b, s0)


def layer_norm(x, eps=1e-6):
    xf = x.astype(F32)
    mu = jnp.mean(xf, -1, keepdims=True)
    var = jnp.mean(jnp.square(xf - mu), -1, keepdims=True)
    return (xf - mu) * lax.rsqrt(var + eps)


def rms_norm(x, g, eps=1e-6):
    xf = x.astype(F32)
    return (xf * lax.rsqrt(jnp.mean(xf * xf, -1, keepdims=True) + eps) * g).astype(x.dtype)


def modulate(x, shift, scale):
    return (layer_norm(x) * (1.0 + scale) + shift).astype(x.dtype)


def post_norm(x, y, g, b):
    return (layer_norm(ALPHA * x + y) * g + b).astype(x.dtype)


def dwconv3(u, w, b):
    up = jnp.pad(u, ((0, 0), (1, 1), (0, 0)))
    return up[:, :-2] * w[0] + up[:, 1:-1] * w[1] + up[:, 2:] * w[2] + b


def token_shift(u, mu):
    up = jnp.pad(u, ((0, 0), (1, 1), (0, 0)))
    return u + mu[0] * (up[:, :-2] - u) + mu[1] * (up[:, 2:] - u)


def axial_rope_tables(n_tokens):
    rows = n_tokens // GRID_W
    row = jnp.repeat(jnp.arange(rows), GRID_W).astype(F32)
    col = jnp.tile(jnp.arange(GRID_W), rows).astype(F32)
    half = QK_ROPE // 2
    inv = ROPE_BASE ** (-jnp.arange(0, half, 2, dtype=F32) / half)
    ang = jnp.stack([row[:, None] * inv, col[:, None] * inv], 1)
    return jnp.cos(ang), jnp.sin(ang)


def apply_axial_rope(x, cos, sin):
    xs = x.reshape(x.shape[:-1] + (2, 2, QK_ROPE // 4)).astype(F32)
    x1, x2 = xs[..., 0, :], xs[..., 1, :]
    out = jnp.stack([x1 * cos - x2 * sin, x2 * cos + x1 * sin], -2)
    return out.reshape(x.shape).astype(x.dtype)


def mla_project(u, p, rope):
    b, n, _ = u.shape
    cq, ckv, kpe = jnp.split(u, [MLA_Q_LORA, MLA_Q_LORA + MLA_KV_LORA], -1)
    q = (rms_norm(cq, p['mla_q_norm']) @ p['mla_w_q_up']).reshape(b, n, MLA_HEADS, QK_NOPE + QK_ROPE)
    kv = (rms_norm(ckv, p['mla_kv_norm']) @ p['mla_w_kv_up']).reshape(b, n, MLA_HEADS, QK_NOPE + V_HEAD)
    q_nope, q_pe = jnp.split(q, [QK_NOPE], -1)
    k_nope, v = jnp.split(kv, [QK_NOPE], -1)
    if rope is not None:
        cos, sin = rope
        q_pe = apply_axial_rope(q_pe, cos[:, None], sin[:, None])
        kpe = apply_axial_rope(kpe, cos, sin)
    q = jnp.concatenate([q_nope, q_pe], -1)
    k = jnp.concatenate([k_nope, jnp.broadcast_to(kpe[:, :, None], (b, n, MLA_HEADS, QK_ROPE))], -1)
    return q, k, v


def block_attention(q, k, v):
    b, n, h, dk = q.shape
    scale = dk ** -0.5
    qb = jnp.moveaxis(q.reshape(b, n // Q_BLOCK, Q_BLOCK, h, dk), 1, 0)

    def attend(qi):
        s = jnp.einsum('bqhd,bkhd->bhqk', qi, k).astype(F32) * scale
        pr = jax.nn.softmax(s, -1).astype(v.dtype)
        return jnp.einsum('bhqk,bkhd->bqhd', pr, v)

    o = lax.map(attend, qb)
    return jnp.moveaxis(o, 0, 1).reshape(b, n, h * v.shape[-1])


def mla_branch(uc, ul, rope, p, need_ctx):
    qc, kc, vc = mla_project(uc, p, None)
    ql, kl, vl = mla_project(ul, p, rope)
    yl = block_attention(ql, jnp.concatenate([kc, kl], 1), jnp.concatenate([vc, vl], 1))
    yc = block_attention(qc, kc, vc) if need_ctx else None
    return yc, yl


def segsum(x):
    t = x.shape[-1]
    xr = jnp.broadcast_to(x[..., :, None], x.shape + (t,))
    xr = jnp.where(jnp.tril(jnp.ones((t, t), bool), -1), xr, 0.0)
    cs = jnp.cumsum(xr, axis=-2)
    return jnp.where(jnp.tril(jnp.ones((t, t), bool)), cs, -jnp.inf)


def ssd_chunked(X, dA, Bm, Cm, init):
    b, L, g, j, p_ = X.shape
    c = L // SSM_CHUNK
    X = X.reshape(b, c, SSM_CHUNK, g, j, p_)
    Bm = Bm.reshape(b, c, SSM_CHUNK, g, -1)
    Cm = Cm.reshape(b, c, SSM_CHUNK, g, -1)
    A = jnp.moveaxis(dA.reshape(b, c, SSM_CHUNK, g, j), (1, 2), (3, 4))
    A_cs = jnp.cumsum(A, -1)
    CB = jnp.einsum('bclgn,bcsgn->bgcls', Cm, Bm)
    M = CB[:, :, None] * jnp.exp(segsum(A))
    y_diag = jnp.einsum('bgjcls,bcsgjp->bclgjp', M, X)
    ds = jnp.moveaxis(jnp.exp(A_cs[..., -1:] - A_cs), (3, 4), (1, 2))
    states = jnp.einsum('bclgn,bclgjp->bcgjpn', Bm, X * ds[..., None])
    states = jnp.concatenate([init[:, None], states], 1)
    chunk_A = jnp.pad(A_cs[..., -1], ((0, 0), (0, 0), (0, 0), (1, 0)))
    new_states = jnp.einsum('bgjzc,bcgjpn->bzgjpn', jnp.exp(segsum(chunk_A)), states)
    prev_states, final = new_states[:, :-1], new_states[:, -1]
    sdo = jnp.moveaxis(jnp.exp(A_cs), (3, 4), (1, 2))
    y_off = jnp.einsum('bclgn,bcgjpn->bclgjp', Cm, prev_states) * sdo[..., None]
    return (y_diag + y_off).reshape(b, L, g, j, p_), final


def mamba_prep(u, p):
    b, n, _ = u.shape
    z, xbc, dt = jnp.split(u, [SSM_INNER, SSM_INNER + SSM_XBC], -1)
    xbc = jax.nn.silu(dwconv3(xbc, p['ssm_conv_w'], p['ssm_conv_b']).astype(F32))
    xs, bm, cm = jnp.split(xbc, [SSM_INNER, SSM_INNER + SSM_GROUPS * SSM_STATE], -1)
    xs = xs.reshape(b, n, SSM_HEADS, SSM_HEAD_DIM)
    bm = bm.reshape(b, n, SSM_GROUPS, SSM_STATE)
    cm = cm.reshape(b, n, SSM_GROUPS, SSM_STATE)
    dt = jax.nn.softplus(dt.astype(F32).reshape(b, n, 2, SSM_HEADS) + p['ssm_dt_bias'])
    return z, xs, bm, cm, dt


def ssd_direction(xs, dt, a, bm, cm, init, reverse):
    if reverse:
        xs, dt, bm, cm = (jnp.flip(t, 1) for t in (xs, dt, bm, cm))
    b, n = dt.shape[:2]
    hg = SSM_HEADS // SSM_GROUPS
    X = (xs * dt[..., None]).reshape(b, n, SSM_GROUPS, hg, SSM_HEAD_DIM)
    dA = (dt * a).reshape(b, n, SSM_GROUPS, hg)
    y, final = ssd_chunked(X, dA, bm, cm, init)
    y = y.reshape(b, n, SSM_HEADS, SSM_HEAD_DIM)
    if reverse:
        y = jnp.flip(y, 1)
    return y, final


def mamba_out(y, xs, z, p):
    b, n = y.shape[:2]
    y = (y + xs * p['ssm_d'][:, None]).reshape(b, n, SSM_INNER) * jax.nn.silu(z.astype(F32))
    yg = y.reshape(b, n, SSM_GROUPS, SSM_INNER // SSM_GROUPS)
    yg = yg * lax.rsqrt(jnp.mean(yg * yg, -1, keepdims=True) + 1e-5)
    return (yg.reshape(b, n, SSM_INNER) * p['ssm_norm']).astype(z.dtype)


def mamba_branch(uc, ul, p, need_ctx):
    a = -jnp.exp(p['ssm_a_log'].astype(F32))
    zc, xc, bc, cc, dtc = mamba_prep(uc, p)
    zl, xl, bl, cl, dtl = mamba_prep(ul, p)
    zero = jnp.zeros((ul.shape[0], SSM_GROUPS, SSM_HEADS // SSM_GROUPS, SSM_HEAD_DIM, SSM_STATE), F32)
    yc = yl = 0.0
    for d, rev in enumerate((False, True)):
        yc_d, sc = ssd_direction(xc, dtc[:, :, d], a[d], bc, cc, zero, rev)
        yl_d, _ = ssd_direction(xl, dtl[:, :, d], a[d], bl, cl, sc, rev)
        yc = yc + yc_d
        yl = yl + yl_d
    return (mamba_out(yc, xc, zc, p) if need_ctx else None), mamba_out(yl, xl, zl, p)


def hyena_filters(n, p):
    t = jnp.linspace(0.0, 1.0, n, dtype=F32)[:, None]
    wpos = 2.0 * math.pi * jnp.arange(n, dtype=F32)[:, None] / n
    f = jnp.linspace(1e-4, HY_BANDS - 1, HY_BANDS, dtype=F32)[None]
    z = jnp.concatenate([t, jnp.cos(f * wpos), -jnp.sin(f * wpos)], -1)
    freq = p['hy_freq']
    hdn = jnp.sin(freq * (z @ p['hy_w1'] + p['hy_b1']))
    hdn = jnp.sin(freq * (hdn @ p['hy_w2'] + p['hy_b2']))
    filt = (hdn @ p['hy_w3']).astype(F32).reshape(n, 2, HY_WIDTH)
    deltas = jnp.abs(jnp.linspace(HY_MIN_DECAY, HY_MAX_DECAY, HY_WIDTH, dtype=F32))
    return filt * jnp.exp(-t * deltas)[:, None]


def bidir_long_conv(v, filt):
    n, ch = v.shape[1], v.shape[2]
    k_full = jnp.concatenate([filt[:, 0], jnp.zeros((1, ch), F32), filt[:0:-1, 1]], 0)
    kf = jnp.fft.rfft(k_full, axis=0)
    vf = jnp.fft.rfft(v, n=2 * n, axis=1)
    return jnp.fft.irfft(vf * kf, n=2 * n, axis=1)[:, :n]


def hyena_seq(u, p):
    n = u.shape[1]
    uc = dwconv3(u, p['hy_conv_w'], p['hy_conv_b']).astype(F32)
    x0, x1, v = jnp.split(uc, 3, -1)
    v = v * x1
    y = bidir_long_conv(v, hyena_filters(n, p)) + v * p['hy_d']
    return (y * x0).astype(u.dtype)


def hyena_branch(uc, ul, p, need_ctx):
    return (hyena_seq(uc, p) if need_ctx else None), hyena_seq(ul, p)


def rwkv_prep(u, p):
    b, n, _ = u.shape
    heads = lambda t: t.reshape(b, n, RW_HEADS, RW_HEAD_DIM)
    us = token_shift(u, p['rw_mu']).astype(F32)
    cuts = np.cumsum([RW_WIDTH, RW_WIDTH, RW_WIDTH, 2 * RW_DECAY_LORA, 2 * RW_ICLR_LORA]).tolist()
    r, k, v, wd, ad, gd = jnp.split(us, cuts, -1)
    g = jax.nn.sigmoid(gd) @ p['rw_g_up']
    kk = heads(k * p['rw_kk'])
    kk = kk / jnp.maximum(jnp.sqrt(jnp.sum(kk * kk, -1, keepdims=True)), 1e-12)
    wd = wd.reshape(b, n, 2, RW_DECAY_LORA)
    ad = ad.reshape(b, n, 2, RW_ICLR_LORA)
    dirs = []
    for d in range(2):
        w_log = -jax.nn.softplus(-(p['rw_w0'][d] + jnp.tanh(wd[:, :, d]) @ p['rw_w_up'][d])) - 0.5
        iclr = jax.nn.sigmoid(p['rw_a0'][d] + ad[:, :, d] @ p['rw_a_up'][d])
        k_d = heads(k * (1.0 + (iclr - 1.0) * p['rw_ka']))
        dirs.append((heads(-jnp.exp(w_log)), k_d, kk * heads(iclr)))
    return heads(r), heads(v), kk, g, dirs


def _to_scans(t, reverse):
    b, n, h, d = t.shape
    t = jnp.moveaxis(t, 2, 1).reshape(b * h, n, d)
    return jnp.flip(t, 1) if reverse else t


def rwkv_bidir_scan(r, v, kk, dirs, s0):
    b, n, h, d = r.shape
    stack = lambda f: jnp.concatenate([f(0), f(1)], 0)
    rr = stack(lambda dd: _to_scans(r, dd == 1))
    vv = stack(lambda dd: _to_scans(v, dd == 1))
    aa = stack(lambda dd: _to_scans(-kk, dd == 1))
    lw = stack(lambda dd: _to_scans(dirs[dd][0], dd == 1))
    kd = stack(lambda dd: _to_scans(dirs[dd][1], dd == 1))
    bd = stack(lambda dd: _to_scans(dirs[dd][2], dd == 1))
    y, sT = rwkv_scan_pallas(rr, lw, kd, vv, aa, bd, s0)
    y = y.reshape(2, b, h, n, d)
    y = y[0] + jnp.flip(y[1], 2)
    return jnp.moveaxis(y, 1, 2), sT


def rwkv_out(y, r, v, g, dirs, p, dtype):
    b, n = y.shape[:2]
    mu = jnp.mean(y, -1, keepdims=True)
    var = jnp.mean(jnp.square(y - mu), -1, keepdims=True)
    yn = ((y - mu) * lax.rsqrt(var + RW_GN_EPS)).reshape(b, n, RW_WIDTH) * p['rw_ln_g'] + p['rw_ln_b']
    bonus = sum(jnp.sum(r * k_d * p['rw_rk'], -1, keepdims=True) * v for _, k_d, _ in dirs)
    return ((yn + bonus.reshape(b, n, RW_WIDTH)) * g).astype(dtype)


def rwkv_branch(uc, ul, p, need_ctx):
    rc, vc, kkc, gc, dc = rwkv_prep(uc, p)
    rl, vl, kkl, gl, dl = rwkv_prep(ul, p)
    s0 = jnp.zeros((2 * ul.shape[0] * RW_HEADS, RW_HEAD_DIM, RW_HEAD_DIM), F32)
    yc, sc = rwkv_bidir_scan(rc, vc, kkc, dc, s0)
    yl, _ = rwkv_bidir_scan(rl, vl, kkl, dl, sc)
    out_c = rwkv_out(yc, rc, vc, gc, dc, p, uc.dtype) if need_ctx else None
    return out_c, rwkv_out(yl, rl, vl, gl, dl, p, ul.dtype)


def merge_branches(branches, gate_logits, p):
    b, n, _ = gate_logits.shape
    gates = jax.nn.sigmoid(gate_logits.reshape(b, n, N_BRANCH, D_MODEL))
    merged = sum(gates[:, :, i] * (y @ p['w_branch'][i]) for i, y in enumerate(branches))
    return merged @ p['w_out']


def hybrid_mixer(hc, hl, p, rope, need_ctx):
    cuts = np.cumsum([MLA_COLS, SSM_COLS, HY_COLS, RW_COLS]).tolist()
    pc = jnp.split(hc @ p['w_in'], cuts, -1)
    pl_ = jnp.split(hl @ p['w_in'], cuts, -1)
    branches = (mla_branch(pc[0], pl_[0], rope, p, need_ctx),
                mamba_branch(pc[1], pl_[1], p, need_ctx),
                hyena_branch(pc[2], pl_[2], p, need_ctx),
                rwkv_branch(pc[3], pl_[3], p, need_ctx))
    yl = merge_branches([br[1] for br in branches], pl_[4], p)
    yc = merge_branches([br[0] for br in branches], pc[4], p) if need_ctx else None
    return yc, yl


def expert_choice_ffn(h, p):
    b, n, _ = h.shape
    cap = EC_CAPACITY * n // N_EXPERTS
    aff = jax.nn.softmax((h @ p['w_router']).astype(F32), -1)
    gate, idx = lax.top_k(jnp.swapaxes(aff, 1, 2), cap)
    bidx = jnp.arange(b)[:, None, None]
    xs = h[bidx, idx]
    hid = jax.nn.silu(jnp.einsum('becd,edf->becf', xs, p['w_gate_e'])) * jnp.einsum('becd,edf->becf', xs, p['w_up_e'])
    ye = jnp.einsum('becf,efd->becd', hid, p['w_down_e']) * gate[..., None].astype(h.dtype)
    return jnp.zeros_like(h).at[bidx, idx].add(ye.astype(h.dtype))


def trunk_layer(xc, xl, mod_c, mod_l, p, rope, need_ctx):
    sh1c, sc1c, g1c, sh2c, sc2c, g2c = jnp.split(mod_c, 6, -1)
    sh1, sc1, g1, sh2, sc2, g2 = (m[:, None] for m in jnp.split(mod_l, 6, -1))
    yc, yl = hybrid_mixer(modulate(xc, sh1c, sc1c), modulate(xl, sh1, sc1), p, rope, need_ctx)
    xl = post_norm(xl, g1 * yl, p['ln1_g'], p['ln1_b'])
    xl = post_norm(xl, g2 * expert_choice_ffn(modulate(xl, sh2, sc2), p), p['ln2_g'], p['ln2_b'])
    if need_ctx:
        xc = post_norm(xc, g1c * yc, p['ln1_g'], p['ln1_b'])
        xc = post_norm(xc, g2c * expert_choice_ffn(modulate(xc, sh2c, sc2c), p), p['ln2_g'], p['ln2_b'])
    return xc, xl


def kernel(x, c, ctx, c_ctx, w_ada, b_ada, w_in, mla_q_norm, mla_w_q_up, mla_kv_norm, mla_w_kv_up,
           ssm_conv_w, ssm_conv_b, ssm_dt_bias, ssm_a_log, ssm_d, ssm_norm,
           hy_conv_w, hy_conv_b, hy_w1, hy_b1, hy_w2, hy_b2, hy_w3, hy_freq, hy_d,
           rw_mu, rw_w0, rw_w_up, rw_a0, rw_a_up, rw_g_up, rw_kk, rw_ka, rw_rk, rw_ln_g, rw_ln_b,
           w_branch, w_out, ln1_g, ln1_b, w_router, w_gate_e, w_up_e, w_down_e, ln2_g, ln2_b):
    stacked = dict(
        w_in=w_in, mla_q_norm=mla_q_norm, mla_w_q_up=mla_w_q_up, mla_kv_norm=mla_kv_norm, mla_w_kv_up=mla_w_kv_up,
        ssm_conv_w=ssm_conv_w, ssm_conv_b=ssm_conv_b, ssm_dt_bias=ssm_dt_bias, ssm_a_log=ssm_a_log,
        ssm_d=ssm_d, ssm_norm=ssm_norm,
        hy_conv_w=hy_conv_w, hy_conv_b=hy_conv_b, hy_w1=hy_w1, hy_b1=hy_b1, hy_w2=hy_w2, hy_b2=hy_b2,
        hy_w3=hy_w3, hy_freq=hy_freq, hy_d=hy_d,
        rw_mu=rw_mu, rw_w0=rw_w0, rw_w_up=rw_w_up, rw_a0=rw_a0, rw_a_up=rw_a_up, rw_g_up=rw_g_up,
        rw_kk=rw_kk, rw_ka=rw_ka, rw_rk=rw_rk, rw_ln_g=rw_ln_g, rw_ln_b=rw_ln_b,
        w_branch=w_branch, w_out=w_out, ln1_g=ln1_g, ln1_b=ln1_b,
        w_router=w_router, w_gate_e=w_gate_e, w_up_e=w_up_e, w_down_e=w_down_e, ln2_g=ln2_g, ln2_b=ln2_b)
    rope = axial_rope_tables(x.shape[1])
    xc, xl = ctx, x
    for i in range(DEPTH):
        p = {name: arr[i] for name, arr in stacked.items()}
        mod_l = jax.nn.silu(c) @ w_ada[i] + b_ada[i]
        mod_c = jax.nn.silu(c_ctx) @ w_ada[i] + b_ada[i]
        xc, xl = trunk_layer(xc, xl, mod_c, mod_l, p, rope, i < DEPTH - 1)
    return xl
```

```python
import functools
import math

import jax
import jax.numpy as jnp
from jax import lax
import numpy as np
from jax.experimental import pallas as pl
from jax.experimental.pallas import tpu as pltpu

D_MODEL = 2048
DEPTH = 2

GRID_W = 64
N_BRANCH = 4
BRANCH_WIDTH = D_MODEL // 2

MLA_Q_LORA = D_MODEL // 4
MLA_KV_LORA = D_MODEL // 4
QK_NOPE = 128
QK_ROPE = 64
V_HEAD = 128
MLA_HEADS = BRANCH_WIDTH // V_HEAD
ROPE_BASE = 10000.0
Q_BLOCK = 128

SSM_HEAD_DIM = 64
SSM_HEADS = BRANCH_WIDTH // SSM_HEAD_DIM
SSM_INNER = SSM_HEADS * SSM_HEAD_DIM
SSM_GROUPS = 2
SSM_STATE = 128
SSM_CHUNK = 128
SSM_XBC = SSM_INNER + 2 * SSM_GROUPS * SSM_STATE

HY_WIDTH = BRANCH_WIDTH
HY_EMB = 33
HY_BANDS = (HY_EMB - 1) // 2
HY_FILTER_HIDDEN = 64
HY_TARGET = 1e-2
HY_SLOW_FRAC = 1.5
HY_QUICK_FRAC = 0.3
HY_MIN_DECAY = math.log(HY_TARGET) / HY_SLOW_FRAC
HY_MAX_DECAY = math.log(HY_TARGET) / HY_QUICK_FRAC

RW_HEAD_DIM = 64
RW_WIDTH = BRANCH_WIDTH
RW_HEADS = RW_WIDTH // RW_HEAD_DIM
RW_DECAY_LORA = 64
RW_ICLR_LORA = 64
RW_GATE_LORA = 160
RW_GN_EPS = 64e-5

N_EXPERTS = 16
EXPERT_FF = 1408
EC_CAPACITY = 2

ALPHA = (2 * DEPTH) ** 0.25

MLA_COLS = MLA_Q_LORA + MLA_KV_LORA + QK_ROPE
SSM_COLS = SSM_INNER + SSM_XBC + 2 * SSM_HEADS
HY_COLS = 3 * HY_WIDTH
RW_COLS = 3 * RW_WIDTH + 2 * RW_DECAY_LORA + 2 * RW_ICLR_LORA + RW_GATE_LORA

F32 = jnp.float32
BF16 = jnp.bfloat16
HI = lax.Precision.HIGHEST

RW_T = 64
RW_P = 128
RW_PAIRS_PER_STEP = 2
RW_INV_BASE = 8
ATT_DK = 256
ATT_DV = 128


def _mm(a, b):
    return jnp.dot(a.astype(BF16), b.astype(BF16), preferred_element_type=F32)


def _mm_nt(a, b):
    return lax.dot_general(a.astype(BF16), b.astype(BF16), (((1,), (1,)), ((), ())), preferred_element_type=F32)


def _mm_tn(a, b):
    return lax.dot_general(a.astype(BF16), b.astype(BF16), (((0,), (0,)), ((), ())), preferred_element_type=F32)


def _mm3(a, b):
    ah = a.astype(BF16)
    bh = b.astype(BF16)
    al = (a - ah.astype(F32)).astype(BF16)
    bl = (b - bh.astype(F32)).astype(BF16)
    d = lambda x, y: jnp.dot(x, y, preferred_element_type=F32)
    return d(ah, bh) + d(ah, bl) + d(al, bh)


def _rwkv_pair_chunks(chains, inv_masks):
    T = RW_T
    each = lambda f, *ls: [f(*xs) for xs in zip(*ls)]
    S = [ch[0] for ch in chains]
    r, lw, k, v, a, b = ([ch[1][i] for ch in chains] for i in range(6))
    rev = [ch[2] for ch in chains]
    cum, head_lo, strict_bd, incl_bd, eye_bd = ([ch[3][i] for ch in chains] for i in range(5))

    def stack(x, lo):
        return jnp.concatenate([jnp.where(lo, x, 0.0), jnp.where(lo, 0.0, x)], axis=0)

    c = each(lambda m, x: jnp.dot(m, x, precision=HI, preferred_element_type=F32), cum, lw)
    cT = each(lambda x, rv: x[0:1, :] if rv else x[T - 1:T, :], c, rev)
    e_neg = each(lambda x: jnp.exp(-x), c)
    e_end = each(lambda x, y: jnp.exp(x - y), cT, c)
    a_s = each(lambda x, cc, l, lo: stack(x * jnp.exp(cc - l), lo), a, c, lw, head_lo)
    r_s = each(lambda x, cc, lo: stack(x * jnp.exp(cc), lo), r, c, head_lo)
    b_s = each(lambda x, e, lo: stack(x * e, lo), b, e_neg, head_lo)
    k_s = each(lambda x, e, lo: stack(x * e, lo), k, e_neg, head_lo)
    bh_s = each(lambda x, e, lo: stack(x * e, lo), b, e_end, head_lo)
    kh_s = each(lambda x, e, lo: stack(x * e, lo), k, e_end, head_lo)
    v_s = each(stack, v, head_lo)
    g_T = each(jnp.exp, cT)

    ar = each(lambda x, y: jnp.concatenate([x, y], axis=0), a_s, r_s)
    gb = each(_mm_nt, ar, b_s)
    gk = each(_mm_nt, ar, k_s)
    n = each(lambda m, g: jnp.where(m, g[:2 * T], 0.0), strict_bd, gb)
    l_ak = each(lambda m, g: jnp.where(m, g[:2 * T], 0.0), strict_bd, gk)
    m_rb = each(lambda m, g: jnp.where(m, g[2 * T:], 0.0), incl_bd, gb)
    m_rk = each(lambda m, g: jnp.where(m, g[2 * T:], 0.0), incl_bd, gk)
    lv = each(_mm, l_ak, v_s)
    n8 = each(lambda x: jnp.where(inv_masks[0], x, 0.0), n)
    p = each(lambda e, x: e + x, eye_bd, n8)
    m = each(_mm, n8, n8)
    p = each(lambda x, y: x + _mm(y, x), p, m)
    m = each(_mm, m, m)
    p = each(lambda x, y: x + _mm(y, x), p, m)
    for merge in inv_masks[1:]:
        n_off = each(lambda x: jnp.where(merge, x, 0.0), n)
        p = each(lambda x, y: x + _mm(_mm(x, y), x), p, n_off)
    w = each(_mm, p, a_s)
    u0 = each(_mm, p, lv)
    wr = each(lambda x, y, s: _mm_nt(jnp.concatenate([x, y], axis=0), s), w, r_s, S)
    uv = each(lambda x, y, z: jnp.concatenate([x[:2 * T] + y, z], axis=0), wr, u0, v_s)
    y_s = each(lambda x, p1, p2, z: x[2 * T:] + _mm(jnp.concatenate([p1, p2], axis=1), z), wr, m_rb, m_rk, uv)
    y = each(lambda x: x[:T] + x[T:], y_s)
    S_new = each(lambda s, g, z, p1, p2: s * g + _mm_tn(z, jnp.concatenate([p1, p2], axis=0)),
                 S, g_T, uv, bh_s, kh_s)
    return list(zip(y, S_new))


def _rwkv_consts():
    T = RW_T
    row = lax.broadcasted_iota(jnp.int32, (T, T), 0)
    col = lax.broadcasted_iota(jnp.int32, (T, T), 1)
    cum_f = (row >= col).astype(F32)
    cum_b = (row <= col).astype(F32)
    head_lo = lax.broadcasted_iota(jnp.int32, (T, RW_P), 1) < RW_P // 2
    r2 = lax.broadcasted_iota(jnp.int32, (2 * T, 2 * T), 0)
    c2 = lax.broadcasted_iota(jnp.int32, (2 * T, 2 * T), 1)
    same = (r2 >= T) == (c2 >= T)
    eye_bd = (r2 == c2).astype(F32)
    fwd = (cum_f, head_lo, same & (r2 > c2), same & (r2 >= c2), eye_bd)
    bwd = (cum_b, head_lo, same & (r2 < c2), same & (r2 <= c2), eye_bd)
    blk = lambda s: (r2 // s) == (c2 // s)
    inv_masks = [blk(RW_INV_BASE)]
    s = RW_INV_BASE
    while s < T:
        inv_masks.append(blk(2 * s) & jnp.logical_not(blk(s)))
        s *= 2
    return fwd, bwd, inv_masks


def _rwkv_kernel(rf, vf, af, lwf, kf, bf, rb, vb, ab, lwb, kb, bb, s0_ref, yf_ref, yb_ref, sT_ref, s_scr, *, nsub):
    j = pl.program_id(2)

    @pl.when(j == 0)
    def _():
        s_scr[...] = s0_ref[:, 0]

    T = RW_T
    cf, cb, inv_masks = _rwkv_consts()

    def body(i, carry):
        sf = pl.ds(pl.multiple_of(i * T, T), T)
        sb = pl.ds(pl.multiple_of((nsub - 1 - i) * T, T), T)
        lanes = [slice(q * RW_P, (q + 1) * RW_P) for q in range(RW_PAIRS_PER_STEP)]
        ins = []
        for ln in lanes:
            ins.append(tuple(ref[0, sf, ln] for ref in (rf, lwf, kf, vf, af, bf)))
            ins.append(tuple(ref[0, sb, ln] for ref in (rb, lwb, kb, vb, ab, bb)))
        res = _rwkv_pair_chunks([(carry[c], ins[c], bool(c % 2), cb if c % 2 else cf)
                                 for c in range(2 * RW_PAIRS_PER_STEP)], inv_masks)
        for q, ln in enumerate(lanes):
            yf_ref[0, sf, ln] = res[2 * q][0]
            yb_ref[0, sb, ln] = res[2 * q + 1][0]
        return tuple(s for _, s in res)

    init = tuple(s_scr[d, q] for q in range(RW_PAIRS_PER_STEP) for d in range(2))
    fin = lax.fori_loop(0, nsub, body, init)
    for q in range(RW_PAIRS_PER_STEP):
        for d in range(2):
            s_scr[d, q] = fin[2 * q + d]

    @pl.when(j == pl.num_programs(2) - 1)
    def _():
        for q in range(RW_PAIRS_PER_STEP):
            for d in range(2):
                sT_ref[d, 0, q] = fin[2 * q + d]


def rwkv_scan_pallas(r, v, kk, lw0, lw1, kd0, kd1, bd0, bd1, s0, *, tb=256):
    B, n, C = r.shape
    tb = min(tb, n)
    nb = n // tb
    pp = RW_PAIRS_PER_STEP
    assert n % tb == 0 and tb % RW_T == 0 and C % (pp * RW_P) == 0
    fw = pl.BlockSpec((1, tb, pp * RW_P), lambda b, p, j: (b, j, p))
    bw = pl.BlockSpec((1, tb, pp * RW_P), lambda b, p, j: (b, nb - 1 - j, p))
    st = pl.BlockSpec((2, 1, pp, RW_P, RW_P), lambda b, p, j: (0, b, p, 0, 0))
    a = -kk
    return pl.pallas_call(
        functools.partial(_rwkv_kernel, nsub=tb // RW_T),
        grid=(B, C // (pp * RW_P), nb),
        in_specs=[fw] * 6 + [bw] * 6 + [st],
        out_specs=[fw, bw, st],
        out_shape=[jax.ShapeDtypeStruct((B, n, C), F32), jax.ShapeDtypeStruct((B, n, C), F32),
                   jax.ShapeDtypeStruct(s0.shape, F32)],
        scratch_shapes=[pltpu.VMEM((2, pp, RW_P, RW_P), F32)],
        compiler_params=pltpu.CompilerParams(dimension_semantics=("parallel", "parallel", "arbitrary")),
        name="rwkv_scan",
    )(r, v, a, lw0, kd0, bd0, r, v, a, lw1, kd1, bd1, s0)


def _flash_kernel(q_ref, k_ref, v_ref, o_ref, m_scr, l_scr, acc_scr):
    j = pl.program_id(3)

    @pl.when(j == 0)
    def _():
        m_scr[...] = jnp.full(m_scr.shape, -jnp.inf, F32)
        l_scr[...] = jnp.zeros(l_scr.shape, F32)
        acc_scr[...] = jnp.zeros(acc_scr.shape, F32)

    s = lax.dot_general(q_ref[0], k_ref[0], (((1,), (1,)), ((), ())), preferred_element_type=F32)
    m_prev = m_scr[...]
    m_new = jnp.maximum(m_prev, jnp.max(s, axis=-1, keepdims=True))
    alpha = jnp.exp(m_prev - m_new)
    p = jnp.exp(s - m_new)
    l_scr[...] = alpha * l_scr[...] + jnp.sum(p, axis=-1, keepdims=True)
    acc_scr[...] = alpha * acc_scr[...] + jnp.dot(p.astype(BF16), v_ref[0], preferred_element_type=F32)
    m_scr[...] = m_new

    @pl.when(j == pl.num_programs(3) - 1)
    def _():
        o_ref[0] = (acc_scr[...] / l_scr[...]).astype(o_ref.dtype)


def flash_attention(q, k, v, *, tq, tk, out_dtype=F32):
    B, nq, hq = q.shape
    nk = k.shape[1]
    H = hq // ATT_DK
    assert nq % tq == 0 and nk % tk == 0
    return pl.pallas_call(
        _flash_kernel,
        grid=(B, H, nq // tq, nk // tk),
        in_specs=[pl.BlockSpec((1, tq, ATT_DK), lambda b, h, i, j: (b, i, h)),
                  pl.BlockSpec((1, tk, ATT_DK), lambda b, h, i, j: (b, j, h)),
                  pl.BlockSpec((1, tk, ATT_DV), lambda b, h, i, j: (b, j, h))],
        out_specs=pl.BlockSpec((1, tq, ATT_DV), lambda b, h, i, j: (b, i, h)),
        out_shape=jax.ShapeDtypeStruct((B, nq, H * ATT_DV), out_dtype),
        scratch_shapes=[pltpu.VMEM((tq, 1), F32), pltpu.VMEM((tq, 1), F32), pltpu.VMEM((tq, ATT_DV), F32)],
        compiler_params=pltpu.CompilerParams(
            dimension_semantics=("parallel", "parallel", "parallel", "arbitrary")),
        name="mla_flash",
    )(q, k, v)


def pack_heads(t, width, scale=None):
    b, n, h, d = t.shape
    if scale is not None:
        t = t * scale
    t = jnp.pad(t, ((0, 0), (0, 0), (0, 0), (0, width - d)))
    return t.reshape(b, n, h * width).astype(BF16)


def layer_norm(x, eps=1e-6):
    xf = x.astype(F32)
    mu = jnp.mean(xf, -1, keepdims=True)
    var = jnp.mean(jnp.square(xf - mu), -1, keepdims=True)
    return (xf - mu) * lax.rsqrt(var + eps)


def rms_norm(x, g, eps=1e-6):
    xf = x.astype(F32)
    return (xf * lax.rsqrt(jnp.mean(xf * xf, -1, keepdims=True) + eps) * g).astype(x.dtype)


def modulate(x, shift, scale):
    return (layer_norm(x) * (1.0 + scale) + shift).astype(x.dtype)


def post_norm(x, y, g, b):
    return (layer_norm(ALPHA * x + y) * g + b).astype(x.dtype)


def dwconv3(u, w, b):
    up = jnp.pad(u, ((0, 0), (1, 1), (0, 0)))
    return up[:, :-2] * w[0] + up[:, 1:-1] * w[1] + up[:, 2:] * w[2] + b


def token_shift(u, mu):
    up = jnp.pad(u, ((0, 0), (1, 1), (0, 0)))
    return u + mu[0] * (up[:, :-2] - u) + mu[1] * (up[:, 2:] - u)


def axial_rope_tables(n_tokens):
    rows = n_tokens // GRID_W
    row = jnp.repeat(jnp.arange(rows), GRID_W).astype(F32)
    col = jnp.tile(jnp.arange(GRID_W), rows).astype(F32)
    half = QK_ROPE // 2
    inv = ROPE_BASE ** (-jnp.arange(0, half, 2, dtype=F32) / half)
    ang = jnp.stack([row[:, None] * inv, col[:, None] * inv], 1)
    return jnp.cos(ang), jnp.sin(ang)


def apply_axial_rope(x, cos, sin):
    xs = x.reshape(x.shape[:-1] + (2, 2, QK_ROPE // 4)).astype(F32)
    x1, x2 = xs[..., 0, :], xs[..., 1, :]
    out = jnp.stack([x1 * cos - x2 * sin, x2 * cos + x1 * sin], -2)
    return out.reshape(x.shape).astype(x.dtype)


def mla_project(u, p, rope):
    b, n, _ = u.shape
    cq, ckv, kpe = jnp.split(u, [MLA_Q_LORA, MLA_Q_LORA + MLA_KV_LORA], -1)
    q = (rms_norm(cq, p['mla_q_norm']) @ p['mla_w_q_up']).reshape(b, n, MLA_HEADS, QK_NOPE + QK_ROPE)
    kv = (rms_norm(ckv, p['mla_kv_norm']) @ p['mla_w_kv_up']).reshape(b, n, MLA_HEADS, QK_NOPE + V_HEAD)
    q_nope, q_pe = jnp.split(q, [QK_NOPE], -1)
    k_nope, v = jnp.split(kv, [QK_NOPE], -1)
    if rope is not None:
        cos, sin = rope
        q_pe = apply_axial_rope(q_pe, cos[:, None], sin[:, None])
        kpe = apply_axial_rope(kpe, cos, sin)
    q = jnp.concatenate([q_nope, q_pe], -1)
    k = jnp.concatenate([k_nope, jnp.broadcast_to(kpe[:, :, None], (b, n, MLA_HEADS, QK_ROPE))], -1)
    return q, k, v


def block_attention(q, k, v):
    nq, nk = q.shape[1], k.shape[1]
    tq = min(nq, 1024)
    tk = 768 if nk % 768 == 0 else min(nk, 256)
    return flash_attention(pack_heads(q, ATT_DK, q.shape[-1] ** -0.5), pack_heads(k, ATT_DK), pack_heads(v, ATT_DV),
                           tq=tq, tk=tk)


def mla_branch(uc, ul, rope, p, need_ctx):
    qc, kc, vc = mla_project(uc, p, None)
    ql, kl, vl = mla_project(ul, p, rope)
    yl = block_attention(ql, jnp.concatenate([kc, kl], 1), jnp.concatenate([vc, vl], 1))
    yc = block_attention(qc, kc, vc) if need_ctx else None
    return yc, yl


def segsum(x):
    t = x.shape[-1]
    xr = jnp.broadcast_to(x[..., :, None], x.shape + (t,))
    xr = jnp.where(jnp.tril(jnp.ones((t, t), bool), -1), xr, 0.0)
    cs = jnp.cumsum(xr, axis=-2)
    return jnp.where(jnp.tril(jnp.ones((t, t), bool)), cs, -jnp.inf)


def ssd_chunked(X, dA, Bm, Cm, init):
    b, L, g, j, p_ = X.shape
    c = L // SSM_CHUNK
    X = X.reshape(b, c, SSM_CHUNK, g, j, p_)
    Bm = Bm.reshape(b, c, SSM_CHUNK, g, -1)
    Cm = Cm.reshape(b, c, SSM_CHUNK, g, -1)
    A = jnp.moveaxis(dA.reshape(b, c, SSM_CHUNK, g, j), (1, 2), (3, 4))
    A_cs = jnp.cumsum(A, -1)
    CB = jnp.einsum('bclgn,bcsgn->bgcls', Cm, Bm)
    M = CB[:, :, None] * jnp.exp(segsum(A))
    y_diag = jnp.einsum('bgjcls,bcsgjp->bclgjp', M, X)
    ds = jnp.moveaxis(jnp.exp(A_cs[..., -1:] - A_cs), (3, 4), (1, 2))
    states = jnp.einsum('bclgn,bclgjp->bcgjpn', Bm, X * ds[..., None])
    states = jnp.concatenate([init[:, None], states], 1)
    chunk_A = jnp.pad(A_cs[..., -1], ((0, 0), (0, 0), (0, 0), (1, 0)))
    new_states = jnp.einsum('bgjzc,bcgjpn->bzgjpn', jnp.exp(segsum(chunk_A)), states)
    prev_states, final = new_states[:, :-1], new_states[:, -1]
    sdo = jnp.moveaxis(jnp.exp(A_cs), (3, 4), (1, 2))
    y_off = jnp.einsum('bclgn,bcgjpn->bclgjp', Cm, prev_states) * sdo[..., None]
    return (y_diag + y_off).reshape(b, L, g, j, p_), final


def mamba_prep(u, p):
    b, n, _ = u.shape
    z, xbc, dt = jnp.split(u, [SSM_INNER, SSM_INNER + SSM_XBC], -1)
    xbc = jax.nn.silu(dwconv3(xbc, p['ssm_conv_w'], p['ssm_conv_b']).astype(F32))
    xs, bm, cm = jnp.split(xbc, [SSM_INNER, SSM_INNER + SSM_GROUPS * SSM_STATE], -1)
    xs = xs.reshape(b, n, SSM_HEADS, SSM_HEAD_DIM)
    bm = bm.reshape(b, n, SSM_GROUPS, SSM_STATE)
    cm = cm.reshape(b, n, SSM_GROUPS, SSM_STATE)
    dt = jax.nn.softplus(dt.astype(F32).reshape(b, n, 2, SSM_HEADS) + p['ssm_dt_bias'])
    return z, xs, bm, cm, dt


def ssd_direction(xs, dt, a, bm, cm, init, reverse):
    if reverse:
        xs, dt, bm, cm = (jnp.flip(t, 1) for t in (xs, dt, bm, cm))
    b, n = dt.shape[:2]
    hg = SSM_HEADS // SSM_GROUPS
    X = (xs * dt[..., None]).reshape(b, n, SSM_GROUPS, hg, SSM_HEAD_DIM)
    dA = (dt * a).reshape(b, n, SSM_GROUPS, hg)
    y, final = ssd_chunked(X, dA, bm, cm, init)
    y = y.reshape(b, n, SSM_HEADS, SSM_HEAD_DIM)
    if reverse:
        y = jnp.flip(y, 1)
    return y, final


def mamba_out(y, xs, z, p):
    b, n = y.shape[:2]
    y = (y + xs * p['ssm_d'][:, None]).reshape(b, n, SSM_INNER) * jax.nn.silu(z.astype(F32))
    yg = y.reshape(b, n, SSM_GROUPS, SSM_INNER // SSM_GROUPS)
    yg = yg * lax.rsqrt(jnp.mean(yg * yg, -1, keepdims=True) + 1e-5)
    return (yg.reshape(b, n, SSM_INNER) * p['ssm_norm']).astype(z.dtype)


def mamba_branch(uc, ul, p, need_ctx):
    a = -jnp.exp(p['ssm_a_log'].astype(F32))
    zc, xc, bc, cc, dtc = mamba_prep(uc, p)
    zl, xl, bl, cl, dtl = mamba_prep(ul, p)
    zero = jnp.zeros((ul.shape[0], SSM_GROUPS, SSM_HEADS // SSM_GROUPS, SSM_HEAD_DIM, SSM_STATE), F32)
    yc = yl = 0.0
    for d, rev in enumerate((False, True)):
        yc_d, sc = ssd_direction(xc, dtc[:, :, d], a[d], bc, cc, zero, rev)
        yl_d, _ = ssd_direction(xl, dtl[:, :, d], a[d], bl, cl, sc, rev)
        yc = yc + yc_d
        yl = yl + yl_d
    return (mamba_out(yc, xc, zc, p) if need_ctx else None), mamba_out(yl, xl, zl, p)


def hyena_filters(n, p):
    t = jnp.linspace(0.0, 1.0, n, dtype=F32)[:, None]
    wpos = 2.0 * math.pi * jnp.arange(n, dtype=F32)[:, None] / n
    f = jnp.linspace(1e-4, HY_BANDS - 1, HY_BANDS, dtype=F32)[None]
    z = jnp.concatenate([t, jnp.cos(f * wpos), -jnp.sin(f * wpos)], -1)
    freq = p['hy_freq']
    hdn = jnp.sin(freq * (z @ p['hy_w1'] + p['hy_b1']))
    hdn = jnp.sin(freq * (hdn @ p['hy_w2'] + p['hy_b2']))
    filt = (hdn @ p['hy_w3']).astype(F32).reshape(n, 2, HY_WIDTH)
    deltas = jnp.abs(jnp.linspace(HY_MIN_DECAY, HY_MAX_DECAY, HY_WIDTH, dtype=F32))
    return filt * jnp.exp(-t * deltas)[:, None]


def bidir_long_conv(v, filt):
    n, ch = v.shape[1], v.shape[2]
    k_full = jnp.concatenate([filt[:, 0], jnp.zeros((1, ch), F32), filt[:0:-1, 1]], 0)
    kf = jnp.fft.rfft(k_full, axis=0)
    vf = jnp.fft.rfft(v, n=2 * n, axis=1)
    return jnp.fft.irfft(vf * kf, n=2 * n, axis=1)[:, :n]


def hyena_seq(u, p):
    n = u.shape[1]
    uc = dwconv3(u, p['hy_conv_w'], p['hy_conv_b']).astype(F32)
    x0, x1, v = jnp.split(uc, 3, -1)
    v = v * x1
    y = bidir_long_conv(v, hyena_filters(n, p)) + v * p['hy_d']
    return (y * x0).astype(u.dtype)


def hyena_branch(uc, ul, p, need_ctx):
    return (hyena_seq(uc, p) if need_ctx else None), hyena_seq(ul, p)


def rwkv_prep(u, p):
    b, n, _ = u.shape
    heads = lambda t: t.reshape(b, n, RW_HEADS, RW_HEAD_DIM)
    us = token_shift(u, p['rw_mu']).astype(F32)
    cuts = np.cumsum([RW_WIDTH, RW_WIDTH, RW_WIDTH, 2 * RW_DECAY_LORA, 2 * RW_ICLR_LORA]).tolist()
    r, k, v, wd, ad, gd = jnp.split(us, cuts, -1)
    g = jax.nn.sigmoid(gd) @ p['rw_g_up']
    kk = heads(k * p['rw_kk'])
    kk = (kk / jnp.maximum(jnp.sqrt(jnp.sum(kk * kk, -1, keepdims=True)), 1e-12)).reshape(b, n, RW_WIDTH)
    wd = wd.reshape(b, n, 2, RW_DECAY_LORA)
    ad = ad.reshape(b, n, 2, RW_ICLR_LORA)
    dirs = []
    for d in range(2):
        w_log = -jax.nn.softplus(-(p['rw_w0'][d] + jnp.tanh(wd[:, :, d]) @ p['rw_w_up'][d])) - 0.5
        iclr = jax.nn.sigmoid(p['rw_a0'][d] + ad[:, :, d] @ p['rw_a_up'][d])
        k_d = k * (1.0 + (iclr - 1.0) * p['rw_ka'])
        dirs.append((-jnp.exp(w_log), k_d, kk * iclr))
    return r, v, kk, g, dirs


def rwkv_out(y, r, v, g, dirs, p, dtype):
    b, n, _ = y.shape
    heads = lambda t: t.reshape(b, n, RW_HEADS, RW_HEAD_DIM)
    y, r, v = heads(y), heads(r), heads(v)
    mu = jnp.mean(y, -1, keepdims=True)
    var = jnp.mean(jnp.square(y - mu), -1, keepdims=True)
    yn = ((y - mu) * lax.rsqrt(var + RW_GN_EPS)).reshape(b, n, RW_WIDTH) * p['rw_ln_g'] + p['rw_ln_b']
    bonus = sum(jnp.sum(r * heads(k_d) * p['rw_rk'], -1, keepdims=True) * v for _, k_d, _ in dirs)
    return ((yn + bonus.reshape(b, n, RW_WIDTH)) * g).astype(dtype)


def rwkv_branch(uc, ul, p, need_ctx):
    rc, vc, kkc, gc, dc = rwkv_prep(uc, p)
    rl, vl, kkl, gl, dl = rwkv_prep(ul, p)
    s0 = jnp.zeros((2, ul.shape[0], RW_WIDTH // RW_P, RW_P, RW_P), F32)
    yfc, ybc, sc = rwkv_scan_pallas(rc, vc, kkc, dc[0][0], dc[1][0], dc[0][1], dc[1][1], dc[0][2], dc[1][2], s0)
    yfl, ybl, _ = rwkv_scan_pallas(rl, vl, kkl, dl[0][0], dl[1][0], dl[0][1], dl[1][1], dl[0][2], dl[1][2], sc)
    out_c = rwkv_out(yfc + ybc, rc, vc, gc, dc, p, uc.dtype) if need_ctx else None
    return out_c, rwkv_out(yfl + ybl, rl, vl, gl, dl, p, ul.dtype)


def merge_branches(branches, gate_logits, p):
    b, n, _ = gate_logits.shape
    gates = jax.nn.sigmoid(gate_logits.reshape(b, n, N_BRANCH, D_MODEL))
    merged = sum(gates[:, :, i] * (y @ p['w_branch'][i]) for i, y in enumerate(branches))
    return merged @ p['w_out']


def hybrid_mixer(hc, hl, p, rope, need_ctx):
    cuts = np.cumsum([MLA_COLS, SSM_COLS, HY_COLS, RW_COLS]).tolist()
    pc = jnp.split(hc @ p['w_in'], cuts, -1)
    pl_ = jnp.split(hl @ p['w_in'], cuts, -1)
    branches = (mla_branch(pc[0], pl_[0], rope, p, need_ctx),
                mamba_branch(pc[1], pl_[1], p, need_ctx),
                hyena_branch(pc[2], pl_[2], p, need_ctx),
                rwkv_branch(pc[3], pl_[3], p, need_ctx))
    yl = merge_branches([br[1] for br in branches], pl_[4], p)
    yc = merge_branches([br[0] for br in branches], pc[4], p) if need_ctx else None
    return yc, yl


def expert_choice_ffn(h, p):
    b, n, _ = h.shape
    cap = EC_CAPACITY * n // N_EXPERTS
    aff = jax.nn.softmax((h @ p['w_router']).astype(F32), -1)
    gate, idx = lax.top_k(jnp.swapaxes(aff, 1, 2), cap)
    bidx = jnp.arange(b)[:, None, None]
    xs = h[bidx, idx]
    hid = jax.nn.silu(jnp.einsum('becd,edf->becf', xs, p['w_gate_e'])) * jnp.einsum('becd,edf->becf', xs, p['w_up_e'])
    ye = jnp.einsum('becf,efd->becd', hid, p['w_down_e']) * gate[..., None].astype(h.dtype)
    return jnp.zeros_like(h).at[bidx, idx].add(ye.astype(h.dtype))


def trunk_layer(xc, xl, mod_c, mod_l, p, rope, need_ctx):
    sh1c, sc1c, g1c, sh2c, sc2c, g2c = jnp.split(mod_c, 6, -1)
    sh1, sc1, g1, sh2, sc2, g2 = (m[:, None] for m in jnp.split(mod_l, 6, -1))
    yc, yl = hybrid_mixer(modulate(xc, sh1c, sc1c), modulate(xl, sh1, sc1), p, rope, need_ctx)
    xl = post_norm(xl, g1 * yl, p['ln1_g'], p['ln1_b'])
    xl = post_norm(xl, g2 * expert_choice_ffn(modulate(xl, sh2, sc2), p), p['ln2_g'], p['ln2_b'])
    if need_ctx:
        xc = post_norm(xc, g1c * yc, p['ln1_g'], p['ln1_b'])
        xc = post_norm(xc, g2c * expert_choice_ffn(modulate(xc, sh2c, sc2c), p), p['ln2_g'], p['ln2_b'])
    return xc, xl


def kernel(x, c, ctx, c_ctx, w_ada, b_ada, w_in, mla_q_norm, mla_w_q_up, mla_kv_norm, mla_w_kv_up,
           ssm_conv_w, ssm_conv_b, ssm_dt_bias, ssm_a_log, ssm_d, ssm_norm,
           hy_conv_w, hy_conv_b, hy_w1, hy_b1, hy_w2, hy_b2, hy_w3, hy_freq, hy_d,
           rw_mu, rw_w0, rw_w_up, rw_a0, rw_a_up, rw_g_up, rw_kk, rw_ka, rw_rk, rw_ln_g, rw_ln_b,
           w_branch, w_out, ln1_g, ln1_b, w_router, w_gate_e, w_up_e, w_down_e, ln2_g, ln2_b):
    stacked = dict(
        w_in=w_in, mla_q_norm=mla_q_norm, mla_w_q_up=mla_w_q_up, mla_kv_norm=mla_kv_norm, mla_w_kv_up=mla_w_kv_up,
        ssm_conv_w=ssm_conv_w, ssm_conv_b=ssm_conv_b, ssm_dt_bias=ssm_dt_bias, ssm_a_log=ssm_a_log,
        ssm_d=ssm_d, ssm_norm=ssm_norm,
        hy_conv_w=hy_conv_w, hy_conv_b=hy_conv_b, hy_w1=hy_w1, hy_b1=hy_b1, hy_w2=hy_w2, hy_b2=hy_b2,
        hy_w3=hy_w3, hy_freq=hy_freq, hy_d=hy_d,
        rw_mu=rw_mu, rw_w0=rw_w0, rw_w_up=rw_w_up, rw_a0=rw_a0, rw_a_up=rw_a_up, rw_g_up=rw_g_up,
        rw_kk=rw_kk, rw_ka=rw_ka, rw_rk=rw_rk, rw_ln_g=rw_ln_g, rw_ln_b=rw_ln_b,
        w_branch=w_branch, w_out=w_out, ln1_g=ln1_g, ln1_b=ln1_b,
        w_router=w_router, w_gate_e=w_gate_e, w_up_e=w_up_e, w_down_e=w_down_e, ln2_g=ln2_g, ln2_b=ln2_b)
    rope = axial_rope_tables(x.shape[1])
    xc, xl = ctx, x
    for i in range(DEPTH):
        p = {name: arr[i] for name, arr in stacked.items()}
        mod_l = jax.nn.silu(c) @ w_ada[i] + b_ada[i]
        mod_c = jax.nn.silu(c_ctx) @ w_ada[i] + b_ada[i]
        xc, xl = trunk_layer(xc, xl, mod_c, mod_l, p, rope, i < DEPTH - 1)
    return xl
```

```python
import functools
import math

import jax
import jax.numpy as jnp
from jax import lax
import numpy as np
from jax.experimental import pallas as pl
from jax.experimental.pallas import tpu as pltpu

D_MODEL = 2048
DEPTH = 2

GRID_W = 64
N_BRANCH = 4
BRANCH_WIDTH = D_MODEL // 2

MLA_Q_LORA = D_MODEL // 4
MLA_KV_LORA = D_MODEL // 4
QK_NOPE = 128
QK_ROPE = 64
V_HEAD = 128
MLA_HEADS = BRANCH_WIDTH // V_HEAD
ROPE_BASE = 10000.0

SSM_HEAD_DIM = 64
SSM_HEADS = BRANCH_WIDTH // SSM_HEAD_DIM
SSM_INNER = SSM_HEADS * SSM_HEAD_DIM
SSM_GROUPS = 2
SSM_STATE = 128
SSM_CHUNK = 128
SSM_XBC = SSM_INNER + 2 * SSM_GROUPS * SSM_STATE

HY_WIDTH = BRANCH_WIDTH
HY_EMB = 33
HY_BANDS = (HY_EMB - 1) // 2
HY_FILTER_HIDDEN = 64
HY_TARGET = 1e-2
HY_SLOW_FRAC = 1.5
HY_QUICK_FRAC = 0.3
HY_MIN_DECAY = math.log(HY_TARGET) / HY_SLOW_FRAC
HY_MAX_DECAY = math.log(HY_TARGET) / HY_QUICK_FRAC

RW_HEAD_DIM = 64
RW_WIDTH = BRANCH_WIDTH
RW_HEADS = RW_WIDTH // RW_HEAD_DIM
RW_DECAY_LORA = 64
RW_ICLR_LORA = 64
RW_GATE_LORA = 160
RW_GN_EPS = 64e-5

N_EXPERTS = 16
EXPERT_FF = 1408
EC_CAPACITY = 2

ALPHA = (2 * DEPTH) ** 0.25

MLA_COLS = MLA_Q_LORA + MLA_KV_LORA + QK_ROPE
SSM_COLS = SSM_INNER + SSM_XBC + 2 * SSM_HEADS
HY_COLS = 3 * HY_WIDTH
RW_COLS = 3 * RW_WIDTH + 2 * RW_DECAY_LORA + 2 * RW_ICLR_LORA + RW_GATE_LORA

F32 = jnp.float32
BF16 = jnp.bfloat16
HI = lax.Precision.HIGHEST

RW_T = 64
RW_P = 128
RW_PAIRS_PER_STEP = 2
RW_INV_BASE = 8
ATT_DK = 256
ATT_DV = 128
TM = 512
VMEM_LIMIT = 48 * 1024 * 1024


def _mm(a, b):
    return jnp.dot(a.astype(BF16), b.astype(BF16), preferred_element_type=F32)


def _mm_nt(a, b):
    return lax.dot_general(a.astype(BF16), b.astype(BF16), (((1,), (1,)), ((), ())), preferred_element_type=F32)


def _mm_tn(a, b):
    return lax.dot_general(a.astype(BF16), b.astype(BF16), (((0,), (0,)), ((), ())), preferred_element_type=F32)


def _ln(x, eps=1e-6):
    mu = jnp.mean(x, -1, keepdims=True)
    xc = x - mu
    return xc * lax.rsqrt(jnp.mean(xc * xc, -1, keepdims=True) + eps)


def _rwkv_pair_chunks(chains, inv_masks):
    T = RW_T
    each = lambda f, *ls: [f(*xs) for xs in zip(*ls)]
    S = [ch[0] for ch in chains]
    r, lw, k, v, kk, b = ([ch[1][i] for ch in chains] for i in range(6))
    a = [-x for x in kk]
    rev = [ch[2] for ch in chains]
    cum, head_lo, strict_bd, incl_bd, eye_bd = ([ch[3][i] for ch in chains] for i in range(5))

    def stack(x, lo):
        return jnp.concatenate([jnp.where(lo, x, 0.0), jnp.where(lo, 0.0, x)], axis=0)

    c = each(lambda m, x: jnp.dot(m, x, precision=HI, preferred_element_type=F32), cum, lw)
    cT = each(lambda x, rv: x[0:1, :] if rv else x[T - 1:T, :], c, rev)
    e_neg = each(lambda x: jnp.exp(-x), c)
    e_end = each(lambda x, y: jnp.exp(x - y), cT, c)
    a_s = each(lambda x, cc, l, lo: stack(x * jnp.exp(cc - l), lo), a, c, lw, head_lo)
    r_s = each(lambda x, cc, lo: stack(x * jnp.exp(cc), lo), r, c, head_lo)
    b_s = each(lambda x, e, lo: stack(x * e, lo), b, e_neg, head_lo)
    k_s = each(lambda x, e, lo: stack(x * e, lo), k, e_neg, head_lo)
    bh_s = each(lambda x, e, lo: stack(x * e, lo), b, e_end, head_lo)
    kh_s = each(lambda x, e, lo: stack(x * e, lo), k, e_end, head_lo)
    v_s = each(stack, v, head_lo)
    g_T = each(jnp.exp, cT)

    ar = each(lambda x, y: jnp.concatenate([x, y], axis=0), a_s, r_s)
    gb = each(_mm_nt, ar, b_s)
    gk = each(_mm_nt, ar, k_s)
    n = each(lambda m, g: jnp.where(m, g[:2 * T], 0.0), strict_bd, gb)
    l_ak = each(lambda m, g: jnp.where(m, g[:2 * T], 0.0), strict_bd, gk)
    m_rb = each(lambda m, g: jnp.where(m, g[2 * T:], 0.0), incl_bd, gb)
    m_rk = each(lambda m, g: jnp.where(m, g[2 * T:], 0.0), incl_bd, gk)
    lv = each(_mm, l_ak, v_s)
    n8 = each(lambda x: jnp.where(inv_masks[0], x, 0.0), n)
    p = each(lambda e, x: e + x, eye_bd, n8)
    m = each(_mm, n8, n8)
    p = each(lambda x, y: x + _mm(y, x), p, m)
    m = each(_mm, m, m)
    p = each(lambda x, y: x + _mm(y, x), p, m)
    for merge in inv_masks[1:]:
        n_off = each(lambda x: jnp.where(merge, x, 0.0), n)
        p = each(lambda x, y: x + _mm(_mm(x, y), x), p, n_off)
    w = each(_mm, p, a_s)
    u0 = each(_mm, p, lv)
    wr = each(lambda x, y, s: _mm_nt(jnp.concatenate([x, y], axis=0), s), w, r_s, S)
    uv = each(lambda x, y, z: jnp.concatenate([x[:2 * T] + y, z], axis=0), wr, u0, v_s)
    y_s = each(lambda x, p1, p2, z: x[2 * T:] + _mm(jnp.concatenate([p1, p2], axis=1), z), wr, m_rb, m_rk, uv)
    y = each(lambda x: x[:T] + x[T:], y_s)
    S_new = each(lambda s, g, z, p1, p2: s * g + _mm_tn(z, jnp.concatenate([p1, p2], axis=0)),
                 S, g_T, uv, bh_s, kh_s)
    return list(zip(y, S_new))


def _rwkv_consts():
    T = RW_T
    row = lax.broadcasted_iota(jnp.int32, (T, T), 0)
    col = lax.broadcasted_iota(jnp.int32, (T, T), 1)
    cum_f = (row >= col).astype(F32)
    cum_b = (row <= col).astype(F32)
    head_lo = lax.broadcasted_iota(jnp.int32, (T, RW_P), 1) < RW_P // 2
    r2 = lax.broadcasted_iota(jnp.int32, (2 * T, 2 * T), 0)
    c2 = lax.broadcasted_iota(jnp.int32, (2 * T, 2 * T), 1)
    same = (r2 >= T) == (c2 >= T)
    eye_bd = (r2 == c2).astype(F32)
    fwd = (cum_f, head_lo, same & (r2 > c2), same & (r2 >= c2), eye_bd)
    bwd = (cum_b, head_lo, same & (r2 < c2), same & (r2 <= c2), eye_bd)
    blk = lambda s: (r2 // s) == (c2 // s)
    inv_masks = [blk(RW_INV_BASE)]
    s = RW_INV_BASE
    while s < T:
        inv_masks.append(blk(2 * s) & jnp.logical_not(blk(s)))
        s *= 2
    return fwd, bwd, inv_masks


def _rwkv_kernel(rf, vf, af, lwf, kf, bf, rb, vb, ab, lwb, kb, bb, s0_ref, yf_ref, yb_ref, sT_ref, s_scr, *, nsub):
    j = pl.program_id(2)

    @pl.when(j == 0)
    def _():
        s_scr[...] = s0_ref[:, 0]

    T = RW_T
    cf, cb, inv_masks = _rwkv_consts()

    def body(i, carry):
        sf = pl.ds(pl.multiple_of(i * T, T), T)
        sb = pl.ds(pl.multiple_of((nsub - 1 - i) * T, T), T)
        lanes = [slice(q * RW_P, (q + 1) * RW_P) for q in range(RW_PAIRS_PER_STEP)]
        ins = []
        for ln in lanes:
            ins.append(tuple(ref[0, sf, ln].astype(F32) for ref in (rf, lwf, kf, vf, af, bf)))
            ins.append(tuple(ref[0, sb, ln].astype(F32) for ref in (rb, lwb, kb, vb, ab, bb)))
        res = _rwkv_pair_chunks([(carry[c], ins[c], bool(c % 2), cb if c % 2 else cf)
                                 for c in range(2 * RW_PAIRS_PER_STEP)], inv_masks)
        for q, ln in enumerate(lanes):
            yf_ref[0, sf, ln] = res[2 * q][0]
            yb_ref[0, sb, ln] = res[2 * q + 1][0]
        return tuple(s for _, s in res)

    init = tuple(s_scr[d, q] for q in range(RW_PAIRS_PER_STEP) for d in range(2))
    fin = lax.fori_loop(0, nsub, body, init)
    for q in range(RW_PAIRS_PER_STEP):
        for d in range(2):
            s_scr[d, q] = fin[2 * q + d]

    @pl.when(j == pl.num_programs(2) - 1)
    def _():
        for q in range(RW_PAIRS_PER_STEP):
            for d in range(2):
                sT_ref[d, 0, q] = fin[2 * q + d]


def rwkv_scan_pallas(r, v, kk, lw0, lw1, kd0, kd1, bd0, bd1, s0, *, tb=256):
    B, n, C = r.shape
    tb = min(tb, n)
    nb = n // tb
    pp = RW_PAIRS_PER_STEP
    assert n % tb == 0 and tb % RW_T == 0 and C % (pp * RW_P) == 0
    fw = pl.BlockSpec((1, tb, pp * RW_P), lambda b, p, j: (b, j, p))
    bw = pl.BlockSpec((1, tb, pp * RW_P), lambda b, p, j: (b, nb - 1 - j, p))
    st = pl.BlockSpec((2, 1, pp, RW_P, RW_P), lambda b, p, j: (0, b, p, 0, 0))
    return pl.pallas_call(
        functools.partial(_rwkv_kernel, nsub=tb // RW_T),
        grid=(B, C // (pp * RW_P), nb),
        in_specs=[fw] * 6 + [bw] * 6 + [st],
        out_specs=[fw, bw, st],
        out_shape=[jax.ShapeDtypeStruct((B, n, C), F32), jax.ShapeDtypeStruct((B, n, C), F32),
                   jax.ShapeDtypeStruct(s0.shape, F32)],
        scratch_shapes=[pltpu.VMEM((2, pp, RW_P, RW_P), F32)],
        compiler_params=pltpu.CompilerParams(dimension_semantics=("parallel", "parallel", "arbitrary")),
        name="rwkv_scan",
    )(r, v, kk, lw0, kd0, bd0, r, v, kk, lw1, kd1, bd1, s0)


def _flash_kernel(q_ref, k_ref, v_ref, o_ref, m_scr, l_scr, acc_scr):
    j = pl.program_id(3)

    @pl.when(j == 0)
    def _():
        m_scr[...] = jnp.full(m_scr.shape, -jnp.inf, F32)
        l_scr[...] = jnp.zeros(l_scr.shape, F32)
        acc_scr[...] = jnp.zeros(acc_scr.shape, F32)

    s = lax.dot_general(q_ref[0], k_ref[0], (((1,), (1,)), ((), ())), preferred_element_type=F32)
    m_prev = m_scr[...]
    m_new = jnp.maximum(m_prev, jnp.max(s, axis=-1, keepdims=True))
    alpha = jnp.exp(m_prev - m_new)
    p = jnp.exp(s - m_new)
    l_scr[...] = alpha * l_scr[...] + jnp.sum(p, axis=-1, keepdims=True)
    acc_scr[...] = alpha * acc_scr[...] + jnp.dot(p.astype(BF16), v_ref[0], preferred_element_type=F32)
    m_scr[...] = m_new

    @pl.when(j == pl.num_programs(3) - 1)
    def _():
        o_ref[0] = (acc_scr[...] / l_scr[...]).astype(o_ref.dtype)


def flash_attention(q, k, v, *, tq, tk, out_dtype=F32):
    B, nq, hq = q.shape
    nk = k.shape[1]
    H = hq // ATT_DK
    assert nq % tq == 0 and nk % tk == 0
    return pl.pallas_call(
        _flash_kernel,
        grid=(B, H, nq // tq, nk // tk),
        in_specs=[pl.BlockSpec((1, tq, ATT_DK), lambda b, h, i, j: (b, i, h)),
                  pl.BlockSpec((1, tk, ATT_DK), lambda b, h, i, j: (b, j, h)),
                  pl.BlockSpec((1, tk, ATT_DV), lambda b, h, i, j: (b, j, h))],
        out_specs=pl.BlockSpec((1, tq, ATT_DV), lambda b, h, i, j: (b, i, h)),
        out_shape=jax.ShapeDtypeStruct((B, nq, H * ATT_DV), out_dtype),
        scratch_shapes=[pltpu.VMEM((tq, 1), F32), pltpu.VMEM((tq, 1), F32), pltpu.VMEM((tq, ATT_DV), F32)],
        compiler_params=pltpu.CompilerParams(
            dimension_semantics=("parallel", "parallel", "parallel", "arbitrary")),
        name="mla_flash",
    )(q, k, v)


PROJ = {}
_off = 0
for _name, _w in (('gate', 4 * D_MODEL), ('hy_x0', HY_WIDTH), ('hy_x1', HY_WIDTH), ('hy_v', HY_WIDTH),
                  ('rw_r', RW_WIDTH), ('rw_k', RW_WIDTH), ('rw_v', RW_WIDTH), ('ssm_z', SSM_INNER),
                  ('ssm_x', SSM_INNER), ('mla_cq', MLA_Q_LORA), ('mla_ckv', MLA_KV_LORA),
                  ('ssm_b', 256), ('ssm_c', 256), ('rw_gd', 256), ('rw_wd', 128), ('rw_ad', 128),
                  ('mla_kpe', 128), ('mla_kpe_sw', 128), ('ssm_dt', 128)):
    assert _off % _w == 0, (_name, _off, _w)
    PROJ[_name] = (_off, _w)
    _off += _w
PROJ_TN = 512
PROJ_N = -(-_off // PROJ_TN) * PROJ_TN


def _rope_swap_matrix():
    p = np.zeros((QK_ROPE, QK_ROPE), np.float32)
    q = QK_ROPE // 4
    for g in range(2):
        for i in range(q):
            p[g * 2 * q + q + i, g * 2 * q + i] = -1.0
            p[g * 2 * q + i, g * 2 * q + q + i] = 1.0
    return jnp.asarray(p)


def build_w_in(w_in):
    cuts = np.cumsum([0, MLA_COLS, SSM_COLS, HY_COLS, RW_COLS, 4 * D_MODEL]).tolist()
    mla, ssm, hy, rw, gate = (w_in[:, cuts[i]:cuts[i + 1]] for i in range(5))
    kpe = mla[:, MLA_Q_LORA + MLA_KV_LORA:]
    src = {
        'gate': gate, 'hy_x0': hy[:, :HY_WIDTH], 'hy_x1': hy[:, HY_WIDTH:2 * HY_WIDTH], 'hy_v': hy[:, 2 * HY_WIDTH:],
        'rw_r': rw[:, :RW_WIDTH], 'rw_k': rw[:, RW_WIDTH:2 * RW_WIDTH], 'rw_v': rw[:, 2 * RW_WIDTH:3 * RW_WIDTH],
        'rw_wd': rw[:, 3 * RW_WIDTH:3 * RW_WIDTH + 128], 'rw_ad': rw[:, 3 * RW_WIDTH + 128:3 * RW_WIDTH + 256],
        'rw_gd': rw[:, 3 * RW_WIDTH + 256:],
        'ssm_z': ssm[:, :SSM_INNER], 'ssm_x': ssm[:, SSM_INNER:2 * SSM_INNER],
        'ssm_b': ssm[:, 2 * SSM_INNER:2 * SSM_INNER + 256], 'ssm_c': ssm[:, 2 * SSM_INNER + 256:2 * SSM_INNER + 512],
        'ssm_dt': ssm[:, 2 * SSM_INNER + 512:],
        'mla_cq': mla[:, :MLA_Q_LORA], 'mla_ckv': mla[:, MLA_Q_LORA:MLA_Q_LORA + MLA_KV_LORA],
        'mla_kpe': kpe, 'mla_kpe_sw': kpe @ _rope_swap_matrix(),
    }
    cols, pos = [], 0
    for name, (off, w) in PROJ.items():
        assert off == pos
        piece = src[name]
        cols.append(jnp.pad(piece, ((0, 0), (0, w - piece.shape[1]))))
        pos += w
    cols.append(jnp.zeros((w_in.shape[0], PROJ_N - pos), w_in.dtype))
    return jnp.concatenate(cols, 1).astype(BF16)


def _inproj_kernel(x_ref, sh_ref, sc_ref, w_ref, o_ref, h_scr):
    @pl.when(pl.program_id(1) == 0)
    def _():
        h_scr[...] = (_ln(x_ref[...]) * (1.0 + sc_ref[0]) + sh_ref[0]).astype(BF16)

    o_ref[...] = jnp.dot(h_scr[...], w_ref[...], preferred_element_type=F32).astype(o_ref.dtype)


def inproj(x2d, shift, scale, w_p, rows_per_mod):
    R, D = x2d.shape
    tm = min(TM, R)
    bpm = max(rows_per_mod // tm, 1)
    mod = pl.BlockSpec((1, 1, D), lambda i, j: (i // bpm, 0, 0))
    return pl.pallas_call(
        _inproj_kernel,
        grid=(R // tm, PROJ_N // PROJ_TN),
        in_specs=[pl.BlockSpec((tm, D), lambda i, j: (i, 0)), mod, mod,
                  pl.BlockSpec((D, PROJ_TN), lambda i, j: (0, j))],
        out_specs=pl.BlockSpec((tm, PROJ_TN), lambda i, j: (i, j)),
        out_shape=jax.ShapeDtypeStruct((R, PROJ_N), BF16),
        scratch_shapes=[pltpu.VMEM((tm, D), BF16)],
        compiler_params=pltpu.CompilerParams(dimension_semantics=("parallel", "arbitrary"),
                                             vmem_limit_bytes=VMEM_LIMIT),
        name="inproj",
    )(x2d, shift, scale, w_p)


def build_mla_weights(p):
    wq = p['mla_w_q_up'].reshape(MLA_Q_LORA, MLA_HEADS, QK_NOPE + QK_ROPE)
    wkv = p['mla_w_kv_up'].reshape(MLA_KV_LORA, MLA_HEADS, QK_NOPE + V_HEAD)
    wq_p = jnp.pad(wq, ((0, 0), (0, 0), (0, ATT_DK - QK_NOPE - QK_ROPE))).reshape(MLA_Q_LORA, MLA_HEADS * ATT_DK)
    wsw = jnp.einsum('chd,de->che', wq[:, :, QK_NOPE:], _rope_swap_matrix())
    wsw_p = jnp.pad(wsw, ((0, 0), (0, 0), (0, 128 - QK_ROPE))).reshape(MLA_Q_LORA, MLA_HEADS * 128)
    wkn = wkv[:, :, :QK_NOPE].reshape(MLA_KV_LORA, MLA_HEADS * QK_NOPE)
    wv = wkv[:, :, QK_NOPE:].reshape(MLA_KV_LORA, MLA_HEADS * V_HEAD)
    return wq_p.astype(BF16), wsw_p.astype(BF16), wkn.astype(BF16), wv.astype(BF16)


def axial_rope_tables(n_tokens):
    rows = n_tokens // GRID_W
    row = jnp.repeat(jnp.arange(rows), GRID_W).astype(F32)
    col = jnp.tile(jnp.arange(GRID_W), rows).astype(F32)
    half = QK_ROPE // 2
    inv = ROPE_BASE ** (-jnp.arange(0, half, 2, dtype=F32) / half)
    ang = jnp.stack([row[:, None] * inv, col[:, None] * inv], 1)
    return jnp.cos(ang), jnp.sin(ang)


def rope_tables_128(n_tokens):
    cos, sin = axial_rope_tables(n_tokens)
    full = lambda t: jnp.concatenate([t[:, 0], t[:, 0], t[:, 1], t[:, 1]], -1)
    pad = lambda t: jnp.pad(t, ((0, 0), (0, 128 - QK_ROPE)))
    return pad(full(cos)), pad(full(sin))


def _mla_proj_kernel(cq_ref, ckv_ref, kpe_ref, ksw_ref, cos_ref, sin_ref, gq_ref, gkv_ref, wq_ref, wsw_ref, wkn_ref,
                     wv_ref, q_ref, k_ref, v_ref, *, q_scale):
    def rms(x, g):
        return (x * lax.rsqrt(jnp.mean(x * x, -1, keepdims=True) + 1e-6) * g).astype(BF16)

    cqn = rms(cq_ref[...].astype(F32), gq_ref[...])
    ckvn = rms(ckv_ref[...].astype(F32), gkv_ref[...])
    cos = cos_ref[...]
    sin = sin_ref[...]
    q = jnp.dot(cqn, wq_ref[...], preferred_element_type=F32)
    qsw = jnp.dot(cqn, wsw_ref[...], preferred_element_type=F32)
    kn = jnp.dot(ckvn, wkn_ref[...], preferred_element_type=F32)
    v_ref[...] = jnp.dot(ckvn, wv_ref[...], preferred_element_type=F32).astype(v_ref.dtype)
    kpe = (kpe_ref[...].astype(F32) * cos + ksw_ref[...].astype(F32) * sin).astype(k_ref.dtype)
    for h in range(MLA_HEADS):
        lo = h * ATT_DK
        q_ref[:, lo:lo + 128] = (q[:, lo:lo + 128] * q_scale).astype(q_ref.dtype)
        pe = q[:, lo + 128:lo + 256] * cos + qsw[:, h * 128:(h + 1) * 128] * sin
        q_ref[:, lo + 128:lo + 256] = (pe * q_scale).astype(q_ref.dtype)
        k_ref[:, lo:lo + 128] = kn[:, h * 128:(h + 1) * 128].astype(k_ref.dtype)
        k_ref[:, lo + 128:lo + 256] = kpe


def mla_proj(proj, cos, sin, gq, gkv, weights):
    R = proj.shape[0]
    tm = min(TM, R)
    wq, wsw, wkn, wv = weights
    col = lambda name: pl.BlockSpec((tm, PROJ[name][1]), lambda i, o=PROJ[name][0] // PROJ[name][1]: (i, o))
    row = lambda w: pl.BlockSpec((tm, w), lambda i: (i, 0))
    full = lambda a: pl.BlockSpec(a.shape, lambda i: (0,) * a.ndim)
    gq2, gkv2 = gq.reshape(1, -1), gkv.reshape(1, -1)
    return pl.pallas_call(
        functools.partial(_mla_proj_kernel, q_scale=float((QK_NOPE + QK_ROPE) ** -0.5)),
        grid=(R // tm,),
        in_specs=[col('mla_cq'), col('mla_ckv'), col('mla_kpe'), col('mla_kpe_sw'), row(128), row(128),
                  full(gq2), full(gkv2), full(wq), full(wsw), full(wkn), full(wv)],
        out_specs=[row(MLA_HEADS * ATT_DK), row(MLA_HEADS * ATT_DK), row(MLA_HEADS * ATT_DV)],
        out_shape=[jax.ShapeDtypeStruct((R, MLA_HEADS * ATT_DK), BF16),
                   jax.ShapeDtypeStruct((R, MLA_HEADS * ATT_DK), BF16),
                   jax.ShapeDtypeStruct((R, MLA_HEADS * ATT_DV), BF16)],
        compiler_params=pltpu.CompilerParams(dimension_semantics=("parallel",), vmem_limit_bytes=VMEM_LIMIT),
        name="mla_proj",
    )(proj, proj, proj, proj, cos, sin, gq2, gkv2, wq, wsw, wkn, wv)


MERGE_TN = 512


def _merge_kernel(b0, b1, b2, b3, g0, g1, g2, g3, w_ref, o_ref):
    acc = None
    for i, (b, g) in enumerate(((b0, g0), (b1, g1), (b2, g2), (b3, g3))):
        t = jax.nn.sigmoid(g[...].astype(F32)) * jnp.dot(b[...], w_ref[i], preferred_element_type=F32)
        acc = t if acc is None else acc + t
    o_ref[...] = acc.astype(o_ref.dtype)


def merge_branches_pallas(branches, proj, w_branch):
    R = proj.shape[0]
    tm = min(TM, R)
    nj = D_MODEL // MERGE_TN
    g0 = PROJ['gate'][0] // MERGE_TN
    br = pl.BlockSpec((tm, BRANCH_WIDTH), lambda i, j: (i, 0))
    gates = [pl.BlockSpec((tm, MERGE_TN), lambda i, j, k=k: (i, g0 + k * nj + j)) for k in range(N_BRANCH)]
    return pl.pallas_call(
        _merge_kernel,
        grid=(R // tm, nj),
        in_specs=[br] * 4 + gates + [pl.BlockSpec((N_BRANCH, BRANCH_WIDTH, MERGE_TN), lambda i, j: (0, 0, j))],
        out_specs=pl.BlockSpec((tm, MERGE_TN), lambda i, j: (i, j)),
        out_shape=jax.ShapeDtypeStruct((R, D_MODEL), BF16),
        compiler_params=pltpu.CompilerParams(dimension_semantics=("parallel", "arbitrary"),
                                             vmem_limit_bytes=VMEM_LIMIT),
        name="branch_merge",
    )(*branches, proj, proj, proj, proj, w_branch)


ROUTER_PAD = 128


def _mm3_f32(a, b_hi, b_lo):
    ah = a.astype(BF16)
    al = (a - ah.astype(F32)).astype(BF16)
    d = lambda x, y: jnp.dot(x, y, preferred_element_type=F32)
    return d(ah, b_hi) + d(ah, b_lo) + d(al, b_hi)


def _outproj_kernel(m_ref, x_ref, g1_ref, sh2_ref, sc2_ref, lng_ref, lnb_ref, w_ref, wr_hi, wr_lo,
                    x_out, h2_out, logit_out):
    y = jnp.dot(m_ref[...], w_ref[...], preferred_element_type=F32)
    xn = _ln(ALPHA * x_ref[...] + g1_ref[0] * y) * lng_ref[...] + lnb_ref[...]
    x_out[...] = xn
    h2 = _ln(xn) * (1.0 + sc2_ref[0]) + sh2_ref[0]
    h2_out[...] = h2.astype(h2_out.dtype)
    logit_out[...] = _mm3_f32(h2, wr_hi[...], wr_lo[...])


def outproj_postnorm(merged, x2d, g1, sh2, sc2, ln_g, ln_b, w_out, w_router, rows_per_mod):
    R, D = x2d.shape
    tm = min(TM, R)
    bpm = max(rows_per_mod // tm, 1)
    mod = pl.BlockSpec((1, 1, D), lambda i: (i // bpm, 0, 0))
    row = lambda w: pl.BlockSpec((tm, w), lambda i: (i, 0))
    full = lambda a: pl.BlockSpec(a.shape, lambda i: (0,) * a.ndim)
    wr = jnp.pad(w_router, ((0, 0), (0, ROUTER_PAD - w_router.shape[1])))
    wr_hi = wr.astype(BF16)
    wr_lo = (wr - wr_hi.astype(F32)).astype(BF16)
    lg, lb = ln_g.reshape(1, D), ln_b.reshape(1, D)
    return pl.pallas_call(
        _outproj_kernel,
        grid=(R // tm,),
        in_specs=[row(D), row(D), mod, mod, mod, full(lg), full(lb), full(w_out), full(wr_hi), full(wr_lo)],
        out_specs=[row(D), row(D), row(ROUTER_PAD)],
        out_shape=[jax.ShapeDtypeStruct((R, D), F32), jax.ShapeDtypeStruct((R, D), BF16),
                   jax.ShapeDtypeStruct((R, ROUTER_PAD), F32)],
        compiler_params=pltpu.CompilerParams(dimension_semantics=("parallel",), vmem_limit_bytes=VMEM_LIMIT),
        name="outproj_postnorm",
    )(merged, x2d, g1, sh2, sc2, lg, lb, w_out, wr_hi, wr_lo)


def _postnorm_kernel(x_ref, y_ref, g_ref, lng_ref, lnb_ref, o_ref):
    o_ref[...] = _ln(ALPHA * x_ref[...] + g_ref[0] * y_ref[...]) * lng_ref[...] + lnb_ref[...]


def postnorm(x2d, y2d, g, ln_g, ln_b, rows_per_mod):
    R, D = x2d.shape
    tm = min(TM, R)
    bpm = max(rows_per_mod // tm, 1)
    row = pl.BlockSpec((tm, D), lambda i: (i, 0))
    vec = pl.BlockSpec((1, D), lambda i: (0, 0))
    return pl.pallas_call(
        _postnorm_kernel,
        grid=(R // tm,),
        in_specs=[row, row, pl.BlockSpec((1, 1, D), lambda i: (i // bpm, 0, 0)), vec, vec],
        out_specs=row,
        out_shape=jax.ShapeDtypeStruct((R, D), F32),
        compiler_params=pltpu.CompilerParams(dimension_semantics=("parallel",), vmem_limit_bytes=VMEM_LIMIT),
        name="postnorm",
    )(x2d, y2d, g, ln_g.reshape(1, D), ln_b.reshape(1, D))


FF_PAD = 1536
FF_TILE = 512


def _expert_ffn_kernel(x_ref, gate_ref, wg_ref, wu_ref, wd_ref, o_ref):
    f = pl.program_id(2)
    x = x_ref[0, 0]
    hid = (jax.nn.silu(jnp.dot(x, wg_ref[0], preferred_element_type=F32))
           * jnp.dot(x, wu_ref[0], preferred_element_type=F32))
    part = jnp.dot(hid.astype(BF16), wd_ref[0], preferred_element_type=F32)

    @pl.when(f == 0)
    def _():
        o_ref[0, 0] = part

    @pl.when(f > 0)
    def _():
        o_ref[0, 0] += part

    @pl.when(f == pl.num_programs(2) - 1)
    def _():
        o_ref[0, 0] = o_ref[0, 0] * gate_ref[0, 0]


def expert_ffn(xs, gate, wg, wu, wd):
    S, E, C, D = xs.shape
    nf = FF_PAD // FF_TILE
    return pl.pallas_call(
        _expert_ffn_kernel,
        grid=(E, S, nf),
        in_specs=[pl.BlockSpec((1, 1, C, D), lambda e, s, f: (s, e, 0, 0)),
                  pl.BlockSpec((1, 1, C, 1), lambda e, s, f: (s, e, 0, 0)),
                  pl.BlockSpec((1, D, FF_TILE), lambda e, s, f: (e, 0, f)),
                  pl.BlockSpec((1, D, FF_TILE), lambda e, s, f: (e, 0, f)),
                  pl.BlockSpec((1, FF_TILE, D), lambda e, s, f: (e, f, 0))],
        out_specs=pl.BlockSpec((1, 1, C, D), lambda e, s, f: (s, e, 0, 0)),
        out_shape=jax.ShapeDtypeStruct((S, E, C, D), F32),
        compiler_params=pltpu.CompilerParams(dimension_semantics=("parallel", "parallel", "arbitrary"),
                                             vmem_limit_bytes=VMEM_LIMIT),
        name="expert_ffn",
    )(xs, gate, wg, wu, wd)


def build_expert_weights(p):
    padf = lambda w: jnp.pad(w, ((0, 0), (0, 0), (0, FF_PAD - EXPERT_FF))).astype(BF16)
    wd = jnp.pad(p['w_down_e'], ((0, 0), (0, FF_PAD - EXPERT_FF), (0, 0))).astype(BF16)
    return padf(p['w_gate_e']), padf(p['w_up_e']), wd


HALO = 8


def _halo_specs(tm, w, col_block, nrows):
    nb8 = nrows // HALO
    prev = pl.BlockSpec((HALO, w), lambda i: (jnp.maximum(i * (tm // HALO) - 1, 0), col_block))
    nxt = pl.BlockSpec((HALO, w), lambda i: (jnp.minimum((i + 1) * (tm // HALO), nb8 - 1), col_block))
    return prev, nxt


def _neighbours(cur, prev8, next8, seq_len):
    tm = cur.shape[0]
    row = lax.broadcasted_iota(jnp.int32, cur.shape, 0)
    g = row + pl.program_id(0) * tm
    before = jnp.where(row == 0, prev8[HALO - 1:HALO, :], pltpu.roll(cur, 1, 0))
    after = jnp.where(row == tm - 1, next8[0:1, :], pltpu.roll(cur, tm - 1, 0))
    pos = g & (seq_len - 1)
    return jnp.where(pos == 0, 0.0, before), jnp.where(pos == seq_len - 1, 0.0, after)


RW_PIECES = ('rw_r', 'rw_k', 'rw_v', 'rw_wd', 'rw_ad', 'rw_gd')


def _head_sum_matrix(scale=1.0):
    r = np.arange(RW_WIDTH)
    return jnp.asarray(((r[:, None] // RW_HEAD_DIM) == (r[None, :] // RW_HEAD_DIM)).astype(np.float32) * scale, BF16)


def _seg_sum(x, m):
    xh = x.astype(BF16)
    xl = (x - xh.astype(F32)).astype(BF16)
    return jnp.dot(xh, m, preferred_element_type=F32) + jnp.dot(xl, m, preferred_element_type=F32)


def _rwkv_prep_kernel(*refs, seq_len):
    cur = refs[0:6]
    prv = refs[6:12]
    nxt = refs[12:18]
    mu = refs[18:24]
    (w0_ref, a0_ref, wup_ref, aup_ref, gup_ref, kkw_ref, ka_ref, rk_ref, hs_ref) = refs[24:33]
    (r_o, v_o, kk_o, lw0_o, lw1_o, kd0_o, kd1_o, bd0_o, bd1_o, g_o, bonus_o) = refs[33:]
    us = []
    for c, p, n, m in zip(cur, prv, nxt, mu):
        x = c[...].astype(F32)
        before, after = _neighbours(x, p[...].astype(F32), n[...].astype(F32), seq_len)
        us.append(x + m[0:1, :] * (before - x) + m[1:2, :] * (after - x))
    r, k, v, wd, ad, gd = us
    hs = hs_ref[...]
    g_o[...] = jnp.dot(jax.nn.sigmoid(gd).astype(BF16), gup_ref[...], preferred_element_type=F32).astype(g_o.dtype)
    kk = k * kkw_ref[...]
    kk = kk / jnp.maximum(jnp.sqrt(_seg_sum(kk * kk, hs)), 1e-12)
    wl = jnp.dot(jnp.tanh(wd).astype(BF16), wup_ref[...], preferred_element_type=F32) + w0_ref[...]
    al = jnp.dot(ad.astype(BF16), aup_ref[...], preferred_element_type=F32) + a0_ref[...]
    r_o[...] = r.astype(r_o.dtype)
    v_o[...] = v.astype(v_o.dtype)
    kk_o[...] = kk.astype(kk_o.dtype)
    rk = rk_ref[...]
    acc = None
    for d, (lw_o, kd_o, bd_o) in enumerate(((lw0_o, kd0_o, bd0_o), (lw1_o, kd1_o, bd1_o))):
        sl = slice(d * RW_WIDTH, (d + 1) * RW_WIDTH)
        w_log = -jax.nn.softplus(-wl[:, sl]) - 0.5
        lw_o[...] = -jnp.exp(w_log)
        iclr = jax.nn.sigmoid(al[:, sl])
        kd = k * (1.0 + (iclr - 1.0) * ka_ref[...])
        kd_o[...] = kd.astype(kd_o.dtype)
        bd_o[...] = (kk * iclr).astype(bd_o.dtype)
        t = r * kd * rk
        acc = t if acc is None else acc + t
    bonus_o[...] = (_seg_sum(acc, hs) * v).astype(bonus_o.dtype)


def build_rwkv_weights(p):
    W = RW_WIDTH
    mu = p['rw_mu']
    cuts = np.cumsum([0, W, W, W, 128, 128, RW_GATE_LORA]).tolist()
    mus = [mu[:, cuts[i]:cuts[i + 1]] for i in range(6)]
    mus[5] = jnp.pad(mus[5], ((0, 0), (0, 256 - RW_GATE_LORA)))
    z = jnp.zeros((RW_DECAY_LORA, W), F32)
    wup = jnp.concatenate([jnp.concatenate([p['rw_w_up'][0], z], 1), jnp.concatenate([z, p['rw_w_up'][1]], 1)], 0)
    aup = jnp.concatenate([jnp.concatenate([p['rw_a_up'][0], z], 1), jnp.concatenate([z, p['rw_a_up'][1]], 1)], 0)
    gup = jnp.pad(p['rw_g_up'], ((0, 256 - RW_GATE_LORA), (0, 0)))
    return dict(mu=mus, w0=p['rw_w0'].reshape(1, 2 * W), a0=p['rw_a0'].reshape(1, 2 * W), wup=wup.astype(BF16),
                aup=aup.astype(BF16), gup=gup.astype(BF16), kkw=p['rw_kk'].reshape(1, W), ka=p['rw_ka'].reshape(1, W),
                rk=p['rw_rk'].reshape(1, W), hs=_head_sum_matrix(), hmean=_head_sum_matrix(1.0 / RW_HEAD_DIM),
                ln_g=p['rw_ln_g'].reshape(1, W), ln_b=p['rw_ln_b'].reshape(1, W))


def rwkv_prep_pallas(proj, wts, seq_len):
    R = proj.shape[0]
    tm = min(TM, R)
    W = RW_WIDTH
    cur, prv, nxt = [], [], []
    for name in RW_PIECES:
        off, w = PROJ[name]
        cur.append(pl.BlockSpec((tm, w), lambda i, o=off // w: (i, o)))
        a, b = _halo_specs(tm, w, off // w, R)
        prv.append(a)
        nxt.append(b)
    full = lambda a: pl.BlockSpec(a.shape, lambda i: (0,) * a.ndim)
    row = pl.BlockSpec((tm, W), lambda i: (i, 0))
    consts = [wts[k] for k in ('w0', 'a0', 'wup', 'aup', 'gup', 'kkw', 'ka', 'rk', 'hs')]
    bf = jax.ShapeDtypeStruct((R, W), BF16)
    f32 = jax.ShapeDtypeStruct((R, W), F32)
    return pl.pallas_call(
        functools.partial(_rwkv_prep_kernel, seq_len=seq_len),
        grid=(R // tm,),
        in_specs=cur + prv + nxt + [full(m) for m in wts['mu']] + [full(c) for c in consts],
        out_specs=[row] * 11,
        out_shape=[bf, bf, bf, f32, f32, bf, bf, bf, bf, bf, bf],
        compiler_params=pltpu.CompilerParams(dimension_semantics=("parallel",), vmem_limit_bytes=VMEM_LIMIT),
        name="rwkv_prep",
    )(*([proj] * 18), *wts['mu'], *consts)


def _rwkv_out_kernel(yf_ref, yb_ref, bonus_ref, g_ref, hm_ref, lng_ref, lnb_ref, o_ref):
    y = yf_ref[...] + yb_ref[...]
    hm = hm_ref[...]
    yc = y - _seg_sum(y, hm)
    yn = yc * lax.rsqrt(_seg_sum(yc * yc, hm) + RW_GN_EPS) * lng_ref[...] + lnb_ref[...]
    o_ref[...] = ((yn + bonus_ref[...].astype(F32)) * g_ref[...].astype(F32)).astype(o_ref.dtype)


def rwkv_out_pallas(yf, yb, bonus, g, wts):
    R, W = yf.shape
    tm = min(TM, R)
    row = pl.BlockSpec((tm, W), lambda i: (i, 0))
    full = lambda a: pl.BlockSpec(a.shape, lambda i: (0,) * a.ndim)
    return pl.pallas_call(
        _rwkv_out_kernel,
        grid=(R // tm,),
        in_specs=[row] * 4 + [full(wts['hmean']), full(wts['ln_g']), full(wts['ln_b'])],
        out_specs=row,
        out_shape=jax.ShapeDtypeStruct((R, W), BF16),
        compiler_params=pltpu.CompilerParams(dimension_semantics=("parallel",), vmem_limit_bytes=VMEM_LIMIT),
        name="rwkv_out",
    )(yf, yb, bonus, g, wts['hmean'], wts['ln_g'], wts['ln_b'])


def dwconv3(u, w, b):
    up = jnp.pad(u, ((0, 0), (1, 1), (0, 0)))
    return up[:, :-2] * w[0] + up[:, 1:-1] * w[1] + up[:, 2:] * w[2] + b


def segsum(x):
    t = x.shape[-1]
    xr = jnp.broadcast_to(x[..., :, None], x.shape + (t,))
    xr = jnp.where(jnp.tril(jnp.ones((t, t), bool), -1), xr, 0.0)
    cs = jnp.cumsum(xr, axis=-2)
    return jnp.where(jnp.tril(jnp.ones((t, t), bool)), cs, -jnp.inf)


def ssd_chunked(X, dA, Bm, Cm, init):
    b, L, g, j, p_ = X.shape
    c = L // SSM_CHUNK
    X = X.reshape(b, c, SSM_CHUNK, g, j, p_)
    Bm = Bm.reshape(b, c, SSM_CHUNK, g, -1)
    Cm = Cm.reshape(b, c, SSM_CHUNK, g, -1)
    A = jnp.moveaxis(dA.reshape(b, c, SSM_CHUNK, g, j), (1, 2), (3, 4))
    A_cs = jnp.cumsum(A, -1)
    CB = jnp.einsum('bclgn,bcsgn->bgcls', Cm, Bm)
    M = CB[:, :, None] * jnp.exp(segsum(A))
    y_diag = jnp.einsum('bgjcls,bcsgjp->bclgjp', M, X)
    ds = jnp.moveaxis(jnp.exp(A_cs[..., -1:] - A_cs), (3, 4), (1, 2))
    states = jnp.einsum('bclgn,bclgjp->bcgjpn', Bm, X * ds[..., None])
    states = jnp.concatenate([init[:, None], states], 1)
    chunk_A = jnp.pad(A_cs[..., -1], ((0, 0), (0, 0), (0, 0), (1, 0)))
    new_states = jnp.einsum('bgjzc,bcgjpn->bzgjpn', jnp.exp(segsum(chunk_A)), states)
    prev_states, final = new_states[:, :-1], new_states[:, -1]
    sdo = jnp.moveaxis(jnp.exp(A_cs), (3, 4), (1, 2))
    y_off = jnp.einsum('bclgn,bcgjpn->bclgjp', Cm, prev_states) * sdo[..., None]
    return (y_diag + y_off).reshape(b, L, g, j, p_), final


def mamba_prep(u, p):
    b, n, _ = u.shape
    z, xbc, dt = jnp.split(u, [SSM_INNER, SSM_INNER + SSM_XBC], -1)
    xbc = jax.nn.silu(dwconv3(xbc, p['ssm_conv_w'], p['ssm_conv_b']).astype(F32))
    xs, bm, cm = jnp.split(xbc, [SSM_INNER, SSM_INNER + SSM_GROUPS * SSM_STATE], -1)
    xs = xs.reshape(b, n, SSM_HEADS, SSM_HEAD_DIM)
    bm = bm.reshape(b, n, SSM_GROUPS, SSM_STATE)
    cm = cm.reshape(b, n, SSM_GROUPS, SSM_STATE)
    dt = jax.nn.softplus(dt.astype(F32).reshape(b, n, 2, SSM_HEADS) + p['ssm_dt_bias'])
    return z, xs, bm, cm, dt


def ssd_direction(xs, dt, a, bm, cm, init, reverse):
    if reverse:
        xs, dt, bm, cm = (jnp.flip(t, 1) for t in (xs, dt, bm, cm))
    b, n = dt.shape[:2]
    hg = SSM_HEADS // SSM_GROUPS
    X = (xs * dt[..., None]).reshape(b, n, SSM_GROUPS, hg, SSM_HEAD_DIM)
    dA = (dt * a).reshape(b, n, SSM_GROUPS, hg)
    y, final = ssd_chunked(X, dA, bm, cm, init)
    y = y.reshape(b, n, SSM_HEADS, SSM_HEAD_DIM)
    if reverse:
        y = jnp.flip(y, 1)
    return y, final


def mamba_out(y, xs, z, p):
    b, n = y.shape[:2]
    y = (y + xs * p['ssm_d'][:, None]).reshape(b, n, SSM_INNER) * jax.nn.silu(z.astype(F32))
    yg = y.reshape(b, n, SSM_GROUPS, SSM_INNER // SSM_GROUPS)
    yg = yg * lax.rsqrt(jnp.mean(yg * yg, -1, keepdims=True) + 1e-5)
    return (yg.reshape(b, n, SSM_INNER) * p['ssm_norm']).astype(z.dtype)


def mamba_branch(uc, ul, p, need_ctx):
    a = -jnp.exp(p['ssm_a_log'].astype(F32))
    zc, xc, bc, cc, dtc = mamba_prep(uc, p)
    zl, xl, bl, cl, dtl = mamba_prep(ul, p)
    zero = jnp.zeros((ul.shape[0], SSM_GROUPS, SSM_HEADS // SSM_GROUPS, SSM_HEAD_DIM, SSM_STATE), F32)
    yc = yl = 0.0
    for d, rev in enumerate((False, True)):
        yc_d, sc = ssd_direction(xc, dtc[:, :, d], a[d], bc, cc, zero, rev)
        yl_d, _ = ssd_direction(xl, dtl[:, :, d], a[d], bl, cl, sc, rev)
        yc = yc + yc_d
        yl = yl + yl_d
    return (mamba_out(yc, xc, zc, p) if need_ctx else None), mamba_out(yl, xl, zl, p)


def hyena_filters(n, p):
    t = jnp.linspace(0.0, 1.0, n, dtype=F32)[:, None]
    wpos = 2.0 * math.pi * jnp.arange(n, dtype=F32)[:, None] / n
    f = jnp.linspace(1e-4, HY_BANDS - 1, HY_BANDS, dtype=F32)[None]
    z = jnp.concatenate([t, jnp.cos(f * wpos), -jnp.sin(f * wpos)], -1)
    freq = p['hy_freq']
    hdn = jnp.sin(freq * (z @ p['hy_w1'] + p['hy_b1']))
    hdn = jnp.sin(freq * (hdn @ p['hy_w2'] + p['hy_b2']))
    filt = (hdn @ p['hy_w3']).astype(F32).reshape(n, 2, HY_WIDTH)
    deltas = jnp.abs(jnp.linspace(HY_MIN_DECAY, HY_MAX_DECAY, HY_WIDTH, dtype=F32))
    return filt * jnp.exp(-t * deltas)[:, None]


def bidir_long_conv(v, filt):
    n, ch = v.shape[1], v.shape[2]
    k_full = jnp.concatenate([filt[:, 0], jnp.zeros((1, ch), F32), filt[:0:-1, 1]], 0)
    kf = jnp.fft.rfft(k_full, axis=0)
    vf = jnp.fft.rfft(v, n=2 * n, axis=1)
    return jnp.fft.irfft(vf * kf, n=2 * n, axis=1)[:, :n]


def hyena_seq(u, p):
    n = u.shape[1]
    uc = dwconv3(u, p['hy_conv_w'], p['hy_conv_b']).astype(F32)
    x0, x1, v = jnp.split(uc, 3, -1)
    v = v * x1
    y = bidir_long_conv(v, hyena_filters(n, p)) + v * p['hy_d']
    return (y * x0).astype(u.dtype)


def hyena_branch(uc, ul, p, need_ctx):
    return (hyena_seq(uc, p) if need_ctx else None), hyena_seq(ul, p)


ADA_TN = 1024


def _ada_kernel(c_ref, w_ref, b_ref, o_ref):
    o_ref[...] = jnp.dot(jax.nn.silu(c_ref[...]).astype(BF16), w_ref[...].astype(BF16),
                         preferred_element_type=F32) + b_ref[...]


def ada_modulation(cond, w_ada, b_ada):
    m, d = cond.shape
    n = w_ada.shape[1]
    return pl.pallas_call(
        _ada_kernel,
        grid=(n // ADA_TN,),
        in_specs=[pl.BlockSpec((m, d), lambda j: (0, 0)), pl.BlockSpec((d, ADA_TN), lambda j: (0, j)),
                  pl.BlockSpec((1, ADA_TN), lambda j: (0, j))],
        out_specs=pl.BlockSpec((m, ADA_TN), lambda j: (0, j)),
        out_shape=jax.ShapeDtypeStruct((m, n), F32),
        compiler_params=pltpu.CompilerParams(dimension_semantics=("parallel",), vmem_limit_bytes=VMEM_LIMIT),
        name="ada_modulation",
    )(cond, w_ada, b_ada.reshape(1, n))


def expert_choice_ffn(h2, logits, ffn_w, b, n):
    cap = EC_CAPACITY * n // N_EXPERTS
    aff = jax.nn.softmax(logits[:, :N_EXPERTS].reshape(b, n, N_EXPERTS), -1)
    gate, idx = lax.top_k(jnp.swapaxes(aff, 1, 2), cap)
    bidx = jnp.arange(b)[:, None, None]
    xs = h2.reshape(b, n, D_MODEL)[bidx, idx]
    ye = expert_ffn(xs, gate[..., None], *ffn_w)
    return jnp.zeros((b, n, D_MODEL), F32).at[bidx, idx].add(ye).reshape(b * n, D_MODEL)


def _proj_piece(proj, name, width=None):
    off, w = PROJ[name]
    return proj[:, off:off + (width or w)].astype(F32)


def trunk_layer(xc, xl, mods, p, tabs, need_ctx, b, n_ctx, n_lat):
    D = D_MODEL
    piece = lambda rows, k: mods[rows, None, k * D:(k + 1) * D]
    lat = dict(x=xl, n=n_lat, rpm=n_lat, rows=slice(0, b), cos=tabs['cos_l'], sin=tabs['sin_l'])
    ctx = dict(x=xc, n=n_ctx, rpm=b * n_ctx, rows=slice(b, b + 1), cos=tabs['cos_c'], sin=tabs['sin_c'])
    for t in (ctx, lat):
        t['proj'] = inproj(t['x'], piece(t['rows'], 0), piece(t['rows'], 1), p['w_in_p'], t['rpm'])
        t['q'], t['k'], t['v'] = mla_proj(t['proj'], t['cos'], t['sin'], p['mla_q_norm'], p['mla_kv_norm'], p['mla_w'])
        t['rw'] = rwkv_prep_pallas(t['proj'], p['rw_w'], t['n'])
    sets = (ctx, lat) if need_ctx else (lat,)

    r3 = lambda a, n: a.reshape(b, n, a.shape[-1])
    k_all = jnp.concatenate([r3(ctx['k'], n_ctx), r3(lat['k'], n_lat)], 1)
    v_all = jnp.concatenate([r3(ctx['v'], n_ctx), r3(lat['v'], n_lat)], 1)
    nk = n_ctx + n_lat
    tk = 768 if nk % 768 == 0 else 256
    lat['mla'] = flash_attention(r3(lat['q'], n_lat), k_all, v_all, tq=min(n_lat, 1024), tk=tk, out_dtype=BF16)
    if need_ctx:
        ctx['mla'] = flash_attention(r3(ctx['q'], n_ctx), r3(ctx['k'], n_ctx), r3(ctx['v'], n_ctx), tq=n_ctx, tk=n_ctx,
                                     out_dtype=BF16)

    def seq_u(t, names):
        return jnp.concatenate([_proj_piece(t['proj'], nm, w) for nm, w in names], -1).reshape(b, t['n'], -1)
    ssm_names = (('ssm_z', None), ('ssm_x', None), ('ssm_b', None), ('ssm_c', None), ('ssm_dt', 2 * SSM_HEADS))
    hy_names = (('hy_x0', None), ('hy_x1', None), ('hy_v', None))
    ssm_c, ssm_l = mamba_branch(seq_u(ctx, ssm_names), seq_u(lat, ssm_names), p, need_ctx)
    hy_c, hy_l = hyena_branch(seq_u(ctx, hy_names), seq_u(lat, hy_names), p, need_ctx)
    lat['ssm'], lat['hy'] = ssm_l, hy_l
    ctx['ssm'], ctx['hy'] = ssm_c, hy_c

    s0 = jnp.zeros((2, b, RW_WIDTH // RW_P, RW_P, RW_P), F32)
    for t in (ctx, lat):
        r_, v_, kk_, lw0, lw1, kd0, kd1, bd0, bd1, g_, bonus = (r3(a, t['n']) for a in t['rw'])
        yf, yb, s0 = rwkv_scan_pallas(r_, v_, kk_, lw0, lw1, kd0, kd1, bd0, bd1, s0)
        if t in sets:
            flat = lambda a: a.reshape(b * t['n'], RW_WIDTH)
            t['rwkv'] = rwkv_out_pallas(flat(yf), flat(yb), flat(bonus), flat(g_), p['rw_w'])

    outs = []
    for t in sets:
        rows = b * t['n']
        branches = [t['mla'].reshape(rows, BRANCH_WIDTH), t['ssm'].reshape(rows, BRANCH_WIDTH).astype(BF16),
                    t['hy'].reshape(rows, BRANCH_WIDTH).astype(BF16), t['rwkv']]
        merged = merge_branches_pallas(branches, t['proj'], p['w_branch_b'])
        x1, h2, logits = outproj_postnorm(merged, t['x'], piece(t['rows'], 2), piece(t['rows'], 3), piece(t['rows'], 4),
                                          p['ln1_g'], p['ln1_b'], p['w_out_b'], p['w_router'], t['rpm'])
        moe = expert_choice_ffn(h2, logits, p['ffn_w'], b, t['n'])
        outs.append(postnorm(x1, moe, piece(t['rows'], 5), p['ln2_g'], p['ln2_b'], t['rpm']))
    return (outs[0], outs[1]) if need_ctx else (None, outs[0])


def kernel(x, c, ctx, c_ctx, w_ada, b_ada, w_in, mla_q_norm, mla_w_q_up, mla_kv_norm, mla_w_kv_up,
           ssm_conv_w, ssm_conv_b, ssm_dt_bias, ssm_a_log, ssm_d, ssm_norm,
           hy_conv_w, hy_conv_b, hy_w1, hy_b1, hy_w2, hy_b2, hy_w3, hy_freq, hy_d,
           rw_mu, rw_w0, rw_w_up, rw_a0, rw_a_up, rw_g_up, rw_kk, rw_ka, rw_rk, rw_ln_g, rw_ln_b,
           w_branch, w_out, ln1_g, ln1_b, w_router, w_gate_e, w_up_e, w_down_e, ln2_g, ln2_b):
    stacked = dict(
        w_in=w_in, mla_q_norm=mla_q_norm, mla_w_q_up=mla_w_q_up, mla_kv_norm=mla_kv_norm, mla_w_kv_up=mla_w_kv_up,
        ssm_conv_w=ssm_conv_w, ssm_conv_b=ssm_conv_b, ssm_dt_bias=ssm_dt_bias, ssm_a_log=ssm_a_log,
        ssm_d=ssm_d, ssm_norm=ssm_norm,
        hy_conv_w=hy_conv_w, hy_conv_b=hy_conv_b, hy_w1=hy_w1, hy_b1=hy_b1, hy_w2=hy_w2, hy_b2=hy_b2,
        hy_w3=hy_w3, hy_freq=hy_freq, hy_d=hy_d,
        rw_mu=rw_mu, rw_w0=rw_w0, rw_w_up=rw_w_up, rw_a0=rw_a0, rw_a_up=rw_a_up, rw_g_up=rw_g_up,
        rw_kk=rw_kk, rw_ka=rw_ka, rw_rk=rw_rk, rw_ln_g=rw_ln_g, rw_ln_b=rw_ln_b,
        w_branch=w_branch, w_out=w_out, ln1_g=ln1_g, ln1_b=ln1_b,
        w_router=w_router, w_gate_e=w_gate_e, w_up_e=w_up_e, w_down_e=w_down_e, ln2_g=ln2_g, ln2_b=ln2_b)
    b, n_lat, D = x.shape
    n_ctx = ctx.shape[1]
    cos, sin = rope_tables_128(n_lat)
    ones = jnp.pad(jnp.ones((b * n_ctx, QK_ROPE), F32), ((0, 0), (0, 128 - QK_ROPE)))
    tabs = dict(cos_l=jnp.tile(cos, (b, 1)), sin_l=jnp.tile(sin, (b, 1)), cos_c=ones, sin_c=jnp.zeros_like(ones))
    cond = jnp.concatenate([c, c_ctx[None], jnp.zeros((8 - b - 1, D), F32)], 0)
    xc, xl = ctx.reshape(b * n_ctx, D), x.reshape(b * n_lat, D)
    for i in range(DEPTH):
        p = {name: arr[i] for name, arr in stacked.items()}
        p['w_in_p'] = build_w_in(p['w_in'])
        p['mla_w'] = build_mla_weights(p)
        p['rw_w'] = build_rwkv_weights(p)
        p['w_branch_b'] = p['w_branch'].astype(BF16)
        p['w_out_b'] = p['w_out'].astype(BF16)
        p['ffn_w'] = build_expert_weights(p)
        mods = ada_modulation(cond, w_ada[i], b_ada[i])
        xc, xl = trunk_layer(xc, xl, mods, p, tabs, i < DEPTH - 1, b, n_ctx, n_lat)
    return xl.reshape(b, n_lat, D)
```

```python
import functools
import math

import jax
import jax.numpy as jnp
from jax import lax
import numpy as np
from jax.experimental import pallas as pl
from jax.experimental.pallas import tpu as pltpu

D_MODEL = 2048
DEPTH = 2

GRID_W = 64
N_BRANCH = 4
BRANCH_WIDTH = D_MODEL // 2

MLA_Q_LORA = D_MODEL // 4
MLA_KV_LORA = D_MODEL // 4
QK_NOPE = 128
QK_ROPE = 64
V_HEAD = 128
MLA_HEADS = BRANCH_WIDTH // V_HEAD
ROPE_BASE = 10000.0

SSM_HEAD_DIM = 64
SSM_HEADS = BRANCH_WIDTH // SSM_HEAD_DIM
SSM_INNER = SSM_HEADS * SSM_HEAD_DIM
SSM_GROUPS = 2
SSM_STATE = 128
SSM_CHUNK = 128
SSM_XBC = SSM_INNER + 2 * SSM_GROUPS * SSM_STATE

HY_WIDTH = BRANCH_WIDTH
HY_EMB = 33
HY_BANDS = (HY_EMB - 1) // 2
HY_FILTER_HIDDEN = 64
HY_TARGET = 1e-2
HY_SLOW_FRAC = 1.5
HY_QUICK_FRAC = 0.3
HY_MIN_DECAY = math.log(HY_TARGET) / HY_SLOW_FRAC
HY_MAX_DECAY = math.log(HY_TARGET) / HY_QUICK_FRAC

RW_HEAD_DIM = 64
RW_WIDTH = BRANCH_WIDTH
RW_HEADS = RW_WIDTH // RW_HEAD_DIM
RW_DECAY_LORA = 64
RW_ICLR_LORA = 64
RW_GATE_LORA = 160
RW_GN_EPS = 64e-5

N_EXPERTS = 16
EXPERT_FF = 1408
EC_CAPACITY = 2

ALPHA = (2 * DEPTH) ** 0.25

MLA_COLS = MLA_Q_LORA + MLA_KV_LORA + QK_ROPE
SSM_COLS = SSM_INNER + SSM_XBC + 2 * SSM_HEADS
HY_COLS = 3 * HY_WIDTH
RW_COLS = 3 * RW_WIDTH + 2 * RW_DECAY_LORA + 2 * RW_ICLR_LORA + RW_GATE_LORA

F32 = jnp.float32
BF16 = jnp.bfloat16
HI = lax.Precision.HIGHEST

RW_T = 64
RW_P = 128
RW_PAIRS_PER_STEP = 2
RW_INV_BASE = 8
ATT_DK = 256
ATT_DV = 128
TM = 512
VMEM_LIMIT = 48 * 1024 * 1024


def _mm(a, b):
    return jnp.dot(a.astype(BF16), b.astype(BF16), preferred_element_type=F32)


def _mm_nt(a, b):
    return lax.dot_general(a.astype(BF16), b.astype(BF16), (((1,), (1,)), ((), ())), preferred_element_type=F32)


def _mm_tn(a, b):
    return lax.dot_general(a.astype(BF16), b.astype(BF16), (((0,), (0,)), ((), ())), preferred_element_type=F32)


def _ln(x, eps=1e-6):
    mu = jnp.mean(x, -1, keepdims=True)
    xc = x - mu
    return xc * lax.rsqrt(jnp.mean(xc * xc, -1, keepdims=True) + eps)


def _rwkv_pair_chunks(chains, inv_masks):
    T = RW_T
    each = lambda f, *ls: [f(*xs) for xs in zip(*ls)]
    S = [ch[0] for ch in chains]
    r, lw, k, v, kk, b = ([ch[1][i] for ch in chains] for i in range(6))
    a = [-x for x in kk]
    rev = [ch[2] for ch in chains]
    cum, head_lo, strict_bd, incl_bd, eye_bd = ([ch[3][i] for ch in chains] for i in range(5))

    def stack(x, lo):
        return jnp.concatenate([jnp.where(lo, x, 0.0), jnp.where(lo, 0.0, x)], axis=0)

    c = each(lambda m, x: jnp.dot(m, x, precision=HI, preferred_element_type=F32), cum, lw)
    cT = each(lambda x, rv: x[0:1, :] if rv else x[T - 1:T, :], c, rev)
    e_neg = each(lambda x: jnp.exp(-x), c)
    e_end = each(lambda x, y: jnp.exp(x - y), cT, c)
    a_s = each(lambda x, cc, l, lo: stack(x * jnp.exp(cc - l), lo), a, c, lw, head_lo)
    r_s = each(lambda x, cc, lo: stack(x * jnp.exp(cc), lo), r, c, head_lo)
    b_s = each(lambda x, e, lo: stack(x * e, lo), b, e_neg, head_lo)
    k_s = each(lambda x, e, lo: stack(x * e, lo), k, e_neg, head_lo)
    bh_s = each(lambda x, e, lo: stack(x * e, lo), b, e_end, head_lo)
    kh_s = each(lambda x, e, lo: stack(x * e, lo), k, e_end, head_lo)
    v_s = each(stack, v, head_lo)
    g_T = each(jnp.exp, cT)

    ar = each(lambda x, y: jnp.concatenate([x, y], axis=0), a_s, r_s)
    gb = each(_mm_nt, ar, b_s)
    gk = each(_mm_nt, ar, k_s)
    n = each(lambda m, g: jnp.where(m, g[:2 * T], 0.0), strict_bd, gb)
    l_ak = each(lambda m, g: jnp.where(m, g[:2 * T], 0.0), strict_bd, gk)
    m_rb = each(lambda m, g: jnp.where(m, g[2 * T:], 0.0), incl_bd, gb)
    m_rk = each(lambda m, g: jnp.where(m, g[2 * T:], 0.0), incl_bd, gk)
    lv = each(_mm, l_ak, v_s)
    n8 = each(lambda x: jnp.where(inv_masks[0], x, 0.0), n)
    p = each(lambda e, x: e + x, eye_bd, n8)
    m = each(_mm, n8, n8)
    p = each(lambda x, y: x + _mm(y, x), p, m)
    m = each(_mm, m, m)
    p = each(lambda x, y: x + _mm(y, x), p, m)
    for merge in inv_masks[1:]:
        n_off = each(lambda x: jnp.where(merge, x, 0.0), n)
        p = each(lambda x, y: x + _mm(_mm(x, y), x), p, n_off)
    w = each(_mm, p, a_s)
    u0 = each(_mm, p, lv)
    wr = each(lambda x, y, s: _mm_nt(jnp.concatenate([x, y], axis=0), s), w, r_s, S)
    uv = each(lambda x, y, z: jnp.concatenate([x[:2 * T] + y, z], axis=0), wr, u0, v_s)
    y_s = each(lambda x, p1, p2, z: x[2 * T:] + _mm(jnp.concatenate([p1, p2], axis=1), z), wr, m_rb, m_rk, uv)
    y = each(lambda x: x[:T] + x[T:], y_s)
    S_new = each(lambda s, g, z, p1, p2: s * g + _mm_tn(z, jnp.concatenate([p1, p2], axis=0)),
                 S, g_T, uv, bh_s, kh_s)
    return list(zip(y, S_new))


def _rwkv_consts():
    T = RW_T
    row = lax.broadcasted_iota(jnp.int32, (T, T), 0)
    col = lax.broadcasted_iota(jnp.int32, (T, T), 1)
    cum_f = (row >= col).astype(F32)
    cum_b = (row <= col).astype(F32)
    head_lo = lax.broadcasted_iota(jnp.int32, (T, RW_P), 1) < RW_P // 2
    r2 = lax.broadcasted_iota(jnp.int32, (2 * T, 2 * T), 0)
    c2 = lax.broadcasted_iota(jnp.int32, (2 * T, 2 * T), 1)
    same = (r2 >= T) == (c2 >= T)
    eye_bd = (r2 == c2).astype(F32)
    fwd = (cum_f, head_lo, same & (r2 > c2), same & (r2 >= c2), eye_bd)
    bwd = (cum_b, head_lo, same & (r2 < c2), same & (r2 <= c2), eye_bd)
    blk = lambda s: (r2 // s) == (c2 // s)
    inv_masks = [blk(RW_INV_BASE)]
    s = RW_INV_BASE
    while s < T:
        inv_masks.append(blk(2 * s) & jnp.logical_not(blk(s)))
        s *= 2
    return fwd, bwd, inv_masks


def _rwkv_kernel(rf, vf, af, lwf, kf, bf, rb, vb, ab, lwb, kb, bb, s0_ref, yf_ref, yb_ref, sT_ref, s_scr, *, nsub):
    j = pl.program_id(2)

    @pl.when(j == 0)
    def _():
        s_scr[...] = s0_ref[:, 0]

    T = RW_T
    cf, cb, inv_masks = _rwkv_consts()

    def body(i, carry):
        sf = pl.ds(pl.multiple_of(i * T, T), T)
        sb = pl.ds(pl.multiple_of((nsub - 1 - i) * T, T), T)
        lanes = [slice(q * RW_P, (q + 1) * RW_P) for q in range(RW_PAIRS_PER_STEP)]
        ins = []
        for ln in lanes:
            ins.append(tuple(ref[0, sf, ln].astype(F32) for ref in (rf, lwf, kf, vf, af, bf)))
            ins.append(tuple(ref[0, sb, ln].astype(F32) for ref in (rb, lwb, kb, vb, ab, bb)))
        res = _rwkv_pair_chunks([(carry[c], ins[c], bool(c % 2), cb if c % 2 else cf)
                                 for c in range(2 * RW_PAIRS_PER_STEP)], inv_masks)
        for q, ln in enumerate(lanes):
            yf_ref[0, sf, ln] = res[2 * q][0]
            yb_ref[0, sb, ln] = res[2 * q + 1][0]
        return tuple(s for _, s in res)

    init = tuple(s_scr[d, q] for q in range(RW_PAIRS_PER_STEP) for d in range(2))
    fin = lax.fori_loop(0, nsub, body, init)
    for q in range(RW_PAIRS_PER_STEP):
        for d in range(2):
            s_scr[d, q] = fin[2 * q + d]

    @pl.when(j == pl.num_programs(2) - 1)
    def _():
        for q in range(RW_PAIRS_PER_STEP):
            for d in range(2):
                sT_ref[d, 0, q] = fin[2 * q + d]


def rwkv_scan_pallas(r, v, kk, lw0, lw1, kd0, kd1, bd0, bd1, s0, *, tb=256):
    B, n, C = r.shape
    tb = min(tb, n)
    nb = n // tb
    pp = RW_PAIRS_PER_STEP
    assert n % tb == 0 and tb % RW_T == 0 and C % (pp * RW_P) == 0
    fw = pl.BlockSpec((1, tb, pp * RW_P), lambda b, p, j: (b, j, p))
    bw = pl.BlockSpec((1, tb, pp * RW_P), lambda b, p, j: (b, nb - 1 - j, p))
    st = pl.BlockSpec((2, 1, pp, RW_P, RW_P), lambda b, p, j: (0, b, p, 0, 0))
    return pl.pallas_call(
        functools.partial(_rwkv_kernel, nsub=tb // RW_T),
        grid=(B, C // (pp * RW_P), nb),
        in_specs=[fw] * 6 + [bw] * 6 + [st],
        out_specs=[fw, bw, st],
        out_shape=[jax.ShapeDtypeStruct((B, n, C), F32), jax.ShapeDtypeStruct((B, n, C), F32),
                   jax.ShapeDtypeStruct(s0.shape, F32)],
        scratch_shapes=[pltpu.VMEM((2, pp, RW_P, RW_P), F32)],
        compiler_params=pltpu.CompilerParams(dimension_semantics=("parallel", "parallel", "arbitrary")),
        name="rwkv_scan",
    )(r, v, kk, lw0, kd0, bd0, r, v, kk, lw1, kd1, bd1, s0)


def _flash_kernel(q_ref, k_ref, v_ref, o_ref, m_scr, l_scr, acc_scr):
    j = pl.program_id(3)

    @pl.when(j == 0)
    def _():
        m_scr[...] = jnp.full(m_scr.shape, -jnp.inf, F32)
        l_scr[...] = jnp.zeros(l_scr.shape, F32)
        acc_scr[...] = jnp.zeros(acc_scr.shape, F32)

    s = lax.dot_general(q_ref[0], k_ref[0], (((1,), (1,)), ((), ())), preferred_element_type=F32)
    m_prev = m_scr[...]
    m_new = jnp.maximum(m_prev, jnp.max(s, axis=-1, keepdims=True))
    alpha = jnp.exp(m_prev - m_new)
    p = jnp.exp(s - m_new)
    l_scr[...] = alpha * l_scr[...] + jnp.sum(p, axis=-1, keepdims=True)
    acc_scr[...] = alpha * acc_scr[...] + jnp.dot(p.astype(BF16), v_ref[0], preferred_element_type=F32)
    m_scr[...] = m_new

    @pl.when(j == pl.num_programs(3) - 1)
    def _():
        o_ref[0] = (acc_scr[...] / l_scr[...]).astype(o_ref.dtype)


def flash_attention(q, k, v, *, tq, tk, out_dtype=F32):
    B, nq, hq = q.shape
    nk = k.shape[1]
    H = hq // ATT_DK
    assert nq % tq == 0 and nk % tk == 0
    return pl.pallas_call(
        _flash_kernel,
        grid=(B, H, nq // tq, nk // tk),
        in_specs=[pl.BlockSpec((1, tq, ATT_DK), lambda b, h, i, j: (b, i, h)),
                  pl.BlockSpec((1, tk, ATT_DK), lambda b, h, i, j: (b, j, h)),
                  pl.BlockSpec((1, tk, ATT_DV), lambda b, h, i, j: (b, j, h))],
        out_specs=pl.BlockSpec((1, tq, ATT_DV), lambda b, h, i, j: (b, i, h)),
        out_shape=jax.ShapeDtypeStruct((B, nq, H * ATT_DV), out_dtype),
        scratch_shapes=[pltpu.VMEM((tq, 1), F32), pltpu.VMEM((tq, 1), F32), pltpu.VMEM((tq, ATT_DV), F32)],
        compiler_params=pltpu.CompilerParams(
            dimension_semantics=("parallel", "parallel", "parallel", "arbitrary")),
        name="mla_flash",
    )(q, k, v)


PROJ = {}
_off = 0
for _name, _w in (('gate', 4 * D_MODEL), ('hy_x0', HY_WIDTH), ('hy_x1', HY_WIDTH), ('hy_v', HY_WIDTH),
                  ('rw_r', RW_WIDTH), ('rw_k', RW_WIDTH), ('rw_v', RW_WIDTH), ('ssm_z', SSM_INNER),
                  ('ssm_x', SSM_INNER), ('mla_cq', MLA_Q_LORA), ('mla_ckv', MLA_KV_LORA),
                  ('ssm_b', 256), ('ssm_c', 256), ('rw_gd', 256), ('rw_wd', 128), ('rw_ad', 128),
                  ('mla_kpe', 128), ('mla_kpe_sw', 128), ('ssm_dt', 128)):
    assert _off % _w == 0, (_name, _off, _w)
    PROJ[_name] = (_off, _w)
    _off += _w
PROJ_TN = 512
PROJ_N = -(-_off // PROJ_TN) * PROJ_TN


def _rope_swap_matrix():
    p = np.zeros((QK_ROPE, QK_ROPE), np.float32)
    q = QK_ROPE // 4
    for g in range(2):
        for i in range(q):
            p[g * 2 * q + q + i, g * 2 * q + i] = -1.0
            p[g * 2 * q + i, g * 2 * q + q + i] = 1.0
    return jnp.asarray(p)


def build_w_in(w_in):
    cuts = np.cumsum([0, MLA_COLS, SSM_COLS, HY_COLS, RW_COLS, 4 * D_MODEL]).tolist()
    mla, ssm, hy, rw, gate = (w_in[:, cuts[i]:cuts[i + 1]] for i in range(5))
    kpe = mla[:, MLA_Q_LORA + MLA_KV_LORA:]
    src = {
        'gate': gate, 'hy_x0': hy[:, :HY_WIDTH], 'hy_x1': hy[:, HY_WIDTH:2 * HY_WIDTH], 'hy_v': hy[:, 2 * HY_WIDTH:],
        'rw_r': rw[:, :RW_WIDTH], 'rw_k': rw[:, RW_WIDTH:2 * RW_WIDTH], 'rw_v': rw[:, 2 * RW_WIDTH:3 * RW_WIDTH],
        'rw_wd': rw[:, 3 * RW_WIDTH:3 * RW_WIDTH + 128], 'rw_ad': rw[:, 3 * RW_WIDTH + 128:3 * RW_WIDTH + 256],
        'rw_gd': rw[:, 3 * RW_WIDTH + 256:],
        'ssm_z': ssm[:, :SSM_INNER], 'ssm_x': ssm[:, SSM_INNER:2 * SSM_INNER],
        'ssm_b': ssm[:, 2 * SSM_INNER:2 * SSM_INNER + 256], 'ssm_c': ssm[:, 2 * SSM_INNER + 256:2 * SSM_INNER + 512],
        'ssm_dt': ssm[:, 2 * SSM_INNER + 512:],
        'mla_cq': mla[:, :MLA_Q_LORA], 'mla_ckv': mla[:, MLA_Q_LORA:MLA_Q_LORA + MLA_KV_LORA],
        'mla_kpe': kpe, 'mla_kpe_sw': kpe @ _rope_swap_matrix(),
    }
    cols, pos = [], 0
    for name, (off, w) in PROJ.items():
        assert off == pos
        piece = src[name]
        cols.append(jnp.pad(piece, ((0, 0), (0, w - piece.shape[1]))))
        pos += w
    cols.append(jnp.zeros((w_in.shape[0], PROJ_N - pos), w_in.dtype))
    return jnp.concatenate(cols, 1).astype(BF16)


def _inproj_kernel(x_ref, sh_ref, sc_ref, w_ref, o_ref, h_scr):
    @pl.when(pl.program_id(1) == 0)
    def _():
        h_scr[...] = (_ln(x_ref[...]) * (1.0 + sc_ref[0]) + sh_ref[0]).astype(BF16)

    o_ref[...] = jnp.dot(h_scr[...], w_ref[...], preferred_element_type=F32).astype(o_ref.dtype)


def inproj(x2d, shift, scale, w_p, rows_per_mod):
    R, D = x2d.shape
    tm = min(TM, R)
    bpm = max(rows_per_mod // tm, 1)
    mod = pl.BlockSpec((1, 1, D), lambda i, j: (i // bpm, 0, 0))
    return pl.pallas_call(
        _inproj_kernel,
        grid=(R // tm, PROJ_N // PROJ_TN),
        in_specs=[pl.BlockSpec((tm, D), lambda i, j: (i, 0)), mod, mod,
                  pl.BlockSpec((D, PROJ_TN), lambda i, j: (0, j))],
        out_specs=pl.BlockSpec((tm, PROJ_TN), lambda i, j: (i, j)),
        out_shape=jax.ShapeDtypeStruct((R, PROJ_N), BF16),
        scratch_shapes=[pltpu.VMEM((tm, D), BF16)],
        compiler_params=pltpu.CompilerParams(dimension_semantics=("parallel", "arbitrary"),
                                             vmem_limit_bytes=VMEM_LIMIT),
        name="inproj",
    )(x2d, shift, scale, w_p)


def build_mla_weights(p):
    wq = p['mla_w_q_up'].reshape(MLA_Q_LORA, MLA_HEADS, QK_NOPE + QK_ROPE)
    wkv = p['mla_w_kv_up'].reshape(MLA_KV_LORA, MLA_HEADS, QK_NOPE + V_HEAD)
    wq_p = jnp.pad(wq, ((0, 0), (0, 0), (0, ATT_DK - QK_NOPE - QK_ROPE))).reshape(MLA_Q_LORA, MLA_HEADS * ATT_DK)
    wsw = jnp.einsum('chd,de->che', wq[:, :, QK_NOPE:], _rope_swap_matrix())
    wsw_p = jnp.pad(wsw, ((0, 0), (0, 0), (0, 128 - QK_ROPE))).reshape(MLA_Q_LORA, MLA_HEADS * 128)
    wkn = wkv[:, :, :QK_NOPE].reshape(MLA_KV_LORA, MLA_HEADS * QK_NOPE)
    wv = wkv[:, :, QK_NOPE:].reshape(MLA_KV_LORA, MLA_HEADS * V_HEAD)
    return wq_p.astype(BF16), wsw_p.astype(BF16), wkn.astype(BF16), wv.astype(BF16)


def axial_rope_tables(n_tokens):
    rows = n_tokens // GRID_W
    row = jnp.repeat(jnp.arange(rows), GRID_W).astype(F32)
    col = jnp.tile(jnp.arange(GRID_W), rows).astype(F32)
    half = QK_ROPE // 2
    inv = ROPE_BASE ** (-jnp.arange(0, half, 2, dtype=F32) / half)
    ang = jnp.stack([row[:, None] * inv, col[:, None] * inv], 1)
    return jnp.cos(ang), jnp.sin(ang)


def rope_tables_128(n_tokens):
    cos, sin = axial_rope_tables(n_tokens)
    full = lambda t: jnp.concatenate([t[:, 0], t[:, 0], t[:, 1], t[:, 1]], -1)
    pad = lambda t: jnp.pad(t, ((0, 0), (0, 128 - QK_ROPE)))
    return pad(full(cos)), pad(full(sin))


def _mla_proj_kernel(cq_ref, ckv_ref, kpe_ref, ksw_ref, cos_ref, sin_ref, gq_ref, gkv_ref, wq_ref, wsw_ref, wkn_ref,
                     wv_ref, q_ref, k_ref, v_ref, *, q_scale):
    def rms(x, g):
        return (x * lax.rsqrt(jnp.mean(x * x, -1, keepdims=True) + 1e-6) * g).astype(BF16)

    cqn = rms(cq_ref[...].astype(F32), gq_ref[...])
    ckvn = rms(ckv_ref[...].astype(F32), gkv_ref[...])
    cos = cos_ref[...]
    sin = sin_ref[...]
    q = jnp.dot(cqn, wq_ref[...], preferred_element_type=F32)
    qsw = jnp.dot(cqn, wsw_ref[...], preferred_element_type=F32)
    kn = jnp.dot(ckvn, wkn_ref[...], preferred_element_type=F32)
    v_ref[...] = jnp.dot(ckvn, wv_ref[...], preferred_element_type=F32).astype(v_ref.dtype)
    kpe = (kpe_ref[...].astype(F32) * cos + ksw_ref[...].astype(F32) * sin).astype(k_ref.dtype)
    for h in range(MLA_HEADS):
        lo = h * ATT_DK
        q_ref[:, lo:lo + 128] = (q[:, lo:lo + 128] * q_scale).astype(q_ref.dtype)
        pe = q[:, lo + 128:lo + 256] * cos + qsw[:, h * 128:(h + 1) * 128] * sin
        q_ref[:, lo + 128:lo + 256] = (pe * q_scale).astype(q_ref.dtype)
        k_ref[:, lo:lo + 128] = kn[:, h * 128:(h + 1) * 128].astype(k_ref.dtype)
        k_ref[:, lo + 128:lo + 256] = kpe


def mla_proj(proj, cos, sin, gq, gkv, weights):
    R = proj.shape[0]
    tm = min(TM, R)
    wq, wsw, wkn, wv = weights
    col = lambda name: pl.BlockSpec((tm, PROJ[name][1]), lambda i, o=PROJ[name][0] // PROJ[name][1]: (i, o))
    row = lambda w: pl.BlockSpec((tm, w), lambda i: (i, 0))
    full = lambda a: pl.BlockSpec(a.shape, lambda i: (0,) * a.ndim)
    gq2, gkv2 = gq.reshape(1, -1), gkv.reshape(1, -1)
    return pl.pallas_call(
        functools.partial(_mla_proj_kernel, q_scale=float((QK_NOPE + QK_ROPE) ** -0.5)),
        grid=(R // tm,),
        in_specs=[col('mla_cq'), col('mla_ckv'), col('mla_kpe'), col('mla_kpe_sw'), row(128), row(128),
                  full(gq2), full(gkv2), full(wq), full(wsw), full(wkn), full(wv)],
        out_specs=[row(MLA_HEADS * ATT_DK), row(MLA_HEADS * ATT_DK), row(MLA_HEADS * ATT_DV)],
        out_shape=[jax.ShapeDtypeStruct((R, MLA_HEADS * ATT_DK), BF16),
                   jax.ShapeDtypeStruct((R, MLA_HEADS * ATT_DK), BF16),
                   jax.ShapeDtypeStruct((R, MLA_HEADS * ATT_DV), BF16)],
        compiler_params=pltpu.CompilerParams(dimension_semantics=("parallel",), vmem_limit_bytes=VMEM_LIMIT),
        name="mla_proj",
    )(proj, proj, proj, proj, cos, sin, gq2, gkv2, wq, wsw, wkn, wv)


MERGE_TN = 512


def _merge_kernel(b0, b1, b2, b3, g0, g1, g2, g3, w_ref, o_ref):
    acc = None
    for i, (b, g) in enumerate(((b0, g0), (b1, g1), (b2, g2), (b3, g3))):
        t = jax.nn.sigmoid(g[...].astype(F32)) * jnp.dot(b[...], w_ref[i], preferred_element_type=F32)
        acc = t if acc is None else acc + t
    o_ref[...] = acc.astype(o_ref.dtype)


def merge_branches_pallas(branches, proj, w_branch):
    R = proj.shape[0]
    tm = min(TM, R)
    nj = D_MODEL // MERGE_TN
    g0 = PROJ['gate'][0] // MERGE_TN
    br = pl.BlockSpec((tm, BRANCH_WIDTH), lambda i, j: (i, 0))
    gates = [pl.BlockSpec((tm, MERGE_TN), lambda i, j, k=k: (i, g0 + k * nj + j)) for k in range(N_BRANCH)]
    return pl.pallas_call(
        _merge_kernel,
        grid=(R // tm, nj),
        in_specs=[br] * 4 + gates + [pl.BlockSpec((N_BRANCH, BRANCH_WIDTH, MERGE_TN), lambda i, j: (0, 0, j))],
        out_specs=pl.BlockSpec((tm, MERGE_TN), lambda i, j: (i, j)),
        out_shape=jax.ShapeDtypeStruct((R, D_MODEL), BF16),
        compiler_params=pltpu.CompilerParams(dimension_semantics=("parallel", "arbitrary"),
                                             vmem_limit_bytes=VMEM_LIMIT),
        name="branch_merge",
    )(*branches, proj, proj, proj, proj, w_branch)


ROUTER_PAD = 128


def _mm3_f32(a, b_hi, b_lo):
    ah = a.astype(BF16)
    al = (a - ah.astype(F32)).astype(BF16)
    d = lambda x, y: jnp.dot(x, y, preferred_element_type=F32)
    return d(ah, b_hi) + d(ah, b_lo) + d(al, b_hi)


def _outproj_kernel(m_ref, x_ref, g1_ref, sh2_ref, sc2_ref, lng_ref, lnb_ref, w_ref, wr_hi, wr_lo,
                    x_out, h2_out, logit_out):
    y = jnp.dot(m_ref[...], w_ref[...], preferred_element_type=F32)
    xn = _ln(ALPHA * x_ref[...] + g1_ref[0] * y) * lng_ref[...] + lnb_ref[...]
    x_out[...] = xn
    h2 = _ln(xn) * (1.0 + sc2_ref[0]) + sh2_ref[0]
    h2_out[...] = h2.astype(h2_out.dtype)
    logit_out[...] = _mm3_f32(h2, wr_hi[...], wr_lo[...])


def outproj_postnorm(merged, x2d, g1, sh2, sc2, ln_g, ln_b, w_out, w_router, rows_per_mod):
    R, D = x2d.shape
    tm = min(TM, R)
    bpm = max(rows_per_mod // tm, 1)
    mod = pl.BlockSpec((1, 1, D), lambda i: (i // bpm, 0, 0))
    row = lambda w: pl.BlockSpec((tm, w), lambda i: (i, 0))
    full = lambda a: pl.BlockSpec(a.shape, lambda i: (0,) * a.ndim)
    wr = jnp.pad(w_router, ((0, 0), (0, ROUTER_PAD - w_router.shape[1])))
    wr_hi = wr.astype(BF16)
    wr_lo = (wr - wr_hi.astype(F32)).astype(BF16)
    lg, lb = ln_g.reshape(1, D), ln_b.reshape(1, D)
    return pl.pallas_call(
        _outproj_kernel,
        grid=(R // tm,),
        in_specs=[row(D), row(D), mod, mod, mod, full(lg), full(lb), full(w_out), full(wr_hi), full(wr_lo)],
        out_specs=[row(D), row(D), row(ROUTER_PAD)],
        out_shape=[jax.ShapeDtypeStruct((R, D), F32), jax.ShapeDtypeStruct((R, D), BF16),
                   jax.ShapeDtypeStruct((R, ROUTER_PAD), F32)],
        compiler_params=pltpu.CompilerParams(dimension_semantics=("parallel",), vmem_limit_bytes=VMEM_LIMIT),
        name="outproj_postnorm",
    )(merged, x2d, g1, sh2, sc2, lg, lb, w_out, wr_hi, wr_lo)


def _postnorm_kernel(x_ref, y_ref, g_ref, lng_ref, lnb_ref, o_ref):
    o_ref[...] = _ln(ALPHA * x_ref[...] + g_ref[0] * y_ref[...]) * lng_ref[...] + lnb_ref[...]


def postnorm(x2d, y2d, g, ln_g, ln_b, rows_per_mod):
    R, D = x2d.shape
    tm = min(TM, R)
    bpm = max(rows_per_mod // tm, 1)
    row = pl.BlockSpec((tm, D), lambda i: (i, 0))
    vec = pl.BlockSpec((1, D), lambda i: (0, 0))
    return pl.pallas_call(
        _postnorm_kernel,
        grid=(R // tm,),
        in_specs=[row, row, pl.BlockSpec((1, 1, D), lambda i: (i // bpm, 0, 0)), vec, vec],
        out_specs=row,
        out_shape=jax.ShapeDtypeStruct((R, D), F32),
        compiler_params=pltpu.CompilerParams(dimension_semantics=("parallel",), vmem_limit_bytes=VMEM_LIMIT),
        name="postnorm",
    )(x2d, y2d, g, ln_g.reshape(1, D), ln_b.reshape(1, D))


FF_PAD = 1536
FF_TILE = 512


def _expert_ffn_kernel(x_ref, gate_ref, wg_ref, wu_ref, wd_ref, o_ref):
    f = pl.program_id(2)
    x = x_ref[0, 0]
    hid = (jax.nn.silu(jnp.dot(x, wg_ref[0], preferred_element_type=F32))
           * jnp.dot(x, wu_ref[0], preferred_element_type=F32))
    part = jnp.dot(hid.astype(BF16), wd_ref[0], preferred_element_type=F32)

    @pl.when(f == 0)
    def _():
        o_ref[0, 0] = part

    @pl.when(f > 0)
    def _():
        o_ref[0, 0] += part

    @pl.when(f == pl.num_programs(2) - 1)
    def _():
        o_ref[0, 0] = o_ref[0, 0] * gate_ref[0, 0]


def expert_ffn(xs, gate, wg, wu, wd):
    S, E, C, D = xs.shape
    nf = FF_PAD // FF_TILE
    return pl.pallas_call(
        _expert_ffn_kernel,
        grid=(E, S, nf),
        in_specs=[pl.BlockSpec((1, 1, C, D), lambda e, s, f: (s, e, 0, 0)),
                  pl.BlockSpec((1, 1, C, 1), lambda e, s, f: (s, e, 0, 0)),
                  pl.BlockSpec((1, D, FF_TILE), lambda e, s, f: (e, 0, f)),
                  pl.BlockSpec((1, D, FF_TILE), lambda e, s, f: (e, 0, f)),
                  pl.BlockSpec((1, FF_TILE, D), lambda e, s, f: (e, f, 0))],
        out_specs=pl.BlockSpec((1, 1, C, D), lambda e, s, f: (s, e, 0, 0)),
        out_shape=jax.ShapeDtypeStruct((S, E, C, D), F32),
        compiler_params=pltpu.CompilerParams(dimension_semantics=("parallel", "parallel", "arbitrary"),
                                             vmem_limit_bytes=VMEM_LIMIT),
        name="expert_ffn",
    )(xs, gate, wg, wu, wd)


def build_expert_weights(p):
    padf = lambda w: jnp.pad(w, ((0, 0), (0, 0), (0, FF_PAD - EXPERT_FF))).astype(BF16)
    wd = jnp.pad(p['w_down_e'], ((0, 0), (0, FF_PAD - EXPERT_FF), (0, 0))).astype(BF16)
    return padf(p['w_gate_e']), padf(p['w_up_e']), wd


HALO = 8


def _halo_specs(tm, w, col_block, nrows):
    nb8 = nrows // HALO
    prev = pl.BlockSpec((HALO, w), lambda i: (jnp.maximum(i * (tm // HALO) - 1, 0), col_block))
    nxt = pl.BlockSpec((HALO, w), lambda i: (jnp.minimum((i + 1) * (tm // HALO), nb8 - 1), col_block))
    return prev, nxt


def _neighbours(cur, prev8, next8, seq_len):
    tm = cur.shape[0]
    row = lax.broadcasted_iota(jnp.int32, cur.shape, 0)
    g = row + pl.program_id(0) * tm
    before = jnp.where(row == 0, prev8[HALO - 1:HALO, :], pltpu.roll(cur, 1, 0))
    after = jnp.where(row == tm - 1, next8[0:1, :], pltpu.roll(cur, tm - 1, 0))
    pos = g & (seq_len - 1)
    return jnp.where(pos == 0, 0.0, before), jnp.where(pos == seq_len - 1, 0.0, after)


RW_PIECES = ('rw_r', 'rw_k', 'rw_v', 'rw_wd', 'rw_ad', 'rw_gd')


def _head_sum_matrix(scale=1.0):
    r = np.arange(RW_WIDTH)
    return jnp.asarray(((r[:, None] // RW_HEAD_DIM) == (r[None, :] // RW_HEAD_DIM)).astype(np.float32) * scale, BF16)


def _seg_sum(x, m):
    xh = x.astype(BF16)
    xl = (x - xh.astype(F32)).astype(BF16)
    return jnp.dot(xh, m, preferred_element_type=F32) + jnp.dot(xl, m, preferred_element_type=F32)


def _rwkv_prep_kernel(*refs, seq_len):
    cur = refs[0:6]
    prv = refs[6:12]
    nxt = refs[12:18]
    mu = refs[18:24]
    (w0_ref, a0_ref, wup_ref, aup_ref, gup_ref, kkw_ref, ka_ref, rk_ref, hs_ref) = refs[24:33]
    (r_o, v_o, kk_o, lw0_o, lw1_o, kd0_o, kd1_o, bd0_o, bd1_o, g_o, bonus_o) = refs[33:]
    us = []
    for c, p, n, m in zip(cur, prv, nxt, mu):
        x = c[...].astype(F32)
        before, after = _neighbours(x, p[...].astype(F32), n[...].astype(F32), seq_len)
        us.append(x + m[0:1, :] * (before - x) + m[1:2, :] * (after - x))
    r, k, v, wd, ad, gd = us
    hs = hs_ref[...]
    g_o[...] = jnp.dot(jax.nn.sigmoid(gd).astype(BF16), gup_ref[...], preferred_element_type=F32).astype(g_o.dtype)
    kk = k * kkw_ref[...]
    kk = kk / jnp.maximum(jnp.sqrt(_seg_sum(kk * kk, hs)), 1e-12)
    wl = jnp.dot(jnp.tanh(wd).astype(BF16), wup_ref[...], preferred_element_type=F32) + w0_ref[...]
    al = jnp.dot(ad.astype(BF16), aup_ref[...], preferred_element_type=F32) + a0_ref[...]
    r_o[...] = r.astype(r_o.dtype)
    v_o[...] = v.astype(v_o.dtype)
    kk_o[...] = kk.astype(kk_o.dtype)
    rk = rk_ref[...]
    acc = None
    for d, (lw_o, kd_o, bd_o) in enumerate(((lw0_o, kd0_o, bd0_o), (lw1_o, kd1_o, bd1_o))):
        sl = slice(d * RW_WIDTH, (d + 1) * RW_WIDTH)
        w_log = -jax.nn.softplus(-wl[:, sl]) - 0.5
        lw_o[...] = -jnp.exp(w_log)
        iclr = jax.nn.sigmoid(al[:, sl])
        kd = k * (1.0 + (iclr - 1.0) * ka_ref[...])
        kd_o[...] = kd.astype(kd_o.dtype)
        bd_o[...] = (kk * iclr).astype(bd_o.dtype)
        t = r * kd * rk
        acc = t if acc is None else acc + t
    bonus_o[...] = (_seg_sum(acc, hs) * v).astype(bonus_o.dtype)


def build_rwkv_weights(p):
    W = RW_WIDTH
    mu = p['rw_mu']
    cuts = np.cumsum([0, W, W, W, 128, 128, RW_GATE_LORA]).tolist()
    mus = [mu[:, cuts[i]:cuts[i + 1]] for i in range(6)]
    mus[5] = jnp.pad(mus[5], ((0, 0), (0, 256 - RW_GATE_LORA)))
    z = jnp.zeros((RW_DECAY_LORA, W), F32)
    wup = jnp.concatenate([jnp.concatenate([p['rw_w_up'][0], z], 1), jnp.concatenate([z, p['rw_w_up'][1]], 1)], 0)
    aup = jnp.concatenate([jnp.concatenate([p['rw_a_up'][0], z], 1), jnp.concatenate([z, p['rw_a_up'][1]], 1)], 0)
    gup = jnp.pad(p['rw_g_up'], ((0, 256 - RW_GATE_LORA), (0, 0)))
    return dict(mu=mus, w0=p['rw_w0'].reshape(1, 2 * W), a0=p['rw_a0'].reshape(1, 2 * W), wup=wup.astype(BF16),
                aup=aup.astype(BF16), gup=gup.astype(BF16), kkw=p['rw_kk'].reshape(1, W), ka=p['rw_ka'].reshape(1, W),
                rk=p['rw_rk'].reshape(1, W), hs=_head_sum_matrix(), hmean=_head_sum_matrix(1.0 / RW_HEAD_DIM),
                ln_g=p['rw_ln_g'].reshape(1, W), ln_b=p['rw_ln_b'].reshape(1, W))


def rwkv_prep_pallas(proj, wts, seq_len):
    R = proj.shape[0]
    tm = min(TM, R)
    W = RW_WIDTH
    cur, prv, nxt = [], [], []
    for name in RW_PIECES:
        off, w = PROJ[name]
        cur.append(pl.BlockSpec((tm, w), lambda i, o=off // w: (i, o)))
        a, b = _halo_specs(tm, w, off // w, R)
        prv.append(a)
        nxt.append(b)
    full = lambda a: pl.BlockSpec(a.shape, lambda i: (0,) * a.ndim)
    row = pl.BlockSpec((tm, W), lambda i: (i, 0))
    consts = [wts[k] for k in ('w0', 'a0', 'wup', 'aup', 'gup', 'kkw', 'ka', 'rk', 'hs')]
    bf = jax.ShapeDtypeStruct((R, W), BF16)
    f32 = jax.ShapeDtypeStruct((R, W), F32)
    return pl.pallas_call(
        functools.partial(_rwkv_prep_kernel, seq_len=seq_len),
        grid=(R // tm,),
        in_specs=cur + prv + nxt + [full(m) for m in wts['mu']] + [full(c) for c in consts],
        out_specs=[row] * 11,
        out_shape=[bf, bf, bf, f32, f32, bf, bf, bf, bf, bf, bf],
        compiler_params=pltpu.CompilerParams(dimension_semantics=("parallel",), vmem_limit_bytes=VMEM_LIMIT),
        name="rwkv_prep",
    )(*([proj] * 18), *wts['mu'], *consts)


def _rwkv_out_kernel(yf_ref, yb_ref, bonus_ref, g_ref, hm_ref, lng_ref, lnb_ref, o_ref):
    y = yf_ref[...] + yb_ref[...]
    hm = hm_ref[...]
    yc = y - _seg_sum(y, hm)
    yn = yc * lax.rsqrt(_seg_sum(yc * yc, hm) + RW_GN_EPS) * lng_ref[...] + lnb_ref[...]
    o_ref[...] = ((yn + bonus_ref[...].astype(F32)) * g_ref[...].astype(F32)).astype(o_ref.dtype)


def rwkv_out_pallas(yf, yb, bonus, g, wts):
    R, W = yf.shape
    tm = min(TM, R)
    row = pl.BlockSpec((tm, W), lambda i: (i, 0))
    full = lambda a: pl.BlockSpec(a.shape, lambda i: (0,) * a.ndim)
    return pl.pallas_call(
        _rwkv_out_kernel,
        grid=(R // tm,),
        in_specs=[row] * 4 + [full(wts['hmean']), full(wts['ln_g']), full(wts['ln_b'])],
        out_specs=row,
        out_shape=jax.ShapeDtypeStruct((R, W), BF16),
        compiler_params=pltpu.CompilerParams(dimension_semantics=("parallel",), vmem_limit_bytes=VMEM_LIMIT),
        name="rwkv_out",
    )(yf, yb, bonus, g, wts['hmean'], wts['ln_g'], wts['ln_b'])


SSD_T = SSM_CHUNK
SSD_GW = SSM_INNER // SSM_GROUPS
SSD_HG = SSM_HEADS // SSM_GROUPS
SSM_PIECES = ('ssm_x', 'ssm_b', 'ssm_c')


def _split_dot(x, m):
    xh = x.astype(BF16)
    xl = (x - xh.astype(F32)).astype(BF16)
    return jnp.dot(xh, m, preferred_element_type=F32) + jnp.dot(xl, m, preferred_element_type=F32)


def _ssd_prep_kernel(*refs, seq_len):
    cur, prv, nxt = refs[0:3], refs[3:6], refs[6:9]
    cw, cb = refs[9:12], refs[12:15]
    dt_ref, dtb_ref, perm_ref = refs[15:18]
    x_o, b_o, c_o, dt_o = refs[18:]
    for c, p, n, w, bias, o in zip(cur, prv, nxt, cw, cb, (x_o, b_o, c_o)):
        x = c[...].astype(F32)
        before, after = _neighbours(x, p[...].astype(F32), n[...].astype(F32), seq_len)
        y = before * w[0:1, :] + x * w[1:2, :] + after * w[2:3, :] + bias[...]
        o[...] = jax.nn.silu(y).astype(o.dtype)
    dt = jax.nn.softplus(dt_ref[...].astype(F32) + dtb_ref[...])
    dt_o[...] = _split_dot(dt, perm_ref[...])


def build_ssd_weights(p):
    w, bias = p['ssm_conv_w'], p['ssm_conv_b']
    cuts = [0, SSM_INNER, SSM_INNER + 256, SSM_INNER + 512]
    cw = [w[:, cuts[i]:cuts[i + 1]] for i in range(3)]
    cb = [bias[cuts[i]:cuts[i + 1]].reshape(1, -1) for i in range(3)]
    dtb = jnp.pad(p['ssm_dt_bias'].reshape(1, 2 * SSM_HEADS), ((0, 0), (0, 128 - 2 * SSM_HEADS)))
    perm = np.zeros((128, SSM_GROUPS * 128), np.float32)
    for d in range(2):
        for g in range(SSM_GROUPS):
            for j in range(SSD_HG):
                perm[d * SSM_HEADS + g * SSD_HG + j, g * 128 + d * SSD_HG + j] = 1.0
    a = -jnp.exp(p['ssm_a_log'].astype(F32))
    a_g = jnp.stack([jnp.pad(jnp.concatenate([a[0, g * SSD_HG:(g + 1) * SSD_HG], a[1, g * SSD_HG:(g + 1) * SSD_HG]]),
                             (0, 128 - 2 * SSD_HG)) for g in range(SSM_GROUPS)]).reshape(SSM_GROUPS, 1, 128)
    d_exp = jnp.repeat(p['ssm_d'], SSM_HEAD_DIM).reshape(1, SSM_INNER)
    return dict(cw=cw, cb=cb, dtb=dtb, perm=jnp.asarray(perm, BF16), a_g=a_g, d_exp=d_exp,
                norm=p['ssm_norm'].reshape(1, SSM_INNER))


def ssd_prep_pallas(proj, wts, seq_len):
    R = proj.shape[0]
    tm = min(TM, R)
    cur, prv, nxt = [], [], []
    for name in SSM_PIECES:
        off, w = PROJ[name]
        cur.append(pl.BlockSpec((tm, w), lambda i, o=off // w: (i, o)))
        a, b = _halo_specs(tm, w, off // w, R)
        prv.append(a)
        nxt.append(b)
    full = lambda a: pl.BlockSpec(a.shape, lambda i: (0,) * a.ndim)
    off, w = PROJ['ssm_dt']
    consts = wts['cw'] + wts['cb']
    row = lambda width: pl.BlockSpec((tm, width), lambda i: (i, 0))
    return pl.pallas_call(
        functools.partial(_ssd_prep_kernel, seq_len=seq_len),
        grid=(R // tm,),
        in_specs=cur + prv + nxt + [full(c) for c in consts]
        + [pl.BlockSpec((tm, w), lambda i: (i, off // w)), full(wts['dtb']), full(wts['perm'])],
        out_specs=[row(SSM_INNER), row(256), row(256), row(SSM_GROUPS * 128)],
        out_shape=[jax.ShapeDtypeStruct((R, SSM_INNER), BF16), jax.ShapeDtypeStruct((R, 256), BF16),
                   jax.ShapeDtypeStruct((R, 256), BF16), jax.ShapeDtypeStruct((R, SSM_GROUPS * 128), F32)],
        compiler_params=pltpu.CompilerParams(dimension_semantics=("parallel",), vmem_limit_bytes=VMEM_LIMIT),
        name="ssd_prep",
    )(*([proj] * 9), *consts, proj, wts['dtb'], wts['perm'])


def _ssd_chunks(chains, consts):
    T = SSD_T
    each = lambda f, *ls: [f(*xs) for xs in zip(*ls)]
    eye, cums, tris, expand, lane_head = consts
    S, xs, bm, cm, dt, a, dirs = (list(t) for t in zip(*chains))
    cum = [cums[d] for d in dirs]
    tri = [tris[d] for d in dirs]
    ex = [expand[d] for d in dirs]
    acs = each(lambda m, x, av: jnp.dot(m, x * av, precision=HI, preferred_element_type=F32), cum, dt, a)
    acs_t = each(lambda x: lax.dot_general(eye, x, (((1,), (1,)), ((), ())), precision=HI,
                                           preferred_element_type=F32), acs)
    dt_e = each(_split_dot, dt, ex)
    acs_e = each(_split_dot, acs, ex)
    atot_e = each(lambda x, d: x[0:1, :] if d else x[T - 1:T, :], acs_e, dirs)
    X = each(lambda x, y: x * y, xs, dt_e)
    cb = each(_mm_nt, cm, bm)
    y = []
    for c in range(len(chains)):
        parts = []
        for pr in range(SSD_HG // 2):
            acc = None
            for hh in range(2):
                i = dirs[c] * SSD_HG + 2 * pr + hh
                diff = acs[c][:, i:i + 1] - acs_t[c][i:i + 1, :]
                m = cb[c] * jnp.exp(jnp.where(tri[c], diff, -jnp.inf))
                xh = jnp.where(lane_head == hh, X[c][:, pr * 128:(pr + 1) * 128], 0.0)
                t = _mm(m, xh)
                acc = t if acc is None else acc + t
            parts.append(acc)
        y.append(jnp.concatenate(parts, axis=1))
    states = each(lambda b_, x, at, ac: _mm_tn(b_, x * jnp.exp(at - ac)), bm, X, atot_e, acs_e)
    y_off = each(lambda c_, s, ac: _mm(c_, s) * jnp.exp(ac), cm, S, acs_e)
    S_new = each(lambda s, at, st: s * jnp.exp(at) + st, S, atot_e, states)
    return [(yd + yo, sn) for yd, yo, sn in zip(y, y_off, S_new)]


def _ssd_consts():
    T = SSD_T
    row = lax.broadcasted_iota(jnp.int32, (T, T), 0)
    col = lax.broadcasted_iota(jnp.int32, (T, T), 1)
    eye = (row == col).astype(F32)
    cums = ((row >= col).astype(F32), (row <= col).astype(F32))
    tris = (row >= col, row <= col)
    er = lax.broadcasted_iota(jnp.int32, (128, SSD_GW), 0)
    ec = lax.broadcasted_iota(jnp.int32, (128, SSD_GW), 1)
    expand = tuple((er == d * SSD_HG + ec // SSM_HEAD_DIM).astype(BF16) for d in range(2))
    lane_head = lax.broadcasted_iota(jnp.int32, (T, 128), 1) // SSM_HEAD_DIM
    return eye, cums, tris, expand, lane_head


def _ssd_kernel(xf, bf, cf, dtf, xb, bb, cb_, dtb, a_ref, s0_ref, yf_ref, yb_ref, sT_ref, s_scr, *, nsub):
    j = pl.program_id(2)

    @pl.when(j == 0)
    def _():
        s_scr[...] = s0_ref[:, 0, 0]

    T = SSD_T
    consts = _ssd_consts()
    a = a_ref[0]

    def body(i, carry):
        sf = pl.ds(pl.multiple_of(i * T, T), T)
        sb = pl.ds(pl.multiple_of((nsub - 1 - i) * T, T), T)
        ld = lambda ref, sl: ref[0, sl, :].astype(F32)
        res = _ssd_chunks([(carry[0], ld(xf, sf), ld(bf, sf), ld(cf, sf), ld(dtf, sf), a, 0),
                           (carry[1], ld(xb, sb), ld(bb, sb), ld(cb_, sb), ld(dtb, sb), a, 1)], consts)
        yf_ref[0, sf, :] = res[0][0]
        yb_ref[0, sb, :] = res[1][0]
        return res[0][1], res[1][1]

    fin = lax.fori_loop(0, nsub, body, (s_scr[0], s_scr[1]))
    s_scr[0] = fin[0]
    s_scr[1] = fin[1]

    @pl.when(j == pl.num_programs(2) - 1)
    def _():
        sT_ref[0, 0, 0] = fin[0]
        sT_ref[1, 0, 0] = fin[1]


def ssd_scan_pallas(xs, bm, cm, dt, a_g, s0, *, tb=256):
    B, n, _ = xs.shape
    tb = min(tb, n)
    nb = n // tb
    G = SSM_GROUPS
    fwd = lambda w: pl.BlockSpec((1, tb, w), lambda b, g, j: (b, j, g))
    bwd = lambda w: pl.BlockSpec((1, tb, w), lambda b, g, j: (b, nb - 1 - j, g))
    st = pl.BlockSpec((2, 1, 1, SSM_STATE, SSD_GW), lambda b, g, j: (0, b, g, 0, 0))
    return pl.pallas_call(
        functools.partial(_ssd_kernel, nsub=tb // SSD_T),
        grid=(B, G, nb),
        in_specs=[fwd(SSD_GW), fwd(128), fwd(128), fwd(128), bwd(SSD_GW), bwd(128), bwd(128), bwd(128),
                  pl.BlockSpec((1, 1, 128), lambda b, g, j: (g, 0, 0)), st],
        out_specs=[fwd(SSD_GW), bwd(SSD_GW), st],
        out_shape=[jax.ShapeDtypeStruct((B, n, SSM_INNER), F32), jax.ShapeDtypeStruct((B, n, SSM_INNER), F32),
                   jax.ShapeDtypeStruct(s0.shape, F32)],
        scratch_shapes=[pltpu.VMEM((2, SSM_STATE, SSD_GW), F32)],
        compiler_params=pltpu.CompilerParams(dimension_semantics=("parallel", "parallel", "arbitrary"),
                                             vmem_limit_bytes=VMEM_LIMIT),
        name="ssd_scan",
    )(xs, bm, cm, dt, xs, bm, cm, dt, a_g, s0)


def _ssd_out_kernel(yf_ref, yb_ref, xs_ref, z_ref, d_ref, nw_ref, o_ref):
    y = (yf_ref[...] + yb_ref[...] + xs_ref[...].astype(F32) * d_ref[...]) * jax.nn.silu(z_ref[...].astype(F32))
    outs = []
    for g in range(SSM_GROUPS):
        yg = y[:, g * SSD_GW:(g + 1) * SSD_GW]
        outs.append(yg * lax.rsqrt(jnp.mean(yg * yg, -1, keepdims=True) + 1e-5))
    o_ref[...] = (jnp.concatenate(outs, axis=1) * nw_ref[...]).astype(o_ref.dtype)


def ssd_out_pallas(yf, yb, xs, proj, wts):
    R = yf.shape[0]
    tm = min(TM, R)
    row = pl.BlockSpec((tm, SSM_INNER), lambda i: (i, 0))
    vec = pl.BlockSpec((1, SSM_INNER), lambda i: (0, 0))
    zoff = PROJ['ssm_z'][0] // SSM_INNER
    return pl.pallas_call(
        _ssd_out_kernel,
        grid=(R // tm,),
        in_specs=[row, row, row, pl.BlockSpec((tm, SSM_INNER), lambda i: (i, zoff)), vec, vec],
        out_specs=row,
        out_shape=jax.ShapeDtypeStruct((R, SSM_INNER), BF16),
        compiler_params=pltpu.CompilerParams(dimension_semantics=("parallel",), vmem_limit_bytes=VMEM_LIMIT),
        name="ssd_out",
    )(yf, yb, xs, proj, wts['d_exp'], wts['norm'])


HY_PIECES = ('hy_x0', 'hy_x1', 'hy_v')
FFT_R = 128


def _hilo(x):
    hi = x.astype(BF16)
    return hi, (x - hi.astype(F32)).astype(BF16)


def _dot3(a_hi, a_lo, x):
    xh, xl = _hilo(x)
    d = lambda p, q: jnp.dot(p, q, preferred_element_type=F32)
    return d(a_hi, xh) + d(a_hi, xl) + d(a_lo, xh)


def _hilo_tables(pairs):
    out = {}
    for name, x in pairs:
        hi, lo = _hilo(jnp.asarray(x, F32))
        out[name + '_hi'], out[name + '_lo'] = hi, lo
    return out


def _hyena_prep_kernel(*refs, seq_len):
    cur, prv, nxt = refs[0:3], refs[3:6], refs[6:9]
    cw, cb = refs[9:12], refs[12:15]
    x0_o, v_o = refs[15:]
    ys = []
    for c, p, n, w, bias in zip(cur, prv, nxt, cw, cb):
        x = c[...].astype(F32)
        before, after = _neighbours(x, p[...].astype(F32), n[...].astype(F32), seq_len)
        ys.append(before * w[0:1, :] + x * w[1:2, :] + after * w[2:3, :] + bias[...])
    x0_o[...] = ys[0].astype(x0_o.dtype)
    v_o[...] = ys[2] * ys[1]


def hyena_prep_pallas(proj, cw, cb, seq_len):
    R = proj.shape[0]
    tm = min(TM, R)
    cur, prv, nxt = [], [], []
    for name in HY_PIECES:
        off, w = PROJ[name]
        cur.append(pl.BlockSpec((tm, w), lambda i, o=off // w: (i, o)))
        a, b = _halo_specs(tm, w, off // w, R)
        prv.append(a)
        nxt.append(b)
    full = lambda a: pl.BlockSpec(a.shape, lambda i: (0,) * a.ndim)
    row = pl.BlockSpec((tm, HY_WIDTH), lambda i: (i, 0))
    return pl.pallas_call(
        functools.partial(_hyena_prep_kernel, seq_len=seq_len),
        grid=(R // tm,),
        in_specs=cur + prv + nxt + [full(c) for c in cw + cb],
        out_specs=[row, row],
        out_shape=[jax.ShapeDtypeStruct((R, HY_WIDTH), BF16), jax.ShapeDtypeStruct((R, HY_WIDTH), F32)],
        compiler_params=pltpu.CompilerParams(dimension_semantics=("parallel",), vmem_limit_bytes=VMEM_LIMIT),
        name="hyena_prep",
    )(*([proj] * 9), *cw, *cb)


def filter_features(n):
    t = jnp.linspace(0.0, 1.0, n, dtype=F32)[:, None]
    wpos = 2.0 * math.pi * jnp.arange(n, dtype=F32)[:, None] / n
    f = jnp.linspace(1e-4, HY_BANDS - 1, HY_BANDS, dtype=F32)[None]
    z = jnp.concatenate([t, jnp.cos(f * wpos), -jnp.sin(f * wpos), t, jnp.ones_like(t)], -1)
    back = jnp.concatenate([jnp.zeros((1, z.shape[1]), F32), z[:0:-1]], 0)
    return jnp.pad(jnp.concatenate([z, back], 0), ((0, 0), (0, 128 - z.shape[1])))


def _filter_kernel(z_ref, w1_ref, b1_ref, w2_ref, b2_ref, fr_ref, w3_ref, dl_ref, o_ref):
    hp = lambda a, b: jnp.dot(a, b, precision=HI, preferred_element_type=F32)
    z = z_ref[...]
    fr = fr_ref[...]
    h = jnp.sin(fr * (hp(z, w1_ref[...]) + b1_ref[...]))
    h = jnp.sin(fr * (hp(h, w2_ref[...]) + b2_ref[...]))
    filt = hp(h, w3_ref[0])
    t = z[:, HY_EMB:HY_EMB + 1]
    o_ref[...] = filt * jnp.exp(-t * dl_ref[...]) * z[:, HY_EMB + 1:HY_EMB + 2]


def hyena_filter_pallas(n, p):
    z = filter_features(n)
    tr = min(512, n)
    nb = 2 * n // tr
    w1 = jnp.pad(p['hy_w1'], ((0, 128 - HY_EMB), (0, 0)))
    w3 = jnp.moveaxis(p['hy_w3'].reshape(HY_FILTER_HIDDEN, 2, HY_WIDTH), 1, 0)
    deltas = jnp.abs(jnp.linspace(HY_MIN_DECAY, HY_MAX_DECAY, HY_WIDTH, dtype=F32)).reshape(1, HY_WIDTH)
    vec = lambda a: a.reshape(1, -1)
    full = lambda a: pl.BlockSpec(a.shape, lambda i: (0,) * a.ndim)
    args = (w1, vec(p['hy_b1']), p['hy_w2'], vec(p['hy_b2']), vec(p['hy_freq']))
    return pl.pallas_call(
        _filter_kernel,
        grid=(nb,),
        in_specs=[pl.BlockSpec((tr, 128), lambda i: (i, 0))] + [full(a) for a in args]
        + [pl.BlockSpec((1, HY_FILTER_HIDDEN, HY_WIDTH), lambda i: (i // (nb // 2), 0, 0)), full(deltas)],
        out_specs=pl.BlockSpec((tr, HY_WIDTH), lambda i: (i, 0)),
        out_shape=jax.ShapeDtypeStruct((2 * n, HY_WIDTH), F32),
        compiler_params=pltpu.CompilerParams(dimension_semantics=("parallel",), vmem_limit_bytes=VMEM_LIMIT),
        name="hyena_filter",
    )(z, *args, w3, deltas)


FFT_TN = 4096
FFT_KB = 8
FFT_TC = 256


def dft_tables():
    r = np.arange(FFT_R, dtype=np.float64)
    th1 = 2.0 * np.pi * np.outer(r, r) / FFT_R
    k = r[:, None, None] + FFT_R * r[None, :, None]
    phi = 2.0 * np.pi * k * r[None, None, :] / (FFT_R * FFT_R)
    return _hilo_tables((('c1', np.cos(th1)), ('s1', np.sin(th1)), ('gc', np.cos(phi)), ('gs', np.sin(phi)),
                         ('gct', np.swapaxes(np.cos(phi), 1, 2)), ('gst', np.swapaxes(np.sin(phi), 1, 2))))


def _dft_l1_kernel(x_ref, ch, cl, sh, sl, ar_ref, ai_ref):
    x = x_ref[0]
    ar_ref[0] = _dot3(ch[...], cl[...], x)
    ai_ref[0] = -_dot3(sh[...], sl[...], x)


def dft_level1(x2, tabs):
    B, nb, L = x2.shape
    cut = lambda a: a[:, :nb]
    consts = [cut(tabs['c1_hi']), cut(tabs['c1_lo']), cut(tabs['s1_hi']), cut(tabs['s1_lo'])]
    full = lambda a: pl.BlockSpec(a.shape, lambda b, j: (0,) * a.ndim)
    out = pl.BlockSpec((1, FFT_R, FFT_TN), lambda b, j: (b, 0, j))
    return pl.pallas_call(
        _dft_l1_kernel,
        grid=(B, L // FFT_TN),
        in_specs=[pl.BlockSpec((1, nb, FFT_TN), lambda b, j: (b, 0, j))] + [full(c) for c in consts],
        out_specs=[out, out],
        out_shape=[jax.ShapeDtypeStruct((B, FFT_R, L), F32)] * 2,
        compiler_params=pltpu.CompilerParams(dimension_semantics=("parallel", "parallel"),
                                             vmem_limit_bytes=VMEM_LIMIT),
        name="dft_level1",
    )(x2, *consts)


def _level2_fwd(ar, ai, gch, gcl, gsh, gsl):
    xr = _dot3(gch, gcl, ar) + _dot3(gsh, gsl, ai)
    xi = _dot3(gch, gcl, ai) - _dot3(gsh, gsl, ar)
    return xr, xi


def _spectrum_kernel(ar_ref, ai_ref, gch, gcl, gsh, gsl, hr_ref, hi_ref):
    for q in range(FFT_KB):
        xr, xi = _level2_fwd(ar_ref[0, q], ai_ref[0, q], gch[q], gcl[q], gsh[q], gsl[q])
        hr_ref[0, q] = xr
        hi_ref[0, q] = xi


def _conv_l2_kernel(ar_ref, ai_ref, hr_ref, hi_ref, gch, gcl, gsh, gsl, tch, tcl, tsh, tsl, br_ref, bi_ref):
    for q in range(FFT_KB):
        xr, xi = _level2_fwd(ar_ref[0, q], ai_ref[0, q], gch[q], gcl[q], gsh[q], gsl[q])
        hr, hi = hr_ref[0, q], hi_ref[0, q]
        yr = xr * hr - xi * hi
        yi = xr * hi + xi * hr
        br_ref[0, q] = _dot3(tch[q], tcl[q], yr) - _dot3(tsh[q], tsl[q], yi)
        bi_ref[0, q] = _dot3(tch[q], tcl[q], yi) + _dot3(tsh[q], tsl[q], yr)


def _l2_specs():
    data = pl.BlockSpec((1, FFT_KB, FFT_R, FFT_TC), lambda b, c, k: (b, k, 0, c))
    filt = pl.BlockSpec((1, FFT_KB, FFT_R, FFT_TC), lambda b, c, k: (0, k, 0, c))
    tab = pl.BlockSpec((FFT_KB, FFT_R, FFT_R), lambda b, c, k: (k, 0, 0))
    return data, filt, tab


def dft_spectrum(ar, ai, tabs):
    B, _, _, C = ar.shape
    data, _, tab = _l2_specs()
    return pl.pallas_call(
        _spectrum_kernel,
        grid=(B, C // FFT_TC, FFT_R // FFT_KB),
        in_specs=[data, data] + [tab] * 4,
        out_specs=[data, data],
        out_shape=[jax.ShapeDtypeStruct(ar.shape, F32)] * 2,
        compiler_params=pltpu.CompilerParams(dimension_semantics=("parallel", "parallel", "parallel"),
                                             vmem_limit_bytes=VMEM_LIMIT),
        name="dft_spectrum",
    )(ar, ai, tabs['gc_hi'], tabs['gc_lo'], tabs['gs_hi'], tabs['gs_lo'])


def conv_level2(ar, ai, hr, hi, tabs):
    B, _, _, C = ar.shape
    data, filt, tab = _l2_specs()
    names = ('gc_hi', 'gc_lo', 'gs_hi', 'gs_lo', 'gct_hi', 'gct_lo', 'gst_hi', 'gst_lo')
    return pl.pallas_call(
        _conv_l2_kernel,
        grid=(B, C // FFT_TC, FFT_R // FFT_KB),
        in_specs=[data, data, filt, filt] + [tab] * 8,
        out_specs=[data, data],
        out_shape=[jax.ShapeDtypeStruct(ar.shape, F32)] * 2,
        compiler_params=pltpu.CompilerParams(dimension_semantics=("parallel", "parallel", "parallel"),
                                             vmem_limit_bytes=VMEM_LIMIT),
        name="conv_level2",
    )(ar, ai, hr, hi, *[tabs[k] for k in names])


def _idft_l1_kernel(br_ref, bi_ref, v_ref, x0_ref, d_ref, ch, cl, sh, sl, o_ref, *, inv_n):
    y = (_dot3(ch[...], cl[...], br_ref[0]) - _dot3(sh[...], sl[...], bi_ref[0])) * inv_n
    o_ref[0] = ((y + v_ref[0] * d_ref[...]) * x0_ref[0].astype(F32)).astype(o_ref.dtype)


def idft_level1_epilogue(br, bi, v2, x02, d_t, tabs):
    B, nb, L = v2.shape
    cut = lambda a: a[:nb, :]
    consts = [cut(tabs['c1_hi']), cut(tabs['c1_lo']), cut(tabs['s1_hi']), cut(tabs['s1_lo'])]
    full = lambda a: pl.BlockSpec(a.shape, lambda b, j: (0,) * a.ndim)
    big = pl.BlockSpec((1, FFT_R, FFT_TN), lambda b, j: (b, 0, j))
    half = pl.BlockSpec((1, nb, FFT_TN), lambda b, j: (b, 0, j))
    return pl.pallas_call(
        functools.partial(_idft_l1_kernel, inv_n=1.0 / (FFT_R * FFT_R)),
        grid=(B, L // FFT_TN),
        in_specs=[big, big, half, half, pl.BlockSpec((1, FFT_TN), lambda b, j: (0, j))] + [full(c) for c in consts],
        out_specs=half,
        out_shape=jax.ShapeDtypeStruct((B, nb, L), BF16),
        compiler_params=pltpu.CompilerParams(dimension_semantics=("parallel", "parallel"),
                                             vmem_limit_bytes=VMEM_LIMIT),
        name="idft_level1",
    )(br, bi, v2, x02, d_t, *consts)


def hyena_long_conv_fft(v, x0, kfull, hy_d, tabs):
    B, n, C = v.shape
    nb = n // FFT_R
    L = FFT_R * C
    kr, ki = dft_level1(kfull.reshape(1, 2 * nb, L), tabs)
    hr, hi = dft_spectrum(kr.reshape(1, FFT_R, FFT_R, C), ki.reshape(1, FFT_R, FFT_R, C), tabs)
    ar, ai = dft_level1(v.reshape(B, nb, L), tabs)
    br, bi = conv_level2(ar.reshape(B, FFT_R, FFT_R, C), ai.reshape(B, FFT_R, FFT_R, C), hr, hi, tabs)
    d_t = jnp.tile(hy_d.reshape(1, C), (1, FFT_R))
    out = idft_level1_epilogue(br.reshape(B, FFT_R, L), bi.reshape(B, FFT_R, L), v.reshape(B, nb, L),
                               x0.reshape(B, nb, L), d_t, tabs)
    return out.reshape(B, n, C)


def dense_tables(n):
    t = np.arange(2 * n, dtype=np.float64)
    th = 2.0 * np.pi * np.outer(t, t) / (2 * n)
    return _hilo_tables((('c', np.cos(th)), ('s', np.sin(th))))


def _dense_conv_kernel(v_ref, x0_ref, k_ref, d_ref, ch, cl, sh, sl, o_ref, *, n):
    c_hi, c_lo, s_hi, s_lo = ch[...], cl[...], sh[...], sl[...]
    k = k_ref[...]
    hr = _dot3(c_hi, c_lo, k)
    hi = -_dot3(s_hi, s_lo, k)
    v = v_ref[0]
    xr = _dot3(c_hi[:, :n], c_lo[:, :n], v)
    xi = -_dot3(s_hi[:, :n], s_lo[:, :n], v)
    yr = xr * hr - xi * hi
    yi = xr * hi + xi * hr
    y = (_dot3(c_hi[:n, :], c_lo[:n, :], yr) - _dot3(s_hi[:n, :], s_lo[:n, :], yi)) * (0.5 / n)
    o_ref[0] = ((y + v * d_ref[...]) * x0_ref[0].astype(F32)).astype(o_ref.dtype)


def hyena_long_conv_dense(v, x0, kfull, hy_d, tabs):
    B, n, C = v.shape
    tc = 256
    seq = pl.BlockSpec((1, n, tc), lambda b, c: (b, 0, c))
    full = lambda a: pl.BlockSpec(a.shape, lambda b, c: (0,) * a.ndim)
    consts = [tabs[k] for k in ('c_hi', 'c_lo', 's_hi', 's_lo')]
    return pl.pallas_call(
        functools.partial(_dense_conv_kernel, n=n),
        grid=(B, C // tc),
        in_specs=[seq, seq, pl.BlockSpec((2 * n, tc), lambda b, c: (0, c)), pl.BlockSpec((1, tc), lambda b, c: (0, c))]
        + [full(c) for c in consts],
        out_specs=seq,
        out_shape=jax.ShapeDtypeStruct((B, n, C), BF16),
        compiler_params=pltpu.CompilerParams(dimension_semantics=("parallel", "parallel"),
                                             vmem_limit_bytes=VMEM_LIMIT),
        name="hyena_dense_conv",
    )(v, x0, kfull, hy_d.reshape(1, C), *consts)


def build_hyena_weights(p):
    w, bias = p['hy_conv_w'], p['hy_conv_b']
    cw = [w[:, i * HY_WIDTH:(i + 1) * HY_WIDTH] for i in range(3)]
    cb = [bias[i * HY_WIDTH:(i + 1) * HY_WIDTH].reshape(1, -1) for i in range(3)]
    return cw, cb


def hyena_branch_pallas(proj, p, tabs, b, n):
    x0, v = hyena_prep_pallas(proj, p['hy_w'][0], p['hy_w'][1], n)
    kfull = hyena_filter_pallas(n, p)
    r3 = lambda a: a.reshape(b, n, HY_WIDTH)
    if 2 * n == FFT_R * FFT_R:
        y = hyena_long_conv_fft(r3(v), r3(x0), kfull, p['hy_d'], tabs['fft'])
    else:
        y = hyena_long_conv_dense(r3(v), r3(x0), kfull, p['hy_d'], dense_tables(n))
    return y.reshape(b * n, HY_WIDTH)


ADA_TN = 1024


def _ada_kernel(c_ref, w_ref, b_ref, o_ref):
    o_ref[...] = jnp.dot(jax.nn.silu(c_ref[...]).astype(BF16), w_ref[...].astype(BF16),
                         preferred_element_type=F32) + b_ref[...]


def ada_modulation(cond, w_ada, b_ada):
    m, d = cond.shape
    n = w_ada.shape[1]
    return pl.pallas_call(
        _ada_kernel,
        grid=(n // ADA_TN,),
        in_specs=[pl.BlockSpec((m, d), lambda j: (0, 0)), pl.BlockSpec((d, ADA_TN), lambda j: (0, j)),
                  pl.BlockSpec((1, ADA_TN), lambda j: (0, j))],
        out_specs=pl.BlockSpec((m, ADA_TN), lambda j: (0, j)),
        out_shape=jax.ShapeDtypeStruct((m, n), F32),
        compiler_params=pltpu.CompilerParams(dimension_semantics=("parallel",), vmem_limit_bytes=VMEM_LIMIT),
        name="ada_modulation",
    )(cond, w_ada, b_ada.reshape(1, n))


def expert_choice_ffn(h2, logits, ffn_w, b, n):
    cap = EC_CAPACITY * n // N_EXPERTS
    aff = jax.nn.softmax(logits[:, :N_EXPERTS].reshape(b, n, N_EXPERTS), -1)
    gate, idx = lax.top_k(jnp.swapaxes(aff, 1, 2), cap)
    bidx = jnp.arange(b)[:, None, None]
    xs = h2.reshape(b, n, D_MODEL)[bidx, idx]
    ye = expert_ffn(xs, gate[..., None], *ffn_w)
    return jnp.zeros((b, n, D_MODEL), F32).at[bidx, idx].add(ye).reshape(b * n, D_MODEL)


def _proj_piece(proj, name, width=None):
    off, w = PROJ[name]
    return proj[:, off:off + (width or w)].astype(F32)


def trunk_layer(xc, xl, mods, p, tabs, need_ctx, b, n_ctx, n_lat):
    D = D_MODEL
    piece = lambda rows, k: mods[rows, None, k * D:(k + 1) * D]
    lat = dict(x=xl, n=n_lat, rpm=n_lat, rows=slice(0, b), cos=tabs['cos_l'], sin=tabs['sin_l'])
    ctx = dict(x=xc, n=n_ctx, rpm=b * n_ctx, rows=slice(b, b + 1), cos=tabs['cos_c'], sin=tabs['sin_c'])
    for t in (ctx, lat):
        t['proj'] = inproj(t['x'], piece(t['rows'], 0), piece(t['rows'], 1), p['w_in_p'], t['rpm'])
        t['q'], t['k'], t['v'] = mla_proj(t['proj'], t['cos'], t['sin'], p['mla_q_norm'], p['mla_kv_norm'], p['mla_w'])
        t['rw'] = rwkv_prep_pallas(t['proj'], p['rw_w'], t['n'])
    sets = (ctx, lat) if need_ctx else (lat,)

    r3 = lambda a, n: a.reshape(b, n, a.shape[-1])
    k_all = jnp.concatenate([r3(ctx['k'], n_ctx), r3(lat['k'], n_lat)], 1)
    v_all = jnp.concatenate([r3(ctx['v'], n_ctx), r3(lat['v'], n_lat)], 1)
    nk = n_ctx + n_lat
    tk = 768 if nk % 768 == 0 else 256
    lat['mla'] = flash_attention(r3(lat['q'], n_lat), k_all, v_all, tq=min(n_lat, 1024), tk=tk, out_dtype=BF16)
    if need_ctx:
        ctx['mla'] = flash_attention(r3(ctx['q'], n_ctx), r3(ctx['k'], n_ctx), r3(ctx['v'], n_ctx), tq=n_ctx, tk=n_ctx,
                                     out_dtype=BF16)

    s0 = jnp.zeros((2, b, SSM_GROUPS, SSM_STATE, SSD_GW), F32)
    for t in (ctx, lat):
        xs, bm, cm, dt = ssd_prep_pallas(t['proj'], p['ssd_w'], t['n'])
        yf, yb, s0 = ssd_scan_pallas(r3(xs, t['n']), r3(bm, t['n']), r3(cm, t['n']), r3(dt, t['n']),
                                     p['ssd_w']['a_g'], s0)
        if t in sets:
            flat = lambda a: a.reshape(b * t['n'], SSM_INNER)
            t['ssm'] = ssd_out_pallas(flat(yf), flat(yb), xs, t['proj'], p['ssd_w'])

    for t in sets:
        t['hy'] = hyena_branch_pallas(t['proj'], p, tabs, b, t['n'])

    s0 = jnp.zeros((2, b, RW_WIDTH // RW_P, RW_P, RW_P), F32)
    for t in (ctx, lat):
        r_, v_, kk_, lw0, lw1, kd0, kd1, bd0, bd1, g_, bonus = (r3(a, t['n']) for a in t['rw'])
        yf, yb, s0 = rwkv_scan_pallas(r_, v_, kk_, lw0, lw1, kd0, kd1, bd0, bd1, s0)
        if t in sets:
            flat = lambda a: a.reshape(b * t['n'], RW_WIDTH)
            t['rwkv'] = rwkv_out_pallas(flat(yf), flat(yb), flat(bonus), flat(g_), p['rw_w'])

    outs = []
    for t in sets:
        rows = b * t['n']
        branches = [t['mla'].reshape(rows, BRANCH_WIDTH), t['ssm'], t['hy'], t['rwkv']]
        merged = merge_branches_pallas(branches, t['proj'], p['w_branch_b'])
        x1, h2, logits = outproj_postnorm(merged, t['x'], piece(t['rows'], 2), piece(t['rows'], 3), piece(t['rows'], 4),
                                          p['ln1_g'], p['ln1_b'], p['w_out_b'], p['w_router'], t['rpm'])
        moe = expert_choice_ffn(h2, logits, p['ffn_w'], b, t['n'])
        outs.append(postnorm(x1, moe, piece(t['rows'], 5), p['ln2_g'], p['ln2_b'], t['rpm']))
    return (outs[0], outs[1]) if need_ctx else (None, outs[0])


def kernel(x, c, ctx, c_ctx, w_ada, b_ada, w_in, mla_q_norm, mla_w_q_up, mla_kv_norm, mla_w_kv_up,
           ssm_conv_w, ssm_conv_b, ssm_dt_bias, ssm_a_log, ssm_d, ssm_norm,
           hy_conv_w, hy_conv_b, hy_w1, hy_b1, hy_w2, hy_b2, hy_w3, hy_freq, hy_d,
           rw_mu, rw_w0, rw_w_up, rw_a0, rw_a_up, rw_g_up, rw_kk, rw_ka, rw_rk, rw_ln_g, rw_ln_b,
           w_branch, w_out, ln1_g, ln1_b, w_router, w_gate_e, w_up_e, w_down_e, ln2_g, ln2_b):
    stacked = dict(
        w_in=w_in, mla_q_norm=mla_q_norm, mla_w_q_up=mla_w_q_up, mla_kv_norm=mla_kv_norm, mla_w_kv_up=mla_w_kv_up,
        ssm_conv_w=ssm_conv_w, ssm_conv_b=ssm_conv_b, ssm_dt_bias=ssm_dt_bias, ssm_a_log=ssm_a_log,
        ssm_d=ssm_d, ssm_norm=ssm_norm,
        hy_conv_w=hy_conv_w, hy_conv_b=hy_conv_b, hy_w1=hy_w1, hy_b1=hy_b1, hy_w2=hy_w2, hy_b2=hy_b2,
        hy_w3=hy_w3, hy_freq=hy_freq, hy_d=hy_d,
        rw_mu=rw_mu, rw_w0=rw_w0, rw_w_up=rw_w_up, rw_a0=rw_a0, rw_a_up=rw_a_up, rw_g_up=rw_g_up,
        rw_kk=rw_kk, rw_ka=rw_ka, rw_rk=rw_rk, rw_ln_g=rw_ln_g, rw_ln_b=rw_ln_b,
        w_branch=w_branch, w_out=w_out, ln1_g=ln1_g, ln1_b=ln1_b,
        w_router=w_router, w_gate_e=w_gate_e, w_up_e=w_up_e, w_down_e=w_down_e, ln2_g=ln2_g, ln2_b=ln2_b)
    b, n_lat, D = x.shape
    n_ctx = ctx.shape[1]
    cos, sin = rope_tables_128(n_lat)
    ones = jnp.pad(jnp.ones((b * n_ctx, QK_ROPE), F32), ((0, 0), (0, 128 - QK_ROPE)))
    tabs = dict(cos_l=jnp.tile(cos, (b, 1)), sin_l=jnp.tile(sin, (b, 1)), cos_c=ones, sin_c=jnp.zeros_like(ones),
                fft=dft_tables())
    cond = jnp.concatenate([c, c_ctx[None], jnp.zeros((8 - b - 1, D), F32)], 0)
    xc, xl = ctx.reshape(b * n_ctx, D), x.reshape(b * n_lat, D)
    for i in range(DEPTH):
        p = {name: arr[i] for name, arr in stacked.items()}
        p['w_in_p'] = build_w_in(p['w_in'])
        p['mla_w'] = build_mla_weights(p)
        p['rw_w'] = build_rwkv_weights(p)
        p['ssd_w'] = build_ssd_weights(p)
        p['hy_w'] = build_hyena_weights(p)
        p['w_branch_b'] = p['w_branch'].astype(BF16)
        p['w_out_b'] = p['w_out'].astype(BF16)
        p['ffn_w'] = build_expert_weights(p)
        mods = ada_modulation(cond, w_ada[i], b_ada[i])
        xc, xl = trunk_layer(xc, xl, mods, p, tabs, i < DEPTH - 1, b, n_ctx, n_lat)
    return xl.reshape(b, n_lat, D)
```

```python
import functools
import math

import jax
import jax.numpy as jnp
from jax import lax
import numpy as np
from jax.experimental import pallas as pl
from jax.experimental.pallas import tpu as pltpu

D_MODEL = 2048
DEPTH = 2

GRID_W = 64
N_BRANCH = 4
BRANCH_WIDTH = D_MODEL // 2

MLA_Q_LORA = D_MODEL // 4
MLA_KV_LORA = D_MODEL // 4
QK_NOPE = 128
QK_ROPE = 64
V_HEAD = 128
MLA_HEADS = BRANCH_WIDTH // V_HEAD
ROPE_BASE = 10000.0

SSM_HEAD_DIM = 64
SSM_HEADS = BRANCH_WIDTH // SSM_HEAD_DIM
SSM_INNER = SSM_HEADS * SSM_HEAD_DIM
SSM_GROUPS = 2
SSM_STATE = 128
SSM_CHUNK = 128
SSM_XBC = SSM_INNER + 2 * SSM_GROUPS * SSM_STATE

HY_WIDTH = BRANCH_WIDTH
HY_EMB = 33
HY_BANDS = (HY_EMB - 1) // 2
HY_FILTER_HIDDEN = 64
HY_TARGET = 1e-2
HY_SLOW_FRAC = 1.5
HY_QUICK_FRAC = 0.3
HY_MIN_DECAY = math.log(HY_TARGET) / HY_SLOW_FRAC
HY_MAX_DECAY = math.log(HY_TARGET) / HY_QUICK_FRAC

RW_HEAD_DIM = 64
RW_WIDTH = BRANCH_WIDTH
RW_HEADS = RW_WIDTH // RW_HEAD_DIM
RW_DECAY_LORA = 64
RW_ICLR_LORA = 64
RW_GATE_LORA = 160
RW_GN_EPS = 64e-5

N_EXPERTS = 16
EXPERT_FF = 1408
EC_CAPACITY = 2

ALPHA = (2 * DEPTH) ** 0.25

MLA_COLS = MLA_Q_LORA + MLA_KV_LORA + QK_ROPE
SSM_COLS = SSM_INNER + SSM_XBC + 2 * SSM_HEADS
HY_COLS = 3 * HY_WIDTH
RW_COLS = 3 * RW_WIDTH + 2 * RW_DECAY_LORA + 2 * RW_ICLR_LORA + RW_GATE_LORA

F32 = jnp.float32
BF16 = jnp.bfloat16
HI = lax.Precision.HIGHEST

RW_T = 64
RW_P = 128
RW_PAIRS_PER_STEP = 4
RW_INV_BASE = 8
ATT_DK = 256
ATT_DV = 128
TM = 512
VMEM_LIMIT = 48 * 1024 * 1024


def _mm(a, b):
    return jnp.dot(a.astype(BF16), b.astype(BF16), preferred_element_type=F32)


def _mm_nt(a, b):
    return lax.dot_general(a.astype(BF16), b.astype(BF16), (((1,), (1,)), ((), ())), preferred_element_type=F32)


def _mm_tn(a, b):
    return lax.dot_general(a.astype(BF16), b.astype(BF16), (((0,), (0,)), ((), ())), preferred_element_type=F32)


def _ln(x, eps=1e-6):
    mu = jnp.mean(x, -1, keepdims=True)
    xc = x - mu
    return xc * lax.rsqrt(jnp.mean(xc * xc, -1, keepdims=True) + eps)


def _rwkv_pair_chunks(chains, inv_masks):
    T = RW_T
    each = lambda f, *ls: [f(*xs) for xs in zip(*ls)]
    S = [ch[0] for ch in chains]
    r, lw, k, v, kk, b = ([ch[1][i] for ch in chains] for i in range(6))
    a = [-x for x in kk]
    rev = [ch[2] for ch in chains]
    cum, head_lo, strict_bd, incl_bd, eye_bd = ([ch[3][i] for ch in chains] for i in range(5))

    def stack(x, lo):
        return jnp.concatenate([jnp.where(lo, x, 0.0), jnp.where(lo, 0.0, x)], axis=0)

    c = each(lambda m, x: jnp.dot(m, x, precision=HI, preferred_element_type=F32), cum, lw)
    cT = each(lambda x, rv: x[0:1, :] if rv else x[T - 1:T, :], c, rev)
    e_neg = each(lambda x: jnp.exp(-x), c)
    e_end = each(lambda x, y: jnp.exp(x - y), cT, c)
    a_s = each(lambda x, cc, l, lo: stack(x * jnp.exp(cc - l), lo), a, c, lw, head_lo)
    r_s = each(lambda x, cc, lo: stack(x * jnp.exp(cc), lo), r, c, head_lo)
    b_s = each(lambda x, e, lo: stack(x * e, lo), b, e_neg, head_lo)
    k_s = each(lambda x, e, lo: stack(x * e, lo), k, e_neg, head_lo)
    bh_s = each(lambda x, e, lo: stack(x * e, lo), b, e_end, head_lo)
    kh_s = each(lambda x, e, lo: stack(x * e, lo), k, e_end, head_lo)
    v_s = each(stack, v, head_lo)
    g_T = each(jnp.exp, cT)

    ar = each(lambda x, y: jnp.concatenate([x, y], axis=0), a_s, r_s)
    gb = each(_mm_nt, ar, b_s)
    gk = each(_mm_nt, ar, k_s)
    n = each(lambda m, g: jnp.where(m, g[:2 * T], 0.0), strict_bd, gb)
    l_ak = each(lambda m, g: jnp.where(m, g[:2 * T], 0.0), strict_bd, gk)
    m_rb = each(lambda m, g: jnp.where(m, g[2 * T:], 0.0), incl_bd, gb)
    m_rk = each(lambda m, g: jnp.where(m, g[2 * T:], 0.0), incl_bd, gk)
    lv = each(_mm, l_ak, v_s)
    n8 = each(lambda x: jnp.where(inv_masks[0], x, 0.0), n)
    p = each(lambda e, x: e + x, eye_bd, n8)
    m = each(_mm, n8, n8)
    p = each(lambda x, y: x + _mm(y, x), p, m)
    m = each(_mm, m, m)
    p = each(lambda x, y: x + _mm(y, x), p, m)
    for merge in inv_masks[1:]:
        n_off = each(lambda x: jnp.where(merge, x, 0.0), n)
        p = each(lambda x, y: x + _mm(_mm(x, y), x), p, n_off)
    w = each(_mm, p, a_s)
    u0 = each(_mm, p, lv)
    wr = each(lambda x, y, s: _mm_nt(jnp.concatenate([x, y], axis=0), s), w, r_s, S)
    uv = each(lambda x, y, z: jnp.concatenate([x[:2 * T] + y, z], axis=0), wr, u0, v_s)
    y_s = each(lambda x, p1, p2, z: x[2 * T:] + _mm(jnp.concatenate([p1, p2], axis=1), z), wr, m_rb, m_rk, uv)
    y = each(lambda x: x[:T] + x[T:], y_s)
    S_new = each(lambda s, g, z, p1, p2: s * g + _mm_tn(z, jnp.concatenate([p1, p2], axis=0)),
                 S, g_T, uv, bh_s, kh_s)
    return list(zip(y, S_new))


def _rwkv_consts():
    T = RW_T
    row = lax.broadcasted_iota(jnp.int32, (T, T), 0)
    col = lax.broadcasted_iota(jnp.int32, (T, T), 1)
    cum_f = (row >= col).astype(F32)
    cum_b = (row <= col).astype(F32)
    head_lo = lax.broadcasted_iota(jnp.int32, (T, RW_P), 1) < RW_P // 2
    r2 = lax.broadcasted_iota(jnp.int32, (2 * T, 2 * T), 0)
    c2 = lax.broadcasted_iota(jnp.int32, (2 * T, 2 * T), 1)
    same = (r2 >= T) == (c2 >= T)
    eye_bd = (r2 == c2).astype(F32)
    fwd = (cum_f, head_lo, same & (r2 > c2), same & (r2 >= c2), eye_bd)
    bwd = (cum_b, head_lo, same & (r2 < c2), same & (r2 <= c2), eye_bd)
    blk = lambda s: (r2 // s) == (c2 // s)
    inv_masks = [blk(RW_INV_BASE)]
    s = RW_INV_BASE
    while s < T:
        inv_masks.append(blk(2 * s) & jnp.logical_not(blk(s)))
        s *= 2
    return fwd, bwd, inv_masks


def _rwkv_kernel(rf, vf, af, lwf, kf, bf, rb, vb, ab, lwb, kb, bb, s0_ref, yf_ref, yb_ref, sT_ref, s_scr, *, nsub):
    j = pl.program_id(2)

    @pl.when(j == 0)
    def _():
        s_scr[...] = s0_ref[:, 0]

    T = RW_T
    cf, cb, inv_masks = _rwkv_consts()

    def body(i, carry):
        sf = pl.ds(pl.multiple_of(i * T, T), T)
        sb = pl.ds(pl.multiple_of((nsub - 1 - i) * T, T), T)
        lanes = [slice(q * RW_P, (q + 1) * RW_P) for q in range(RW_PAIRS_PER_STEP)]
        ins = []
        for ln in lanes:
            ins.append(tuple(ref[0, sf, ln].astype(F32) for ref in (rf, lwf, kf, vf, af, bf)))
            ins.append(tuple(ref[0, sb, ln].astype(F32) for ref in (rb, lwb, kb, vb, ab, bb)))
        res = _rwkv_pair_chunks([(carry[c], ins[c], bool(c % 2), cb if c % 2 else cf)
                                 for c in range(2 * RW_PAIRS_PER_STEP)], inv_masks)
        for q, ln in enumerate(lanes):
            yf_ref[0, sf, ln] = res[2 * q][0]
            yb_ref[0, sb, ln] = res[2 * q + 1][0]
        return tuple(s for _, s in res)

    init = tuple(s_scr[d, q] for q in range(RW_PAIRS_PER_STEP) for d in range(2))
    fin = lax.fori_loop(0, nsub, body, init)
    for q in range(RW_PAIRS_PER_STEP):
        for d in range(2):
            s_scr[d, q] = fin[2 * q + d]

    @pl.when(j == pl.num_programs(2) - 1)
    def _():
        for q in range(RW_PAIRS_PER_STEP):
            for d in range(2):
                sT_ref[d, 0, q] = fin[2 * q + d]


def rwkv_scan_pallas(r, v, kk, lw0, lw1, kd0, kd1, bd0, bd1, s0, *, tb=256):
    B, n, C = r.shape
    tb = min(tb, n)
    nb = n // tb
    pp = RW_PAIRS_PER_STEP
    assert n % tb == 0 and tb % RW_T == 0 and C % (pp * RW_P) == 0
    fw = pl.BlockSpec((1, tb, pp * RW_P), lambda b, p, j: (b, j, p))
    bw = pl.BlockSpec((1, tb, pp * RW_P), lambda b, p, j: (b, nb - 1 - j, p))
    st = pl.BlockSpec((2, 1, pp, RW_P, RW_P), lambda b, p, j: (0, b, p, 0, 0))
    return pl.pallas_call(
        functools.partial(_rwkv_kernel, nsub=tb // RW_T),
        grid=(B, C // (pp * RW_P), nb),
        in_specs=[fw] * 6 + [bw] * 6 + [st],
        out_specs=[fw, bw, st],
        out_shape=[jax.ShapeDtypeStruct((B, n, C), F32), jax.ShapeDtypeStruct((B, n, C), F32),
                   jax.ShapeDtypeStruct(s0.shape, F32)],
        scratch_shapes=[pltpu.VMEM((2, pp, RW_P, RW_P), F32)],
        compiler_params=pltpu.CompilerParams(dimension_semantics=("parallel", "parallel", "arbitrary")),
        name="rwkv_scan",
    )(r, v, kk, lw0, kd0, bd0, r, v, kk, lw1, kd1, bd1, s0)


FLASH_STREAMS = 2


def _flash_kernel(q_ref, k_ref, v_ref, o_ref, m_scr, acc_scr):
    j = pl.program_id(3)

    @pl.when(j == 0)
    def _():
        m_scr[...] = jnp.full(m_scr.shape, -jnp.inf, F32)
        acc_scr[...] = jnp.zeros(acc_scr.shape, F32)

    tq = q_ref.shape[1]
    tk = k_ref.shape[1]
    k = k_ref[0]
    ones_col = (lax.broadcasted_iota(jnp.int32, (tk, ATT_DV), 1) == 0).astype(BF16)
    v_aug = jnp.concatenate([v_ref[0], ones_col], axis=1)
    rows = [slice(i * tq // FLASH_STREAMS, (i + 1) * tq // FLASH_STREAMS) for i in range(FLASH_STREAMS)]
    s = [lax.dot_general(q_ref[0, r, :], k, (((1,), (1,)), ((), ())), preferred_element_type=F32) for r in rows]
    m_prev = [m_scr[r, :] for r in rows]
    m_new = [jnp.maximum(mp, jnp.max(x, axis=-1, keepdims=True)) for mp, x in zip(m_prev, s)]
    p = [jnp.exp((x - mn).astype(BF16)) for x, mn in zip(s, m_new)]
    pv = [jnp.dot(x, v_aug, preferred_element_type=F32) for x in p]
    for r, mp, mn, x in zip(rows, m_prev, m_new, pv):
        acc_scr[r, :] = jnp.exp(mp - mn) * acc_scr[r, :] + x
        m_scr[r, :] = mn

    @pl.when(j == pl.num_programs(3) - 1)
    def _():
        acc = acc_scr[...]
        o_ref[0] = (acc[:, :ATT_DV] / acc[:, ATT_DV:ATT_DV + 1]).astype(o_ref.dtype)


def flash_attention(q, k, v, *, tq, tk, out_dtype=F32):
    B, nq, hq = q.shape
    nk = k.shape[1]
    H = hq // ATT_DK
    assert nq % tq == 0 and nk % tk == 0
    return pl.pallas_call(
        _flash_kernel,
        grid=(B, H, nq // tq, nk // tk),
        in_specs=[pl.BlockSpec((1, tq, ATT_DK), lambda b, h, i, j: (b, i, h)),
                  pl.BlockSpec((1, tk, ATT_DK), lambda b, h, i, j: (b, j, h)),
                  pl.BlockSpec((1, tk, ATT_DV), lambda b, h, i, j: (b, j, h))],
        out_specs=pl.BlockSpec((1, tq, ATT_DV), lambda b, h, i, j: (b, i, h)),
        out_shape=jax.ShapeDtypeStruct((B, nq, H * ATT_DV), out_dtype),
        scratch_shapes=[pltpu.VMEM((tq, 1), F32), pltpu.VMEM((tq, 2 * ATT_DV), F32)],
        compiler_params=pltpu.CompilerParams(
            dimension_semantics=("parallel", "parallel", "parallel", "arbitrary")),
        name="mla_flash",
    )(q, k, v)


PROJ = {}
_off = 0
for _name, _w in (('gate', 4 * D_MODEL), ('hy_x0', HY_WIDTH), ('hy_x1', HY_WIDTH), ('hy_v', HY_WIDTH),
                  ('rw_r', RW_WIDTH), ('rw_k', RW_WIDTH), ('rw_v', RW_WIDTH), ('ssm_z', SSM_INNER),
                  ('ssm_x', SSM_INNER), ('mla_cq', MLA_Q_LORA), ('mla_ckv', MLA_KV_LORA),
                  ('ssm_b', 256), ('ssm_c', 256), ('rw_gd', 256), ('rw_wd', 128), ('rw_ad', 128),
                  ('mla_kpe', 128), ('mla_kpe_sw', 128), ('ssm_dt', 128)):
    assert _off % _w == 0, (_name, _off, _w)
    PROJ[_name] = (_off, _w)
    _off += _w
PROJ_TN = 512
PROJ_N = -(-_off // PROJ_TN) * PROJ_TN


def _rope_swap_matrix():
    p = np.zeros((QK_ROPE, QK_ROPE), np.float32)
    q = QK_ROPE // 4
    for g in range(2):
        for i in range(q):
            p[g * 2 * q + q + i, g * 2 * q + i] = -1.0
            p[g * 2 * q + i, g * 2 * q + q + i] = 1.0
    return jnp.asarray(p)


def build_w_in(w_in):
    cuts = np.cumsum([0, MLA_COLS, SSM_COLS, HY_COLS, RW_COLS, 4 * D_MODEL]).tolist()
    mla, ssm, hy, rw, gate = (w_in[:, cuts[i]:cuts[i + 1]] for i in range(5))
    kpe = mla[:, MLA_Q_LORA + MLA_KV_LORA:]
    src = {
        'gate': gate, 'hy_x0': hy[:, :HY_WIDTH], 'hy_x1': hy[:, HY_WIDTH:2 * HY_WIDTH], 'hy_v': hy[:, 2 * HY_WIDTH:],
        'rw_r': rw[:, :RW_WIDTH], 'rw_k': rw[:, RW_WIDTH:2 * RW_WIDTH], 'rw_v': rw[:, 2 * RW_WIDTH:3 * RW_WIDTH],
        'rw_wd': rw[:, 3 * RW_WIDTH:3 * RW_WIDTH + 128], 'rw_ad': rw[:, 3 * RW_WIDTH + 128:3 * RW_WIDTH + 256],
        'rw_gd': rw[:, 3 * RW_WIDTH + 256:],
        'ssm_z': ssm[:, :SSM_INNER], 'ssm_x': ssm[:, SSM_INNER:2 * SSM_INNER],
        'ssm_b': ssm[:, 2 * SSM_INNER:2 * SSM_INNER + 256], 'ssm_c': ssm[:, 2 * SSM_INNER + 256:2 * SSM_INNER + 512],
        'ssm_dt': ssm[:, 2 * SSM_INNER + 512:],
        'mla_cq': mla[:, :MLA_Q_LORA], 'mla_ckv': mla[:, MLA_Q_LORA:MLA_Q_LORA + MLA_KV_LORA],
        'mla_kpe': kpe, 'mla_kpe_sw': kpe @ _rope_swap_matrix(),
    }
    cols, pos = [], 0
    for name, (off, w) in PROJ.items():
        assert off == pos
        piece = src[name]
        cols.append(jnp.pad(piece, ((0, 0), (0, w - piece.shape[1]))))
        pos += w
    cols.append(jnp.zeros((w_in.shape[0], PROJ_N - pos), w_in.dtype))
    return jnp.concatenate(cols, 1).astype(BF16)


def _inproj_kernel(x_ref, sh_ref, sc_ref, w_ref, o_ref, h_scr):
    @pl.when(pl.program_id(1) == 0)
    def _():
        h_scr[...] = (_ln(x_ref[...]) * (1.0 + sc_ref[0]) + sh_ref[0]).astype(BF16)

    o_ref[...] = jnp.dot(h_scr[...], w_ref[...], preferred_element_type=F32).astype(o_ref.dtype)


def inproj(x2d, shift, scale, w_p, rows_per_mod):
    R, D = x2d.shape
    tm = min(TM, R)
    bpm = max(rows_per_mod // tm, 1)
    mod = pl.BlockSpec((1, 1, D), lambda i, j: (i // bpm, 0, 0))
    return pl.pallas_call(
        _inproj_kernel,
        grid=(R // tm, PROJ_N // PROJ_TN),
        in_specs=[pl.BlockSpec((tm, D), lambda i, j: (i, 0)), mod, mod,
                  pl.BlockSpec((D, PROJ_TN), lambda i, j: (0, j))],
        out_specs=pl.BlockSpec((tm, PROJ_TN), lambda i, j: (i, j)),
        out_shape=jax.ShapeDtypeStruct((R, PROJ_N), BF16),
        scratch_shapes=[pltpu.VMEM((tm, D), BF16)],
        compiler_params=pltpu.CompilerParams(dimension_semantics=("parallel", "arbitrary"),
                                             vmem_limit_bytes=VMEM_LIMIT),
        name="inproj",
    )(x2d, shift, scale, w_p)


def build_mla_weights(p):
    wq = p['mla_w_q_up'].reshape(MLA_Q_LORA, MLA_HEADS, QK_NOPE + QK_ROPE)
    wkv = p['mla_w_kv_up'].reshape(MLA_KV_LORA, MLA_HEADS, QK_NOPE + V_HEAD)
    wq_p = jnp.pad(wq, ((0, 0), (0, 0), (0, ATT_DK - QK_NOPE - QK_ROPE))).reshape(MLA_Q_LORA, MLA_HEADS * ATT_DK)
    wsw = jnp.einsum('chd,de->che', wq[:, :, QK_NOPE:], _rope_swap_matrix())
    wsw_p = jnp.pad(wsw, ((0, 0), (0, 0), (0, 128 - QK_ROPE))).reshape(MLA_Q_LORA, MLA_HEADS * 128)
    wkn = wkv[:, :, :QK_NOPE].reshape(MLA_KV_LORA, MLA_HEADS * QK_NOPE)
    wv = wkv[:, :, QK_NOPE:].reshape(MLA_KV_LORA, MLA_HEADS * V_HEAD)
    return wq_p.astype(BF16), wsw_p.astype(BF16), wkn.astype(BF16), wv.astype(BF16)


def axial_rope_tables(n_tokens):
    rows = n_tokens // GRID_W
    row = jnp.repeat(jnp.arange(rows), GRID_W).astype(F32)
    col = jnp.tile(jnp.arange(GRID_W), rows).astype(F32)
    half = QK_ROPE // 2
    inv = ROPE_BASE ** (-jnp.arange(0, half, 2, dtype=F32) / half)
    ang = jnp.stack([row[:, None] * inv, col[:, None] * inv], 1)
    return jnp.cos(ang), jnp.sin(ang)


def rope_tables_128(n_tokens):
    cos, sin = axial_rope_tables(n_tokens)
    full = lambda t: jnp.concatenate([t[:, 0], t[:, 0], t[:, 1], t[:, 1]], -1)
    pad = lambda t: jnp.pad(t, ((0, 0), (0, 128 - QK_ROPE)))
    return pad(full(cos)), pad(full(sin))


def _mla_proj_kernel(cq_ref, ckv_ref, kpe_ref, ksw_ref, cos_ref, sin_ref, gq_ref, gkv_ref, wq_ref, wsw_ref, wkn_ref,
                     wv_ref, q_ref, k_ref, v_ref, *, q_scale):
    def rms(x, g):
        return (x * lax.rsqrt(jnp.mean(x * x, -1, keepdims=True) + 1e-6) * g).astype(BF16)

    cqn = rms(cq_ref[...].astype(F32), gq_ref[...])
    ckvn = rms(ckv_ref[...].astype(F32), gkv_ref[...])
    cos = cos_ref[...]
    sin = sin_ref[...]
    q = jnp.dot(cqn, wq_ref[...], preferred_element_type=F32)
    qsw = jnp.dot(cqn, wsw_ref[...], preferred_element_type=F32)
    kn = jnp.dot(ckvn, wkn_ref[...], preferred_element_type=F32)
    v_ref[...] = jnp.dot(ckvn, wv_ref[...], preferred_element_type=F32).astype(v_ref.dtype)
    kpe = (kpe_ref[...].astype(F32) * cos + ksw_ref[...].astype(F32) * sin).astype(k_ref.dtype)
    for h in range(MLA_HEADS):
        lo = h * ATT_DK
        q_ref[:, lo:lo + 128] = (q[:, lo:lo + 128] * q_scale).astype(q_ref.dtype)
        pe = q[:, lo + 128:lo + 256] * cos + qsw[:, h * 128:(h + 1) * 128] * sin
        q_ref[:, lo + 128:lo + 256] = (pe * q_scale).astype(q_ref.dtype)
        k_ref[:, lo:lo + 128] = kn[:, h * 128:(h + 1) * 128].astype(k_ref.dtype)
        k_ref[:, lo + 128:lo + 256] = kpe


def mla_proj(proj, cos, sin, gq, gkv, weights):
    R = proj.shape[0]
    tm = min(TM, R)
    wq, wsw, wkn, wv = weights
    col = lambda name: pl.BlockSpec((tm, PROJ[name][1]), lambda i, o=PROJ[name][0] // PROJ[name][1]: (i, o))
    row = lambda w: pl.BlockSpec((tm, w), lambda i: (i, 0))
    full = lambda a: pl.BlockSpec(a.shape, lambda i: (0,) * a.ndim)
    gq2, gkv2 = gq.reshape(1, -1), gkv.reshape(1, -1)
    return pl.pallas_call(
        functools.partial(_mla_proj_kernel, q_scale=float((QK_NOPE + QK_ROPE) ** -0.5)),
        grid=(R // tm,),
        in_specs=[col('mla_cq'), col('mla_ckv'), col('mla_kpe'), col('mla_kpe_sw'), row(128), row(128),
                  full(gq2), full(gkv2), full(wq), full(wsw), full(wkn), full(wv)],
        out_specs=[row(MLA_HEADS * ATT_DK), row(MLA_HEADS * ATT_DK), row(MLA_HEADS * ATT_DV)],
        out_shape=[jax.ShapeDtypeStruct((R, MLA_HEADS * ATT_DK), BF16),
                   jax.ShapeDtypeStruct((R, MLA_HEADS * ATT_DK), BF16),
                   jax.ShapeDtypeStruct((R, MLA_HEADS * ATT_DV), BF16)],
        compiler_params=pltpu.CompilerParams(dimension_semantics=("parallel",), vmem_limit_bytes=VMEM_LIMIT),
        name="mla_proj",
    )(proj, proj, proj, proj, cos, sin, gq2, gkv2, wq, wsw, wkn, wv)


MERGE_TN = 512


def _merge_kernel(b0, b1, b2, b3, g0, g1, g2, g3, w_ref, o_ref):
    acc = None
    for i, (b, g) in enumerate(((b0, g0), (b1, g1), (b2, g2), (b3, g3))):
        t = jax.nn.sigmoid(g[...].astype(F32)) * jnp.dot(b[...], w_ref[i], preferred_element_type=F32)
        acc = t if acc is None else acc + t
    o_ref[...] = acc.astype(o_ref.dtype)


def merge_branches_pallas(branches, proj, w_branch):
    R = proj.shape[0]
    tm = min(TM, R)
    nj = D_MODEL // MERGE_TN
    g0 = PROJ['gate'][0] // MERGE_TN
    br = pl.BlockSpec((tm, BRANCH_WIDTH), lambda i, j: (i, 0))
    gates = [pl.BlockSpec((tm, MERGE_TN), lambda i, j, k=k: (i, g0 + k * nj + j)) for k in range(N_BRANCH)]
    return pl.pallas_call(
        _merge_kernel,
        grid=(R // tm, nj),
        in_specs=[br] * 4 + gates + [pl.BlockSpec((N_BRANCH, BRANCH_WIDTH, MERGE_TN), lambda i, j: (0, 0, j))],
        out_specs=pl.BlockSpec((tm, MERGE_TN), lambda i, j: (i, j)),
        out_shape=jax.ShapeDtypeStruct((R, D_MODEL), BF16),
        compiler_params=pltpu.CompilerParams(dimension_semantics=("parallel", "arbitrary"),
                                             vmem_limit_bytes=VMEM_LIMIT),
        name="branch_merge",
    )(*branches, proj, proj, proj, proj, w_branch)


ROUTER_PAD = 128


def _mm3_f32(a, b_hi, b_lo):
    ah = a.astype(BF16)
    al = (a - ah.astype(F32)).astype(BF16)
    d = lambda x, y: jnp.dot(x, y, preferred_element_type=F32)
    return d(ah, b_hi) + d(ah, b_lo) + d(al, b_hi)


def _outproj_kernel(m_ref, x_ref, g1_ref, sh2_ref, sc2_ref, lng_ref, lnb_ref, w_ref, wr_hi, wr_lo,
                    x_out, h2_out, logit_out):
    y = jnp.dot(m_ref[...], w_ref[...], preferred_element_type=F32)
    xn = _ln(ALPHA * x_ref[...] + g1_ref[0] * y) * lng_ref[...] + lnb_ref[...]
    x_out[...] = xn
    h2 = _ln(xn) * (1.0 + sc2_ref[0]) + sh2_ref[0]
    h2_out[...] = h2.astype(h2_out.dtype)
    logit_out[...] = _mm3_f32(h2, wr_hi[...], wr_lo[...])


def outproj_postnorm(merged, x2d, g1, sh2, sc2, ln_g, ln_b, w_out, w_router, rows_per_mod):
    R, D = x2d.shape
    tm = min(TM, R)
    bpm = max(rows_per_mod // tm, 1)
    mod = pl.BlockSpec((1, 1, D), lambda i: (i // bpm, 0, 0))
    row = lambda w: pl.BlockSpec((tm, w), lambda i: (i, 0))
    full = lambda a: pl.BlockSpec(a.shape, lambda i: (0,) * a.ndim)
    wr = jnp.pad(w_router, ((0, 0), (0, ROUTER_PAD - w_router.shape[1])))
    wr_hi = wr.astype(BF16)
    wr_lo = (wr - wr_hi.astype(F32)).astype(BF16)
    lg, lb = ln_g.reshape(1, D), ln_b.reshape(1, D)
    return pl.pallas_call(
        _outproj_kernel,
        grid=(R // tm,),
        in_specs=[row(D), row(D), mod, mod, mod, full(lg), full(lb), full(w_out), full(wr_hi), full(wr_lo)],
        out_specs=[row(D), row(D), row(ROUTER_PAD)],
        out_shape=[jax.ShapeDtypeStruct((R, D), F32), jax.ShapeDtypeStruct((R, D), BF16),
                   jax.ShapeDtypeStruct((R, ROUTER_PAD), F32)],
        compiler_params=pltpu.CompilerParams(dimension_semantics=("parallel",), vmem_limit_bytes=VMEM_LIMIT),
        name="outproj_postnorm",
    )(merged, x2d, g1, sh2, sc2, lg, lb, w_out, wr_hi, wr_lo)


def _postnorm_kernel(x_ref, y_ref, g_ref, lng_ref, lnb_ref, o_ref):
    o_ref[...] = _ln(ALPHA * x_ref[...] + g_ref[0] * y_ref[...]) * lng_ref[...] + lnb_ref[...]


def postnorm(x2d, y2d, g, ln_g, ln_b, rows_per_mod):
    R, D = x2d.shape
    tm = min(TM, R)
    bpm = max(rows_per_mod // tm, 1)
    row = pl.BlockSpec((tm, D), lambda i: (i, 0))
    vec = pl.BlockSpec((1, D), lambda i: (0, 0))
    return pl.pallas_call(
        _postnorm_kernel,
        grid=(R // tm,),
        in_specs=[row, row, pl.BlockSpec((1, 1, D), lambda i: (i // bpm, 0, 0)), vec, vec],
        out_specs=row,
        out_shape=jax.ShapeDtypeStruct((R, D), F32),
        compiler_params=pltpu.CompilerParams(dimension_semantics=("parallel",), vmem_limit_bytes=VMEM_LIMIT),
        name="postnorm",
    )(x2d, y2d, g, ln_g.reshape(1, D), ln_b.reshape(1, D))


FF_PAD = 1536
FF_TILE = 512


def _expert_ffn_kernel(x_ref, gate_ref, wg_ref, wu_ref, wd_ref, o_ref):
    f = pl.program_id(2)
    x = x_ref[0, 0]
    hid = (jax.nn.silu(jnp.dot(x, wg_ref[0], preferred_element_type=F32))
           * jnp.dot(x, wu_ref[0], preferred_element_type=F32))
    part = jnp.dot(hid.astype(BF16), wd_ref[0], preferred_element_type=F32)

    @pl.when(f == 0)
    def _():
        o_ref[0, 0] = part

    @pl.when(f > 0)
    def _():
        o_ref[0, 0] += part

    @pl.when(f == pl.num_programs(2) - 1)
    def _():
        o_ref[0, 0] = o_ref[0, 0] * gate_ref[0, 0]


def expert_ffn(xs, gate, wg, wu, wd):
    S, E, C, D = xs.shape
    nf = FF_PAD // FF_TILE
    return pl.pallas_call(
        _expert_ffn_kernel,
        grid=(E, S, nf),
        in_specs=[pl.BlockSpec((1, 1, C, D), lambda e, s, f: (s, e, 0, 0)),
                  pl.BlockSpec((1, 1, C, 1), lambda e, s, f: (s, e, 0, 0)),
                  pl.BlockSpec((1, D, FF_TILE), lambda e, s, f: (e, 0, f)),
                  pl.BlockSpec((1, D, FF_TILE), lambda e, s, f: (e, 0, f)),
                  pl.BlockSpec((1, FF_TILE, D), lambda e, s, f: (e, f, 0))],
        out_specs=pl.BlockSpec((1, 1, C, D), lambda e, s, f: (s, e, 0, 0)),
        out_shape=jax.ShapeDtypeStruct((S, E, C, D), F32),
        compiler_params=pltpu.CompilerParams(dimension_semantics=("parallel", "parallel", "arbitrary"),
                                             vmem_limit_bytes=VMEM_LIMIT),
        name="expert_ffn",
    )(xs, gate, wg, wu, wd)


def build_expert_weights(p):
    padf = lambda w: jnp.pad(w, ((0, 0), (0, 0), (0, FF_PAD - EXPERT_FF))).astype(BF16)
    wd = jnp.pad(p['w_down_e'], ((0, 0), (0, FF_PAD - EXPERT_FF), (0, 0))).astype(BF16)
    return padf(p['w_gate_e']), padf(p['w_up_e']), wd


HALO = 8


def _halo_specs(tm, w, col_block, nrows):
    nb8 = nrows // HALO
    prev = pl.BlockSpec((HALO, w), lambda i: (jnp.maximum(i * (tm // HALO) - 1, 0), col_block))
    nxt = pl.BlockSpec((HALO, w), lambda i: (jnp.minimum((i + 1) * (tm // HALO), nb8 - 1), col_block))
    return prev, nxt


def _neighbours(cur, prev8, next8, seq_len):
    tm = cur.shape[0]
    row = lax.broadcasted_iota(jnp.int32, cur.shape, 0)
    g = row + pl.program_id(0) * tm
    before = jnp.where(row == 0, prev8[HALO - 1:HALO, :], pltpu.roll(cur, 1, 0))
    after = jnp.where(row == tm - 1, next8[0:1, :], pltpu.roll(cur, tm - 1, 0))
    pos = g & (seq_len - 1)
    return jnp.where(pos == 0, 0.0, before), jnp.where(pos == seq_len - 1, 0.0, after)


RW_PIECES = ('rw_r', 'rw_k', 'rw_v', 'rw_wd', 'rw_ad', 'rw_gd')


def _head_sum_matrix(scale=1.0):
    r = np.arange(RW_WIDTH)
    return jnp.asarray(((r[:, None] // RW_HEAD_DIM) == (r[None, :] // RW_HEAD_DIM)).astype(np.float32) * scale, BF16)


def _seg_sum(x, m):
    xh = x.astype(BF16)
    xl = (x - xh.astype(F32)).astype(BF16)
    return jnp.dot(xh, m, preferred_element_type=F32) + jnp.dot(xl, m, preferred_element_type=F32)


def _rwkv_prep_kernel(*refs, seq_len):
    cur = refs[0:6]
    prv = refs[6:12]
    nxt = refs[12:18]
    mu = refs[18:24]
    (w0_ref, a0_ref, wup_ref, aup_ref, gup_ref, kkw_ref, ka_ref, rk_ref, hs_ref) = refs[24:33]
    (r_o, v_o, kk_o, lw0_o, lw1_o, kd0_o, kd1_o, bd0_o, bd1_o, g_o, bonus_o) = refs[33:]
    us = []
    for c, p, n, m in zip(cur, prv, nxt, mu):
        x = c[...].astype(F32)
        before, after = _neighbours(x, p[...].astype(F32), n[...].astype(F32), seq_len)
        us.append(x + m[0:1, :] * (before - x) + m[1:2, :] * (after - x))
    r, k, v, wd, ad, gd = us
    hs = hs_ref[...]
    g_o[...] = jnp.dot(jax.nn.sigmoid(gd).astype(BF16), gup_ref[...], preferred_element_type=F32).astype(g_o.dtype)
    kk = k * kkw_ref[...]
    kk = kk / jnp.maximum(jnp.sqrt(_seg_sum(kk * kk, hs)), 1e-12)
    wl = jnp.dot(jnp.tanh(wd).astype(BF16), wup_ref[...], preferred_element_type=F32) + w0_ref[...]
    al = jnp.dot(ad.astype(BF16), aup_ref[...], preferred_element_type=F32) + a0_ref[...]
    r_o[...] = r.astype(r_o.dtype)
    v_o[...] = v.astype(v_o.dtype)
    kk_o[...] = kk.astype(kk_o.dtype)
    rk = rk_ref[...]
    acc = None
    for d, (lw_o, kd_o, bd_o) in enumerate(((lw0_o, kd0_o, bd0_o), (lw1_o, kd1_o, bd1_o))):
        sl = slice(d * RW_WIDTH, (d + 1) * RW_WIDTH)
        w_log = -jax.nn.softplus(-wl[:, sl]) - 0.5
        lw_o[...] = -jnp.exp(w_log)
        iclr = jax.nn.sigmoid(al[:, sl])
        kd = k * (1.0 + (iclr - 1.0) * ka_ref[...])
        kd_o[...] = kd.astype(kd_o.dtype)
        bd_o[...] = (kk * iclr).astype(bd_o.dtype)
        t = r * kd * rk
        acc = t if acc is None else acc + t
    bonus_o[...] = (_seg_sum(acc, hs) * v).astype(bonus_o.dtype)


def build_rwkv_weights(p):
    W = RW_WIDTH
    mu = p['rw_mu']
    cuts = np.cumsum([0, W, W, W, 128, 128, RW_GATE_LORA]).tolist()
    mus = [mu[:, cuts[i]:cuts[i + 1]] for i in range(6)]
    mus[5] = jnp.pad(mus[5], ((0, 0), (0, 256 - RW_GATE_LORA)))
    z = jnp.zeros((RW_DECAY_LORA, W), F32)
    wup = jnp.concatenate([jnp.concatenate([p['rw_w_up'][0], z], 1), jnp.concatenate([z, p['rw_w_up'][1]], 1)], 0)
    aup = jnp.concatenate([jnp.concatenate([p['rw_a_up'][0], z], 1), jnp.concatenate([z, p['rw_a_up'][1]], 1)], 0)
    gup = jnp.pad(p['rw_g_up'], ((0, 256 - RW_GATE_LORA), (0, 0)))
    return dict(mu=mus, w0=p['rw_w0'].reshape(1, 2 * W), a0=p['rw_a0'].reshape(1, 2 * W), wup=wup.astype(BF16),
                aup=aup.astype(BF16), gup=gup.astype(BF16), kkw=p['rw_kk'].reshape(1, W), ka=p['rw_ka'].reshape(1, W),
                rk=p['rw_rk'].reshape(1, W), hs=_head_sum_matrix(), hmean=_head_sum_matrix(1.0 / RW_HEAD_DIM),
                ln_g=p['rw_ln_g'].reshape(1, W), ln_b=p['rw_ln_b'].reshape(1, W))


def rwkv_prep_pallas(proj, wts, seq_len):
    R = proj.shape[0]
    tm = min(TM, R)
    W = RW_WIDTH
    cur, prv, nxt = [], [], []
    for name in RW_PIECES:
        off, w = PROJ[name]
        cur.append(pl.BlockSpec((tm, w), lambda i, o=off // w: (i, o)))
        a, b = _halo_specs(tm, w, off // w, R)
        prv.append(a)
        nxt.append(b)
    full = lambda a: pl.BlockSpec(a.shape, lambda i: (0,) * a.ndim)
    row = pl.BlockSpec((tm, W), lambda i: (i, 0))
    consts = [wts[k] for k in ('w0', 'a0', 'wup', 'aup', 'gup', 'kkw', 'ka', 'rk', 'hs')]
    bf = jax.ShapeDtypeStruct((R, W), BF16)
    f32 = jax.ShapeDtypeStruct((R, W), F32)
    return pl.pallas_call(
        functools.partial(_rwkv_prep_kernel, seq_len=seq_len),
        grid=(R // tm,),
        in_specs=cur + prv + nxt + [full(m) for m in wts['mu']] + [full(c) for c in consts],
        out_specs=[row] * 11,
        out_shape=[bf, bf, bf, f32, f32, bf, bf, bf, bf, bf, bf],
        compiler_params=pltpu.CompilerParams(dimension_semantics=("parallel",), vmem_limit_bytes=VMEM_LIMIT),
        name="rwkv_prep",
    )(*([proj] * 18), *wts['mu'], *consts)


def _rwkv_out_kernel(yf_ref, yb_ref, bonus_ref, g_ref, hm_ref, lng_ref, lnb_ref, o_ref):
    y = yf_ref[...] + yb_ref[...]
    hm = hm_ref[...]
    yc = y - _seg_sum(y, hm)
    yn = yc * lax.rsqrt(_seg_sum(yc * yc, hm) + RW_GN_EPS) * lng_ref[...] + lnb_ref[...]
    o_ref[...] = ((yn + bonus_ref[...].astype(F32)) * g_ref[...].astype(F32)).astype(o_ref.dtype)


def rwkv_out_pallas(yf, yb, bonus, g, wts):
    R, W = yf.shape
    tm = min(TM, R)
    row = pl.BlockSpec((tm, W), lambda i: (i, 0))
    full = lambda a: pl.BlockSpec(a.shape, lambda i: (0,) * a.ndim)
    return pl.pallas_call(
        _rwkv_out_kernel,
        grid=(R // tm,),
        in_specs=[row] * 4 + [full(wts['hmean']), full(wts['ln_g']), full(wts['ln_b'])],
        out_specs=row,
        out_shape=jax.ShapeDtypeStruct((R, W), BF16),
        compiler_params=pltpu.CompilerParams(dimension_semantics=("parallel",), vmem_limit_bytes=VMEM_LIMIT),
        name="rwkv_out",
    )(yf, yb, bonus, g, wts['hmean'], wts['ln_g'], wts['ln_b'])


SSD_T = SSM_CHUNK
SSD_GW = SSM_INNER // SSM_GROUPS
SSD_HG = SSM_HEADS // SSM_GROUPS
SSM_PIECES = ('ssm_x', 'ssm_b', 'ssm_c')


def _split_dot(x, m):
    xh = x.astype(BF16)
    xl = (x - xh.astype(F32)).astype(BF16)
    return jnp.dot(xh, m, preferred_element_type=F32) + jnp.dot(xl, m, preferred_element_type=F32)


def _ssd_prep_kernel(*refs, seq_len):
    cur, prv, nxt = refs[0:3], refs[3:6], refs[6:9]
    cw, cb = refs[9:12], refs[12:15]
    dt_ref, dtb_ref, perm_ref = refs[15:18]
    x_o, b_o, c_o, dt_o = refs[18:]
    for c, p, n, w, bias, o in zip(cur, prv, nxt, cw, cb, (x_o, b_o, c_o)):
        x = c[...].astype(F32)
        before, after = _neighbours(x, p[...].astype(F32), n[...].astype(F32), seq_len)
        y = before * w[0:1, :] + x * w[1:2, :] + after * w[2:3, :] + bias[...]
        o[...] = jax.nn.silu(y).astype(o.dtype)
    dt = jax.nn.softplus(dt_ref[...].astype(F32) + dtb_ref[...])
    dt_o[...] = _split_dot(dt, perm_ref[...])


def build_ssd_weights(p):
    w, bias = p['ssm_conv_w'], p['ssm_conv_b']
    cuts = [0, SSM_INNER, SSM_INNER + 256, SSM_INNER + 512]
    cw = [w[:, cuts[i]:cuts[i + 1]] for i in range(3)]
    cb = [bias[cuts[i]:cuts[i + 1]].reshape(1, -1) for i in range(3)]
    dtb = jnp.pad(p['ssm_dt_bias'].reshape(1, 2 * SSM_HEADS), ((0, 0), (0, 128 - 2 * SSM_HEADS)))
    perm = np.zeros((128, SSM_GROUPS * 128), np.float32)
    for d in range(2):
        for g in range(SSM_GROUPS):
            for j in range(SSD_HG):
                perm[d * SSM_HEADS + g * SSD_HG + j, g * 128 + d * SSD_HG + j] = 1.0
    a = -jnp.exp(p['ssm_a_log'].astype(F32))
    a_g = jnp.stack([jnp.pad(jnp.concatenate([a[0, g * SSD_HG:(g + 1) * SSD_HG], a[1, g * SSD_HG:(g + 1) * SSD_HG]]),
                             (0, 128 - 2 * SSD_HG)) for g in range(SSM_GROUPS)]).reshape(SSM_GROUPS, 1, 128)
    d_exp = jnp.repeat(p['ssm_d'], SSM_HEAD_DIM).reshape(1, SSM_INNER)
    return dict(cw=cw, cb=cb, dtb=dtb, perm=jnp.asarray(perm, BF16), a_g=a_g, d_exp=d_exp,
                norm=p['ssm_norm'].reshape(1, SSM_INNER))


def ssd_prep_pallas(proj, wts, seq_len):
    R = proj.shape[0]
    tm = min(TM, R)
    cur, prv, nxt = [], [], []
    for name in SSM_PIECES:
        off, w = PROJ[name]
        cur.append(pl.BlockSpec((tm, w), lambda i, o=off // w: (i, o)))
        a, b = _halo_specs(tm, w, off // w, R)
        prv.append(a)
        nxt.append(b)
    full = lambda a: pl.BlockSpec(a.shape, lambda i: (0,) * a.ndim)
    off, w = PROJ['ssm_dt']
    consts = wts['cw'] + wts['cb']
    row = lambda width: pl.BlockSpec((tm, width), lambda i: (i, 0))
    return pl.pallas_call(
        functools.partial(_ssd_prep_kernel, seq_len=seq_len),
        grid=(R // tm,),
        in_specs=cur + prv + nxt + [full(c) for c in consts]
        + [pl.BlockSpec((tm, w), lambda i: (i, off // w)), full(wts['dtb']), full(wts['perm'])],
        out_specs=[row(SSM_INNER), row(256), row(256), row(SSM_GROUPS * 128)],
        out_shape=[jax.ShapeDtypeStruct((R, SSM_INNER), BF16), jax.ShapeDtypeStruct((R, 256), BF16),
                   jax.ShapeDtypeStruct((R, 256), BF16), jax.ShapeDtypeStruct((R, SSM_GROUPS * 128), F32)],
        compiler_params=pltpu.CompilerParams(dimension_semantics=("parallel",), vmem_limit_bytes=VMEM_LIMIT),
        name="ssd_prep",
    )(*([proj] * 9), *consts, proj, wts['dtb'], wts['perm'])


def _ssd_chunks(chains, consts):
    T = SSD_T
    each = lambda f, *ls: [f(*xs) for xs in zip(*ls)]
    eye, cums, tris, expand, lane_head = consts
    S, xs, bm, cm, dt, a, dirs = (list(t) for t in zip(*chains))
    cum = [cums[d] for d in dirs]
    tri = [tris[d] for d in dirs]
    ex = [expand[d] for d in dirs]
    acs = each(lambda m, x, av: jnp.dot(m, x * av, precision=HI, preferred_element_type=F32), cum, dt, a)
    acs_t = each(lambda x: lax.dot_general(eye, x, (((1,), (1,)), ((), ())), precision=HI,
                                           preferred_element_type=F32), acs)
    dt_e = each(_split_dot, dt, ex)
    acs_e = each(_split_dot, acs, ex)
    atot_e = each(lambda x, d: x[0:1, :] if d else x[T - 1:T, :], acs_e, dirs)
    X = each(lambda x, y: x * y, xs, dt_e)
    cb = each(_mm_nt, cm, bm)
    y = []
    for c in range(len(chains)):
        parts = []
        for pr in range(SSD_HG // 2):
            acc = None
            for hh in range(2):
                i = dirs[c] * SSD_HG + 2 * pr + hh
                diff = acs[c][:, i:i + 1] - acs_t[c][i:i + 1, :]
                m = cb[c] * jnp.exp(jnp.where(tri[c], diff, -jnp.inf))
                xh = jnp.where(lane_head == hh, X[c][:, pr * 128:(pr + 1) * 128], 0.0)
                t = _mm(m, xh)
                acc = t if acc is None else acc + t
            parts.append(acc)
        y.append(jnp.concatenate(parts, axis=1))
    states = each(lambda b_, x, at, ac: _mm_tn(b_, x * jnp.exp(at - ac)), bm, X, atot_e, acs_e)
    y_off = each(lambda c_, s, ac: _mm(c_, s) * jnp.exp(ac), cm, S, acs_e)
    S_new = each(lambda s, at, st: s * jnp.exp(at) + st, S, atot_e, states)
    return [(yd + yo, sn) for yd, yo, sn in zip(y, y_off, S_new)]


def _ssd_consts():
    T = SSD_T
    row = lax.broadcasted_iota(jnp.int32, (T, T), 0)
    col = lax.broadcasted_iota(jnp.int32, (T, T), 1)
    eye = (row == col).astype(F32)
    cums = ((row >= col).astype(F32), (row <= col).astype(F32))
    tris = (row >= col, row <= col)
    er = lax.broadcasted_iota(jnp.int32, (128, SSD_GW), 0)
    ec = lax.broadcasted_iota(jnp.int32, (128, SSD_GW), 1)
    expand = tuple((er == d * SSD_HG + ec // SSM_HEAD_DIM).astype(BF16) for d in range(2))
    lane_head = lax.broadcasted_iota(jnp.int32, (T, 128), 1) // SSM_HEAD_DIM
    return eye, cums, tris, expand, lane_head


def _ssd_kernel(xf, bf, cf, dtf, xb, bb, cb_, dtb, a_ref, s0_ref, yf_ref, yb_ref, sT_ref, s_scr, *, nsub):
    j = pl.program_id(2)

    @pl.when(j == 0)
    def _():
        s_scr[...] = s0_ref[:, 0, 0]

    T = SSD_T
    consts = _ssd_consts()
    a = a_ref[0]

    def body(i, carry):
        sf = pl.ds(pl.multiple_of(i * T, T), T)
        sb = pl.ds(pl.multiple_of((nsub - 1 - i) * T, T), T)
        ld = lambda ref, sl: ref[0, sl, :].astype(F32)
        res = _ssd_chunks([(carry[0], ld(xf, sf), ld(bf, sf), ld(cf, sf), ld(dtf, sf), a, 0),
                           (carry[1], ld(xb, sb), ld(bb, sb), ld(cb_, sb), ld(dtb, sb), a, 1)], consts)
        yf_ref[0, sf, :] = res[0][0]
        yb_ref[0, sb, :] = res[1][0]
        return res[0][1], res[1][1]

    fin = lax.fori_loop(0, nsub, body, (s_scr[0], s_scr[1]))
    s_scr[0] = fin[0]
    s_scr[1] = fin[1]

    @pl.when(j == pl.num_programs(2) - 1)
    def _():
        sT_ref[0, 0, 0] = fin[0]
        sT_ref[1, 0, 0] = fin[1]


def ssd_scan_pallas(xs, bm, cm, dt, a_g, s0, *, tb=256):
    B, n, _ = xs.shape
    tb = min(tb, n)
    nb = n // tb
    G = SSM_GROUPS
    fwd = lambda w: pl.BlockSpec((1, tb, w), lambda b, g, j: (b, j, g))
    bwd = lambda w: pl.BlockSpec((1, tb, w), lambda b, g, j: (b, nb - 1 - j, g))
    st = pl.BlockSpec((2, 1, 1, SSM_STATE, SSD_GW), lambda b, g, j: (0, b, g, 0, 0))
    return pl.pallas_call(
        functools.partial(_ssd_kernel, nsub=tb // SSD_T),
        grid=(B, G, nb),
        in_specs=[fwd(SSD_GW), fwd(128), fwd(128), fwd(128), bwd(SSD_GW), bwd(128), bwd(128), bwd(128),
                  pl.BlockSpec((1, 1, 128), lambda b, g, j: (g, 0, 0)), st],
        out_specs=[fwd(SSD_GW), bwd(SSD_GW), st],
        out_shape=[jax.ShapeDtypeStruct((B, n, SSM_INNER), F32), jax.ShapeDtypeStruct((B, n, SSM_INNER), F32),
                   jax.ShapeDtypeStruct(s0.shape, F32)],
        scratch_shapes=[pltpu.VMEM((2, SSM_STATE, SSD_GW), F32)],
        compiler_params=pltpu.CompilerParams(dimension_semantics=("parallel", "parallel", "arbitrary"),
                                             vmem_limit_bytes=VMEM_LIMIT),
        name="ssd_scan",
    )(xs, bm, cm, dt, xs, bm, cm, dt, a_g, s0)


def _ssd_out_kernel(yf_ref, yb_ref, xs_ref, z_ref, d_ref, nw_ref, o_ref):
    y = (yf_ref[...] + yb_ref[...] + xs_ref[...].astype(F32) * d_ref[...]) * jax.nn.silu(z_ref[...].astype(F32))
    outs = []
    for g in range(SSM_GROUPS):
        yg = y[:, g * SSD_GW:(g + 1) * SSD_GW]
        outs.append(yg * lax.rsqrt(jnp.mean(yg * yg, -1, keepdims=True) + 1e-5))
    o_ref[...] = (jnp.concatenate(outs, axis=1) * nw_ref[...]).astype(o_ref.dtype)


def ssd_out_pallas(yf, yb, xs, proj, wts):
    R = yf.shape[0]
    tm = min(TM, R)
    row = pl.BlockSpec((tm, SSM_INNER), lambda i: (i, 0))
    vec = pl.BlockSpec((1, SSM_INNER), lambda i: (0, 0))
    zoff = PROJ['ssm_z'][0] // SSM_INNER
    return pl.pallas_call(
        _ssd_out_kernel,
        grid=(R // tm,),
        in_specs=[row, row, row, pl.BlockSpec((tm, SSM_INNER), lambda i: (i, zoff)), vec, vec],
        out_specs=row,
        out_shape=jax.ShapeDtypeStruct((R, SSM_INNER), BF16),
        compiler_params=pltpu.CompilerParams(dimension_semantics=("parallel",), vmem_limit_bytes=VMEM_LIMIT),
        name="ssd_out",
    )(yf, yb, xs, proj, wts['d_exp'], wts['norm'])


HY_PIECES = ('hy_x0', 'hy_x1', 'hy_v')
FFT_R = 128


def _hilo(x):
    hi = x.astype(BF16)
    return hi, (x - hi.astype(F32)).astype(BF16)


def _dot3(a_hi, a_lo, x):
    xh, xl = _hilo(x)
    d = lambda p, q: jnp.dot(p, q, preferred_element_type=F32)
    return d(a_hi, xh) + d(a_hi, xl) + d(a_lo, xh)


def _hilo_tables(pairs):
    out = {}
    for name, x in pairs:
        hi, lo = _hilo(jnp.asarray(x, F32))
        out[name + '_hi'], out[name + '_lo'] = hi, lo
    return out


def _hyena_prep_kernel(*refs, seq_len):
    cur, prv, nxt = refs[0:3], refs[3:6], refs[6:9]
    cw, cb = refs[9:12], refs[12:15]
    x0_o, v_o = refs[15:]
    ys = []
    for c, p, n, w, bias in zip(cur, prv, nxt, cw, cb):
        x = c[...].astype(F32)
        before, after = _neighbours(x, p[...].astype(F32), n[...].astype(F32), seq_len)
        ys.append(before * w[0:1, :] + x * w[1:2, :] + after * w[2:3, :] + bias[...])
    x0_o[...] = ys[0].astype(x0_o.dtype)
    v_o[...] = ys[2] * ys[1]


def hyena_prep_pallas(proj, cw, cb, seq_len):
    R = proj.shape[0]
    tm = min(TM, R)
    cur, prv, nxt = [], [], []
    for name in HY_PIECES:
        off, w = PROJ[name]
        cur.append(pl.BlockSpec((tm, w), lambda i, o=off // w: (i, o)))
        a, b = _halo_specs(tm, w, off // w, R)
        prv.append(a)
        nxt.append(b)
    full = lambda a: pl.BlockSpec(a.shape, lambda i: (0,) * a.ndim)
    row = pl.BlockSpec((tm, HY_WIDTH), lambda i: (i, 0))
    return pl.pallas_call(
        functools.partial(_hyena_prep_kernel, seq_len=seq_len),
        grid=(R // tm,),
        in_specs=cur + prv + nxt + [full(c) for c in cw + cb],
        out_specs=[row, row],
        out_shape=[jax.ShapeDtypeStruct((R, HY_WIDTH), BF16), jax.ShapeDtypeStruct((R, HY_WIDTH), F32)],
        compiler_params=pltpu.CompilerParams(dimension_semantics=("parallel",), vmem_limit_bytes=VMEM_LIMIT),
        name="hyena_prep",
    )(*([proj] * 9), *cw, *cb)


def filter_features(n):
    t = jnp.linspace(0.0, 1.0, n, dtype=F32)[:, None]
    wpos = 2.0 * math.pi * jnp.arange(n, dtype=F32)[:, None] / n
    f = jnp.linspace(1e-4, HY_BANDS - 1, HY_BANDS, dtype=F32)[None]
    z = jnp.concatenate([t, jnp.cos(f * wpos), -jnp.sin(f * wpos), t, jnp.ones_like(t)], -1)
    back = jnp.concatenate([jnp.zeros((1, z.shape[1]), F32), z[:0:-1]], 0)
    return jnp.pad(jnp.concatenate([z, back], 0), ((0, 0), (0, 128 - z.shape[1])))


def _filter_kernel(z_ref, w1_ref, b1_ref, w2_ref, b2_ref, fr_ref, w3_ref, dl_ref, o_ref):
    hp = lambda a, b: jnp.dot(a, b, precision=HI, preferred_element_type=F32)
    z = z_ref[...]
    fr = fr_ref[...]
    h = jnp.sin(fr * (hp(z, w1_ref[...]) + b1_ref[...]))
    h = jnp.sin(fr * (hp(h, w2_ref[...]) + b2_ref[...]))
    filt = hp(h, w3_ref[0])
    t = z[:, HY_EMB:HY_EMB + 1]
    o_ref[...] = filt * jnp.exp(-t * dl_ref[...]) * z[:, HY_EMB + 1:HY_EMB + 2]


def hyena_filter_pallas(n, p):
    z = filter_features(n)
    tr = min(512, n)
    nb = 2 * n // tr
    w1 = jnp.pad(p['hy_w1'], ((0, 128 - HY_EMB), (0, 0)))
    w3 = jnp.moveaxis(p['hy_w3'].reshape(HY_FILTER_HIDDEN, 2, HY_WIDTH), 1, 0)
    deltas = jnp.abs(jnp.linspace(HY_MIN_DECAY, HY_MAX_DECAY, HY_WIDTH, dtype=F32)).reshape(1, HY_WIDTH)
    vec = lambda a: a.reshape(1, -1)
    full = lambda a: pl.BlockSpec(a.shape, lambda i: (0,) * a.ndim)
    args = (w1, vec(p['hy_b1']), p['hy_w2'], vec(p['hy_b2']), vec(p['hy_freq']))
    return pl.pallas_call(
        _filter_kernel,
        grid=(nb,),
        in_specs=[pl.BlockSpec((tr, 128), lambda i: (i, 0))] + [full(a) for a in args]
        + [pl.BlockSpec((1, HY_FILTER_HIDDEN, HY_WIDTH), lambda i: (i // (nb // 2), 0, 0)), full(deltas)],
        out_specs=pl.BlockSpec((tr, HY_WIDTH), lambda i: (i, 0)),
        out_shape=jax.ShapeDtypeStruct((2 * n, HY_WIDTH), F32),
        compiler_params=pltpu.CompilerParams(dimension_semantics=("parallel",), vmem_limit_bytes=VMEM_LIMIT),
        name="hyena_filter",
    )(z, *args, w3, deltas)


FFT_TN = 4096
FFT_KB = 8
FFT_TC = 256


def dft_tables():
    r = np.arange(FFT_R, dtype=np.float64)
    th1 = 2.0 * np.pi * np.outer(r, r) / FFT_R
    k = r[:, None, None] + FFT_R * r[None, :, None]
    phi = 2.0 * np.pi * k * r[None, None, :] / (FFT_R * FFT_R)
    return _hilo_tables((('c1', np.cos(th1)), ('s1', np.sin(th1)), ('gc', np.cos(phi)), ('gs', np.sin(phi)),
                         ('gct', np.swapaxes(np.cos(phi), 1, 2)), ('gst', np.swapaxes(np.sin(phi), 1, 2))))


def _dft_l1_kernel(x_ref, ch, cl, sh, sl, ar_ref, ai_ref):
    x = x_ref[0]
    ar_ref[0] = _dot3(ch[...], cl[...], x)
    ai_ref[0] = -_dot3(sh[...], sl[...], x)


def dft_level1(x2, tabs):
    B, nb, L = x2.shape
    cut = lambda a: a[:, :nb]
    consts = [cut(tabs['c1_hi']), cut(tabs['c1_lo']), cut(tabs['s1_hi']), cut(tabs['s1_lo'])]
    full = lambda a: pl.BlockSpec(a.shape, lambda b, j: (0,) * a.ndim)
    out = pl.BlockSpec((1, FFT_R, FFT_TN), lambda b, j: (b, 0, j))
    return pl.pallas_call(
        _dft_l1_kernel,
        grid=(B, L // FFT_TN),
        in_specs=[pl.BlockSpec((1, nb, FFT_TN), lambda b, j: (b, 0, j))] + [full(c) for c in consts],
        out_specs=[out, out],
        out_shape=[jax.ShapeDtypeStruct((B, FFT_R, L), F32)] * 2,
        compiler_params=pltpu.CompilerParams(dimension_semantics=("parallel", "parallel"),
                                             vmem_limit_bytes=VMEM_LIMIT),
        name="dft_level1",
    )(x2, *consts)


def _level2_fwd(ar, ai, gch, gcl, gsh, gsl):
    xr = _dot3(gch, gcl, ar) + _dot3(gsh, gsl, ai)
    xi = _dot3(gch, gcl, ai) - _dot3(gsh, gsl, ar)
    return xr, xi


def _spectrum_kernel(ar_ref, ai_ref, gch, gcl, gsh, gsl, hr_ref, hi_ref):
    for q in range(FFT_KB):
        xr, xi = _level2_fwd(ar_ref[0, q], ai_ref[0, q], gch[q], gcl[q], gsh[q], gsl[q])
        hr_ref[0, q] = xr
        hi_ref[0, q] = xi


def _conv_l2_kernel(ar_ref, ai_ref, hr_ref, hi_ref, gch, gcl, gsh, gsl, tch, tcl, tsh, tsl, br_ref, bi_ref):
    for q in range(FFT_KB):
        xr, xi = _level2_fwd(ar_ref[0, q], ai_ref[0, q], gch[q], gcl[q], gsh[q], gsl[q])
        hr, hi = hr_ref[0, q], hi_ref[0, q]
        yr = xr * hr - xi * hi
        yi = xr * hi + xi * hr
        br_ref[0, q] = _dot3(tch[q], tcl[q], yr) - _dot3(tsh[q], tsl[q], yi)
        bi_ref[0, q] = _dot3(tch[q], tcl[q], yi) + _dot3(tsh[q], tsl[q], yr)


def _l2_specs():
    data = pl.BlockSpec((1, FFT_KB, FFT_R, FFT_TC), lambda b, c, k: (b, k, 0, c))
    filt = pl.BlockSpec((1, FFT_KB, FFT_R, FFT_TC), lambda b, c, k: (0, k, 0, c))
    tab = pl.BlockSpec((FFT_KB, FFT_R, FFT_R), lambda b, c, k: (k, 0, 0))
    return data, filt, tab


def dft_spectrum(ar, ai, tabs):
    B, _, _, C = ar.shape
    data, _, tab = _l2_specs()
    return pl.pallas_call(
        _spectrum_kernel,
        grid=(B, C // FFT_TC, FFT_R // FFT_KB),
        in_specs=[data, data] + [tab] * 4,
        out_specs=[data, data],
        out_shape=[jax.ShapeDtypeStruct(ar.shape, F32)] * 2,
        compiler_params=pltpu.CompilerParams(dimension_semantics=("parallel", "parallel", "parallel"),
                                             vmem_limit_bytes=VMEM_LIMIT),
        name="dft_spectrum",
    )(ar, ai, tabs['gc_hi'], tabs['gc_lo'], tabs['gs_hi'], tabs['gs_lo'])


def conv_level2(ar, ai, hr, hi, tabs):
    B, _, _, C = ar.shape
    data, filt, tab = _l2_specs()
    names = ('gc_hi', 'gc_lo', 'gs_hi', 'gs_lo', 'gct_hi', 'gct_lo', 'gst_hi', 'gst_lo')
    return pl.pallas_call(
        _conv_l2_kernel,
        grid=(B, C // FFT_TC, FFT_R // FFT_KB),
        in_specs=[data, data, filt, filt] + [tab] * 8,
        out_specs=[data, data],
        out_shape=[jax.ShapeDtypeStruct(ar.shape, F32)] * 2,
        compiler_params=pltpu.CompilerParams(dimension_semantics=("parallel", "parallel", "parallel"),
                                             vmem_limit_bytes=VMEM_LIMIT),
        name="conv_level2",
    )(ar, ai, hr, hi, *[tabs[k] for k in names])


def _idft_l1_kernel(br_ref, bi_ref, v_ref, x0_ref, d_ref, ch, cl, sh, sl, o_ref, *, inv_n):
    y = (_dot3(ch[...], cl[...], br_ref[0]) - _dot3(sh[...], sl[...], bi_ref[0])) * inv_n
    o_ref[0] = ((y + v_ref[0] * d_ref[...]) * x0_ref[0].astype(F32)).astype(o_ref.dtype)


def idft_level1_epilogue(br, bi, v2, x02, d_t, tabs):
    B, nb, L = v2.shape
    cut = lambda a: a[:nb, :]
    consts = [cut(tabs['c1_hi']), cut(tabs['c1_lo']), cut(tabs['s1_hi']), cut(tabs['s1_lo'])]
    full = lambda a: pl.BlockSpec(a.shape, lambda b, j: (0,) * a.ndim)
    big = pl.BlockSpec((1, FFT_R, FFT_TN), lambda b, j: (b, 0, j))
    half = pl.BlockSpec((1, nb, FFT_TN), lambda b, j: (b, 0, j))
    return pl.pallas_call(
        functools.partial(_idft_l1_kernel, inv_n=1.0 / (FFT_R * FFT_R)),
        grid=(B, L // FFT_TN),
        in_specs=[big, big, half, half, pl.BlockSpec((1, FFT_TN), lambda b, j: (0, j))] + [full(c) for c in consts],
        out_specs=half,
        out_shape=jax.ShapeDtypeStruct((B, nb, L), BF16),
        compiler_params=pltpu.CompilerParams(dimension_semantics=("parallel", "parallel"),
                                             vmem_limit_bytes=VMEM_LIMIT),
        name="idft_level1",
    )(br, bi, v2, x02, d_t, *consts)


def hyena_long_conv_fft(v, x0, kfull, hy_d, tabs):
    B, n, C = v.shape
    nb = n // FFT_R
    L = FFT_R * C
    kr, ki = dft_level1(kfull.reshape(1, 2 * nb, L), tabs)
    hr, hi = dft_spectrum(kr.reshape(1, FFT_R, FFT_R, C), ki.reshape(1, FFT_R, FFT_R, C), tabs)
    ar, ai = dft_level1(v.reshape(B, nb, L), tabs)
    br, bi = conv_level2(ar.reshape(B, FFT_R, FFT_R, C), ai.reshape(B, FFT_R, FFT_R, C), hr, hi, tabs)
    d_t = jnp.tile(hy_d.reshape(1, C), (1, FFT_R))
    out = idft_level1_epilogue(br.reshape(B, FFT_R, L), bi.reshape(B, FFT_R, L), v.reshape(B, nb, L),
                               x0.reshape(B, nb, L), d_t, tabs)
    return out.reshape(B, n, C)


def dense_tables(n):
    t = np.arange(2 * n, dtype=np.float64)
    th = 2.0 * np.pi * np.outer(t, t) / (2 * n)
    return _hilo_tables((('c', np.cos(th)), ('s', np.sin(th))))


def _dense_conv_kernel(v_ref, x0_ref, k_ref, d_ref, ch, cl, sh, sl, o_ref, *, n):
    c_hi, c_lo, s_hi, s_lo = ch[...], cl[...], sh[...], sl[...]
    k = k_ref[...]
    hr = _dot3(c_hi, c_lo, k)
    hi = -_dot3(s_hi, s_lo, k)
    v = v_ref[0]
    xr = _dot3(c_hi[:, :n], c_lo[:, :n], v)
    xi = -_dot3(s_hi[:, :n], s_lo[:, :n], v)
    yr = xr * hr - xi * hi
    yi = xr * hi + xi * hr
    y = (_dot3(c_hi[:n, :], c_lo[:n, :], yr) - _dot3(s_hi[:n, :], s_lo[:n, :], yi)) * (0.5 / n)
    o_ref[0] = ((y + v * d_ref[...]) * x0_ref[0].astype(F32)).astype(o_ref.dtype)


def hyena_long_conv_dense(v, x0, kfull, hy_d, tabs):
    B, n, C = v.shape
    tc = 256
    seq = pl.BlockSpec((1, n, tc), lambda b, c: (b, 0, c))
    full = lambda a: pl.BlockSpec(a.shape, lambda b, c: (0,) * a.ndim)
    consts = [tabs[k] for k in ('c_hi', 'c_lo', 's_hi', 's_lo')]
    return pl.pallas_call(
        functools.partial(_dense_conv_kernel, n=n),
        grid=(B, C // tc),
        in_specs=[seq, seq, pl.BlockSpec((2 * n, tc), lambda b, c: (0, c)), pl.BlockSpec((1, tc), lambda b, c: (0, c))]
        + [full(c) for c in consts],
        out_specs=seq,
        out_shape=jax.ShapeDtypeStruct((B, n, C), BF16),
        compiler_params=pltpu.CompilerParams(dimension_semantics=("parallel", "parallel"),
                                             vmem_limit_bytes=VMEM_LIMIT),
        name="hyena_dense_conv",
    )(v, x0, kfull, hy_d.reshape(1, C), *consts)


def build_hyena_weights(p):
    w, bias = p['hy_conv_w'], p['hy_conv_b']
    cw = [w[:, i * HY_WIDTH:(i + 1) * HY_WIDTH] for i in range(3)]
    cb = [bias[i * HY_WIDTH:(i + 1) * HY_WIDTH].reshape(1, -1) for i in range(3)]
    return cw, cb


def hyena_branch_pallas(proj, p, tabs, b, n):
    x0, v = hyena_prep_pallas(proj, p['hy_w'][0], p['hy_w'][1], n)
    kfull = hyena_filter_pallas(n, p)
    r3 = lambda a: a.reshape(b, n, HY_WIDTH)
    if 2 * n == FFT_R * FFT_R:
        y = hyena_long_conv_fft(r3(v), r3(x0), kfull, p['hy_d'], tabs['fft'])
    else:
        y = hyena_long_conv_dense(r3(v), r3(x0), kfull, p['hy_d'], dense_tables(n))
    return y.reshape(b * n, HY_WIDTH)


ADA_TN = 1024


def _ada_kernel(c_ref, w_ref, b_ref, o_ref):
    o_ref[...] = jnp.dot(jax.nn.silu(c_ref[...]).astype(BF16), w_ref[...].astype(BF16),
                         preferred_element_type=F32) + b_ref[...]


def ada_modulation(cond, w_ada, b_ada):
    m, d = cond.shape
    n = w_ada.shape[1]
    return pl.pallas_call(
        _ada_kernel,
        grid=(n // ADA_TN,),
        in_specs=[pl.BlockSpec((m, d), lambda j: (0, 0)), pl.BlockSpec((d, ADA_TN), lambda j: (0, j)),
                  pl.BlockSpec((1, ADA_TN), lambda j: (0, j))],
        out_specs=pl.BlockSpec((m, ADA_TN), lambda j: (0, j)),
        out_shape=jax.ShapeDtypeStruct((m, n), F32),
        compiler_params=pltpu.CompilerParams(dimension_semantics=("parallel",), vmem_limit_bytes=VMEM_LIMIT),
        name="ada_modulation",
    )(cond, w_ada, b_ada.reshape(1, n))


def expert_choice_ffn(h2, logits, ffn_w, b, n):
    cap = EC_CAPACITY * n // N_EXPERTS
    aff = jax.nn.softmax(logits[:, :N_EXPERTS].reshape(b, n, N_EXPERTS), -1)
    gate, idx = lax.top_k(jnp.swapaxes(aff, 1, 2), cap)
    bidx = jnp.arange(b)[:, None, None]
    xs = h2.reshape(b, n, D_MODEL)[bidx, idx]
    ye = expert_ffn(xs, gate[..., None], *ffn_w)
    return jnp.zeros((b, n, D_MODEL), F32).at[bidx, idx].add(ye).reshape(b * n, D_MODEL)


def _proj_piece(proj, name, width=None):
    off, w = PROJ[name]
    return proj[:, off:off + (width or w)].astype(F32)


def trunk_layer(xc, xl, mods, p, tabs, need_ctx, b, n_ctx, n_lat):
    D = D_MODEL
    piece = lambda rows, k: mods[rows, None, k * D:(k + 1) * D]
    lat = dict(x=xl, n=n_lat, rpm=n_lat, rows=slice(0, b), cos=tabs['cos_l'], sin=tabs['sin_l'])
    ctx = dict(x=xc, n=n_ctx, rpm=b * n_ctx, rows=slice(b, b + 1), cos=tabs['cos_c'], sin=tabs['sin_c'])
    for t in (ctx, lat):
        t['proj'] = inproj(t['x'], piece(t['rows'], 0), piece(t['rows'], 1), p['w_in_p'], t['rpm'])
        t['q'], t['k'], t['v'] = mla_proj(t['proj'], t['cos'], t['sin'], p['mla_q_norm'], p['mla_kv_norm'], p['mla_w'])
        t['rw'] = rwkv_prep_pallas(t['proj'], p['rw_w'], t['n'])
    sets = (ctx, lat) if need_ctx else (lat,)

    r3 = lambda a, n: a.reshape(b, n, a.shape[-1])
    k_all = jnp.concatenate([r3(ctx['k'], n_ctx), r3(lat['k'], n_lat)], 1)
    v_all = jnp.concatenate([r3(ctx['v'], n_ctx), r3(lat['v'], n_lat)], 1)
    nk = n_ctx + n_lat
    tk = 768 if nk % 768 == 0 else 256
    lat['mla'] = flash_attention(r3(lat['q'], n_lat), k_all, v_all, tq=min(n_lat, 1024), tk=tk, out_dtype=BF16)
    if need_ctx:
        ctx['mla'] = flash_attention(r3(ctx['q'], n_ctx), r3(ctx['k'], n_ctx), r3(ctx['v'], n_ctx), tq=n_ctx, tk=n_ctx,
                                     out_dtype=BF16)

    s0 = jnp.zeros((2, b, SSM_GROUPS, SSM_STATE, SSD_GW), F32)
    for t in (ctx, lat):
        xs, bm, cm, dt = ssd_prep_pallas(t['proj'], p['ssd_w'], t['n'])
        yf, yb, s0 = ssd_scan_pallas(r3(xs, t['n']), r3(bm, t['n']), r3(cm, t['n']), r3(dt, t['n']),
                                     p['ssd_w']['a_g'], s0)
        if t in sets:
            flat = lambda a: a.reshape(b * t['n'], SSM_INNER)
            t['ssm'] = ssd_out_pallas(flat(yf), flat(yb), xs, t['proj'], p['ssd_w'])

    for t in sets:
        t['hy'] = hyena_branch_pallas(t['proj'], p, tabs, b, t['n'])

    s0 = jnp.zeros((2, b, RW_WIDTH // RW_P, RW_P, RW_P), F32)
    for t in (ctx, lat):
        r_, v_, kk_, lw0, lw1, kd0, kd1, bd0, bd1, g_, bonus = (r3(a, t['n']) for a in t['rw'])
        yf, yb, s0 = rwkv_scan_pallas(r_, v_, kk_, lw0, lw1, kd0, kd1, bd0, bd1, s0)
        if t in sets:
            flat = lambda a: a.reshape(b * t['n'], RW_WIDTH)
            t['rwkv'] = rwkv_out_pallas(flat(yf), flat(yb), flat(bonus), flat(g_), p['rw_w'])

    outs = []
    for t in sets:
        rows = b * t['n']
        branches = [t['mla'].reshape(rows, BRANCH_WIDTH), t['ssm'], t['hy'], t['rwkv']]
        merged = merge_branches_pallas(branches, t['proj'], p['w_branch_b'])
        x1, h2, logits = outproj_postnorm(merged, t['x'], piece(t['rows'], 2), piece(t['rows'], 3), piece(t['rows'], 4),
                                          p['ln1_g'], p['ln1_b'], p['w_out_b'], p['w_router'], t['rpm'])
        moe = expert_choice_ffn(h2, logits, p['ffn_w'], b, t['n'])
        outs.append(postnorm(x1, moe, piece(t['rows'], 5), p['ln2_g'], p['ln2_b'], t['rpm']))
    return (outs[0], outs[1]) if need_ctx else (None, outs[0])


def kernel(x, c, ctx, c_ctx, w_ada, b_ada, w_in, mla_q_norm, mla_w_q_up, mla_kv_norm, mla_w_kv_up,
           ssm_conv_w, ssm_conv_b, ssm_dt_bias, ssm_a_log, ssm_d, ssm_norm,
           hy_conv_w, hy_conv_b, hy_w1, hy_b1, hy_w2, hy_b2, hy_w3, hy_freq, hy_d,
           rw_mu, rw_w0, rw_w_up, rw_a0, rw_a_up, rw_g_up, rw_kk, rw_ka, rw_rk, rw_ln_g, rw_ln_b,
           w_branch, w_out, ln1_g, ln1_b, w_router, w_gate_e, w_up_e, w_down_e, ln2_g, ln2_b):
    stacked = dict(
        w_in=w_in, mla_q_norm=mla_q_norm, mla_w_q_up=mla_w_q_up, mla_kv_norm=mla_kv_norm, mla_w_kv_up=mla_w_kv_up,
        ssm_conv_w=ssm_conv_w, ssm_conv_b=ssm_conv_b, ssm_dt_bias=ssm_dt_bias, ssm_a_log=ssm_a_log,
        ssm_d=ssm_d, ssm_norm=ssm_norm,
        hy_conv_w=hy_conv_w, hy_conv_b=hy_conv_b, hy_w1=hy_w1, hy_b1=hy_b1, hy_w2=hy_w2, hy_b2=hy_b2,
        hy_w3=hy_w3, hy_freq=hy_freq, hy_d=hy_d,
        rw_mu=rw_mu, rw_w0=rw_w0, rw_w_up=rw_w_up, rw_a0=rw_a0, rw_a_up=rw_a_up, rw_g_up=rw_g_up,
        rw_kk=rw_kk, rw_ka=rw_ka, rw_rk=rw_rk, rw_ln_g=rw_ln_g, rw_ln_b=rw_ln_b,
        w_branch=w_branch, w_out=w_out, ln1_g=ln1_g, ln1_b=ln1_b,
        w_router=w_router, w_gate_e=w_gate_e, w_up_e=w_up_e, w_down_e=w_down_e, ln2_g=ln2_g, ln2_b=ln2_b)
    b, n_lat, D = x.shape
    n_ctx = ctx.shape[1]
    cos, sin = rope_tables_128(n_lat)
    ones = jnp.pad(jnp.ones((b * n_ctx, QK_ROPE), F32), ((0, 0), (0, 128 - QK_ROPE)))
    tabs = dict(cos_l=jnp.tile(cos, (b, 1)), sin_l=jnp.tile(sin, (b, 1)), cos_c=ones, sin_c=jnp.zeros_like(ones),
                fft=dft_tables())
    cond = jnp.concatenate([c, c_ctx[None], jnp.zeros((8 - b - 1, D), F32)], 0)
    xc, xl = ctx.reshape(b * n_ctx, D), x.reshape(b * n_lat, D)
    for i in range(DEPTH):
        p = {name: arr[i] for name, arr in stacked.items()}
        p['w_in_p'] = build_w_in(p['w_in'])
        p['mla_w'] = build_mla_weights(p)
        p['rw_w'] = build_rwkv_weights(p)
        p['ssd_w'] = build_ssd_weights(p)
        p['hy_w'] = build_hyena_weights(p)
        p['w_branch_b'] = p['w_branch'].astype(BF16)
        p['w_out_b'] = p['w_out'].astype(BF16)
        p['ffn_w'] = build_expert_weights(p)
        mods = ada_modulation(cond, w_ada[i], b_ada[i])
        xc, xl = trunk_layer(xc, xl, mods, p, tabs, i < DEPTH - 1, b, n_ctx, n_lat)
    return xl.reshape(b, n_lat, D)
```

```python
import functools
import math

import jax
import jax.numpy as jnp
from jax import lax
import numpy as np
from jax.experimental import pallas as pl
from jax.experimental.pallas import tpu as pltpu

D_MODEL = 2048
DEPTH = 2

GRID_W = 64
N_BRANCH = 4
BRANCH_WIDTH = D_MODEL // 2

MLA_Q_LORA = D_MODEL // 4
MLA_KV_LORA = D_MODEL // 4
QK_NOPE = 128
QK_ROPE = 64
V_HEAD = 128
MLA_HEADS = BRANCH_WIDTH // V_HEAD
ROPE_BASE = 10000.0

SSM_HEAD_DIM = 64
SSM_HEADS = BRANCH_WIDTH // SSM_HEAD_DIM
SSM_INNER = SSM_HEADS * SSM_HEAD_DIM
SSM_GROUPS = 2
SSM_STATE = 128
SSM_CHUNK = 128
SSM_XBC = SSM_INNER + 2 * SSM_GROUPS * SSM_STATE

HY_WIDTH = BRANCH_WIDTH
HY_EMB = 33
HY_BANDS = (HY_EMB - 1) // 2
HY_FILTER_HIDDEN = 64
HY_TARGET = 1e-2
HY_SLOW_FRAC = 1.5
HY_QUICK_FRAC = 0.3
HY_MIN_DECAY = math.log(HY_TARGET) / HY_SLOW_FRAC
HY_MAX_DECAY = math.log(HY_TARGET) / HY_QUICK_FRAC

RW_HEAD_DIM = 64
RW_WIDTH = BRANCH_WIDTH
RW_HEADS = RW_WIDTH // RW_HEAD_DIM
RW_DECAY_LORA = 64
RW_ICLR_LORA = 64
RW_GATE_LORA = 160
RW_GN_EPS = 64e-5

N_EXPERTS = 16
EXPERT_FF = 1408
EC_CAPACITY = 2

ALPHA = (2 * DEPTH) ** 0.25

MLA_COLS = MLA_Q_LORA + MLA_KV_LORA + QK_ROPE
SSM_COLS = SSM_INNER + SSM_XBC + 2 * SSM_HEADS
HY_COLS = 3 * HY_WIDTH
RW_COLS = 3 * RW_WIDTH + 2 * RW_DECAY_LORA + 2 * RW_ICLR_LORA + RW_GATE_LORA

F32 = jnp.float32
BF16 = jnp.bfloat16
HI = lax.Precision.HIGHEST

RW_T = 64
RW_P = 128
RW_PAIRS_PER_STEP = 4
RW_INV_BASE = 8
ATT_DK = 256
ATT_DV = 128
TM = 512
VMEM_LIMIT = 48 * 1024 * 1024


def _mm(a, b):
    return jnp.dot(a.astype(BF16), b.astype(BF16), preferred_element_type=F32)


def _mm_nt(a, b):
    return lax.dot_general(a.astype(BF16), b.astype(BF16), (((1,), (1,)), ((), ())), preferred_element_type=F32)


def _mm_tn(a, b):
    return lax.dot_general(a.astype(BF16), b.astype(BF16), (((0,), (0,)), ((), ())), preferred_element_type=F32)


def _ln(x, eps=1e-6):
    mu = jnp.mean(x, -1, keepdims=True)
    xc = x - mu
    return xc * lax.rsqrt(jnp.mean(xc * xc, -1, keepdims=True) + eps)


def _rwkv_pair_chunks(chains, inv_masks):
    T = RW_T
    each = lambda f, *ls: [f(*xs) for xs in zip(*ls)]
    S = [ch[0] for ch in chains]
    r, lw, k, v, kk, b = ([ch[1][i] for ch in chains] for i in range(6))
    a = [-x for x in kk]
    rev = [ch[2] for ch in chains]
    cum, head_lo, strict_bd, incl_bd, eye_bd = ([ch[3][i] for ch in chains] for i in range(5))

    def stack(x, lo):
        return jnp.concatenate([jnp.where(lo, x, 0.0), jnp.where(lo, 0.0, x)], axis=0)

    c = each(lambda m, x: jnp.dot(m, x, precision=HI, preferred_element_type=F32), cum, lw)
    cT = each(lambda x, rv: x[0:1, :] if rv else x[T - 1:T, :], c, rev)
    e_neg = each(lambda x: jnp.exp(-x), c)
    e_end = each(lambda x, y: jnp.exp(x - y), cT, c)
    a_s = each(lambda x, cc, l, lo: stack(x * jnp.exp(cc - l), lo), a, c, lw, head_lo)
    r_s = each(lambda x, cc, lo: stack(x * jnp.exp(cc), lo), r, c, head_lo)
    b_s = each(lambda x, e, lo: stack(x * e, lo), b, e_neg, head_lo)
    k_s = each(lambda x, e, lo: stack(x * e, lo), k, e_neg, head_lo)
    bh_s = each(lambda x, e, lo: stack(x * e, lo), b, e_end, head_lo)
    kh_s = each(lambda x, e, lo: stack(x * e, lo), k, e_end, head_lo)
    v_s = each(stack, v, head_lo)
    g_T = each(jnp.exp, cT)

    ar = each(lambda x, y: jnp.concatenate([x, y], axis=0), a_s, r_s)
    gb = each(_mm_nt, ar, b_s)
    gk = each(_mm_nt, ar, k_s)
    n = each(lambda m, g: jnp.where(m, g[:2 * T], 0.0), strict_bd, gb)
    l_ak = each(lambda m, g: jnp.where(m, g[:2 * T], 0.0), strict_bd, gk)
    m_rb = each(lambda m, g: jnp.where(m, g[2 * T:], 0.0), incl_bd, gb)
    m_rk = each(lambda m, g: jnp.where(m, g[2 * T:], 0.0), incl_bd, gk)
    lv = each(_mm, l_ak, v_s)
    n8 = each(lambda x: jnp.where(inv_masks[0], x, 0.0), n)
    p = each(lambda e, x: e + x, eye_bd, n8)
    m = each(_mm, n8, n8)
    p = each(lambda x, y: x + _mm(y, x), p, m)
    m = each(_mm, m, m)
    p = each(lambda x, y: x + _mm(y, x), p, m)
    for merge in inv_masks[1:]:
        n_off = each(lambda x: jnp.where(merge, x, 0.0), n)
        p = each(lambda x, y: x + _mm(_mm(x, y), x), p, n_off)
    w = each(_mm, p, a_s)
    u0 = each(_mm, p, lv)
    wr = each(lambda x, y, s: _mm_nt(jnp.concatenate([x, y], axis=0), s), w, r_s, S)
    uv = each(lambda x, y, z: jnp.concatenate([x[:2 * T] + y, z], axis=0), wr, u0, v_s)
    y_s = each(lambda x, p1, p2, z: x[2 * T:] + _mm(jnp.concatenate([p1, p2], axis=1), z), wr, m_rb, m_rk, uv)
    y = each(lambda x: x[:T] + x[T:], y_s)
    S_new = each(lambda s, g, z, p1, p2: s * g + _mm_tn(z, jnp.concatenate([p1, p2], axis=0)),
                 S, g_T, uv, bh_s, kh_s)
    return list(zip(y, S_new))


def _rwkv_consts():
    T = RW_T
    row = lax.broadcasted_iota(jnp.int32, (T, T), 0)
    col = lax.broadcasted_iota(jnp.int32, (T, T), 1)
    cum_f = (row >= col).astype(F32)
    cum_b = (row <= col).astype(F32)
    head_lo = lax.broadcasted_iota(jnp.int32, (T, RW_P), 1) < RW_P // 2
    r2 = lax.broadcasted_iota(jnp.int32, (2 * T, 2 * T), 0)
    c2 = lax.broadcasted_iota(jnp.int32, (2 * T, 2 * T), 1)
    same = (r2 >= T) == (c2 >= T)
    eye_bd = (r2 == c2).astype(F32)
    fwd = (cum_f, head_lo, same & (r2 > c2), same & (r2 >= c2), eye_bd)
    bwd = (cum_b, head_lo, same & (r2 < c2), same & (r2 <= c2), eye_bd)
    blk = lambda s: (r2 // s) == (c2 // s)
    inv_masks = [blk(RW_INV_BASE)]
    s = RW_INV_BASE
    while s < T:
        inv_masks.append(blk(2 * s) & jnp.logical_not(blk(s)))
        s *= 2
    return fwd, bwd, inv_masks


def _rwkv_kernel(rf, vf, af, lwf, kf, bf, rb, vb, ab, lwb, kb, bb, s0_ref, yf_ref, yb_ref, sT_ref, s_scr, *, nsub):
    j = pl.program_id(2)

    @pl.when(j == 0)
    def _():
        s_scr[...] = s0_ref[:, 0]

    T = RW_T
    cf, cb, inv_masks = _rwkv_consts()

    def body(i, carry):
        sf = pl.ds(pl.multiple_of(i * T, T), T)
        sb = pl.ds(pl.multiple_of((nsub - 1 - i) * T, T), T)
        lanes = [slice(q * RW_P, (q + 1) * RW_P) for q in range(RW_PAIRS_PER_STEP)]
        ins = []
        for ln in lanes:
            ins.append(tuple(ref[0, sf, ln].astype(F32) for ref in (rf, lwf, kf, vf, af, bf)))
            ins.append(tuple(ref[0, sb, ln].astype(F32) for ref in (rb, lwb, kb, vb, ab, bb)))
        res = _rwkv_pair_chunks([(carry[c], ins[c], bool(c % 2), cb if c % 2 else cf)
                                 for c in range(2 * RW_PAIRS_PER_STEP)], inv_masks)
        for q, ln in enumerate(lanes):
            yf_ref[0, sf, ln] = res[2 * q][0]
            yb_ref[0, sb, ln] = res[2 * q + 1][0]
        return tuple(s for _, s in res)

    init = tuple(s_scr[d, q] for q in range(RW_PAIRS_PER_STEP) for d in range(2))
    fin = lax.fori_loop(0, nsub, body, init)
    for q in range(RW_PAIRS_PER_STEP):
        for d in range(2):
            s_scr[d, q] = fin[2 * q + d]

    @pl.when(j == pl.num_programs(2) - 1)
    def _():
        for q in range(RW_PAIRS_PER_STEP):
            for d in range(2):
                sT_ref[d, 0, q] = fin[2 * q + d]


def rwkv_scan_pallas(r, v, kk, lw0, lw1, kd0, kd1, bd0, bd1, s0, *, tb=256):
    B, n, C = r.shape
    tb = min(tb, n)
    nb = n // tb
    pp = RW_PAIRS_PER_STEP
    assert n % tb == 0 and tb % RW_T == 0 and C % (pp * RW_P) == 0
    fw = pl.BlockSpec((1, tb, pp * RW_P), lambda b, p, j: (b, j, p))
    bw = pl.BlockSpec((1, tb, pp * RW_P), lambda b, p, j: (b, nb - 1 - j, p))
    st = pl.BlockSpec((2, 1, pp, RW_P, RW_P), lambda b, p, j: (0, b, p, 0, 0))
    return pl.pallas_call(
        functools.partial(_rwkv_kernel, nsub=tb // RW_T),
        grid=(B, C // (pp * RW_P), nb),
        in_specs=[fw] * 6 + [bw] * 6 + [st],
        out_specs=[fw, bw, st],
        out_shape=[jax.ShapeDtypeStruct((B, n, C), F32), jax.ShapeDtypeStruct((B, n, C), F32),
                   jax.ShapeDtypeStruct(s0.shape, F32)],
        scratch_shapes=[pltpu.VMEM((2, pp, RW_P, RW_P), F32)],
        compiler_params=pltpu.CompilerParams(dimension_semantics=("parallel", "parallel", "arbitrary")),
        name="rwkv_scan",
    )(r, v, kk, lw0, kd0, bd0, r, v, kk, lw1, kd1, bd1, s0)


FLASH_STREAMS = 2


def _flash_kernel(q_ref, k_ref, vt_ref, o_ref, m_scr, acc_scr):
    j = pl.program_id(3)

    @pl.when(j == 0)
    def _():
        m_scr[...] = jnp.full(m_scr.shape, -jnp.inf, F32)
        acc_scr[...] = jnp.zeros(acc_scr.shape, F32)

    tq = q_ref.shape[1]
    k = k_ref[0]
    vt = vt_ref[0]
    cols = [slice(i * tq // FLASH_STREAMS, (i + 1) * tq // FLASH_STREAMS) for i in range(FLASH_STREAMS)]
    s = [lax.dot_general(k, q_ref[0, c, :], (((1,), (1,)), ((), ())), preferred_element_type=F32) for c in cols]
    m_prev = [m_scr[:, c] for c in cols]
    m_new = [jnp.maximum(mp, jnp.max(x, axis=0, keepdims=True)) for mp, x in zip(m_prev, s)]
    p = [jnp.exp((x - mn).astype(BF16)) for x, mn in zip(s, m_new)]
    pv = [jnp.dot(vt, x, preferred_element_type=F32) for x in p]
    for c, mp, mn, x in zip(cols, m_prev, m_new, pv):
        acc_scr[:, c] = jnp.exp(mp - mn) * acc_scr[:, c] + x
        m_scr[:, c] = mn

    @pl.when(j == pl.num_programs(3) - 1)
    def _():
        acc = acc_scr[...]
        o_ref[0] = jnp.transpose(acc[:ATT_DV] / acc[ATT_DV:ATT_DV + 1]).astype(o_ref.dtype)


def transpose_values(v):
    B, nk, hv = v.shape
    H = hv // ATT_DV
    ext = jnp.zeros((B, nk, H, ATT_DV), v.dtype).at[..., 0].set(1)
    va = jnp.concatenate([v.reshape(B, nk, H, ATT_DV), ext], -1)
    return jnp.transpose(va, (0, 2, 3, 1)).reshape(B, H * 2 * ATT_DV, nk)


def flash_attention(q, k, v, *, tq, tk, out_dtype=F32):
    B, nq, hq = q.shape
    nk = k.shape[1]
    H = hq // ATT_DK
    assert nq % tq == 0 and nk % tk == 0
    return pl.pallas_call(
        _flash_kernel,
        grid=(B, H, nq // tq, nk // tk),
        in_specs=[pl.BlockSpec((1, tq, ATT_DK), lambda b, h, i, j: (b, i, h)),
                  pl.BlockSpec((1, tk, ATT_DK), lambda b, h, i, j: (b, j, h)),
                  pl.BlockSpec((1, 2 * ATT_DV, tk), lambda b, h, i, j: (b, h, j))],
        out_specs=pl.BlockSpec((1, tq, ATT_DV), lambda b, h, i, j: (b, i, h)),
        out_shape=jax.ShapeDtypeStruct((B, nq, H * ATT_DV), out_dtype),
        scratch_shapes=[pltpu.VMEM((1, tq), F32), pltpu.VMEM((2 * ATT_DV, tq), F32)],
        compiler_params=pltpu.CompilerParams(
            dimension_semantics=("parallel", "parallel", "parallel", "arbitrary")),
        name="mla_flash",
    )(q, k, transpose_values(v))


PROJ = {}
_off = 0
for _name, _w in (('gate', 4 * D_MODEL), ('hy_x0', HY_WIDTH), ('hy_x1', HY_WIDTH), ('hy_v', HY_WIDTH),
                  ('rw_r', RW_WIDTH), ('rw_k', RW_WIDTH), ('rw_v', RW_WIDTH), ('ssm_z', SSM_INNER),
                  ('ssm_x', SSM_INNER), ('mla_cq', MLA_Q_LORA), ('mla_ckv', MLA_KV_LORA),
                  ('ssm_b', 256), ('ssm_c', 256), ('rw_gd', 256), ('rw_wd', 128), ('rw_ad', 128),
                  ('mla_kpe', 128), ('mla_kpe_sw', 128), ('ssm_dt', 128)):
    assert _off % _w == 0, (_name, _off, _w)
    PROJ[_name] = (_off, _w)
    _off += _w
PROJ_TN = 512
PROJ_N = -(-_off // PROJ_TN) * PROJ_TN


def _rope_swap_matrix():
    p = np.zeros((QK_ROPE, QK_ROPE), np.float32)
    q = QK_ROPE // 4
    for g in range(2):
        for i in range(q):
            p[g * 2 * q + q + i, g * 2 * q + i] = -1.0
            p[g * 2 * q + i, g * 2 * q + q + i] = 1.0
    return jnp.asarray(p)


def build_w_in(w_in):
    cuts = np.cumsum([0, MLA_COLS, SSM_COLS, HY_COLS, RW_COLS, 4 * D_MODEL]).tolist()
    mla, ssm, hy, rw, gate = (w_in[:, cuts[i]:cuts[i + 1]] for i in range(5))
    kpe = mla[:, MLA_Q_LORA + MLA_KV_LORA:]
    src = {
        'gate': gate, 'hy_x0': hy[:, :HY_WIDTH], 'hy_x1': hy[:, HY_WIDTH:2 * HY_WIDTH], 'hy_v': hy[:, 2 * HY_WIDTH:],
        'rw_r': rw[:, :RW_WIDTH], 'rw_k': rw[:, RW_WIDTH:2 * RW_WIDTH], 'rw_v': rw[:, 2 * RW_WIDTH:3 * RW_WIDTH],
        'rw_wd': rw[:, 3 * RW_WIDTH:3 * RW_WIDTH + 128], 'rw_ad': rw[:, 3 * RW_WIDTH + 128:3 * RW_WIDTH + 256],
        'rw_gd': rw[:, 3 * RW_WIDTH + 256:],
        'ssm_z': ssm[:, :SSM_INNER], 'ssm_x': ssm[:, SSM_INNER:2 * SSM_INNER],
        'ssm_b': ssm[:, 2 * SSM_INNER:2 * SSM_INNER + 256], 'ssm_c': ssm[:, 2 * SSM_INNER + 256:2 * SSM_INNER + 512],
        'ssm_dt': ssm[:, 2 * SSM_INNER + 512:],
        'mla_cq': mla[:, :MLA_Q_LORA], 'mla_ckv': mla[:, MLA_Q_LORA:MLA_Q_LORA + MLA_KV_LORA],
        'mla_kpe': kpe, 'mla_kpe_sw': kpe @ _rope_swap_matrix(),
    }
    cols, pos = [], 0
    for name, (off, w) in PROJ.items():
        assert off == pos
        piece = src[name]
        cols.append(jnp.pad(piece, ((0, 0), (0, w - piece.shape[1]))))
        pos += w
    cols.append(jnp.zeros((w_in.shape[0], PROJ_N - pos), w_in.dtype))
    return jnp.concatenate(cols, 1).astype(BF16)


def _inproj_kernel(x_ref, sh_ref, sc_ref, w_ref, o_ref, h_scr):
    @pl.when(pl.program_id(1) == 0)
    def _():
        h_scr[...] = (_ln(x_ref[...]) * (1.0 + sc_ref[0]) + sh_ref[0]).astype(BF16)

    o_ref[...] = jnp.dot(h_scr[...], w_ref[...], preferred_element_type=F32).astype(o_ref.dtype)


def inproj(x2d, shift, scale, w_p, rows_per_mod):
    R, D = x2d.shape
    tm = min(TM, R)
    bpm = max(rows_per_mod // tm, 1)
    mod = pl.BlockSpec((1, 1, D), lambda i, j: (i // bpm, 0, 0))
    return pl.pallas_call(
        _inproj_kernel,
        grid=(R // tm, PROJ_N // PROJ_TN),
        in_specs=[pl.BlockSpec((tm, D), lambda i, j: (i, 0)), mod, mod,
                  pl.BlockSpec((D, PROJ_TN), lambda i, j: (0, j))],
        out_specs=pl.BlockSpec((tm, PROJ_TN), lambda i, j: (i, j)),
        out_shape=jax.ShapeDtypeStruct((R, PROJ_N), BF16),
        scratch_shapes=[pltpu.VMEM((tm, D), BF16)],
        compiler_params=pltpu.CompilerParams(dimension_semantics=("parallel", "arbitrary"),
                                             vmem_limit_bytes=VMEM_LIMIT),
        name="inproj",
    )(x2d, shift, scale, w_p)


def build_mla_weights(p):
    wq = p['mla_w_q_up'].reshape(MLA_Q_LORA, MLA_HEADS, QK_NOPE + QK_ROPE)
    wkv = p['mla_w_kv_up'].reshape(MLA_KV_LORA, MLA_HEADS, QK_NOPE + V_HEAD)
    wq_p = jnp.pad(wq, ((0, 0), (0, 0), (0, ATT_DK - QK_NOPE - QK_ROPE))).reshape(MLA_Q_LORA, MLA_HEADS * ATT_DK)
    wsw = jnp.einsum('chd,de->che', wq[:, :, QK_NOPE:], _rope_swap_matrix())
    wsw_p = jnp.pad(wsw, ((0, 0), (0, 0), (0, 128 - QK_ROPE))).reshape(MLA_Q_LORA, MLA_HEADS * 128)
    wkn = wkv[:, :, :QK_NOPE].reshape(MLA_KV_LORA, MLA_HEADS * QK_NOPE)
    wv = wkv[:, :, QK_NOPE:].reshape(MLA_KV_LORA, MLA_HEADS * V_HEAD)
    return wq_p.astype(BF16), wsw_p.astype(BF16), wkn.astype(BF16), wv.astype(BF16)


def axial_rope_tables(n_tokens):
    rows = n_tokens // GRID_W
    row = jnp.repeat(jnp.arange(rows), GRID_W).astype(F32)
    col = jnp.tile(jnp.arange(GRID_W), rows).astype(F32)
    half = QK_ROPE // 2
    inv = ROPE_BASE ** (-jnp.arange(0, half, 2, dtype=F32) / half)
    ang = jnp.stack([row[:, None] * inv, col[:, None] * inv], 1)
    return jnp.cos(ang), jnp.sin(ang)


def rope_tables_128(n_tokens):
    cos, sin = axial_rope_tables(n_tokens)
    full = lambda t: jnp.concatenate([t[:, 0], t[:, 0], t[:, 1], t[:, 1]], -1)
    pad = lambda t: jnp.pad(t, ((0, 0), (0, 128 - QK_ROPE)))
    return pad(full(cos)), pad(full(sin))


def _mla_proj_kernel(cq_ref, ckv_ref, kpe_ref, ksw_ref, cos_ref, sin_ref, gq_ref, gkv_ref, wq_ref, wsw_ref, wkn_ref,
                     wv_ref, q_ref, k_ref, v_ref, *, q_scale):
    def rms(x, g):
        return (x * lax.rsqrt(jnp.mean(x * x, -1, keepdims=True) + 1e-6) * g).astype(BF16)

    cqn = rms(cq_ref[...].astype(F32), gq_ref[...])
    ckvn = rms(ckv_ref[...].astype(F32), gkv_ref[...])
    cos = cos_ref[...]
    sin = sin_ref[...]
    q = jnp.dot(cqn, wq_ref[...], preferred_element_type=F32)
    qsw = jnp.dot(cqn, wsw_ref[...], preferred_element_type=F32)
    kn = jnp.dot(ckvn, wkn_ref[...], preferred_element_type=F32)
    v_ref[...] = jnp.dot(ckvn, wv_ref[...], preferred_element_type=F32).astype(v_ref.dtype)
    kpe = (kpe_ref[...].astype(F32) * cos + ksw_ref[...].astype(F32) * sin).astype(k_ref.dtype)
    for h in range(MLA_HEADS):
        lo = h * ATT_DK
        q_ref[:, lo:lo + 128] = (q[:, lo:lo + 128] * q_scale).astype(q_ref.dtype)
        pe = q[:, lo + 128:lo + 256] * cos + qsw[:, h * 128:(h + 1) * 128] * sin
        q_ref[:, lo + 128:lo + 256] = (pe * q_scale).astype(q_ref.dtype)
        k_ref[:, lo:lo + 128] = kn[:, h * 128:(h + 1) * 128].astype(k_ref.dtype)
        k_ref[:, lo + 128:lo + 256] = kpe


def mla_proj(proj, cos, sin, gq, gkv, weights):
    R = proj.shape[0]
    tm = min(TM, R)
    wq, wsw, wkn, wv = weights
    col = lambda name: pl.BlockSpec((tm, PROJ[name][1]), lambda i, o=PROJ[name][0] // PROJ[name][1]: (i, o))
    row = lambda w: pl.BlockSpec((tm, w), lambda i: (i, 0))
    full = lambda a: pl.BlockSpec(a.shape, lambda i: (0,) * a.ndim)
    gq2, gkv2 = gq.reshape(1, -1), gkv.reshape(1, -1)
    return pl.pallas_call(
        functools.partial(_mla_proj_kernel, q_scale=float((QK_NOPE + QK_ROPE) ** -0.5)),
        grid=(R // tm,),
        in_specs=[col('mla_cq'), col('mla_ckv'), col('mla_kpe'), col('mla_kpe_sw'), row(128), row(128),
                  full(gq2), full(gkv2), full(wq), full(wsw), full(wkn), full(wv)],
        out_specs=[row(MLA_HEADS * ATT_DK), row(MLA_HEADS * ATT_DK), row(MLA_HEADS * ATT_DV)],
        out_shape=[jax.ShapeDtypeStruct((R, MLA_HEADS * ATT_DK), BF16),
                   jax.ShapeDtypeStruct((R, MLA_HEADS * ATT_DK), BF16),
                   jax.ShapeDtypeStruct((R, MLA_HEADS * ATT_DV), BF16)],
        compiler_params=pltpu.CompilerParams(dimension_semantics=("parallel",), vmem_limit_bytes=VMEM_LIMIT),
        name="mla_proj",
    )(proj, proj, proj, proj, cos, sin, gq2, gkv2, wq, wsw, wkn, wv)


MERGE_TN = 512


def _merge_kernel(b0, b1, b2, b3, g0, g1, g2, g3, w_ref, o_ref):
    acc = None
    for i, (b, g) in enumerate(((b0, g0), (b1, g1), (b2, g2), (b3, g3))):
        t = jax.nn.sigmoid(g[...].astype(F32)) * jnp.dot(b[...], w_ref[i], preferred_element_type=F32)
        acc = t if acc is None else acc + t
    o_ref[...] = acc.astype(o_ref.dtype)


def merge_branches_pallas(branches, proj, w_branch):
    R = proj.shape[0]
    tm = min(TM, R)
    nj = D_MODEL // MERGE_TN
    g0 = PROJ['gate'][0] // MERGE_TN
    br = pl.BlockSpec((tm, BRANCH_WIDTH), lambda i, j: (i, 0))
    gates = [pl.BlockSpec((tm, MERGE_TN), lambda i, j, k=k: (i, g0 + k * nj + j)) for k in range(N_BRANCH)]
    return pl.pallas_call(
        _merge_kernel,
        grid=(R // tm, nj),
        in_specs=[br] * 4 + gates + [pl.BlockSpec((N_BRANCH, BRANCH_WIDTH, MERGE_TN), lambda i, j: (0, 0, j))],
        out_specs=pl.BlockSpec((tm, MERGE_TN), lambda i, j: (i, j)),
        out_shape=jax.ShapeDtypeStruct((R, D_MODEL), BF16),
        compiler_params=pltpu.CompilerParams(dimension_semantics=("parallel", "arbitrary"),
                                             vmem_limit_bytes=VMEM_LIMIT),
        name="branch_merge",
    )(*branches, proj, proj, proj, proj, w_branch)


ROUTER_PAD = 128


def _mm3_f32(a, b_hi, b_lo):
    ah = a.astype(BF16)
    al = (a - ah.astype(F32)).astype(BF16)
    d = lambda x, y: jnp.dot(x, y, preferred_element_type=F32)
    return d(ah, b_hi) + d(ah, b_lo) + d(al, b_hi)


def _outproj_kernel(m_ref, x_ref, g1_ref, sh2_ref, sc2_ref, lng_ref, lnb_ref, w_ref, wr_hi, wr_lo,
                    x_out, h2_out, logit_out):
    y = jnp.dot(m_ref[...], w_ref[...], preferred_element_type=F32)
    xn = _ln(ALPHA * x_ref[...] + g1_ref[0] * y) * lng_ref[...] + lnb_ref[...]
    x_out[...] = xn
    h2 = _ln(xn) * (1.0 + sc2_ref[0]) + sh2_ref[0]
    h2_out[...] = h2.astype(h2_out.dtype)
    logit_out[...] = _mm3_f32(h2, wr_hi[...], wr_lo[...])


def outproj_postnorm(merged, x2d, g1, sh2, sc2, ln_g, ln_b, w_out, w_router, rows_per_mod):
    R, D = x2d.shape
    tm = min(TM, R)
    bpm = max(rows_per_mod // tm, 1)
    mod = pl.BlockSpec((1, 1, D), lambda i: (i // bpm, 0, 0))
    row = lambda w: pl.BlockSpec((tm, w), lambda i: (i, 0))
    full = lambda a: pl.BlockSpec(a.shape, lambda i: (0,) * a.ndim)
    wr = jnp.pad(w_router, ((0, 0), (0, ROUTER_PAD - w_router.shape[1])))
    wr_hi = wr.astype(BF16)
    wr_lo = (wr - wr_hi.astype(F32)).astype(BF16)
    lg, lb = ln_g.reshape(1, D), ln_b.reshape(1, D)
    return pl.pallas_call(
        _outproj_kernel,
        grid=(R // tm,),
        in_specs=[row(D), row(D), mod, mod, mod, full(lg), full(lb), full(w_out), full(wr_hi), full(wr_lo)],
        out_specs=[row(D), row(D), row(ROUTER_PAD)],
        out_shape=[jax.ShapeDtypeStruct((R, D), F32), jax.ShapeDtypeStruct((R, D), BF16),
                   jax.ShapeDtypeStruct((R, ROUTER_PAD), F32)],
        compiler_params=pltpu.CompilerParams(dimension_semantics=("parallel",), vmem_limit_bytes=VMEM_LIMIT),
        name="outproj_postnorm",
    )(merged, x2d, g1, sh2, sc2, lg, lb, w_out, wr_hi, wr_lo)


def _postnorm_kernel(x_ref, y_ref, g_ref, lng_ref, lnb_ref, o_ref):
    o_ref[...] = _ln(ALPHA * x_ref[...] + g_ref[0] * y_ref[...]) * lng_ref[...] + lnb_ref[...]


def postnorm(x2d, y2d, g, ln_g, ln_b, rows_per_mod):
    R, D = x2d.shape
    tm = min(TM, R)
    bpm = max(rows_per_mod // tm, 1)
    row = pl.BlockSpec((tm, D), lambda i: (i, 0))
    vec = pl.BlockSpec((1, D), lambda i: (0, 0))
    return pl.pallas_call(
        _postnorm_kernel,
        grid=(R // tm,),
        in_specs=[row, row, pl.BlockSpec((1, 1, D), lambda i: (i // bpm, 0, 0)), vec, vec],
        out_specs=row,
        out_shape=jax.ShapeDtypeStruct((R, D), F32),
        compiler_params=pltpu.CompilerParams(dimension_semantics=("parallel",), vmem_limit_bytes=VMEM_LIMIT),
        name="postnorm",
    )(x2d, y2d, g, ln_g.reshape(1, D), ln_b.reshape(1, D))


FF_PAD = 1536
FF_TILE = 512


def _expert_ffn_kernel(x_ref, gate_ref, wg_ref, wu_ref, wd_ref, o_ref):
    f = pl.program_id(2)
    x = x_ref[0, 0]
    hid = (jax.nn.silu(jnp.dot(x, wg_ref[0], preferred_element_type=F32))
           * jnp.dot(x, wu_ref[0], preferred_element_type=F32))
    part = jnp.dot(hid.astype(BF16), wd_ref[0], preferred_element_type=F32)

    @pl.when(f == 0)
    def _():
        o_ref[0, 0] = part

    @pl.when(f > 0)
    def _():
        o_ref[0, 0] += part

    @pl.when(f == pl.num_programs(2) - 1)
    def _():
        o_ref[0, 0] = o_ref[0, 0] * gate_ref[0, 0]


def expert_ffn(xs, gate, wg, wu, wd):
    S, E, C, D = xs.shape
    nf = FF_PAD // FF_TILE
    return pl.pallas_call(
        _expert_ffn_kernel,
        grid=(E, S, nf),
        in_specs=[pl.BlockSpec((1, 1, C, D), lambda e, s, f: (s, e, 0, 0)),
                  pl.BlockSpec((1, 1, C, 1), lambda e, s, f: (s, e, 0, 0)),
                  pl.BlockSpec((1, D, FF_TILE), lambda e, s, f: (e, 0, f)),
                  pl.BlockSpec((1, D, FF_TILE), lambda e, s, f: (e, 0, f)),
                  pl.BlockSpec((1, FF_TILE, D), lambda e, s, f: (e, f, 0))],
        out_specs=pl.BlockSpec((1, 1, C, D), lambda e, s, f: (s, e, 0, 0)),
        out_shape=jax.ShapeDtypeStruct((S, E, C, D), F32),
        compiler_params=pltpu.CompilerParams(dimension_semantics=("parallel", "parallel", "arbitrary"),
                                             vmem_limit_bytes=VMEM_LIMIT),
        name="expert_ffn",
    )(xs, gate, wg, wu, wd)


def build_expert_weights(p):
    padf = lambda w: jnp.pad(w, ((0, 0), (0, 0), (0, FF_PAD - EXPERT_FF))).astype(BF16)
    wd = jnp.pad(p['w_down_e'], ((0, 0), (0, FF_PAD - EXPERT_FF), (0, 0))).astype(BF16)
    return padf(p['w_gate_e']), padf(p['w_up_e']), wd


HALO = 8


def _halo_specs(tm, w, col_block, nrows):
    nb8 = nrows // HALO
    prev = pl.BlockSpec((HALO, w), lambda i: (jnp.maximum(i * (tm // HALO) - 1, 0), col_block))
    nxt = pl.BlockSpec((HALO, w), lambda i: (jnp.minimum((i + 1) * (tm // HALO), nb8 - 1), col_block))
    return prev, nxt


def _neighbours(cur, prev8, next8, seq_len):
    tm = cur.shape[0]
    row = lax.broadcasted_iota(jnp.int32, cur.shape, 0)
    g = row + pl.program_id(0) * tm
    before = jnp.where(row == 0, prev8[HALO - 1:HALO, :], pltpu.roll(cur, 1, 0))
    after = jnp.where(row == tm - 1, next8[0:1, :], pltpu.roll(cur, tm - 1, 0))
    pos = g & (seq_len - 1)
    return jnp.where(pos == 0, 0.0, before), jnp.where(pos == seq_len - 1, 0.0, after)


RW_PIECES = ('rw_r', 'rw_k', 'rw_v', 'rw_wd', 'rw_ad', 'rw_gd')


def _head_sum_matrix(scale=1.0):
    r = np.arange(RW_WIDTH)
    return jnp.asarray(((r[:, None] // RW_HEAD_DIM) == (r[None, :] // RW_HEAD_DIM)).astype(np.float32) * scale, BF16)


def _seg_sum(x, m):
    xh = x.astype(BF16)
    xl = (x - xh.astype(F32)).astype(BF16)
    return jnp.dot(xh, m, preferred_element_type=F32) + jnp.dot(xl, m, preferred_element_type=F32)


def _rwkv_prep_kernel(*refs, seq_len):
    cur = refs[0:6]
    prv = refs[6:12]
    nxt = refs[12:18]
    mu = refs[18:24]
    (w0_ref, a0_ref, wup_ref, aup_ref, gup_ref, kkw_ref, ka_ref, rk_ref, hs_ref) = refs[24:33]
    (r_o, v_o, kk_o, lw0_o, lw1_o, kd0_o, kd1_o, bd0_o, bd1_o, g_o, bonus_o) = refs[33:]
    us = []
    for c, p, n, m in zip(cur, prv, nxt, mu):
        x = c[...].astype(F32)
        before, after = _neighbours(x, p[...].astype(F32), n[...].astype(F32), seq_len)
        us.append(x + m[0:1, :] * (before - x) + m[1:2, :] * (after - x))
    r, k, v, wd, ad, gd = us
    hs = hs_ref[...]
    g_o[...] = jnp.dot(jax.nn.sigmoid(gd).astype(BF16), gup_ref[...], preferred_element_type=F32).astype(g_o.dtype)
    kk = k * kkw_ref[...]
    kk = kk / jnp.maximum(jnp.sqrt(_seg_sum(kk * kk, hs)), 1e-12)
    wl = jnp.dot(jnp.tanh(wd).astype(BF16), wup_ref[...], preferred_element_type=F32) + w0_ref[...]
    al = jnp.dot(ad.astype(BF16), aup_ref[...], preferred_element_type=F32) + a0_ref[...]
    r_o[...] = r.astype(r_o.dtype)
    v_o[...] = v.astype(v_o.dtype)
    kk_o[...] = kk.astype(kk_o.dtype)
    rk = rk_ref[...]
    acc = None
    for d, (lw_o, kd_o, bd_o) in enumerate(((lw0_o, kd0_o, bd0_o), (lw1_o, kd1_o, bd1_o))):
        sl = slice(d * RW_WIDTH, (d + 1) * RW_WIDTH)
        w_log = -jax.nn.softplus(-wl[:, sl]) - 0.5
        lw_o[...] = -jnp.exp(w_log)
        iclr = jax.nn.sigmoid(al[:, sl])
        kd = k * (1.0 + (iclr - 1.0) * ka_ref[...])
        kd_o[...] = kd.astype(kd_o.dtype)
        bd_o[...] = (kk * iclr).astype(bd_o.dtype)
        t = r * kd * rk
        acc = t if acc is None else acc + t
    bonus_o[...] = (_seg_sum(acc, hs) * v).astype(bonus_o.dtype)


def build_rwkv_weights(p):
    W = RW_WIDTH
    mu = p['rw_mu']
    cuts = np.cumsum([0, W, W, W, 128, 128, RW_GATE_LORA]).tolist()
    mus = [mu[:, cuts[i]:cuts[i + 1]] for i in range(6)]
    mus[5] = jnp.pad(mus[5], ((0, 0), (0, 256 - RW_GATE_LORA)))
    z = jnp.zeros((RW_DECAY_LORA, W), F32)
    wup = jnp.concatenate([jnp.concatenate([p['rw_w_up'][0], z], 1), jnp.concatenate([z, p['rw_w_up'][1]], 1)], 0)
    aup = jnp.concatenate([jnp.concatenate([p['rw_a_up'][0], z], 1), jnp.concatenate([z, p['rw_a_up'][1]], 1)], 0)
    gup = jnp.pad(p['rw_g_up'], ((0, 256 - RW_GATE_LORA), (0, 0)))
    return dict(mu=mus, w0=p['rw_w0'].reshape(1, 2 * W), a0=p['rw_a0'].reshape(1, 2 * W), wup=wup.astype(BF16),
                aup=aup.astype(BF16), gup=gup.astype(BF16), kkw=p['rw_kk'].reshape(1, W), ka=p['rw_ka'].reshape(1, W),
                rk=p['rw_rk'].reshape(1, W), hs=_head_sum_matrix(), hmean=_head_sum_matrix(1.0 / RW_HEAD_DIM),
                ln_g=p['rw_ln_g'].reshape(1, W), ln_b=p['rw_ln_b'].reshape(1, W))


def rwkv_prep_pallas(proj, wts, seq_len):
    R = proj.shape[0]
    tm = min(TM, R)
    W = RW_WIDTH
    cur, prv, nxt = [], [], []
    for name in RW_PIECES:
        off, w = PROJ[name]
        cur.append(pl.BlockSpec((tm, w), lambda i, o=off // w: (i, o)))
        a, b = _halo_specs(tm, w, off // w, R)
        prv.append(a)
        nxt.append(b)
    full = lambda a: pl.BlockSpec(a.shape, lambda i: (0,) * a.ndim)
    row = pl.BlockSpec((tm, W), lambda i: (i, 0))
    consts = [wts[k] for k in ('w0', 'a0', 'wup', 'aup', 'gup', 'kkw', 'ka', 'rk', 'hs')]
    bf = jax.ShapeDtypeStruct((R, W), BF16)
    f32 = jax.ShapeDtypeStruct((R, W), F32)
    return pl.pallas_call(
        functools.partial(_rwkv_prep_kernel, seq_len=seq_len),
        grid=(R // tm,),
        in_specs=cur + prv + nxt + [full(m) for m in wts['mu']] + [full(c) for c in consts],
        out_specs=[row] * 11,
        out_shape=[bf, bf, bf, f32, f32, bf, bf, bf, bf, bf, bf],
        compiler_params=pltpu.CompilerParams(dimension_semantics=("parallel",), vmem_limit_bytes=VMEM_LIMIT),
        name="rwkv_prep",
    )(*([proj] * 18), *wts['mu'], *consts)


def _rwkv_out_kernel(yf_ref, yb_ref, bonus_ref, g_ref, hm_ref, lng_ref, lnb_ref, o_ref):
    y = yf_ref[...] + yb_ref[...]
    hm = hm_ref[...]
    yc = y - _seg_sum(y, hm)
    yn = yc * lax.rsqrt(_seg_sum(yc * yc, hm) + RW_GN_EPS) * lng_ref[...] + lnb_ref[...]
    o_ref[...] = ((yn + bonus_ref[...].astype(F32)) * g_ref[...].astype(F32)).astype(o_ref.dtype)


def rwkv_out_pallas(yf, yb, bonus, g, wts):
    R, W = yf.shape
    tm = min(TM, R)
    row = pl.BlockSpec((tm, W), lambda i: (i, 0))
    full = lambda a: pl.BlockSpec(a.shape, lambda i: (0,) * a.ndim)
    return pl.pallas_call(
        _rwkv_out_kernel,
        grid=(R // tm,),
        in_specs=[row] * 4 + [full(wts['hmean']), full(wts['ln_g']), full(wts['ln_b'])],
        out_specs=row,
        out_shape=jax.ShapeDtypeStruct((R, W), BF16),
        compiler_params=pltpu.CompilerParams(dimension_semantics=("parallel",), vmem_limit_bytes=VMEM_LIMIT),
        name="rwkv_out",
    )(yf, yb, bonus, g, wts['hmean'], wts['ln_g'], wts['ln_b'])


SSD_T = SSM_CHUNK
SSD_GW = SSM_INNER // SSM_GROUPS
SSD_HG = SSM_HEADS // SSM_GROUPS
SSM_PIECES = ('ssm_x', 'ssm_b', 'ssm_c')


def _split_dot(x, m):
    xh = x.astype(BF16)
    xl = (x - xh.astype(F32)).astype(BF16)
    return jnp.dot(xh, m, preferred_element_type=F32) + jnp.dot(xl, m, preferred_element_type=F32)


def _ssd_prep_kernel(*refs, seq_len):
    cur, prv, nxt = refs[0:3], refs[3:6], refs[6:9]
    cw, cb = refs[9:12], refs[12:15]
    dt_ref, dtb_ref, perm_ref = refs[15:18]
    x_o, b_o, c_o, dt_o = refs[18:]
    for c, p, n, w, bias, o in zip(cur, prv, nxt, cw, cb, (x_o, b_o, c_o)):
        x = c[...].astype(F32)
        before, after = _neighbours(x, p[...].astype(F32), n[...].astype(F32), seq_len)
        y = before * w[0:1, :] + x * w[1:2, :] + after * w[2:3, :] + bias[...]
        o[...] = jax.nn.silu(y).astype(o.dtype)
    dt = jax.nn.softplus(dt_ref[...].astype(F32) + dtb_ref[...])
    dt_o[...] = _split_dot(dt, perm_ref[...])


def build_ssd_weights(p):
    w, bias = p['ssm_conv_w'], p['ssm_conv_b']
    cuts = [0, SSM_INNER, SSM_INNER + 256, SSM_INNER + 512]
    cw = [w[:, cuts[i]:cuts[i + 1]] for i in range(3)]
    cb = [bias[cuts[i]:cuts[i + 1]].reshape(1, -1) for i in range(3)]
    dtb = jnp.pad(p['ssm_dt_bias'].reshape(1, 2 * SSM_HEADS), ((0, 0), (0, 128 - 2 * SSM_HEADS)))
    perm = np.zeros((128, SSM_GROUPS * 128), np.float32)
    for d in range(2):
        for g in range(SSM_GROUPS):
            for j in range(SSD_HG):
                perm[d * SSM_HEADS + g * SSD_HG + j, g * 128 + d * SSD_HG + j] = 1.0
    a = -jnp.exp(p['ssm_a_log'].astype(F32))
    a_g = jnp.stack([jnp.pad(jnp.concatenate([a[0, g * SSD_HG:(g + 1) * SSD_HG], a[1, g * SSD_HG:(g + 1) * SSD_HG]]),
                             (0, 128 - 2 * SSD_HG)) for g in range(SSM_GROUPS)]).reshape(SSM_GROUPS, 1, 128)
    d_exp = jnp.repeat(p['ssm_d'], SSM_HEAD_DIM).reshape(1, SSM_INNER)
    return dict(cw=cw, cb=cb, dtb=dtb, perm=jnp.asarray(perm, BF16), a_g=a_g, d_exp=d_exp,
                norm=p['ssm_norm'].reshape(1, SSM_INNER))


def ssd_prep_pallas(proj, wts, seq_len):
    R = proj.shape[0]
    tm = min(TM, R)
    cur, prv, nxt = [], [], []
    for name in SSM_PIECES:
        off, w = PROJ[name]
        cur.append(pl.BlockSpec((tm, w), lambda i, o=off // w: (i, o)))
        a, b = _halo_specs(tm, w, off // w, R)
        prv.append(a)
        nxt.append(b)
    full = lambda a: pl.BlockSpec(a.shape, lambda i: (0,) * a.ndim)
    off, w = PROJ['ssm_dt']
    consts = wts['cw'] + wts['cb']
    row = lambda width: pl.BlockSpec((tm, width), lambda i: (i, 0))
    return pl.pallas_call(
        functools.partial(_ssd_prep_kernel, seq_len=seq_len),
        grid=(R // tm,),
        in_specs=cur + prv + nxt + [full(c) for c in consts]
        + [pl.BlockSpec((tm, w), lambda i: (i, off // w)), full(wts['dtb']), full(wts['perm'])],
        out_specs=[row(SSM_INNER), row(256), row(256), row(SSM_GROUPS * 128)],
        out_shape=[jax.ShapeDtypeStruct((R, SSM_INNER), BF16), jax.ShapeDtypeStruct((R, 256), BF16),
                   jax.ShapeDtypeStruct((R, 256), BF16), jax.ShapeDtypeStruct((R, SSM_GROUPS * 128), F32)],
        compiler_params=pltpu.CompilerParams(dimension_semantics=("parallel",), vmem_limit_bytes=VMEM_LIMIT),
        name="ssd_prep",
    )(*([proj] * 9), *consts, proj, wts['dtb'], wts['perm'])


def _ssd_chunks(chains, consts):
    T = SSD_T
    each = lambda f, *ls: [f(*xs) for xs in zip(*ls)]
    eye, cums, tris, expand, lane_head = consts
    S, xs, bm, cm, dt, a, dirs = (list(t) for t in zip(*chains))
    cum = [cums[d] for d in dirs]
    tri = [tris[d] for d in dirs]
    ex = [expand[d] for d in dirs]
    acs = each(lambda m, x, av: jnp.dot(m, x * av, precision=HI, preferred_element_type=F32), cum, dt, a)
    acs_t = each(lambda x: lax.dot_general(eye, x, (((1,), (1,)), ((), ())), precision=HI,
                                           preferred_element_type=F32), acs)
    dt_e = each(_split_dot, dt, ex)
    acs_e = each(_split_dot, acs, ex)
    atot_e = each(lambda x, d: x[0:1, :] if d else x[T - 1:T, :], acs_e, dirs)
    X = each(lambda x, y: x * y, xs, dt_e)
    cb = each(_mm_nt, cm, bm)
    y = []
    for c in range(len(chains)):
        parts = []
        for pr in range(SSD_HG // 2):
            acc = None
            for hh in range(2):
                i = dirs[c] * SSD_HG + 2 * pr + hh
                diff = acs[c][:, i:i + 1] - acs_t[c][i:i + 1, :]
                m = cb[c] * jnp.exp(jnp.where(tri[c], diff, -jnp.inf))
                xh = jnp.where(lane_head == hh, X[c][:, pr * 128:(pr + 1) * 128], 0.0)
                t = _mm(m, xh)
                acc = t if acc is None else acc + t
            parts.append(acc)
        y.append(jnp.concatenate(parts, axis=1))
    states = each(lambda b_, x, at, ac: _mm_tn(b_, x * jnp.exp(at - ac)), bm, X, atot_e, acs_e)
    y_off = each(lambda c_, s, ac: _mm(c_, s) * jnp.exp(ac), cm, S, acs_e)
    S_new = each(lambda s, at, st: s * jnp.exp(at) + st, S, atot_e, states)
    return [(yd + yo, sn) for yd, yo, sn in zip(y, y_off, S_new)]


def _ssd_consts():
    T = SSD_T
    row = lax.broadcasted_iota(jnp.int32, (T, T), 0)
    col = lax.broadcasted_iota(jnp.int32, (T, T), 1)
    eye = (row == col).astype(F32)
    cums = ((row >= col).astype(F32), (row <= col).astype(F32))
    tris = (row >= col, row <= col)
    er = lax.broadcasted_iota(jnp.int32, (128, SSD_GW), 0)
    ec = lax.broadcasted_iota(jnp.int32, (128, SSD_GW), 1)
    expand = tuple((er == d * SSD_HG + ec // SSM_HEAD_DIM).astype(BF16) for d in range(2))
    lane_head = lax.broadcasted_iota(jnp.int32, (T, 128), 1) // SSM_HEAD_DIM
    return eye, cums, tris, expand, lane_head


def _ssd_kernel(xf, bf, cf, dtf, xb, bb, cb_, dtb, a_ref, s0_ref, yf_ref, yb_ref, sT_ref, s_scr, *, nsub):
    j = pl.program_id(2)

    @pl.when(j == 0)
    def _():
        s_scr[...] = s0_ref[:, 0, 0]

    T = SSD_T
    consts = _ssd_consts()
    a = a_ref[0]

    def body(i, carry):
        sf = pl.ds(pl.multiple_of(i * T, T), T)
        sb = pl.ds(pl.multiple_of((nsub - 1 - i) * T, T), T)
        ld = lambda ref, sl: ref[0, sl, :].astype(F32)
        res = _ssd_chunks([(carry[0], ld(xf, sf), ld(bf, sf), ld(cf, sf), ld(dtf, sf), a, 0),
                           (carry[1], ld(xb, sb), ld(bb, sb), ld(cb_, sb), ld(dtb, sb), a, 1)], consts)
        yf_ref[0, sf, :] = res[0][0]
        yb_ref[0, sb, :] = res[1][0]
        return res[0][1], res[1][1]

    fin = lax.fori_loop(0, nsub, body, (s_scr[0], s_scr[1]))
    s_scr[0] = fin[0]
    s_scr[1] = fin[1]

    @pl.when(j == pl.num_programs(2) - 1)
    def _():
        sT_ref[0, 0, 0] = fin[0]
        sT_ref[1, 0, 0] = fin[1]


def ssd_scan_pallas(xs, bm, cm, dt, a_g, s0, *, tb=256):
    B, n, _ = xs.shape
    tb = min(tb, n)
    nb = n // tb
    G = SSM_GROUPS
    fwd = lambda w: pl.BlockSpec((1, tb, w), lambda b, g, j: (b, j, g))
    bwd = lambda w: pl.BlockSpec((1, tb, w), lambda b, g, j: (b, nb - 1 - j, g))
    st = pl.BlockSpec((2, 1, 1, SSM_STATE, SSD_GW), lambda b, g, j: (0, b, g, 0, 0))
    return pl.pallas_call(
        functools.partial(_ssd_kernel, nsub=tb // SSD_T),
        grid=(B, G, nb),
        in_specs=[fwd(SSD_GW), fwd(128), fwd(128), fwd(128), bwd(SSD_GW), bwd(128), bwd(128), bwd(128),
                  pl.BlockSpec((1, 1, 128), lambda b, g, j: (g, 0, 0)), st],
        out_specs=[fwd(SSD_GW), bwd(SSD_GW), st],
        out_shape=[jax.ShapeDtypeStruct((B, n, SSM_INNER), F32), jax.ShapeDtypeStruct((B, n, SSM_INNER), F32),
                   jax.ShapeDtypeStruct(s0.shape, F32)],
        scratch_shapes=[pltpu.VMEM((2, SSM_STATE, SSD_GW), F32)],
        compiler_params=pltpu.CompilerParams(dimension_semantics=("parallel", "parallel", "arbitrary"),
                                             vmem_limit_bytes=VMEM_LIMIT),
        name="ssd_scan",
    )(xs, bm, cm, dt, xs, bm, cm, dt, a_g, s0)


def _ssd_out_kernel(yf_ref, yb_ref, xs_ref, z_ref, d_ref, nw_ref, o_ref):
    y = (yf_ref[...] + yb_ref[...] + xs_ref[...].astype(F32) * d_ref[...]) * jax.nn.silu(z_ref[...].astype(F32))
    outs = []
    for g in range(SSM_GROUPS):
        yg = y[:, g * SSD_GW:(g + 1) * SSD_GW]
        outs.append(yg * lax.rsqrt(jnp.mean(yg * yg, -1, keepdims=True) + 1e-5))
    o_ref[...] = (jnp.concatenate(outs, axis=1) * nw_ref[...]).astype(o_ref.dtype)


def ssd_out_pallas(yf, yb, xs, proj, wts):
    R = yf.shape[0]
    tm = min(TM, R)
    row = pl.BlockSpec((tm, SSM_INNER), lambda i: (i, 0))
    vec = pl.BlockSpec((1, SSM_INNER), lambda i: (0, 0))
    zoff = PROJ['ssm_z'][0] // SSM_INNER
    return pl.pallas_call(
        _ssd_out_kernel,
        grid=(R // tm,),
        in_specs=[row, row, row, pl.BlockSpec((tm, SSM_INNER), lambda i: (i, zoff)), vec, vec],
        out_specs=row,
        out_shape=jax.ShapeDtypeStruct((R, SSM_INNER), BF16),
        compiler_params=pltpu.CompilerParams(dimension_semantics=("parallel",), vmem_limit_bytes=VMEM_LIMIT),
        name="ssd_out",
    )(yf, yb, xs, proj, wts['d_exp'], wts['norm'])


HY_PIECES = ('hy_x0', 'hy_x1', 'hy_v')
FFT_R = 128


def _hilo(x):
    hi = x.astype(BF16)
    return hi, (x - hi.astype(F32)).astype(BF16)


def _dot3(a_hi, a_lo, x):
    xh, xl = _hilo(x)
    d = lambda p, q: jnp.dot(p, q, preferred_element_type=F32)
    return d(a_hi, xh) + d(a_hi, xl) + d(a_lo, xh)


def _hilo_tables(pairs):
    out = {}
    for name, x in pairs:
        hi, lo = _hilo(jnp.asarray(x, F32))
        out[name + '_hi'], out[name + '_lo'] = hi, lo
    return out


def _hyena_prep_kernel(*refs, seq_len):
    cur, prv, nxt = refs[0:3], refs[3:6], refs[6:9]
    cw, cb = refs[9:12], refs[12:15]
    x0_o, v_o = refs[15:]
    ys = []
    for c, p, n, w, bias in zip(cur, prv, nxt, cw, cb):
        x = c[...].astype(F32)
        before, after = _neighbours(x, p[...].astype(F32), n[...].astype(F32), seq_len)
        ys.append(before * w[0:1, :] + x * w[1:2, :] + after * w[2:3, :] + bias[...])
    x0_o[...] = ys[0].astype(x0_o.dtype)
    v_o[...] = ys[2] * ys[1]


def hyena_prep_pallas(proj, cw, cb, seq_len):
    R = proj.shape[0]
    tm = min(TM, R)
    cur, prv, nxt = [], [], []
    for name in HY_PIECES:
        off, w = PROJ[name]
        cur.append(pl.BlockSpec((tm, w), lambda i, o=off // w: (i, o)))
        a, b = _halo_specs(tm, w, off // w, R)
        prv.append(a)
        nxt.append(b)
    full = lambda a: pl.BlockSpec(a.shape, lambda i: (0,) * a.ndim)
    row = pl.BlockSpec((tm, HY_WIDTH), lambda i: (i, 0))
    return pl.pallas_call(
        functools.partial(_hyena_prep_kernel, seq_len=seq_len),
        grid=(R // tm,),
        in_specs=cur + prv + nxt + [full(c) for c in cw + cb],
        out_specs=[row, row],
        out_shape=[jax.ShapeDtypeStruct((R, HY_WIDTH), BF16), jax.ShapeDtypeStruct((R, HY_WIDTH), F32)],
        compiler_params=pltpu.CompilerParams(dimension_semantics=("parallel",), vmem_limit_bytes=VMEM_LIMIT),
        name="hyena_prep",
    )(*([proj] * 9), *cw, *cb)


def filter_features(n):
    t = jnp.linspace(0.0, 1.0, n, dtype=F32)[:, None]
    wpos = 2.0 * math.pi * jnp.arange(n, dtype=F32)[:, None] / n
    f = jnp.linspace(1e-4, HY_BANDS - 1, HY_BANDS, dtype=F32)[None]
    z = jnp.concatenate([t, jnp.cos(f * wpos), -jnp.sin(f * wpos), t, jnp.ones_like(t)], -1)
    back = jnp.concatenate([jnp.zeros((1, z.shape[1]), F32), z[:0:-1]], 0)
    return jnp.pad(jnp.concatenate([z, back], 0), ((0, 0), (0, 128 - z.shape[1])))


def _filter_kernel(z_ref, w1_ref, b1_ref, w2_ref, b2_ref, fr_ref, w3_ref, dl_ref, o_ref):
    hp = lambda a, b: jnp.dot(a, b, precision=HI, preferred_element_type=F32)
    z = z_ref[...]
    fr = fr_ref[...]
    h = jnp.sin(fr * (hp(z, w1_ref[...]) + b1_ref[...]))
    h = jnp.sin(fr * (hp(h, w2_ref[...]) + b2_ref[...]))
    filt = hp(h, w3_ref[0])
    t = z[:, HY_EMB:HY_EMB + 1]
    o_ref[...] = filt * jnp.exp(-t * dl_ref[...]) * z[:, HY_EMB + 1:HY_EMB + 2]


def hyena_filter_pallas(n, p):
    z = filter_features(n)
    tr = min(512, n)
    nb = 2 * n // tr
    w1 = jnp.pad(p['hy_w1'], ((0, 128 - HY_EMB), (0, 0)))
    w3 = jnp.moveaxis(p['hy_w3'].reshape(HY_FILTER_HIDDEN, 2, HY_WIDTH), 1, 0)
    deltas = jnp.abs(jnp.linspace(HY_MIN_DECAY, HY_MAX_DECAY, HY_WIDTH, dtype=F32)).reshape(1, HY_WIDTH)
    vec = lambda a: a.reshape(1, -1)
    full = lambda a: pl.BlockSpec(a.shape, lambda i: (0,) * a.ndim)
    args = (w1, vec(p['hy_b1']), p['hy_w2'], vec(p['hy_b2']), vec(p['hy_freq']))
    return pl.pallas_call(
        _filter_kernel,
        grid=(nb,),
        in_specs=[pl.BlockSpec((tr, 128), lambda i: (i, 0))] + [full(a) for a in args]
        + [pl.BlockSpec((1, HY_FILTER_HIDDEN, HY_WIDTH), lambda i: (i // (nb // 2), 0, 0)), full(deltas)],
        out_specs=pl.BlockSpec((tr, HY_WIDTH), lambda i: (i, 0)),
        out_shape=jax.ShapeDtypeStruct((2 * n, HY_WIDTH), F32),
        compiler_params=pltpu.CompilerParams(dimension_semantics=("parallel",), vmem_limit_bytes=VMEM_LIMIT),
        name="hyena_filter",
    )(z, *args, w3, deltas)


FFT_TN = 4096
FFT_KB = 8
FFT_TC = 256


def dft_tables():
    r = np.arange(FFT_R, dtype=np.float64)
    th1 = 2.0 * np.pi * np.outer(r, r) / FFT_R
    k = r[:, None, None] + FFT_R * r[None, :, None]
    phi = 2.0 * np.pi * k * r[None, None, :] / (FFT_R * FFT_R)
    return _hilo_tables((('c1', np.cos(th1)), ('s1', np.sin(th1)), ('gc', np.cos(phi)), ('gs', np.sin(phi)),
                         ('gct', np.swapaxes(np.cos(phi), 1, 2)), ('gst', np.swapaxes(np.sin(phi), 1, 2))))


def _dft_l1_kernel(x_ref, ch, cl, sh, sl, ar_ref, ai_ref):
    x = x_ref[0]
    ar_ref[0] = _dot3(ch[...], cl[...], x)
    ai_ref[0] = -_dot3(sh[...], sl[...], x)


def dft_level1(x2, tabs):
    B, nb, L = x2.shape
    cut = lambda a: a[:, :nb]
    consts = [cut(tabs['c1_hi']), cut(tabs['c1_lo']), cut(tabs['s1_hi']), cut(tabs['s1_lo'])]
    full = lambda a: pl.BlockSpec(a.shape, lambda b, j: (0,) * a.ndim)
    out = pl.BlockSpec((1, FFT_R, FFT_TN), lambda b, j: (b, 0, j))
    return pl.pallas_call(
        _dft_l1_kernel,
        grid=(B, L // FFT_TN),
        in_specs=[pl.BlockSpec((1, nb, FFT_TN), lambda b, j: (b, 0, j))] + [full(c) for c in consts],
        out_specs=[out, out],
        out_shape=[jax.ShapeDtypeStruct((B, FFT_R, L), F32)] * 2,
        compiler_params=pltpu.CompilerParams(dimension_semantics=("parallel", "parallel"),
                                             vmem_limit_bytes=VMEM_LIMIT),
        name="dft_level1",
    )(x2, *consts)


def _level2_fwd(ar, ai, gch, gcl, gsh, gsl):
    xr = _dot3(gch, gcl, ar) + _dot3(gsh, gsl, ai)
    xi = _dot3(gch, gcl, ai) - _dot3(gsh, gsl, ar)
    return xr, xi


def _spectrum_kernel(ar_ref, ai_ref, gch, gcl, gsh, gsl, hr_ref, hi_ref):
    for q in range(FFT_KB):
        xr, xi = _level2_fwd(ar_ref[0, q], ai_ref[0, q], gch[q], gcl[q], gsh[q], gsl[q])
        hr_ref[0, q] = xr
        hi_ref[0, q] = xi


def _conv_l2_kernel(ar_ref, ai_ref, hr_ref, hi_ref, gch, gcl, gsh, gsl, tch, tcl, tsh, tsl, br_ref, bi_ref):
    for q in range(FFT_KB):
        xr, xi = _level2_fwd(ar_ref[0, q], ai_ref[0, q], gch[q], gcl[q], gsh[q], gsl[q])
        hr, hi = hr_ref[0, q], hi_ref[0, q]
        yr = xr * hr - xi * hi
        yi = xr * hi + xi * hr
        br_ref[0, q] = _dot3(tch[q], tcl[q], yr) - _dot3(tsh[q], tsl[q], yi)
        bi_ref[0, q] = _dot3(tch[q], tcl[q], yi) + _dot3(tsh[q], tsl[q], yr)


def _l2_specs():
    data = pl.BlockSpec((1, FFT_KB, FFT_R, FFT_TC), lambda b, c, k: (b, k, 0, c))
    filt = pl.BlockSpec((1, FFT_KB, FFT_R, FFT_TC), lambda b, c, k: (0, k, 0, c))
    tab = pl.BlockSpec((FFT_KB, FFT_R, FFT_R), lambda b, c, k: (k, 0, 0))
    return data, filt, tab


def dft_spectrum(ar, ai, tabs):
    B, _, _, C = ar.shape
    data, _, tab = _l2_specs()
    return pl.pallas_call(
        _spectrum_kernel,
        grid=(B, C // FFT_TC, FFT_R // FFT_KB),
        in_specs=[data, data] + [tab] * 4,
        out_specs=[data, data],
        out_shape=[jax.ShapeDtypeStruct(ar.shape, F32)] * 2,
        compiler_params=pltpu.CompilerParams(dimension_semantics=("parallel", "parallel", "parallel"),
                                             vmem_limit_bytes=VMEM_LIMIT),
        name="dft_spectrum",
    )(ar, ai, tabs['gc_hi'], tabs['gc_lo'], tabs['gs_hi'], tabs['gs_lo'])


def conv_level2(ar, ai, hr, hi, tabs):
    B, _, _, C = ar.shape
    data, filt, tab = _l2_specs()
    names = ('gc_hi', 'gc_lo', 'gs_hi', 'gs_lo', 'gct_hi', 'gct_lo', 'gst_hi', 'gst_lo')
    return pl.pallas_call(
        _conv_l2_kernel,
        grid=(B, C // FFT_TC, FFT_R // FFT_KB),
        in_specs=[data, data, filt, filt] + [tab] * 8,
        out_specs=[data, data],
        out_shape=[jax.ShapeDtypeStruct(ar.shape, F32)] * 2,
        compiler_params=pltpu.CompilerParams(dimension_semantics=("parallel", "parallel", "parallel"),
                                             vmem_limit_bytes=VMEM_LIMIT),
        name="conv_level2",
    )(ar, ai, hr, hi, *[tabs[k] for k in names])


def _idft_l1_kernel(br_ref, bi_ref, v_ref, x0_ref, d_ref, ch, cl, sh, sl, o_ref, *, inv_n):
    y = (_dot3(ch[...], cl[...], br_ref[0]) - _dot3(sh[...], sl[...], bi_ref[0])) * inv_n
    o_ref[0] = ((y + v_ref[0] * d_ref[...]) * x0_ref[0].astype(F32)).astype(o_ref.dtype)


def idft_level1_epilogue(br, bi, v2, x02, d_t, tabs):
    B, nb, L = v2.shape
    cut = lambda a: a[:nb, :]
    consts = [cut(tabs['c1_hi']), cut(tabs['c1_lo']), cut(tabs['s1_hi']), cut(tabs['s1_lo'])]
    full = lambda a: pl.BlockSpec(a.shape, lambda b, j: (0,) * a.ndim)
    big = pl.BlockSpec((1, FFT_R, FFT_TN), lambda b, j: (b, 0, j))
    half = pl.BlockSpec((1, nb, FFT_TN), lambda b, j: (b, 0, j))
    return pl.pallas_call(
        functools.partial(_idft_l1_kernel, inv_n=1.0 / (FFT_R * FFT_R)),
        grid=(B, L // FFT_TN),
        in_specs=[big, big, half, half, pl.BlockSpec((1, FFT_TN), lambda b, j: (0, j))] + [full(c) for c in consts],
        out_specs=half,
        out_shape=jax.ShapeDtypeStruct((B, nb, L), BF16),
        compiler_params=pltpu.CompilerParams(dimension_semantics=("parallel", "parallel"),
                                             vmem_limit_bytes=VMEM_LIMIT),
        name="idft_level1",
    )(br, bi, v2, x02, d_t, *consts)


def hyena_long_conv_fft(v, x0, kfull, hy_d, tabs):
    B, n, C = v.shape
    nb = n // FFT_R
    L = FFT_R * C
    kr, ki = dft_level1(kfull.reshape(1, 2 * nb, L), tabs)
    hr, hi = dft_spectrum(kr.reshape(1, FFT_R, FFT_R, C), ki.reshape(1, FFT_R, FFT_R, C), tabs)
    ar, ai = dft_level1(v.reshape(B, nb, L), tabs)
    br, bi = conv_level2(ar.reshape(B, FFT_R, FFT_R, C), ai.reshape(B, FFT_R, FFT_R, C), hr, hi, tabs)
    d_t = jnp.tile(hy_d.reshape(1, C), (1, FFT_R))
    out = idft_level1_epilogue(br.reshape(B, FFT_R, L), bi.reshape(B, FFT_R, L), v.reshape(B, nb, L),
                               x0.reshape(B, nb, L), d_t, tabs)
    return out.reshape(B, n, C)


def dense_tables(n):
    t = np.arange(2 * n, dtype=np.float64)
    th = 2.0 * np.pi * np.outer(t, t) / (2 * n)
    return _hilo_tables((('c', np.cos(th)), ('s', np.sin(th))))


def _dense_conv_kernel(v_ref, x0_ref, k_ref, d_ref, ch, cl, sh, sl, o_ref, *, n):
    c_hi, c_lo, s_hi, s_lo = ch[...], cl[...], sh[...], sl[...]
    k = k_ref[...]
    hr = _dot3(c_hi, c_lo, k)
    hi = -_dot3(s_hi, s_lo, k)
    v = v_ref[0]
    xr = _dot3(c_hi[:, :n], c_lo[:, :n], v)
    xi = -_dot3(s_hi[:, :n], s_lo[:, :n], v)
    yr = xr * hr - xi * hi
    yi = xr * hi + xi * hr
    y = (_dot3(c_hi[:n, :], c_lo[:n, :], yr) - _dot3(s_hi[:n, :], s_lo[:n, :], yi)) * (0.5 / n)
    o_ref[0] = ((y + v * d_ref[...]) * x0_ref[0].astype(F32)).astype(o_ref.dtype)


def hyena_long_conv_dense(v, x0, kfull, hy_d, tabs):
    B, n, C = v.shape
    tc = 256
    seq = pl.BlockSpec((1, n, tc), lambda b, c: (b, 0, c))
    full = lambda a: pl.BlockSpec(a.shape, lambda b, c: (0,) * a.ndim)
    consts = [tabs[k] for k in ('c_hi', 'c_lo', 's_hi', 's_lo')]
    return pl.pallas_call(
        functools.partial(_dense_conv_kernel, n=n),
        grid=(B, C // tc),
        in_specs=[seq, seq, pl.BlockSpec((2 * n, tc), lambda b, c: (0, c)), pl.BlockSpec((1, tc), lambda b, c: (0, c))]
        + [full(c) for c in consts],
        out_specs=seq,
        out_shape=jax.ShapeDtypeStruct((B, n, C), BF16),
        compiler_params=pltpu.CompilerParams(dimension_semantics=("parallel", "parallel"),
                                             vmem_limit_bytes=VMEM_LIMIT),
        name="hyena_dense_conv",
    )(v, x0, kfull, hy_d.reshape(1, C), *consts)


def build_hyena_weights(p):
    w, bias = p['hy_conv_w'], p['hy_conv_b']
    cw = [w[:, i * HY_WIDTH:(i + 1) * HY_WIDTH] for i in range(3)]
    cb = [bias[i * HY_WIDTH:(i + 1) * HY_WIDTH].reshape(1, -1) for i in range(3)]
    return cw, cb


def hyena_branch_pallas(proj, p, tabs, b, n):
    x0, v = hyena_prep_pallas(proj, p['hy_w'][0], p['hy_w'][1], n)
    kfull = hyena_filter_pallas(n, p)
    r3 = lambda a: a.reshape(b, n, HY_WIDTH)
    if 2 * n == FFT_R * FFT_R:
        y = hyena_long_conv_fft(r3(v), r3(x0), kfull, p['hy_d'], tabs['fft'])
    else:
        y = hyena_long_conv_dense(r3(v), r3(x0), kfull, p['hy_d'], dense_tables(n))
    return y.reshape(b * n, HY_WIDTH)


ADA_TN = 1024


def _ada_kernel(c_ref, w_ref, b_ref, o_ref):
    o_ref[...] = jnp.dot(jax.nn.silu(c_ref[...]).astype(BF16), w_ref[...].astype(BF16),
                         preferred_element_type=F32) + b_ref[...]


def ada_modulation(cond, w_ada, b_ada):
    m, d = cond.shape
    n = w_ada.shape[1]
    return pl.pallas_call(
        _ada_kernel,
        grid=(n // ADA_TN,),
        in_specs=[pl.BlockSpec((m, d), lambda j: (0, 0)), pl.BlockSpec((d, ADA_TN), lambda j: (0, j)),
                  pl.BlockSpec((1, ADA_TN), lambda j: (0, j))],
        out_specs=pl.BlockSpec((m, ADA_TN), lambda j: (0, j)),
        out_shape=jax.ShapeDtypeStruct((m, n), F32),
        compiler_params=pltpu.CompilerParams(dimension_semantics=("parallel",), vmem_limit_bytes=VMEM_LIMIT),
        name="ada_modulation",
    )(cond, w_ada, b_ada.reshape(1, n))


def expert_choice_ffn(h2, logits, ffn_w, b, n):
    cap = EC_CAPACITY * n // N_EXPERTS
    aff = jax.nn.softmax(logits[:, :N_EXPERTS].reshape(b, n, N_EXPERTS), -1)
    gate, idx = lax.top_k(jnp.swapaxes(aff, 1, 2), cap)
    bidx = jnp.arange(b)[:, None, None]
    xs = h2.reshape(b, n, D_MODEL)[bidx, idx]
    ye = expert_ffn(xs, gate[..., None], *ffn_w)
    return jnp.zeros((b, n, D_MODEL), F32).at[bidx, idx].add(ye).reshape(b * n, D_MODEL)


def _proj_piece(proj, name, width=None):
    off, w = PROJ[name]
    return proj[:, off:off + (width or w)].astype(F32)


def trunk_layer(xc, xl, mods, p, tabs, need_ctx, b, n_ctx, n_lat):
    D = D_MODEL
    piece = lambda rows, k: mods[rows, None, k * D:(k + 1) * D]
    lat = dict(x=xl, n=n_lat, rpm=n_lat, rows=slice(0, b), cos=tabs['cos_l'], sin=tabs['sin_l'])
    ctx = dict(x=xc, n=n_ctx, rpm=b * n_ctx, rows=slice(b, b + 1), cos=tabs['cos_c'], sin=tabs['sin_c'])
    for t in (ctx, lat):
        t['proj'] = inproj(t['x'], piece(t['rows'], 0), piece(t['rows'], 1), p['w_in_p'], t['rpm'])
        t['q'], t['k'], t['v'] = mla_proj(t['proj'], t['cos'], t['sin'], p['mla_q_norm'], p['mla_kv_norm'], p['mla_w'])
        t['rw'] = rwkv_prep_pallas(t['proj'], p['rw_w'], t['n'])
    sets = (ctx, lat) if need_ctx else (lat,)

    r3 = lambda a, n: a.reshape(b, n, a.shape[-1])
    k_all = jnp.concatenate([r3(ctx['k'], n_ctx), r3(lat['k'], n_lat)], 1)
    v_all = jnp.concatenate([r3(ctx['v'], n_ctx), r3(lat['v'], n_lat)], 1)
    nk = n_ctx + n_lat
    tk = 768 if nk % 768 == 0 else 256
    lat['mla'] = flash_attention(r3(lat['q'], n_lat), k_all, v_all, tq=min(n_lat, 1024), tk=tk, out_dtype=BF16)
    if need_ctx:
        ctx['mla'] = flash_attention(r3(ctx['q'], n_ctx), r3(ctx['k'], n_ctx), r3(ctx['v'], n_ctx), tq=n_ctx, tk=n_ctx,
                                     out_dtype=BF16)

    s0 = jnp.zeros((2, b, SSM_GROUPS, SSM_STATE, SSD_GW), F32)
    for t in (ctx, lat):
        xs, bm, cm, dt = ssd_prep_pallas(t['proj'], p['ssd_w'], t['n'])
        yf, yb, s0 = ssd_scan_pallas(r3(xs, t['n']), r3(bm, t['n']), r3(cm, t['n']), r3(dt, t['n']),
                                     p['ssd_w']['a_g'], s0)
        if t in sets:
            flat = lambda a: a.reshape(b * t['n'], SSM_INNER)
            t['ssm'] = ssd_out_pallas(flat(yf), flat(yb), xs, t['proj'], p['ssd_w'])

    for t in sets:
        t['hy'] = hyena_branch_pallas(t['proj'], p, tabs, b, t['n'])

    s0 = jnp.zeros((2, b, RW_WIDTH // RW_P, RW_P, RW_P), F32)
    for t in (ctx, lat):
        r_, v_, kk_, lw0, lw1, kd0, kd1, bd0, bd1, g_, bonus = (r3(a, t['n']) for a in t['rw'])
        yf, yb, s0 = rwkv_scan_pallas(r_, v_, kk_, lw0, lw1, kd0, kd1, bd0, bd1, s0)
        if t in sets:
            flat = lambda a: a.reshape(b * t['n'], RW_WIDTH)
            t['rwkv'] = rwkv_out_pallas(flat(yf), flat(yb), flat(bonus), flat(g_), p['rw_w'])

    outs = []
    for t in sets:
        rows = b * t['n']
        branches = [t['mla'].reshape(rows, BRANCH_WIDTH), t['ssm'], t['hy'], t['rwkv']]
        merged = merge_branches_pallas(branches, t['proj'], p['w_branch_b'])
        x1, h2, logits = outproj_postnorm(merged, t['x'], piece(t['rows'], 2), piece(t['rows'], 3), piece(t['rows'], 4),
                                          p['ln1_g'], p['ln1_b'], p['w_out_b'], p['w_router'], t['rpm'])
        moe = expert_choice_ffn(h2, logits, p['ffn_w'], b, t['n'])
        outs.append(postnorm(x1, moe, piece(t['rows'], 5), p['ln2_g'], p['ln2_b'], t['rpm']))
    return (outs[0], outs[1]) if need_ctx else (None, outs[0])


def kernel(x, c, ctx, c_ctx, w_ada, b_ada, w_in, mla_q_norm, mla_w_q_up, mla_kv_norm, mla_w_kv_up,
           ssm_conv_w, ssm_conv_b, ssm_dt_bias, ssm_a_log, ssm_d, ssm_norm,
           hy_conv_w, hy_conv_b, hy_w1, hy_b1, hy_w2, hy_b2, hy_w3, hy_freq, hy_d,
           rw_mu, rw_w0, rw_w_up, rw_a0, rw_a_up, rw_g_up, rw_kk, rw_ka, rw_rk, rw_ln_g, rw_ln_b,
           w_branch, w_out, ln1_g, ln1_b, w_router, w_gate_e, w_up_e, w_down_e, ln2_g, ln2_b):
    stacked = dict(
        w_in=w_in, mla_q_norm=mla_q_norm, mla_w_q_up=mla_w_q_up, mla_kv_norm=mla_kv_norm, mla_w_kv_up=mla_w_kv_up,
        ssm_conv_w=ssm_conv_w, ssm_conv_b=ssm_conv_b, ssm_dt_bias=ssm_dt_bias, ssm_a_log=ssm_a_log,
        ssm_d=ssm_d, ssm_norm=ssm_norm,
        hy_conv_w=hy_conv_w, hy_conv_b=hy_conv_b, hy_w1=hy_w1, hy_b1=hy_b1, hy_w2=hy_w2, hy_b2=hy_b2,
        hy_w3=hy_w3, hy_freq=hy_freq, hy_d=hy_d,
        rw_mu=rw_mu, rw_w0=rw_w0, rw_w_up=rw_w_up, rw_a0=rw_a0, rw_a_up=rw_a_up, rw_g_up=rw_g_up,
        rw_kk=rw_kk, rw_ka=rw_ka, rw_rk=rw_rk, rw_ln_g=rw_ln_g, rw_ln_b=rw_ln_b,
        w_branch=w_branch, w_out=w_out, ln1_g=ln1_g, ln1_b=ln1_b,
        w_router=w_router, w_gate_e=w_gate_e, w_up_e=w_up_e, w_down_e=w_down_e, ln2_g=ln2_g, ln2_b=ln2_b)
    b, n_lat, D = x.shape
    n_ctx = ctx.shape[1]
    cos, sin = rope_tables_128(n_lat)
    ones = jnp.pad(jnp.ones((b * n_ctx, QK_ROPE), F32), ((0, 0), (0, 128 - QK_ROPE)))
    tabs = dict(cos_l=jnp.tile(cos, (b, 1)), sin_l=jnp.tile(sin, (b, 1)), cos_c=ones, sin_c=jnp.zeros_like(ones),
                fft=dft_tables())
    cond = jnp.concatenate([c, c_ctx[None], jnp.zeros((8 - b - 1, D), F32)], 0)
    xc, xl = ctx.reshape(b * n_ctx, D), x.reshape(b * n_lat, D)
    for i in range(DEPTH):
        p = {name: arr[i] for name, arr in stacked.items()}
        p['w_in_p'] = build_w_in(p['w_in'])
        p['mla_w'] = build_mla_weights(p)
        p['rw_w'] = build_rwkv_weights(p)
        p['ssd_w'] = build_ssd_weights(p)
        p['hy_w'] = build_hyena_weights(p)
        p['w_branch_b'] = p['w_branch'].astype(BF16)
        p['w_out_b'] = p['w_out'].astype(BF16)
        p['ffn_w'] = build_expert_weights(p)
        mods = ada_modulation(cond, w_ada[i], b_ada[i])
        xc, xl = trunk_layer(xc, xl, mods, p, tabs, i < DEPTH - 1, b, n_ctx, n_lat)
    return xl.reshape(b, n_lat, D)
```

```python
import functools
import math

import jax
import jax.numpy as jnp
from jax import lax
import numpy as np
from jax.experimental import pallas as pl
from jax.experimental.pallas import tpu as pltpu

D_MODEL = 2048
DEPTH = 2

GRID_W = 64
N_BRANCH = 4
BRANCH_WIDTH = D_MODEL // 2

MLA_Q_LORA = D_MODEL // 4
MLA_KV_LORA = D_MODEL // 4
QK_NOPE = 128
QK_ROPE = 64
V_HEAD = 128
MLA_HEADS = BRANCH_WIDTH // V_HEAD
ROPE_BASE = 10000.0

SSM_HEAD_DIM = 64
SSM_HEADS = BRANCH_WIDTH // SSM_HEAD_DIM
SSM_INNER = SSM_HEADS * SSM_HEAD_DIM
SSM_GROUPS = 2
SSM_STATE = 128
SSM_CHUNK = 128
SSM_XBC = SSM_INNER + 2 * SSM_GROUPS * SSM_STATE

HY_WIDTH = BRANCH_WIDTH
HY_EMB = 33
HY_BANDS = (HY_EMB - 1) // 2
HY_FILTER_HIDDEN = 64
HY_TARGET = 1e-2
HY_SLOW_FRAC = 1.5
HY_QUICK_FRAC = 0.3
HY_MIN_DECAY = math.log(HY_TARGET) / HY_SLOW_FRAC
HY_MAX_DECAY = math.log(HY_TARGET) / HY_QUICK_FRAC

RW_HEAD_DIM = 64
RW_WIDTH = BRANCH_WIDTH
RW_HEADS = RW_WIDTH // RW_HEAD_DIM
RW_DECAY_LORA = 64
RW_ICLR_LORA = 64
RW_GATE_LORA = 160
RW_GN_EPS = 64e-5

N_EXPERTS = 16
EXPERT_FF = 1408
EC_CAPACITY = 2

ALPHA = (2 * DEPTH) ** 0.25

MLA_COLS = MLA_Q_LORA + MLA_KV_LORA + QK_ROPE
SSM_COLS = SSM_INNER + SSM_XBC + 2 * SSM_HEADS
HY_COLS = 3 * HY_WIDTH
RW_COLS = 3 * RW_WIDTH + 2 * RW_DECAY_LORA + 2 * RW_ICLR_LORA + RW_GATE_LORA

F32 = jnp.float32
BF16 = jnp.bfloat16
HI = lax.Precision.HIGHEST

RW_T = 64
RW_P = 128
RW_PAIRS_PER_STEP = 4
RW_INV_BASE = 8
ATT_DK = 256
ATT_DV = 128
TM = 512
VMEM_LIMIT = 48 * 1024 * 1024


def _mm(a, b):
    return jnp.dot(a.astype(BF16), b.astype(BF16), preferred_element_type=F32)


def _mm_nt(a, b):
    return lax.dot_general(a.astype(BF16), b.astype(BF16), (((1,), (1,)), ((), ())), preferred_element_type=F32)


def _mm_tn(a, b):
    return lax.dot_general(a.astype(BF16), b.astype(BF16), (((0,), (0,)), ((), ())), preferred_element_type=F32)


def _ln(x, eps=1e-6):
    mu = jnp.mean(x, -1, keepdims=True)
    xc = x - mu
    return xc * lax.rsqrt(jnp.mean(xc * xc, -1, keepdims=True) + eps)


def _rwkv_pair_chunks(chains, inv_masks):
    T = RW_T
    each = lambda f, *ls: [f(*xs) for xs in zip(*ls)]
    S = [ch[0] for ch in chains]
    r, lw, k, v, kk, b = ([ch[1][i] for ch in chains] for i in range(6))
    a = [-x for x in kk]
    rev = [ch[2] for ch in chains]
    cum, head_lo, strict_bd, incl_bd, eye_bd = ([ch[3][i] for ch in chains] for i in range(5))

    def stack(x, lo):
        return jnp.concatenate([jnp.where(lo, x, 0.0), jnp.where(lo, 0.0, x)], axis=0)

    c = each(lambda m, x: jnp.dot(m, x, precision=HI, preferred_element_type=F32), cum, lw)
    cT = each(lambda x, rv: x[0:1, :] if rv else x[T - 1:T, :], c, rev)
    e_neg = each(lambda x: jnp.exp(-x), c)
    e_end = each(lambda x, y: jnp.exp(x - y), cT, c)
    a_s = each(lambda x, cc, l, lo: stack(x * jnp.exp(cc - l), lo), a, c, lw, head_lo)
    r_s = each(lambda x, cc, lo: stack(x * jnp.exp(cc), lo), r, c, head_lo)
    b_s = each(lambda x, e, lo: stack(x * e, lo), b, e_neg, head_lo)
    k_s = each(lambda x, e, lo: stack(x * e, lo), k, e_neg, head_lo)
    bh_s = each(lambda x, e, lo: stack(x * e, lo), b, e_end, head_lo)
    kh_s = each(lambda x, e, lo: stack(x * e, lo), k, e_end, head_lo)
    v_s = each(stack, v, head_lo)
    g_T = each(jnp.exp, cT)

    ar = each(lambda x, y: jnp.concatenate([x, y], axis=0), a_s, r_s)
    gb = each(_mm_nt, ar, b_s)
    gk = each(_mm_nt, ar, k_s)
    n = each(lambda m, g: jnp.where(m, g[:2 * T], 0.0), strict_bd, gb)
    l_ak = each(lambda m, g: jnp.where(m, g[:2 * T], 0.0), strict_bd, gk)
    m_rb = each(lambda m, g: jnp.where(m, g[2 * T:], 0.0), incl_bd, gb)
    m_rk = each(lambda m, g: jnp.where(m, g[2 * T:], 0.0), incl_bd, gk)
    lv = each(_mm, l_ak, v_s)
    n8 = each(lambda x: jnp.where(inv_masks[0], x, 0.0), n)
    p = each(lambda e, x: e + x, eye_bd, n8)
    m = each(_mm, n8, n8)
    p = each(lambda x, y: x + _mm(y, x), p, m)
    m = each(_mm, m, m)
    p = each(lambda x, y: x + _mm(y, x), p, m)
    for merge in inv_masks[1:]:
        n_off = each(lambda x: jnp.where(merge, x, 0.0), n)
        p = each(lambda x, y: x + _mm(_mm(x, y), x), p, n_off)
    w = each(_mm, p, a_s)
    u0 = each(_mm, p, lv)
    wr = each(lambda x, y, s: _mm_nt(jnp.concatenate([x, y], axis=0), s), w, r_s, S)
    uv = each(lambda x, y, z: jnp.concatenate([x[:2 * T] + y, z], axis=0), wr, u0, v_s)
    y_s = each(lambda x, p1, p2, z: x[2 * T:] + _mm(jnp.concatenate([p1, p2], axis=1), z), wr, m_rb, m_rk, uv)
    y = each(lambda x: x[:T] + x[T:], y_s)
    S_new = each(lambda s, g, z, p1, p2: s * g + _mm_tn(z, jnp.concatenate([p1, p2], axis=0)),
                 S, g_T, uv, bh_s, kh_s)
    return list(zip(y, S_new))


def _rwkv_consts():
    T = RW_T
    row = lax.broadcasted_iota(jnp.int32, (T, T), 0)
    col = lax.broadcasted_iota(jnp.int32, (T, T), 1)
    cum_f = (row >= col).astype(F32)
    cum_b = (row <= col).astype(F32)
    head_lo = lax.broadcasted_iota(jnp.int32, (T, RW_P), 1) < RW_P // 2
    r2 = lax.broadcasted_iota(jnp.int32, (2 * T, 2 * T), 0)
    c2 = lax.broadcasted_iota(jnp.int32, (2 * T, 2 * T), 1)
    same = (r2 >= T) == (c2 >= T)
    eye_bd = (r2 == c2).astype(F32)
    fwd = (cum_f, head_lo, same & (r2 > c2), same & (r2 >= c2), eye_bd)
    bwd = (cum_b, head_lo, same & (r2 < c2), same & (r2 <= c2), eye_bd)
    blk = lambda s: (r2 // s) == (c2 // s)
    inv_masks = [blk(RW_INV_BASE)]
    s = RW_INV_BASE
    while s < T:
        inv_masks.append(blk(2 * s) & jnp.logical_not(blk(s)))
        s *= 2
    return fwd, bwd, inv_masks


def _rwkv_kernel(rf, vf, af, lwf, kf, bf, rb, vb, ab, lwb, kb, bb, s0_ref, yf_ref, yb_ref, sT_ref, s_scr, *, nsub):
    j = pl.program_id(2)

    @pl.when(j == 0)
    def _():
        s_scr[...] = s0_ref[:, 0]

    T = RW_T
    cf, cb, inv_masks = _rwkv_consts()

    def body(i, carry):
        sf = pl.ds(pl.multiple_of(i * T, T), T)
        sb = pl.ds(pl.multiple_of((nsub - 1 - i) * T, T), T)
        lanes = [slice(q * RW_P, (q + 1) * RW_P) for q in range(RW_PAIRS_PER_STEP)]
        ins = []
        for ln in lanes:
            ins.append(tuple(ref[0, sf, ln].astype(F32) for ref in (rf, lwf, kf, vf, af, bf)))
            ins.append(tuple(ref[0, sb, ln].astype(F32) for ref in (rb, lwb, kb, vb, ab, bb)))
        res = _rwkv_pair_chunks([(carry[c], ins[c], bool(c % 2), cb if c % 2 else cf)
                                 for c in range(2 * RW_PAIRS_PER_STEP)], inv_masks)
        for q, ln in enumerate(lanes):
            yf_ref[0, sf, ln] = res[2 * q][0]
            yb_ref[0, sb, ln] = res[2 * q + 1][0]
        return tuple(s for _, s in res)

    init = tuple(s_scr[d, q] for q in range(RW_PAIRS_PER_STEP) for d in range(2))
    fin = lax.fori_loop(0, nsub, body, init)
    for q in range(RW_PAIRS_PER_STEP):
        for d in range(2):
            s_scr[d, q] = fin[2 * q + d]

    @pl.when(j == pl.num_programs(2) - 1)
    def _():
        for q in range(RW_PAIRS_PER_STEP):
            for d in range(2):
                sT_ref[d, 0, q] = fin[2 * q + d]


def rwkv_scan_pallas(r, v, kk, lw0, lw1, kd0, kd1, bd0, bd1, s0, *, tb=256):
    B, n, C = r.shape
    tb = min(tb, n)
    nb = n // tb
    pp = RW_PAIRS_PER_STEP
    assert n % tb == 0 and tb % RW_T == 0 and C % (pp * RW_P) == 0
    fw = pl.BlockSpec((1, tb, pp * RW_P), lambda b, p, j: (b, j, p))
    bw = pl.BlockSpec((1, tb, pp * RW_P), lambda b, p, j: (b, nb - 1 - j, p))
    st = pl.BlockSpec((2, 1, pp, RW_P, RW_P), lambda b, p, j: (0, b, p, 0, 0))
    return pl.pallas_call(
        functools.partial(_rwkv_kernel, nsub=tb // RW_T),
        grid=(B, C // (pp * RW_P), nb),
        in_specs=[fw] * 6 + [bw] * 6 + [st],
        out_specs=[fw, bw, st],
        out_shape=[jax.ShapeDtypeStruct((B, n, C), F32), jax.ShapeDtypeStruct((B, n, C), F32),
                   jax.ShapeDtypeStruct(s0.shape, F32)],
        scratch_shapes=[pltpu.VMEM((2, pp, RW_P, RW_P), F32)],
        compiler_params=pltpu.CompilerParams(dimension_semantics=("parallel", "parallel", "arbitrary")),
        name="rwkv_scan",
    )(r, v, kk, lw0, kd0, bd0, r, v, kk, lw1, kd1, bd1, s0)


FLASH_STREAMS = 2


def _flash_update(q_ref, k, vt, m_scr, acc_scr):
    tq = q_ref.shape[1]
    cols = [slice(i * tq // FLASH_STREAMS, (i + 1) * tq // FLASH_STREAMS) for i in range(FLASH_STREAMS)]
    s = [lax.dot_general(k, q_ref[0, c, :], (((1,), (1,)), ((), ())), preferred_element_type=F32) for c in cols]
    m_prev = [m_scr[:, c] for c in cols]
    m_new = [jnp.maximum(mp, jnp.max(x, axis=0, keepdims=True)) for mp, x in zip(m_prev, s)]
    p = [jnp.exp((x - mn).astype(BF16)) for x, mn in zip(s, m_new)]
    pv = [jnp.dot(vt, x, preferred_element_type=F32) for x in p]
    for c, mp, mn, x in zip(cols, m_prev, m_new, pv):
        acc_scr[:, c] = jnp.exp(mp - mn) * acc_scr[:, c] + x
        m_scr[:, c] = mn


def _flash_kernel(q_ref, kc_ref, vtc_ref, *rest, with_latent):
    if with_latent:
        kl_ref, vtl_ref, o_ref, m_scr, acc_scr = rest
    else:
        o_ref, m_scr, acc_scr = rest
    j = pl.program_id(3)

    @pl.when(j == 0)
    def _():
        m_scr[...] = jnp.full(m_scr.shape, -jnp.inf, F32)
        acc_scr[...] = jnp.zeros(acc_scr.shape, F32)
        _flash_update(q_ref, kc_ref[0], vtc_ref[0], m_scr, acc_scr)

    if with_latent:
        @pl.when(j > 0)
        def _():
            _flash_update(q_ref, kl_ref[0], vtl_ref[0], m_scr, acc_scr)

    @pl.when(j == pl.num_programs(3) - 1)
    def _():
        acc = acc_scr[...]
        o_ref[0] = jnp.transpose(acc[:ATT_DV] / acc[ATT_DV:ATT_DV + 1]).astype(o_ref.dtype)


def transpose_values(v):
    B, nk, hv = v.shape
    H = hv // ATT_DV
    ext = jnp.zeros((B, nk, H, ATT_DV), v.dtype).at[..., 0].set(1)
    va = jnp.concatenate([v.reshape(B, nk, H, ATT_DV), ext], -1)
    return jnp.transpose(va, (0, 2, 3, 1)).reshape(B, H * 2 * ATT_DV, nk)


def flash_attention(q, kc, vc, kl=None, vl=None, *, tq, tk=None, out_dtype=F32):
    B, nq, hq = q.shape
    nc = kc.shape[1]
    H = hq // ATT_DK
    assert nq % tq == 0
    with_latent = kl is not None
    in_specs = [pl.BlockSpec((1, tq, ATT_DK), lambda b, h, i, j: (b, i, h)),
                pl.BlockSpec((1, nc, ATT_DK), lambda b, h, i, j: (b, 0, h)),
                pl.BlockSpec((1, 2 * ATT_DV, nc), lambda b, h, i, j: (b, h, 0))]
    args = [q, kc, transpose_values(vc)]
    steps = 1
    if with_latent:
        assert kl.shape[1] % tk == 0
        steps += kl.shape[1] // tk
        in_specs += [pl.BlockSpec((1, tk, ATT_DK), lambda b, h, i, j: (b, jnp.maximum(j - 1, 0), h)),
                     pl.BlockSpec((1, 2 * ATT_DV, tk), lambda b, h, i, j: (b, h, jnp.maximum(j - 1, 0)))]
        args += [kl, transpose_values(vl)]
    return pl.pallas_call(
        functools.partial(_flash_kernel, with_latent=with_latent),
        grid=(B, H, nq // tq, steps),
        in_specs=in_specs,
        out_specs=pl.BlockSpec((1, tq, ATT_DV), lambda b, h, i, j: (b, i, h)),
        out_shape=jax.ShapeDtypeStruct((B, nq, H * ATT_DV), out_dtype),
        scratch_shapes=[pltpu.VMEM((1, tq), F32), pltpu.VMEM((2 * ATT_DV, tq), F32)],
        compiler_params=pltpu.CompilerParams(
            dimension_semantics=("parallel", "parallel", "parallel", "arbitrary")),
        name="mla_flash",
    )(*args)


PROJ = {}
_off = 0
for _name, _w in (('gate', 4 * D_MODEL), ('hy_x0', HY_WIDTH), ('hy_x1', HY_WIDTH), ('hy_v', HY_WIDTH),
                  ('rw_r', RW_WIDTH), ('rw_k', RW_WIDTH), ('rw_v', RW_WIDTH), ('ssm_z', SSM_INNER),
                  ('ssm_x', SSM_INNER), ('mla_cq', MLA_Q_LORA), ('mla_ckv', MLA_KV_LORA),
                  ('ssm_b', 256), ('ssm_c', 256), ('rw_gd', 256), ('rw_wd', 128), ('rw_ad', 128),
                  ('mla_kpe', 128), ('mla_kpe_sw', 128), ('ssm_dt', 128)):
    assert _off % _w == 0, (_name, _off, _w)
    PROJ[_name] = (_off, _w)
    _off += _w
PROJ_TN = 512
PROJ_N = -(-_off // PROJ_TN) * PROJ_TN


def _rope_swap_matrix():
    p = np.zeros((QK_ROPE, QK_ROPE), np.float32)
    q = QK_ROPE // 4
    for g in range(2):
        for i in range(q):
            p[g * 2 * q + q + i, g * 2 * q + i] = -1.0
            p[g * 2 * q + i, g * 2 * q + q + i] = 1.0
    return jnp.asarray(p)


def build_w_in(w_in):
    cuts = np.cumsum([0, MLA_COLS, SSM_COLS, HY_COLS, RW_COLS, 4 * D_MODEL]).tolist()
    mla, ssm, hy, rw, gate = (w_in[:, cuts[i]:cuts[i + 1]] for i in range(5))
    kpe = mla[:, MLA_Q_LORA + MLA_KV_LORA:]
    src = {
        'gate': gate, 'hy_x0': hy[:, :HY_WIDTH], 'hy_x1': hy[:, HY_WIDTH:2 * HY_WIDTH], 'hy_v': hy[:, 2 * HY_WIDTH:],
        'rw_r': rw[:, :RW_WIDTH], 'rw_k': rw[:, RW_WIDTH:2 * RW_WIDTH], 'rw_v': rw[:, 2 * RW_WIDTH:3 * RW_WIDTH],
        'rw_wd': rw[:, 3 * RW_WIDTH:3 * RW_WIDTH + 128], 'rw_ad': rw[:, 3 * RW_WIDTH + 128:3 * RW_WIDTH + 256],
        'rw_gd': rw[:, 3 * RW_WIDTH + 256:],
        'ssm_z': ssm[:, :SSM_INNER], 'ssm_x': ssm[:, SSM_INNER:2 * SSM_INNER],
        'ssm_b': ssm[:, 2 * SSM_INNER:2 * SSM_INNER + 256], 'ssm_c': ssm[:, 2 * SSM_INNER + 256:2 * SSM_INNER + 512],
        'ssm_dt': ssm[:, 2 * SSM_INNER + 512:],
        'mla_cq': mla[:, :MLA_Q_LORA], 'mla_ckv': mla[:, MLA_Q_LORA:MLA_Q_LORA + MLA_KV_LORA],
        'mla_kpe': kpe, 'mla_kpe_sw': kpe @ _rope_swap_matrix(),
    }
    cols, pos = [], 0
    for name, (off, w) in PROJ.items():
        assert off == pos
        piece = src[name]
        cols.append(jnp.pad(piece, ((0, 0), (0, w - piece.shape[1]))))
        pos += w
    cols.append(jnp.zeros((w_in.shape[0], PROJ_N - pos), w_in.dtype))
    return jnp.concatenate(cols, 1).astype(BF16)


def _inproj_kernel(x_ref, sh_ref, sc_ref, w_ref, o_ref, h_scr):
    @pl.when(pl.program_id(1) == 0)
    def _():
        h_scr[...] = (_ln(x_ref[...]) * (1.0 + sc_ref[0]) + sh_ref[0]).astype(BF16)

    o_ref[...] = jnp.dot(h_scr[...], w_ref[...], preferred_element_type=F32).astype(o_ref.dtype)


def inproj(x2d, shift, scale, w_p, rows_per_mod):
    R, D = x2d.shape
    tm = min(TM, R)
    bpm = max(rows_per_mod // tm, 1)
    mod = pl.BlockSpec((1, 1, D), lambda i, j: (i // bpm, 0, 0))
    return pl.pallas_call(
        _inproj_kernel,
        grid=(R // tm, PROJ_N // PROJ_TN),
        in_specs=[pl.BlockSpec((tm, D), lambda i, j: (i, 0)), mod, mod,
                  pl.BlockSpec((D, PROJ_TN), lambda i, j: (0, j))],
        out_specs=pl.BlockSpec((tm, PROJ_TN), lambda i, j: (i, j)),
        out_shape=jax.ShapeDtypeStruct((R, PROJ_N), BF16),
        scratch_shapes=[pltpu.VMEM((tm, D), BF16)],
        compiler_params=pltpu.CompilerParams(dimension_semantics=("parallel", "arbitrary"),
                                             vmem_limit_bytes=VMEM_LIMIT),
        name="inproj",
    )(x2d, shift, scale, w_p)


def build_mla_weights(p):
    wq = p['mla_w_q_up'].reshape(MLA_Q_LORA, MLA_HEADS, QK_NOPE + QK_ROPE)
    wkv = p['mla_w_kv_up'].reshape(MLA_KV_LORA, MLA_HEADS, QK_NOPE + V_HEAD)
    wq_p = jnp.pad(wq, ((0, 0), (0, 0), (0, ATT_DK - QK_NOPE - QK_ROPE))).reshape(MLA_Q_LORA, MLA_HEADS * ATT_DK)
    wsw = jnp.einsum('chd,de->che', wq[:, :, QK_NOPE:], _rope_swap_matrix())
    wsw_p = jnp.pad(wsw, ((0, 0), (0, 0), (0, 128 - QK_ROPE))).reshape(MLA_Q_LORA, MLA_HEADS * 128)
    wkn = wkv[:, :, :QK_NOPE].reshape(MLA_KV_LORA, MLA_HEADS * QK_NOPE)
    wv = wkv[:, :, QK_NOPE:].reshape(MLA_KV_LORA, MLA_HEADS * V_HEAD)
    return wq_p.astype(BF16), wsw_p.astype(BF16), wkn.astype(BF16), wv.astype(BF16)


def axial_rope_tables(n_tokens):
    rows = n_tokens // GRID_W
    row = jnp.repeat(jnp.arange(rows), GRID_W).astype(F32)
    col = jnp.tile(jnp.arange(GRID_W), rows).astype(F32)
    half = QK_ROPE // 2
    inv = ROPE_BASE ** (-jnp.arange(0, half, 2, dtype=F32) / half)
    ang = jnp.stack([row[:, None] * inv, col[:, None] * inv], 1)
    return jnp.cos(ang), jnp.sin(ang)


def rope_tables_128(n_tokens):
    cos, sin = axial_rope_tables(n_tokens)
    full = lambda t: jnp.concatenate([t[:, 0], t[:, 0], t[:, 1], t[:, 1]], -1)
    pad = lambda t: jnp.pad(t, ((0, 0), (0, 128 - QK_ROPE)))
    return pad(full(cos)), pad(full(sin))


def _mla_proj_kernel(cq_ref, ckv_ref, kpe_ref, ksw_ref, cos_ref, sin_ref, gq_ref, gkv_ref, wq_ref, wsw_ref, wkn_ref,
                     wv_ref, q_ref, k_ref, v_ref, *, q_scale):
    def rms(x, g):
        return (x * lax.rsqrt(jnp.mean(x * x, -1, keepdims=True) + 1e-6) * g).astype(BF16)

    cqn = rms(cq_ref[...].astype(F32), gq_ref[...])
    ckvn = rms(ckv_ref[...].astype(F32), gkv_ref[...])
    cos = cos_ref[...]
    sin = sin_ref[...]
    q = jnp.dot(cqn, wq_ref[...], preferred_element_type=F32)
    qsw = jnp.dot(cqn, wsw_ref[...], preferred_element_type=F32)
    kn = jnp.dot(ckvn, wkn_ref[...], preferred_element_type=F32)
    v_ref[...] = jnp.dot(ckvn, wv_ref[...], preferred_element_type=F32).astype(v_ref.dtype)
    kpe = (kpe_ref[...].astype(F32) * cos + ksw_ref[...].astype(F32) * sin).astype(k_ref.dtype)
    for h in range(MLA_HEADS):
        lo = h * ATT_DK
        q_ref[:, lo:lo + 128] = (q[:, lo:lo + 128] * q_scale).astype(q_ref.dtype)
        pe = q[:, lo + 128:lo + 256] * cos + qsw[:, h * 128:(h + 1) * 128] * sin
        q_ref[:, lo + 128:lo + 256] = (pe * q_scale).astype(q_ref.dtype)
        k_ref[:, lo:lo + 128] = kn[:, h * 128:(h + 1) * 128].astype(k_ref.dtype)
        k_ref[:, lo + 128:lo + 256] = kpe


def mla_proj(proj, cos, sin, gq, gkv, weights):
    R = proj.shape[0]
    tm = min(TM, R)
    wq, wsw, wkn, wv = weights
    col = lambda name: pl.BlockSpec((tm, PROJ[name][1]), lambda i, o=PROJ[name][0] // PROJ[name][1]: (i, o))
    row = lambda w: pl.BlockSpec((tm, w), lambda i: (i, 0))
    full = lambda a: pl.BlockSpec(a.shape, lambda i: (0,) * a.ndim)
    gq2, gkv2 = gq.reshape(1, -1), gkv.reshape(1, -1)
    return pl.pallas_call(
        functools.partial(_mla_proj_kernel, q_scale=float((QK_NOPE + QK_ROPE) ** -0.5)),
        grid=(R // tm,),
        in_specs=[col('mla_cq'), col('mla_ckv'), col('mla_kpe'), col('mla_kpe_sw'), row(128), row(128),
                  full(gq2), full(gkv2), full(wq), full(wsw), full(wkn), full(wv)],
        out_specs=[row(MLA_HEADS * ATT_DK), row(MLA_HEADS * ATT_DK), row(MLA_HEADS * ATT_DV)],
        out_shape=[jax.ShapeDtypeStruct((R, MLA_HEADS * ATT_DK), BF16),
                   jax.ShapeDtypeStruct((R, MLA_HEADS * ATT_DK), BF16),
                   jax.ShapeDtypeStruct((R, MLA_HEADS * ATT_DV), BF16)],
        compiler_params=pltpu.CompilerParams(dimension_semantics=("parallel",), vmem_limit_bytes=VMEM_LIMIT),
        name="mla_proj",
    )(proj, proj, proj, proj, cos, sin, gq2, gkv2, wq, wsw, wkn, wv)


MERGE_TN = 512


def _merge_kernel(b0, b1, b2, b3, g0, g1, g2, g3, w_ref, o_ref):
    acc = None
    for i, (b, g) in enumerate(((b0, g0), (b1, g1), (b2, g2), (b3, g3))):
        t = jax.nn.sigmoid(g[...].astype(F32)) * jnp.dot(b[...], w_ref[i], preferred_element_type=F32)
        acc = t if acc is None else acc + t
    o_ref[...] = acc.astype(o_ref.dtype)


def merge_branches_pallas(branches, proj, w_branch):
    R = proj.shape[0]
    tm = min(TM, R)
    nj = D_MODEL // MERGE_TN
    g0 = PROJ['gate'][0] // MERGE_TN
    br = pl.BlockSpec((tm, BRANCH_WIDTH), lambda i, j: (i, 0))
    gates = [pl.BlockSpec((tm, MERGE_TN), lambda i, j, k=k: (i, g0 + k * nj + j)) for k in range(N_BRANCH)]
    return pl.pallas_call(
        _merge_kernel,
        grid=(R // tm, nj),
        in_specs=[br] * 4 + gates + [pl.BlockSpec((N_BRANCH, BRANCH_WIDTH, MERGE_TN), lambda i, j: (0, 0, j))],
        out_specs=pl.BlockSpec((tm, MERGE_TN), lambda i, j: (i, j)),
        out_shape=jax.ShapeDtypeStruct((R, D_MODEL), BF16),
        compiler_params=pltpu.CompilerParams(dimension_semantics=("parallel", "arbitrary"),
                                             vmem_limit_bytes=VMEM_LIMIT),
        name="branch_merge",
    )(*branches, proj, proj, proj, proj, w_branch)


ROUTER_PAD = 128


def _mm3_f32(a, b_hi, b_lo):
    ah = a.astype(BF16)
    al = (a - ah.astype(F32)).astype(BF16)
    d = lambda x, y: jnp.dot(x, y, preferred_element_type=F32)
    return d(ah, b_hi) + d(ah, b_lo) + d(al, b_hi)


def _outproj_kernel(m_ref, x_ref, g1_ref, sh2_ref, sc2_ref, lng_ref, lnb_ref, w_ref, wr_hi, wr_lo,
                    x_out, h2_out, logit_out):
    y = jnp.dot(m_ref[...], w_ref[...], preferred_element_type=F32)
    xn = _ln(ALPHA * x_ref[...] + g1_ref[0] * y) * lng_ref[...] + lnb_ref[...]
    x_out[...] = xn
    h2 = _ln(xn) * (1.0 + sc2_ref[0]) + sh2_ref[0]
    h2_out[...] = h2.astype(h2_out.dtype)
    logit_out[...] = _mm3_f32(h2, wr_hi[...], wr_lo[...])


def outproj_postnorm(merged, x2d, g1, sh2, sc2, ln_g, ln_b, w_out, w_router, rows_per_mod):
    R, D = x2d.shape
    tm = min(TM, R)
    bpm = max(rows_per_mod // tm, 1)
    mod = pl.BlockSpec((1, 1, D), lambda i: (i // bpm, 0, 0))
    row = lambda w: pl.BlockSpec((tm, w), lambda i: (i, 0))
    full = lambda a: pl.BlockSpec(a.shape, lambda i: (0,) * a.ndim)
    wr = jnp.pad(w_router, ((0, 0), (0, ROUTER_PAD - w_router.shape[1])))
    wr_hi = wr.astype(BF16)
    wr_lo = (wr - wr_hi.astype(F32)).astype(BF16)
    lg, lb = ln_g.reshape(1, D), ln_b.reshape(1, D)
    return pl.pallas_call(
        _outproj_kernel,
        grid=(R // tm,),
        in_specs=[row(D), row(D), mod, mod, mod, full(lg), full(lb), full(w_out), full(wr_hi), full(wr_lo)],
        out_specs=[row(D), row(D), row(ROUTER_PAD)],
        out_shape=[jax.ShapeDtypeStruct((R, D), F32), jax.ShapeDtypeStruct((R, D), BF16),
                   jax.ShapeDtypeStruct((R, ROUTER_PAD), F32)],
        compiler_params=pltpu.CompilerParams(dimension_semantics=("parallel",), vmem_limit_bytes=VMEM_LIMIT),
        name="outproj_postnorm",
    )(merged, x2d, g1, sh2, sc2, lg, lb, w_out, wr_hi, wr_lo)


def _postnorm_kernel(x_ref, y_ref, g_ref, lng_ref, lnb_ref, o_ref):
    o_ref[...] = _ln(ALPHA * x_ref[...] + g_ref[0] * y_ref[...]) * lng_ref[...] + lnb_ref[...]


def postnorm(x2d, y2d, g, ln_g, ln_b, rows_per_mod):
    R, D = x2d.shape
    tm = min(TM, R)
    bpm = max(rows_per_mod // tm, 1)
    row = pl.BlockSpec((tm, D), lambda i: (i, 0))
    vec = pl.BlockSpec((1, D), lambda i: (0, 0))
    return pl.pallas_call(
        _postnorm_kernel,
        grid=(R // tm,),
        in_specs=[row, row, pl.BlockSpec((1, 1, D), lambda i: (i // bpm, 0, 0)), vec, vec],
        out_specs=row,
        out_shape=jax.ShapeDtypeStruct((R, D), F32),
        compiler_params=pltpu.CompilerParams(dimension_semantics=("parallel",), vmem_limit_bytes=VMEM_LIMIT),
        name="postnorm",
    )(x2d, y2d, g, ln_g.reshape(1, D), ln_b.reshape(1, D))


FF_PAD = 1536
FF_TILE = 512


def _expert_ffn_kernel(x_ref, gate_ref, wg_ref, wu_ref, wd_ref, o_ref):
    f = pl.program_id(2)
    x = x_ref[0, 0]
    hid = (jax.nn.silu(jnp.dot(x, wg_ref[0], preferred_element_type=F32))
           * jnp.dot(x, wu_ref[0], preferred_element_type=F32))
    part = jnp.dot(hid.astype(BF16), wd_ref[0], preferred_element_type=F32)

    @pl.when(f == 0)
    def _():
        o_ref[0, 0] = part

    @pl.when(f > 0)
    def _():
        o_ref[0, 0] += part

    @pl.when(f == pl.num_programs(2) - 1)
    def _():
        o_ref[0, 0] = o_ref[0, 0] * gate_ref[0, 0]


def expert_ffn(xs, gate, wg, wu, wd):
    S, E, C, D = xs.shape
    nf = FF_PAD // FF_TILE
    return pl.pallas_call(
        _expert_ffn_kernel,
        grid=(E, S, nf),
        in_specs=[pl.BlockSpec((1, 1, C, D), lambda e, s, f: (s, e, 0, 0)),
                  pl.BlockSpec((1, 1, C, 1), lambda e, s, f: (s, e, 0, 0)),
                  pl.BlockSpec((1, D, FF_TILE), lambda e, s, f: (e, 0, f)),
                  pl.BlockSpec((1, D, FF_TILE), lambda e, s, f: (e, 0, f)),
                  pl.BlockSpec((1, FF_TILE, D), lambda e, s, f: (e, f, 0))],
        out_specs=pl.BlockSpec((1, 1, C, D), lambda e, s, f: (s, e, 0, 0)),
        out_shape=jax.ShapeDtypeStruct((S, E, C, D), F32),
        compiler_params=pltpu.CompilerParams(dimension_semantics=("parallel", "parallel", "arbitrary"),
                                             vmem_limit_bytes=VMEM_LIMIT),
        name="expert_ffn",
    )(xs, gate, wg, wu, wd)


def build_expert_weights(p):
    padf = lambda w: jnp.pad(w, ((0, 0), (0, 0), (0, FF_PAD - EXPERT_FF))).astype(BF16)
    wd = jnp.pad(p['w_down_e'], ((0, 0), (0, FF_PAD - EXPERT_FF), (0, 0))).astype(BF16)
    return padf(p['w_gate_e']), padf(p['w_up_e']), wd


HALO = 8


def _halo_specs(tm, w, col_block, nrows):
    nb8 = nrows // HALO
    prev = pl.BlockSpec((HALO, w), lambda i: (jnp.maximum(i * (tm // HALO) - 1, 0), col_block))
    nxt = pl.BlockSpec((HALO, w), lambda i: (jnp.minimum((i + 1) * (tm // HALO), nb8 - 1), col_block))
    return prev, nxt


def _neighbours(cur, prev8, next8, seq_len):
    tm = cur.shape[0]
    row = lax.broadcasted_iota(jnp.int32, cur.shape, 0)
    g = row + pl.program_id(0) * tm
    before = jnp.where(row == 0, prev8[HALO - 1:HALO, :], pltpu.roll(cur, 1, 0))
    after = jnp.where(row == tm - 1, next8[0:1, :], pltpu.roll(cur, tm - 1, 0))
    pos = g & (seq_len - 1)
    return jnp.where(pos == 0, 0.0, before), jnp.where(pos == seq_len - 1, 0.0, after)


RW_PIECES = ('rw_r', 'rw_k', 'rw_v', 'rw_wd', 'rw_ad', 'rw_gd')


def _head_sum_matrix(scale=1.0):
    r = np.arange(RW_WIDTH)
    return jnp.asarray(((r[:, None] // RW_HEAD_DIM) == (r[None, :] // RW_HEAD_DIM)).astype(np.float32) * scale, BF16)


def _seg_sum(x, m):
    xh = x.astype(BF16)
    xl = (x - xh.astype(F32)).astype(BF16)
    return jnp.dot(xh, m, preferred_element_type=F32) + jnp.dot(xl, m, preferred_element_type=F32)


def _rwkv_prep_kernel(*refs, seq_len):
    cur = refs[0:6]
    prv = refs[6:12]
    nxt = refs[12:18]
    mu = refs[18:24]
    (w0_ref, a0_ref, wup_ref, aup_ref, gup_ref, kkw_ref, ka_ref, rk_ref, hs_ref) = refs[24:33]
    (r_o, v_o, kk_o, lw0_o, lw1_o, kd0_o, kd1_o, bd0_o, bd1_o, g_o, bonus_o) = refs[33:]
    us = []
    for c, p, n, m in zip(cur, prv, nxt, mu):
        x = c[...].astype(F32)
        before, after = _neighbours(x, p[...].astype(F32), n[...].astype(F32), seq_len)
        us.append(x + m[0:1, :] * (before - x) + m[1:2, :] * (after - x))
    r, k, v, wd, ad, gd = us
    hs = hs_ref[...]
    g_o[...] = jnp.dot(jax.nn.sigmoid(gd).astype(BF16), gup_ref[...], preferred_element_type=F32).astype(g_o.dtype)
    kk = k * kkw_ref[...]
    kk = kk / jnp.maximum(jnp.sqrt(_seg_sum(kk * kk, hs)), 1e-12)
    wl = jnp.dot(jnp.tanh(wd).astype(BF16), wup_ref[...], preferred_element_type=F32) + w0_ref[...]
    al = jnp.dot(ad.astype(BF16), aup_ref[...], preferred_element_type=F32) + a0_ref[...]
    r_o[...] = r.astype(r_o.dtype)
    v_o[...] = v.astype(v_o.dtype)
    kk_o[...] = kk.astype(kk_o.dtype)
    rk = rk_ref[...]
    acc = None
    for d, (lw_o, kd_o, bd_o) in enumerate(((lw0_o, kd0_o, bd0_o), (lw1_o, kd1_o, bd1_o))):
        sl = slice(d * RW_WIDTH, (d + 1) * RW_WIDTH)
        w_log = -jax.nn.softplus(-wl[:, sl]) - 0.5
        lw_o[...] = -jnp.exp(w_log)
        iclr = jax.nn.sigmoid(al[:, sl])
        kd = k * (1.0 + (iclr - 1.0) * ka_ref[...])
        kd_o[...] = kd.astype(kd_o.dtype)
        bd_o[...] = (kk * iclr).astype(bd_o.dtype)
        t = r * kd * rk
        acc = t if acc is None else acc + t
    bonus_o[...] = (_seg_sum(acc, hs) * v).astype(bonus_o.dtype)


def build_rwkv_weights(p):
    W = RW_WIDTH
    mu = p['rw_mu']
    cuts = np.cumsum([0, W, W, W, 128, 128, RW_GATE_LORA]).tolist()
    mus = [mu[:, cuts[i]:cuts[i + 1]] for i in range(6)]
    mus[5] = jnp.pad(mus[5], ((0, 0), (0, 256 - RW_GATE_LORA)))
    z = jnp.zeros((RW_DECAY_LORA, W), F32)
    wup = jnp.concatenate([jnp.concatenate([p['rw_w_up'][0], z], 1), jnp.concatenate([z, p['rw_w_up'][1]], 1)], 0)
    aup = jnp.concatenate([jnp.concatenate([p['rw_a_up'][0], z], 1), jnp.concatenate([z, p['rw_a_up'][1]], 1)], 0)
    gup = jnp.pad(p['rw_g_up'], ((0, 256 - RW_GATE_LORA), (0, 0)))
    return dict(mu=mus, w0=p['rw_w0'].reshape(1, 2 * W), a0=p['rw_a0'].reshape(1, 2 * W), wup=wup.astype(BF16),
                aup=aup.astype(BF16), gup=gup.astype(BF16), kkw=p['rw_kk'].reshape(1, W), ka=p['rw_ka'].reshape(1, W),
                rk=p['rw_rk'].reshape(1, W), hs=_head_sum_matrix(), hmean=_head_sum_matrix(1.0 / RW_HEAD_DIM),
                ln_g=p['rw_ln_g'].reshape(1, W), ln_b=p['rw_ln_b'].reshape(1, W))


def rwkv_prep_pallas(proj, wts, seq_len):
    R = proj.shape[0]
    tm = min(TM, R)
    W = RW_WIDTH
    cur, prv, nxt = [], [], []
    for name in RW_PIECES:
        off, w = PROJ[name]
        cur.append(pl.BlockSpec((tm, w), lambda i, o=off // w: (i, o)))
        a, b = _halo_specs(tm, w, off // w, R)
        prv.append(a)
        nxt.append(b)
    full = lambda a: pl.BlockSpec(a.shape, lambda i: (0,) * a.ndim)
    row = pl.BlockSpec((tm, W), lambda i: (i, 0))
    consts = [wts[k] for k in ('w0', 'a0', 'wup', 'aup', 'gup', 'kkw', 'ka', 'rk', 'hs')]
    bf = jax.ShapeDtypeStruct((R, W), BF16)
    f32 = jax.ShapeDtypeStruct((R, W), F32)
    return pl.pallas_call(
        functools.partial(_rwkv_prep_kernel, seq_len=seq_len),
        grid=(R // tm,),
        in_specs=cur + prv + nxt + [full(m) for m in wts['mu']] + [full(c) for c in consts],
        out_specs=[row] * 11,
        out_shape=[bf, bf, bf, f32, f32, bf, bf, bf, bf, bf, bf],
        compiler_params=pltpu.CompilerParams(dimension_semantics=("parallel",), vmem_limit_bytes=VMEM_LIMIT),
        name="rwkv_prep",
    )(*([proj] * 18), *wts['mu'], *consts)


def _rwkv_out_kernel(yf_ref, yb_ref, bonus_ref, g_ref, hm_ref, lng_ref, lnb_ref, o_ref):
    y = yf_ref[...] + yb_ref[...]
    hm = hm_ref[...]
    yc = y - _seg_sum(y, hm)
    yn = yc * lax.rsqrt(_seg_sum(yc * yc, hm) + RW_GN_EPS) * lng_ref[...] + lnb_ref[...]
    o_ref[...] = ((yn + bonus_ref[...].astype(F32)) * g_ref[...].astype(F32)).astype(o_ref.dtype)


def rwkv_out_pallas(yf, yb, bonus, g, wts):
    R, W = yf.shape
    tm = min(TM, R)
    row = pl.BlockSpec((tm, W), lambda i: (i, 0))
    full = lambda a: pl.BlockSpec(a.shape, lambda i: (0,) * a.ndim)
    return pl.pallas_call(
        _rwkv_out_kernel,
        grid=(R // tm,),
        in_specs=[row] * 4 + [full(wts['hmean']), full(wts['ln_g']), full(wts['ln_b'])],
        out_specs=row,
        out_shape=jax.ShapeDtypeStruct((R, W), BF16),
        compiler_params=pltpu.CompilerParams(dimension_semantics=("parallel",), vmem_limit_bytes=VMEM_LIMIT),
        name="rwkv_out",
    )(yf, yb, bonus, g, wts['hmean'], wts['ln_g'], wts['ln_b'])


SSD_T = SSM_CHUNK
SSD_GW = SSM_INNER // SSM_GROUPS
SSD_HG = SSM_HEADS // SSM_GROUPS
SSM_PIECES = ('ssm_x', 'ssm_b', 'ssm_c')


def _split_dot(x, m):
    xh = x.astype(BF16)
    xl = (x - xh.astype(F32)).astype(BF16)
    return jnp.dot(xh, m, preferred_element_type=F32) + jnp.dot(xl, m, preferred_element_type=F32)


def _ssd_prep_kernel(*refs, seq_len):
    cur, prv, nxt = refs[0:3], refs[3:6], refs[6:9]
    cw, cb = refs[9:12], refs[12:15]
    dt_ref, dtb_ref, perm_ref = refs[15:18]
    x_o, b_o, c_o, dt_o = refs[18:]
    for c, p, n, w, bias, o in zip(cur, prv, nxt, cw, cb, (x_o, b_o, c_o)):
        x = c[...].astype(F32)
        before, after = _neighbours(x, p[...].astype(F32), n[...].astype(F32), seq_len)
        y = before * w[0:1, :] + x * w[1:2, :] + after * w[2:3, :] + bias[...]
        o[...] = jax.nn.silu(y).astype(o.dtype)
    dt = jax.nn.softplus(dt_ref[...].astype(F32) + dtb_ref[...])
    dt_o[...] = _split_dot(dt, perm_ref[...])


def build_ssd_weights(p):
    w, bias = p['ssm_conv_w'], p['ssm_conv_b']
    cuts = [0, SSM_INNER, SSM_INNER + 256, SSM_INNER + 512]
    cw = [w[:, cuts[i]:cuts[i + 1]] for i in range(3)]
    cb = [bias[cuts[i]:cuts[i + 1]].reshape(1, -1) for i in range(3)]
    dtb = jnp.pad(p['ssm_dt_bias'].reshape(1, 2 * SSM_HEADS), ((0, 0), (0, 128 - 2 * SSM_HEADS)))
    perm = np.zeros((128, SSM_GROUPS * 128), np.float32)
    for d in range(2):
        for g in range(SSM_GROUPS):
            for j in range(SSD_HG):
                perm[d * SSM_HEADS + g * SSD_HG + j, g * 128 + d * SSD_HG + j] = 1.0
    a = -jnp.exp(p['ssm_a_log'].astype(F32))
    a_g = jnp.stack([jnp.pad(jnp.concatenate([a[0, g * SSD_HG:(g + 1) * SSD_HG], a[1, g * SSD_HG:(g + 1) * SSD_HG]]),
                             (0, 128 - 2 * SSD_HG)) for g in range(SSM_GROUPS)]).reshape(SSM_GROUPS, 1, 128)
    d_exp = jnp.repeat(p['ssm_d'], SSM_HEAD_DIM).reshape(1, SSM_INNER)
    return dict(cw=cw, cb=cb, dtb=dtb, perm=jnp.asarray(perm, BF16), a_g=a_g, d_exp=d_exp,
                norm=p['ssm_norm'].reshape(1, SSM_INNER))


def ssd_prep_pallas(proj, wts, seq_len):
    R = proj.shape[0]
    tm = min(TM, R)
    cur, prv, nxt = [], [], []
    for name in SSM_PIECES:
        off, w = PROJ[name]
        cur.append(pl.BlockSpec((tm, w), lambda i, o=off // w: (i, o)))
        a, b = _halo_specs(tm, w, off // w, R)
        prv.append(a)
        nxt.append(b)
    full = lambda a: pl.BlockSpec(a.shape, lambda i: (0,) * a.ndim)
    off, w = PROJ['ssm_dt']
    consts = wts['cw'] + wts['cb']
    row = lambda width: pl.BlockSpec((tm, width), lambda i: (i, 0))
    return pl.pallas_call(
        functools.partial(_ssd_prep_kernel, seq_len=seq_len),
        grid=(R // tm,),
        in_specs=cur + prv + nxt + [full(c) for c in consts]
        + [pl.BlockSpec((tm, w), lambda i: (i, off // w)), full(wts['dtb']), full(wts['perm'])],
        out_specs=[row(SSM_INNER), row(256), row(256), row(SSM_GROUPS * 128)],
        out_shape=[jax.ShapeDtypeStruct((R, SSM_INNER), BF16), jax.ShapeDtypeStruct((R, 256), BF16),
                   jax.ShapeDtypeStruct((R, 256), BF16), jax.ShapeDtypeStruct((R, SSM_GROUPS * 128), F32)],
        compiler_params=pltpu.CompilerParams(dimension_semantics=("parallel",), vmem_limit_bytes=VMEM_LIMIT),
        name="ssd_prep",
    )(*([proj] * 9), *consts, proj, wts['dtb'], wts['perm'])


def _ssd_chunks(chains, consts):
    T = SSD_T
    each = lambda f, *ls: [f(*xs) for xs in zip(*ls)]
    eye, cums, tris, expand, lane_head = consts
    S, xs, bm, cm, dt, a, dirs = (list(t) for t in zip(*chains))
    cum = [cums[d] for d in dirs]
    tri = [tris[d] for d in dirs]
    ex = [expand[d] for d in dirs]
    acs = each(lambda m, x, av: jnp.dot(m, x * av, precision=HI, preferred_element_type=F32), cum, dt, a)
    acs_t = each(lambda x: lax.dot_general(eye, x, (((1,), (1,)), ((), ())), precision=HI,
                                           preferred_element_type=F32), acs)
    dt_e = each(_split_dot, dt, ex)
    acs_e = each(_split_dot, acs, ex)
    atot_e = each(lambda x, d: x[0:1, :] if d else x[T - 1:T, :], acs_e, dirs)
    X = each(lambda x, y: x * y, xs, dt_e)
    cb = each(_mm_nt, cm, bm)
    y = []
    for c in range(len(chains)):
        parts = []
        for pr in range(SSD_HG // 2):
            acc = None
            for hh in range(2):
                i = dirs[c] * SSD_HG + 2 * pr + hh
                diff = acs[c][:, i:i + 1] - acs_t[c][i:i + 1, :]
                m = cb[c] * jnp.exp(jnp.where(tri[c], diff, -jnp.inf))
                xh = jnp.where(lane_head == hh, X[c][:, pr * 128:(pr + 1) * 128], 0.0)
                t = _mm(m, xh)
                acc = t if acc is None else acc + t
            parts.append(acc)
        y.append(jnp.concatenate(parts, axis=1))
    states = each(lambda b_, x, at, ac: _mm_tn(b_, x * jnp.exp(at - ac)), bm, X, atot_e, acs_e)
    y_off = each(lambda c_, s, ac: _mm(c_, s) * jnp.exp(ac), cm, S, acs_e)
    S_new = each(lambda s, at, st: s * jnp.exp(at) + st, S, atot_e, states)
    return [(yd + yo, sn) for yd, yo, sn in zip(y, y_off, S_new)]


def _ssd_consts():
    T = SSD_T
    row = lax.broadcasted_iota(jnp.int32, (T, T), 0)
    col = lax.broadcasted_iota(jnp.int32, (T, T), 1)
    eye = (row == col).astype(F32)
    cums = ((row >= col).astype(F32), (row <= col).astype(F32))
    tris = (row >= col, row <= col)
    er = lax.broadcasted_iota(jnp.int32, (128, SSD_GW), 0)
    ec = lax.broadcasted_iota(jnp.int32, (128, SSD_GW), 1)
    expand = tuple((er == d * SSD_HG + ec // SSM_HEAD_DIM).astype(BF16) for d in range(2))
    lane_head = lax.broadcasted_iota(jnp.int32, (T, 128), 1) // SSM_HEAD_DIM
    return eye, cums, tris, expand, lane_head


def _ssd_kernel(xf, bf, cf, dtf, xb, bb, cb_, dtb, a_ref, s0_ref, yf_ref, yb_ref, sT_ref, s_scr, *, nsub):
    j = pl.program_id(2)

    @pl.when(j == 0)
    def _():
        s_scr[...] = s0_ref[:, 0, 0]

    T = SSD_T
    consts = _ssd_consts()
    a = a_ref[0]

    def body(i, carry):
        sf = pl.ds(pl.multiple_of(i * T, T), T)
        sb = pl.ds(pl.multiple_of((nsub - 1 - i) * T, T), T)
        ld = lambda ref, sl: ref[0, sl, :].astype(F32)
        res = _ssd_chunks([(carry[0], ld(xf, sf), ld(bf, sf), ld(cf, sf), ld(dtf, sf), a, 0),
                           (carry[1], ld(xb, sb), ld(bb, sb), ld(cb_, sb), ld(dtb, sb), a, 1)], consts)
        yf_ref[0, sf, :] = res[0][0]
        yb_ref[0, sb, :] = res[1][0]
        return res[0][1], res[1][1]

    fin = lax.fori_loop(0, nsub, body, (s_scr[0], s_scr[1]))
    s_scr[0] = fin[0]
    s_scr[1] = fin[1]

    @pl.when(j == pl.num_programs(2) - 1)
    def _():
        sT_ref[0, 0, 0] = fin[0]
        sT_ref[1, 0, 0] = fin[1]


def ssd_scan_pallas(xs, bm, cm, dt, a_g, s0, *, tb=256):
    B, n, _ = xs.shape
    tb = min(tb, n)
    nb = n // tb
    G = SSM_GROUPS
    fwd = lambda w: pl.BlockSpec((1, tb, w), lambda b, g, j: (b, j, g))
    bwd = lambda w: pl.BlockSpec((1, tb, w), lambda b, g, j: (b, nb - 1 - j, g))
    st = pl.BlockSpec((2, 1, 1, SSM_STATE, SSD_GW), lambda b, g, j: (0, b, g, 0, 0))
    return pl.pallas_call(
        functools.partial(_ssd_kernel, nsub=tb // SSD_T),
        grid=(B, G, nb),
        in_specs=[fwd(SSD_GW), fwd(128), fwd(128), fwd(128), bwd(SSD_GW), bwd(128), bwd(128), bwd(128),
                  pl.BlockSpec((1, 1, 128), lambda b, g, j: (g, 0, 0)), st],
        out_specs=[fwd(SSD_GW), bwd(SSD_GW), st],
        out_shape=[jax.ShapeDtypeStruct((B, n, SSM_INNER), F32), jax.ShapeDtypeStruct((B, n, SSM_INNER), F32),
                   jax.ShapeDtypeStruct(s0.shape, F32)],
        scratch_shapes=[pltpu.VMEM((2, SSM_STATE, SSD_GW), F32)],
        compiler_params=pltpu.CompilerParams(dimension_semantics=("parallel", "parallel", "arbitrary"),
                                             vmem_limit_bytes=VMEM_LIMIT),
        name="ssd_scan",
    )(xs, bm, cm, dt, xs, bm, cm, dt, a_g, s0)


def _ssd_out_kernel(yf_ref, yb_ref, xs_ref, z_ref, d_ref, nw_ref, o_ref):
    y = (yf_ref[...] + yb_ref[...] + xs_ref[...].astype(F32) * d_ref[...]) * jax.nn.silu(z_ref[...].astype(F32))
    outs = []
    for g in range(SSM_GROUPS):
        yg = y[:, g * SSD_GW:(g + 1) * SSD_GW]
        outs.append(yg * lax.rsqrt(jnp.mean(yg * yg, -1, keepdims=True) + 1e-5))
    o_ref[...] = (jnp.concatenate(outs, axis=1) * nw_ref[...]).astype(o_ref.dtype)


def ssd_out_pallas(yf, yb, xs, proj, wts):
    R = yf.shape[0]
    tm = min(TM, R)
    row = pl.BlockSpec((tm, SSM_INNER), lambda i: (i, 0))
    vec = pl.BlockSpec((1, SSM_INNER), lambda i: (0, 0))
    zoff = PROJ['ssm_z'][0] // SSM_INNER
    return pl.pallas_call(
        _ssd_out_kernel,
        grid=(R // tm,),
        in_specs=[row, row, row, pl.BlockSpec((tm, SSM_INNER), lambda i: (i, zoff)), vec, vec],
        out_specs=row,
        out_shape=jax.ShapeDtypeStruct((R, SSM_INNER), BF16),
        compiler_params=pltpu.CompilerParams(dimension_semantics=("parallel",), vmem_limit_bytes=VMEM_LIMIT),
        name="ssd_out",
    )(yf, yb, xs, proj, wts['d_exp'], wts['norm'])


HY_PIECES = ('hy_x0', 'hy_x1', 'hy_v')
FFT_R = 128


def _hilo(x):
    hi = x.astype(BF16)
    return hi, (x - hi.astype(F32)).astype(BF16)


def _dot3(a_hi, a_lo, x):
    xh, xl = _hilo(x)
    d = lambda p, q: jnp.dot(p, q, preferred_element_type=F32)
    return d(a_hi, xh) + d(a_hi, xl) + d(a_lo, xh)


def _hilo_tables(pairs):
    out = {}
    for name, x in pairs:
        hi, lo = _hilo(jnp.asarray(x, F32))
        out[name + '_hi'], out[name + '_lo'] = hi, lo
    return out


def _hyena_prep_kernel(*refs, seq_len):
    cur, prv, nxt = refs[0:3], refs[3:6], refs[6:9]
    cw, cb = refs[9:12], refs[12:15]
    x0_o, v_o = refs[15:]
    ys = []
    for c, p, n, w, bias in zip(cur, prv, nxt, cw, cb):
        x = c[...].astype(F32)
        before, after = _neighbours(x, p[...].astype(F32), n[...].astype(F32), seq_len)
        ys.append(before * w[0:1, :] + x * w[1:2, :] + after * w[2:3, :] + bias[...])
    x0_o[...] = ys[0].astype(x0_o.dtype)
    v_o[...] = ys[2] * ys[1]


def hyena_prep_pallas(proj, cw, cb, seq_len):
    R = proj.shape[0]
    tm = min(TM, R)
    cur, prv, nxt = [], [], []
    for name in HY_PIECES:
        off, w = PROJ[name]
        cur.append(pl.BlockSpec((tm, w), lambda i, o=off // w: (i, o)))
        a, b = _halo_specs(tm, w, off // w, R)
        prv.append(a)
        nxt.append(b)
    full = lambda a: pl.BlockSpec(a.shape, lambda i: (0,) * a.ndim)
    row = pl.BlockSpec((tm, HY_WIDTH), lambda i: (i, 0))
    return pl.pallas_call(
        functools.partial(_hyena_prep_kernel, seq_len=seq_len),
        grid=(R // tm,),
        in_specs=cur + prv + nxt + [full(c) for c in cw + cb],
        out_specs=[row, row],
        out_shape=[jax.ShapeDtypeStruct((R, HY_WIDTH), BF16), jax.ShapeDtypeStruct((R, HY_WIDTH), F32)],
        compiler_params=pltpu.CompilerParams(dimension_semantics=("parallel",), vmem_limit_bytes=VMEM_LIMIT),
        name="hyena_prep",
    )(*([proj] * 9), *cw, *cb)


def filter_features(n):
    t = jnp.linspace(0.0, 1.0, n, dtype=F32)[:, None]
    wpos = 2.0 * math.pi * jnp.arange(n, dtype=F32)[:, None] / n
    f = jnp.linspace(1e-4, HY_BANDS - 1, HY_BANDS, dtype=F32)[None]
    z = jnp.concatenate([t, jnp.cos(f * wpos), -jnp.sin(f * wpos), t, jnp.ones_like(t)], -1)
    back = jnp.concatenate([jnp.zeros((1, z.shape[1]), F32), z[:0:-1]], 0)
    return jnp.pad(jnp.concatenate([z, back], 0), ((0, 0), (0, 128 - z.shape[1])))


def _filter_kernel(z_ref, w1_ref, b1_ref, w2_ref, b2_ref, fr_ref, w3_ref, dl_ref, o_ref):
    hp = lambda a, b: jnp.dot(a, b, precision=HI, preferred_element_type=F32)
    z = z_ref[...]
    fr = fr_ref[...]
    h = jnp.sin(fr * (hp(z, w1_ref[...]) + b1_ref[...]))
    h = jnp.sin(fr * (hp(h, w2_ref[...]) + b2_ref[...]))
    filt = hp(h, w3_ref[0])
    t = z[:, HY_EMB:HY_EMB + 1]
    o_ref[...] = filt * jnp.exp(-t * dl_ref[...]) * z[:, HY_EMB + 1:HY_EMB + 2]


def hyena_filter_pallas(n, p):
    z = filter_features(n)
    tr = min(512, n)
    nb = 2 * n // tr
    w1 = jnp.pad(p['hy_w1'], ((0, 128 - HY_EMB), (0, 0)))
    w3 = jnp.moveaxis(p['hy_w3'].reshape(HY_FILTER_HIDDEN, 2, HY_WIDTH), 1, 0)
    deltas = jnp.abs(jnp.linspace(HY_MIN_DECAY, HY_MAX_DECAY, HY_WIDTH, dtype=F32)).reshape(1, HY_WIDTH)
    vec = lambda a: a.reshape(1, -1)
    full = lambda a: pl.BlockSpec(a.shape, lambda i: (0,) * a.ndim)
    args = (w1, vec(p['hy_b1']), p['hy_w2'], vec(p['hy_b2']), vec(p['hy_freq']))
    return pl.pallas_call(
        _filter_kernel,
        grid=(nb,),
        in_specs=[pl.BlockSpec((tr, 128), lambda i: (i, 0))] + [full(a) for a in args]
        + [pl.BlockSpec((1, HY_FILTER_HIDDEN, HY_WIDTH), lambda i: (i // (nb // 2), 0, 0)), full(deltas)],
        out_specs=pl.BlockSpec((tr, HY_WIDTH), lambda i: (i, 0)),
        out_shape=jax.ShapeDtypeStruct((2 * n, HY_WIDTH), F32),
        compiler_params=pltpu.CompilerParams(dimension_semantics=("parallel",), vmem_limit_bytes=VMEM_LIMIT),
        name="hyena_filter",
    )(z, *args, w3, deltas)


FFT_TN = 4096
FFT_KB = 8
FFT_TC = 256


def dft_tables():
    r = np.arange(FFT_R, dtype=np.float64)
    th1 = 2.0 * np.pi * np.outer(r, r) / FFT_R
    k = r[:, None, None] + FFT_R * r[None, :, None]
    phi = 2.0 * np.pi * k * r[None, None, :] / (FFT_R * FFT_R)
    return _hilo_tables((('c1', np.cos(th1)), ('s1', np.sin(th1)), ('gc', np.cos(phi)), ('gs', np.sin(phi)),
                         ('gct', np.swapaxes(np.cos(phi), 1, 2)), ('gst', np.swapaxes(np.sin(phi), 1, 2))))


def _dft_l1_kernel(x_ref, ch, cl, sh, sl, ar_ref, ai_ref):
    x = x_ref[0]
    ar_ref[0] = _dot3(ch[...], cl[...], x)
    ai_ref[0] = -_dot3(sh[...], sl[...], x)


def dft_level1(x2, tabs):
    B, nb, L = x2.shape
    cut = lambda a: a[:, :nb]
    consts = [cut(tabs['c1_hi']), cut(tabs['c1_lo']), cut(tabs['s1_hi']), cut(tabs['s1_lo'])]
    full = lambda a: pl.BlockSpec(a.shape, lambda b, j: (0,) * a.ndim)
    out = pl.BlockSpec((1, FFT_R, FFT_TN), lambda b, j: (b, 0, j))
    return pl.pallas_call(
        _dft_l1_kernel,
        grid=(B, L // FFT_TN),
        in_specs=[pl.BlockSpec((1, nb, FFT_TN), lambda b, j: (b, 0, j))] + [full(c) for c in consts],
        out_specs=[out, out],
        out_shape=[jax.ShapeDtypeStruct((B, FFT_R, L), F32)] * 2,
        compiler_params=pltpu.CompilerParams(dimension_semantics=("parallel", "parallel"),
                                             vmem_limit_bytes=VMEM_LIMIT),
        name="dft_level1",
    )(x2, *consts)


def _level2_fwd(ar, ai, gch, gcl, gsh, gsl):
    xr = _dot3(gch, gcl, ar) + _dot3(gsh, gsl, ai)
    xi = _dot3(gch, gcl, ai) - _dot3(gsh, gsl, ar)
    return xr, xi


def _spectrum_kernel(ar_ref, ai_ref, gch, gcl, gsh, gsl, hr_ref, hi_ref):
    for q in range(FFT_KB):
        xr, xi = _level2_fwd(ar_ref[0, q], ai_ref[0, q], gch[q], gcl[q], gsh[q], gsl[q])
        hr_ref[0, q] = xr
        hi_ref[0, q] = xi


def _conv_l2_kernel(ar_ref, ai_ref, hr_ref, hi_ref, gc, gs, tc, ts, br_ref, bi_ref):
    d1 = lambda a, x: jnp.dot(a, x.astype(BF16), preferred_element_type=F32)
    for q in range(FFT_KB):
        ar, ai = ar_ref[0, q], ai_ref[0, q]
        xr = d1(gc[q], ar) + d1(gs[q], ai)
        xi = d1(gc[q], ai) - d1(gs[q], ar)
        hr, hi = hr_ref[0, q], hi_ref[0, q]
        yr = xr * hr - xi * hi
        yi = xr * hi + xi * hr
        br_ref[0, q] = d1(tc[q], yr) - d1(ts[q], yi)
        bi_ref[0, q] = d1(tc[q], yi) + d1(ts[q], yr)


def _l2_specs():
    data = pl.BlockSpec((1, FFT_KB, FFT_R, FFT_TC), lambda b, c, k: (b, k, 0, c))
    filt = pl.BlockSpec((1, FFT_KB, FFT_R, FFT_TC), lambda b, c, k: (0, k, 0, c))
    tab = pl.BlockSpec((FFT_KB, FFT_R, FFT_R), lambda b, c, k: (k, 0, 0))
    return data, filt, tab


def dft_spectrum(ar, ai, tabs):
    B, _, _, C = ar.shape
    data, _, tab = _l2_specs()
    return pl.pallas_call(
        _spectrum_kernel,
        grid=(B, C // FFT_TC, FFT_R // FFT_KB),
        in_specs=[data, data] + [tab] * 4,
        out_specs=[data, data],
        out_shape=[jax.ShapeDtypeStruct(ar.shape, F32)] * 2,
        compiler_params=pltpu.CompilerParams(dimension_semantics=("parallel", "parallel", "parallel"),
                                             vmem_limit_bytes=VMEM_LIMIT),
        name="dft_spectrum",
    )(ar, ai, tabs['gc_hi'], tabs['gc_lo'], tabs['gs_hi'], tabs['gs_lo'])


def conv_level2(ar, ai, hr, hi, tabs):
    B, _, _, C = ar.shape
    data, filt, tab = _l2_specs()
    names = ('gc_hi', 'gs_hi', 'gct_hi', 'gst_hi')
    return pl.pallas_call(
        _conv_l2_kernel,
        grid=(B, C // FFT_TC, FFT_R // FFT_KB),
        in_specs=[data, data, filt, filt] + [tab] * 4,
        out_specs=[data, data],
        out_shape=[jax.ShapeDtypeStruct(ar.shape, F32)] * 2,
        compiler_params=pltpu.CompilerParams(dimension_semantics=("parallel", "parallel", "parallel"),
                                             vmem_limit_bytes=VMEM_LIMIT),
        name="conv_level2",
    )(ar, ai, hr, hi, *[tabs[k] for k in names])


def _idft_l1_kernel(br_ref, bi_ref, v_ref, x0_ref, d_ref, ch, cl, sh, sl, o_ref, *, inv_n):
    y = (_dot3(ch[...], cl[...], br_ref[0]) - _dot3(sh[...], sl[...], bi_ref[0])) * inv_n
    o_ref[0] = ((y + v_ref[0] * d_ref[...]) * x0_ref[0].astype(F32)).astype(o_ref.dtype)


def idft_level1_epilogue(br, bi, v2, x02, d_t, tabs):
    B, nb, L = v2.shape
    cut = lambda a: a[:nb, :]
    consts = [cut(tabs['c1_hi']), cut(tabs['c1_lo']), cut(tabs['s1_hi']), cut(tabs['s1_lo'])]
    full = lambda a: pl.BlockSpec(a.shape, lambda b, j: (0,) * a.ndim)
    big = pl.BlockSpec((1, FFT_R, FFT_TN), lambda b, j: (b, 0, j))
    half = pl.BlockSpec((1, nb, FFT_TN), lambda b, j: (b, 0, j))
    return pl.pallas_call(
        functools.partial(_idft_l1_kernel, inv_n=1.0 / (FFT_R * FFT_R)),
        grid=(B, L // FFT_TN),
        in_specs=[big, big, half, half, pl.BlockSpec((1, FFT_TN), lambda b, j: (0, j))] + [full(c) for c in consts],
        out_specs=half,
        out_shape=jax.ShapeDtypeStruct((B, nb, L), BF16),
        compiler_params=pltpu.CompilerParams(dimension_semantics=("parallel", "parallel"),
                                             vmem_limit_bytes=VMEM_LIMIT),
        name="idft_level1",
    )(br, bi, v2, x02, d_t, *consts)


def hyena_long_conv_fft(v, x0, kfull, hy_d, tabs):
    B, n, C = v.shape
    nb = n // FFT_R
    L = FFT_R * C
    kr, ki = dft_level1(kfull.reshape(1, 2 * nb, L), tabs)
    hr, hi = dft_spectrum(kr.reshape(1, FFT_R, FFT_R, C), ki.reshape(1, FFT_R, FFT_R, C), tabs)
    ar, ai = dft_level1(v.reshape(B, nb, L), tabs)
    br, bi = conv_level2(ar.reshape(B, FFT_R, FFT_R, C), ai.reshape(B, FFT_R, FFT_R, C), hr, hi, tabs)
    d_t = jnp.tile(hy_d.reshape(1, C), (1, FFT_R))
    out = idft_level1_epilogue(br.reshape(B, FFT_R, L), bi.reshape(B, FFT_R, L), v.reshape(B, nb, L),
                               x0.reshape(B, nb, L), d_t, tabs)
    return out.reshape(B, n, C)


def dense_tables(n):
    t = np.arange(2 * n, dtype=np.float64)
    th = 2.0 * np.pi * np.outer(t, t) / (2 * n)
    return _hilo_tables((('c', np.cos(th)), ('s', np.sin(th))))


def _dense_conv_kernel(v_ref, x0_ref, k_ref, d_ref, ch, cl, sh, sl, o_ref, *, n):
    c_hi, c_lo, s_hi, s_lo = ch[...], cl[...], sh[...], sl[...]
    k = k_ref[...]
    hr = _dot3(c_hi, c_lo, k)
    hi = -_dot3(s_hi, s_lo, k)
    v = v_ref[0]
    xr = _dot3(c_hi[:, :n], c_lo[:, :n], v)
    xi = -_dot3(s_hi[:, :n], s_lo[:, :n], v)
    yr = xr * hr - xi * hi
    yi = xr * hi + xi * hr
    y = (_dot3(c_hi[:n, :], c_lo[:n, :], yr) - _dot3(s_hi[:n, :], s_lo[:n, :], yi)) * (0.5 / n)
    o_ref[0] = ((y + v * d_ref[...]) * x0_ref[0].astype(F32)).astype(o_ref.dtype)


def hyena_long_conv_dense(v, x0, kfull, hy_d, tabs):
    B, n, C = v.shape
    tc = 256
    seq = pl.BlockSpec((1, n, tc), lambda b, c: (b, 0, c))
    full = lambda a: pl.BlockSpec(a.shape, lambda b, c: (0,) * a.ndim)
    consts = [tabs[k] for k in ('c_hi', 'c_lo', 's_hi', 's_lo')]
    return pl.pallas_call(
        functools.partial(_dense_conv_kernel, n=n),
        grid=(B, C // tc),
        in_specs=[seq, seq, pl.BlockSpec((2 * n, tc), lambda b, c: (0, c)), pl.BlockSpec((1, tc), lambda b, c: (0, c))]
        + [full(c) for c in consts],
        out_specs=seq,
        out_shape=jax.ShapeDtypeStruct((B, n, C), BF16),
        compiler_params=pltpu.CompilerParams(dimension_semantics=("parallel", "parallel"),
                                             vmem_limit_bytes=VMEM_LIMIT),
        name="hyena_dense_conv",
    )(v, x0, kfull, hy_d.reshape(1, C), *consts)


def build_hyena_weights(p):
    w, bias = p['hy_conv_w'], p['hy_conv_b']
    cw = [w[:, i * HY_WIDTH:(i + 1) * HY_WIDTH] for i in range(3)]
    cb = [bias[i * HY_WIDTH:(i + 1) * HY_WIDTH].reshape(1, -1) for i in range(3)]
    return cw, cb


def hyena_branch_pallas(proj, p, tabs, b, n):
    x0, v = hyena_prep_pallas(proj, p['hy_w'][0], p['hy_w'][1], n)
    kfull = hyena_filter_pallas(n, p)
    r3 = lambda a: a.reshape(b, n, HY_WIDTH)
    if 2 * n == FFT_R * FFT_R:
        y = hyena_long_conv_fft(r3(v), r3(x0), kfull, p['hy_d'], tabs['fft'])
    else:
        y = hyena_long_conv_dense(r3(v), r3(x0), kfull, p['hy_d'], dense_tables(n))
    return y.reshape(b * n, HY_WIDTH)


ADA_TN = 1024


def _ada_kernel(c_ref, w_ref, b_ref, o_ref):
    o_ref[...] = jnp.dot(jax.nn.silu(c_ref[...]).astype(BF16), w_ref[...].astype(BF16),
                         preferred_element_type=F32) + b_ref[...]


def ada_modulation(cond, w_ada, b_ada):
    m, d = cond.shape
    n = w_ada.shape[1]
    return pl.pallas_call(
        _ada_kernel,
        grid=(n // ADA_TN,),
        in_specs=[pl.BlockSpec((m, d), lambda j: (0, 0)), pl.BlockSpec((d, ADA_TN), lambda j: (0, j)),
                  pl.BlockSpec((1, ADA_TN), lambda j: (0, j))],
        out_specs=pl.BlockSpec((m, ADA_TN), lambda j: (0, j)),
        out_shape=jax.ShapeDtypeStruct((m, n), F32),
        compiler_params=pltpu.CompilerParams(dimension_semantics=("parallel",), vmem_limit_bytes=VMEM_LIMIT),
        name="ada_modulation",
    )(cond, w_ada, b_ada.reshape(1, n))


def expert_choice_ffn(h2, logits, ffn_w, b, n):
    cap = EC_CAPACITY * n // N_EXPERTS
    aff = jax.nn.softmax(logits[:, :N_EXPERTS].reshape(b, n, N_EXPERTS), -1)
    gate, idx = lax.top_k(jnp.swapaxes(aff, 1, 2), cap)
    bidx = jnp.arange(b)[:, None, None]
    xs = h2.reshape(b, n, D_MODEL)[bidx, idx]
    ye = expert_ffn(xs, gate[..., None], *ffn_w)
    return jnp.zeros((b, n, D_MODEL), F32).at[bidx, idx].add(ye).reshape(b * n, D_MODEL)


def _proj_piece(proj, name, width=None):
    off, w = PROJ[name]
    return proj[:, off:off + (width or w)].astype(F32)


def trunk_layer(xc, xl, mods, p, tabs, need_ctx, b, n_ctx, n_lat):
    D = D_MODEL
    piece = lambda rows, k: mods[rows, None, k * D:(k + 1) * D]
    lat = dict(x=xl, n=n_lat, rpm=n_lat, rows=slice(0, b), cos=tabs['cos_l'], sin=tabs['sin_l'])
    ctx = dict(x=xc, n=n_ctx, rpm=b * n_ctx, rows=slice(b, b + 1), cos=tabs['cos_c'], sin=tabs['sin_c'])
    for t in (ctx, lat):
        t['proj'] = inproj(t['x'], piece(t['rows'], 0), piece(t['rows'], 1), p['w_in_p'], t['rpm'])
        t['q'], t['k'], t['v'] = mla_proj(t['proj'], t['cos'], t['sin'], p['mla_q_norm'], p['mla_kv_norm'], p['mla_w'])
        t['rw'] = rwkv_prep_pallas(t['proj'], p['rw_w'], t['n'])
    sets = (ctx, lat) if need_ctx else (lat,)

    r3 = lambda a, n: a.reshape(b, n, a.shape[-1])
    kc, vc = r3(ctx['k'], n_ctx), r3(ctx['v'], n_ctx)
    lat['mla'] = flash_attention(r3(lat['q'], n_lat), kc, vc, r3(lat['k'], n_lat), r3(lat['v'], n_lat),
                                 tq=min(n_lat, 1024), tk=min(n_lat, 1024), out_dtype=BF16)
    if need_ctx:
        ctx['mla'] = flash_attention(r3(ctx['q'], n_ctx), kc, vc, tq=n_ctx, out_dtype=BF16)

    s0 = jnp.zeros((2, b, SSM_GROUPS, SSM_STATE, SSD_GW), F32)
    for t in (ctx, lat):
        xs, bm, cm, dt = ssd_prep_pallas(t['proj'], p['ssd_w'], t['n'])
        yf, yb, s0 = ssd_scan_pallas(r3(xs, t['n']), r3(bm, t['n']), r3(cm, t['n']), r3(dt, t['n']),
                                     p['ssd_w']['a_g'], s0)
        if t in sets:
            flat = lambda a: a.reshape(b * t['n'], SSM_INNER)
            t['ssm'] = ssd_out_pallas(flat(yf), flat(yb), xs, t['proj'], p['ssd_w'])

    for t in sets:
        t['hy'] = hyena_branch_pallas(t['proj'], p, tabs, b, t['n'])

    s0 = jnp.zeros((2, b, RW_WIDTH // RW_P, RW_P, RW_P), F32)
    for t in (ctx, lat):
        r_, v_, kk_, lw0, lw1, kd0, kd1, bd0, bd1, g_, bonus = (r3(a, t['n']) for a in t['rw'])
        yf, yb, s0 = rwkv_scan_pallas(r_, v_, kk_, lw0, lw1, kd0, kd1, bd0, bd1, s0)
        if t in sets:
            flat = lambda a: a.reshape(b * t['n'], RW_WIDTH)
            t['rwkv'] = rwkv_out_pallas(flat(yf), flat(yb), flat(bonus), flat(g_), p['rw_w'])

    outs = []
    for t in sets:
        rows = b * t['n']
        branches = [t['mla'].reshape(rows, BRANCH_WIDTH), t['ssm'], t['hy'], t['rwkv']]
        merged = merge_branches_pallas(branches, t['proj'], p['w_branch_b'])
        x1, h2, logits = outproj_postnorm(merged, t['x'], piece(t['rows'], 2), piece(t['rows'], 3), piece(t['rows'], 4),
                                          p['ln1_g'], p['ln1_b'], p['w_out_b'], p['w_router'], t['rpm'])
        moe = expert_choice_ffn(h2, logits, p['ffn_w'], b, t['n'])
        outs.append(postnorm(x1, moe, piece(t['rows'], 5), p['ln2_g'], p['ln2_b'], t['rpm']))
    return (outs[0], outs[1]) if need_ctx else (None, outs[0])


def kernel(x, c, ctx, c_ctx, w_ada, b_ada, w_in, mla_q_norm, mla_w_q_up, mla_kv_norm, mla_w_kv_up,
           ssm_conv_w, ssm_conv_b, ssm_dt_bias, ssm_a_log, ssm_d, ssm_norm,
           hy_conv_w, hy_conv_b, hy_w1, hy_b1, hy_w2, hy_b2, hy_w3, hy_freq, hy_d,
           rw_mu, rw_w0, rw_w_up, rw_a0, rw_a_up, rw_g_up, rw_kk, rw_ka, rw_rk, rw_ln_g, rw_ln_b,
           w_branch, w_out, ln1_g, ln1_b, w_router, w_gate_e, w_up_e, w_down_e, ln2_g, ln2_b):
    stacked = dict(
        w_in=w_in, mla_q_norm=mla_q_norm, mla_w_q_up=mla_w_q_up, mla_kv_norm=mla_kv_norm, mla_w_kv_up=mla_w_kv_up,
        ssm_conv_w=ssm_conv_w, ssm_conv_b=ssm_conv_b, ssm_dt_bias=ssm_dt_bias, ssm_a_log=ssm_a_log,
        ssm_d=ssm_d, ssm_norm=ssm_norm,
        hy_conv_w=hy_conv_w, hy_conv_b=hy_conv_b, hy_w1=hy_w1, hy_b1=hy_b1, hy_w2=hy_w2, hy_b2=hy_b2,
        hy_w3=hy_w3, hy_freq=hy_freq, hy_d=hy_d,
        rw_mu=rw_mu, rw_w0=rw_w0, rw_w_up=rw_w_up, rw_a0=rw_a0, rw_a_up=rw_a_up, rw_g_up=rw_g_up,
        rw_kk=rw_kk, rw_ka=rw_ka, rw_rk=rw_rk, rw_ln_g=rw_ln_g, rw_ln_b=rw_ln_b,
        w_branch=w_branch, w_out=w_out, ln1_g=ln1_g, ln1_b=ln1_b,
        w_router=w_router, w_gate_e=w_gate_e, w_up_e=w_up_e, w_down_e=w_down_e, ln2_g=ln2_g, ln2_b=ln2_b)
    b, n_lat, D = x.shape
    n_ctx = ctx.shape[1]
    cos, sin = rope_tables_128(n_lat)
    ones = jnp.pad(jnp.ones((b * n_ctx, QK_ROPE), F32), ((0, 0), (0, 128 - QK_ROPE)))
    tabs = dict(cos_l=jnp.tile(cos, (b, 1)), sin_l=jnp.tile(sin, (b, 1)), cos_c=ones, sin_c=jnp.zeros_like(ones),
                fft=dft_tables())
    cond = jnp.concatenate([c, c_ctx[None], jnp.zeros((8 - b - 1, D), F32)], 0)
    xc, xl = ctx.reshape(b * n_ctx, D), x.reshape(b * n_lat, D)
    for i in range(DEPTH):
        p = {name: arr[i] for name, arr in stacked.items()}
        p['w_in_p'] = build_w_in(p['w_in'])
        p['mla_w'] = build_mla_weights(p)
        p['rw_w'] = build_rwkv_weights(p)
        p['ssd_w'] = build_ssd_weights(p)
        p['hy_w'] = build_hyena_weights(p)
        p['w_branch_b'] = p['w_branch'].astype(BF16)
        p['w_out_b'] = p['w_out'].astype(BF16)
        p['ffn_w'] = build_expert_weights(p)
        mods = ada_modulation(cond, w_ada[i], b_ada[i])
        xc, xl = trunk_layer(xc, xl, mods, p, tabs, i < DEPTH - 1, b, n_ctx, n_lat)
    return xl.reshape(b, n_lat, D)
```

```python
import functools
import math

import jax
import jax.numpy as jnp
from jax import lax
import numpy as np
from jax.experimental import pallas as pl
from jax.experimental.pallas import tpu as pltpu

D_MODEL = 2048
DEPTH = 2

GRID_W = 64
N_BRANCH = 4
BRANCH_WIDTH = D_MODEL // 2

MLA_Q_LORA = D_MODEL // 4
MLA_KV_LORA = D_MODEL // 4
QK_NOPE = 128
QK_ROPE = 64
V_HEAD = 128
MLA_HEADS = BRANCH_WIDTH // V_HEAD
ROPE_BASE = 10000.0

SSM_HEAD_DIM = 64
SSM_HEADS = BRANCH_WIDTH // SSM_HEAD_DIM
SSM_INNER = SSM_HEADS * SSM_HEAD_DIM
SSM_GROUPS = 2
SSM_STATE = 128
SSM_CHUNK = 128
SSM_XBC = SSM_INNER + 2 * SSM_GROUPS * SSM_STATE

HY_WIDTH = BRANCH_WIDTH
HY_EMB = 33
HY_BANDS = (HY_EMB - 1) // 2
HY_FILTER_HIDDEN = 64
HY_TARGET = 1e-2
HY_SLOW_FRAC = 1.5
HY_QUICK_FRAC = 0.3
HY_MIN_DECAY = math.log(HY_TARGET) / HY_SLOW_FRAC
HY_MAX_DECAY = math.log(HY_TARGET) / HY_QUICK_FRAC

RW_HEAD_DIM = 64
RW_WIDTH = BRANCH_WIDTH
RW_HEADS = RW_WIDTH // RW_HEAD_DIM
RW_DECAY_LORA = 64
RW_ICLR_LORA = 64
RW_GATE_LORA = 160
RW_GN_EPS = 64e-5

N_EXPERTS = 16
EXPERT_FF = 1408
EC_CAPACITY = 2

ALPHA = (2 * DEPTH) ** 0.25

MLA_COLS = MLA_Q_LORA + MLA_KV_LORA + QK_ROPE
SSM_COLS = SSM_INNER + SSM_XBC + 2 * SSM_HEADS
HY_COLS = 3 * HY_WIDTH
RW_COLS = 3 * RW_WIDTH + 2 * RW_DECAY_LORA + 2 * RW_ICLR_LORA + RW_GATE_LORA

F32 = jnp.float32
BF16 = jnp.bfloat16
HI = lax.Precision.HIGHEST

RW_T = 64
RW_P = 128
RW_PAIRS_PER_STEP = 4
RW_INV_BASE = 8
ATT_DK = 256
ATT_DV = 128
TM = 512
VMEM_LIMIT = 48 * 1024 * 1024


def _mm(a, b):
    return jnp.dot(a.astype(BF16), b.astype(BF16), preferred_element_type=F32)


def _mm_nt(a, b):
    return lax.dot_general(a.astype(BF16), b.astype(BF16), (((1,), (1,)), ((), ())), preferred_element_type=F32)


def _mm_tn(a, b):
    return lax.dot_general(a.astype(BF16), b.astype(BF16), (((0,), (0,)), ((), ())), preferred_element_type=F32)


def _ln(x, eps=1e-6):
    mu = jnp.mean(x, -1, keepdims=True)
    xc = x - mu
    return xc * lax.rsqrt(jnp.mean(xc * xc, -1, keepdims=True) + eps)


def _rwkv_pair_chunks(chains, inv_masks):
    T = RW_T
    each = lambda f, *ls: [f(*xs) for xs in zip(*ls)]
    S = [ch[0] for ch in chains]
    r, lw, k, v, kk, b = ([ch[1][i] for ch in chains] for i in range(6))
    a = [-x for x in kk]
    rev = [ch[2] for ch in chains]
    cum, head_lo, strict_bd, incl_bd, eye_bd = ([ch[3][i] for ch in chains] for i in range(5))

    def stack(x, lo):
        return jnp.concatenate([jnp.where(lo, x, 0.0), jnp.where(lo, 0.0, x)], axis=0)

    c = each(lambda m, x: jnp.dot(m, x, precision=HI, preferred_element_type=F32), cum, lw)
    cT = each(lambda x, rv: x[0:1, :] if rv else x[T - 1:T, :], c, rev)
    e_neg = each(lambda x: jnp.exp(-x), c)
    e_end = each(lambda x, y: jnp.exp(x - y), cT, c)
    a_s = each(lambda x, cc, l, lo: stack(x * jnp.exp(cc - l), lo), a, c, lw, head_lo)
    r_s = each(lambda x, cc, lo: stack(x * jnp.exp(cc), lo), r, c, head_lo)
    b_s = each(lambda x, e, lo: stack(x * e, lo), b, e_neg, head_lo)
    k_s = each(lambda x, e, lo: stack(x * e, lo), k, e_neg, head_lo)
    bh_s = each(lambda x, e, lo: stack(x * e, lo), b, e_end, head_lo)
    kh_s = each(lambda x, e, lo: stack(x * e, lo), k, e_end, head_lo)
    v_s = each(stack, v, head_lo)
    g_T = each(jnp.exp, cT)

    ar = each(lambda x, y: jnp.concatenate([x, y], axis=0), a_s, r_s)
    gbk = each(lambda x, y, z: _mm_nt(x, jnp.concatenate([y, z], axis=0)), ar, b_s, k_s)
    gb = [x[:, :2 * T] for x in gbk]
    gk = [x[:, 2 * T:] for x in gbk]
    n = each(lambda m, g: jnp.where(m, g[:2 * T], 0.0), strict_bd, gb)
    l_ak = each(lambda m, g: jnp.where(m, g[:2 * T], 0.0), strict_bd, gk)
    m_rb = each(lambda m, g: jnp.where(m, g[2 * T:], 0.0), incl_bd, gb)
    m_rk = each(lambda m, g: jnp.where(m, g[2 * T:], 0.0), incl_bd, gk)
    lv = each(_mm, l_ak, v_s)
    n8 = each(lambda x: jnp.where(inv_masks[0], x, 0.0), n)
    p = each(lambda e, x: e + x, eye_bd, n8)
    m = each(_mm, n8, n8)
    p = each(lambda x, y: x + _mm(y, x), p, m)
    m = each(_mm, m, m)
    p = each(lambda x, y: x + _mm(y, x), p, m)
    for merge in inv_masks[1:]:
        n_off = each(lambda x: jnp.where(merge, x, 0.0), n)
        p = each(lambda x, y: x + _mm(_mm(x, y), x), p, n_off)
    wu = each(lambda x, y, z: _mm(x, jnp.concatenate([y, z], axis=1)), p, a_s, lv)
    w = [x[:, :RW_P] for x in wu]
    u0 = [x[:, RW_P:] for x in wu]
    wr = each(lambda x, y, s: _mm_nt(jnp.concatenate([x, y], axis=0), s), w, r_s, S)
    uv = each(lambda x, y, z: jnp.concatenate([x[:2 * T] + y, z], axis=0), wr, u0, v_s)
    y_s = each(lambda x, p1, p2, z: x[2 * T:] + _mm(jnp.concatenate([p1, p2], axis=1), z), wr, m_rb, m_rk, uv)
    y = each(lambda x: x[:T] + x[T:], y_s)
    S_new = each(lambda s, g, z, p1, p2: s * g + _mm_tn(z, jnp.concatenate([p1, p2], axis=0)),
                 S, g_T, uv, bh_s, kh_s)
    return list(zip(y, S_new))


def _rwkv_consts():
    T = RW_T
    row = lax.broadcasted_iota(jnp.int32, (T, T), 0)
    col = lax.broadcasted_iota(jnp.int32, (T, T), 1)
    cum_f = (row >= col).astype(F32)
    cum_b = (row <= col).astype(F32)
    head_lo = lax.broadcasted_iota(jnp.int32, (T, RW_P), 1) < RW_P // 2
    r2 = lax.broadcasted_iota(jnp.int32, (2 * T, 2 * T), 0)
    c2 = lax.broadcasted_iota(jnp.int32, (2 * T, 2 * T), 1)
    same = (r2 >= T) == (c2 >= T)
    eye_bd = (r2 == c2).astype(F32)
    fwd = (cum_f, head_lo, same & (r2 > c2), same & (r2 >= c2), eye_bd)
    bwd = (cum_b, head_lo, same & (r2 < c2), same & (r2 <= c2), eye_bd)
    blk = lambda s: (r2 // s) == (c2 // s)
    inv_masks = [blk(RW_INV_BASE)]
    s = RW_INV_BASE
    while s < T:
        inv_masks.append(blk(2 * s) & jnp.logical_not(blk(s)))
        s *= 2
    return fwd, bwd, inv_masks


def _rwkv_kernel(rf, vf, af, lwf, kf, bf, rb, vb, ab, lwb, kb, bb, s0_ref, yf_ref, yb_ref, sT_ref, s_scr, *, nsub):
    j = pl.program_id(2)

    @pl.when(j == 0)
    def _():
        s_scr[...] = s0_ref[:, 0]

    T = RW_T
    cf, cb, inv_masks = _rwkv_consts()

    def body(i, carry):
        sf = pl.ds(pl.multiple_of(i * T, T), T)
        sb = pl.ds(pl.multiple_of((nsub - 1 - i) * T, T), T)
        lanes = [slice(q * RW_P, (q + 1) * RW_P) for q in range(RW_PAIRS_PER_STEP)]
        ins = []
        for ln in lanes:
            ins.append(tuple(ref[0, sf, ln].astype(F32) for ref in (rf, lwf, kf, vf, af, bf)))
            ins.append(tuple(ref[0, sb, ln].astype(F32) for ref in (rb, lwb, kb, vb, ab, bb)))
        res = _rwkv_pair_chunks([(carry[c], ins[c], bool(c % 2), cb if c % 2 else cf)
                                 for c in range(2 * RW_PAIRS_PER_STEP)], inv_masks)
        for q, ln in enumerate(lanes):
            yf_ref[0, sf, ln] = res[2 * q][0]
            yb_ref[0, sb, ln] = res[2 * q + 1][0]
        return tuple(s for _, s in res)

    init = tuple(s_scr[d, q] for q in range(RW_PAIRS_PER_STEP) for d in range(2))
    fin = lax.fori_loop(0, nsub, body, init)
    for q in range(RW_PAIRS_PER_STEP):
        for d in range(2):
            s_scr[d, q] = fin[2 * q + d]

    @pl.when(j == pl.num_programs(2) - 1)
    def _():
        for q in range(RW_PAIRS_PER_STEP):
            for d in range(2):
                sT_ref[d, 0, q] = fin[2 * q + d]


def rwkv_scan_pallas(r, v, kk, lw0, lw1, kd0, kd1, bd0, bd1, s0, *, tb=256):
    B, n, C = r.shape
    tb = min(tb, n)
    nb = n // tb
    pp = RW_PAIRS_PER_STEP
    assert n % tb == 0 and tb % RW_T == 0 and C % (pp * RW_P) == 0
    fw = pl.BlockSpec((1, tb, pp * RW_P), lambda b, p, j: (b, j, p))
    bw = pl.BlockSpec((1, tb, pp * RW_P), lambda b, p, j: (b, nb - 1 - j, p))
    st = pl.BlockSpec((2, 1, pp, RW_P, RW_P), lambda b, p, j: (0, b, p, 0, 0))
    return pl.pallas_call(
        functools.partial(_rwkv_kernel, nsub=tb // RW_T),
        grid=(B, C // (pp * RW_P), nb),
        in_specs=[fw] * 6 + [bw] * 6 + [st],
        out_specs=[fw, bw, st],
        out_shape=[jax.ShapeDtypeStruct((B, n, C), F32), jax.ShapeDtypeStruct((B, n, C), F32),
                   jax.ShapeDtypeStruct(s0.shape, F32)],
        scratch_shapes=[pltpu.VMEM((2, pp, RW_P, RW_P), F32)],
        compiler_params=pltpu.CompilerParams(dimension_semantics=("parallel", "parallel", "arbitrary")),
        name="rwkv_scan",
    )(r, v, kk, lw0, kd0, bd0, r, v, kk, lw1, kd1, bd1, s0)


FLASH_STREAMS = 4


def _flash_update(q_ref, k, vt, m_scr, acc_scr):
    tq = q_ref.shape[1]
    cols = [slice(i * tq // FLASH_STREAMS, (i + 1) * tq // FLASH_STREAMS) for i in range(FLASH_STREAMS)]
    s = [lax.dot_general(k, q_ref[0, c, :], (((1,), (1,)), ((), ())), preferred_element_type=F32) for c in cols]
    m_prev = [m_scr[:, c] for c in cols]
    m_new = [jnp.maximum(mp, jnp.max(x, axis=0, keepdims=True)) for mp, x in zip(m_prev, s)]
    p = [jnp.exp((x - mn).astype(BF16)) for x, mn in zip(s, m_new)]
    pv = [jnp.dot(vt, x, preferred_element_type=F32) for x in p]
    for c, mp, mn, x in zip(cols, m_prev, m_new, pv):
        acc_scr[:, c] = jnp.exp(mp - mn) * acc_scr[:, c] + x
        m_scr[:, c] = mn


def _flash_kernel(q_ref, kc_ref, vtc_ref, *rest, with_latent):
    if with_latent:
        kl_ref, vtl_ref, o_ref, m_scr, acc_scr = rest
    else:
        o_ref, m_scr, acc_scr = rest
    j = pl.program_id(3)

    @pl.when(j == 0)
    def _():
        m_scr[...] = jnp.full(m_scr.shape, -jnp.inf, F32)
        acc_scr[...] = jnp.zeros(acc_scr.shape, F32)
        _flash_update(q_ref, kc_ref[0], vtc_ref[0], m_scr, acc_scr)

    if with_latent:
        @pl.when(j > 0)
        def _():
            _flash_update(q_ref, kl_ref[0], vtl_ref[0], m_scr, acc_scr)

    @pl.when(j == pl.num_programs(3) - 1)
    def _():
        acc = acc_scr[...]
        o_ref[0] = jnp.transpose(acc[:ATT_DV] / acc[ATT_DV:ATT_DV + 1]).astype(o_ref.dtype)


def transpose_values(v):
    B, nk, hv = v.shape
    H = hv // ATT_DV
    ext = jnp.zeros((B, nk, H, ATT_DV), v.dtype).at[..., 0].set(1)
    va = jnp.concatenate([v.reshape(B, nk, H, ATT_DV), ext], -1)
    return jnp.transpose(va, (0, 2, 3, 1)).reshape(B, H * 2 * ATT_DV, nk)


def flash_attention(q, kc, vc, kl=None, vl=None, *, tq, tk=None, out_dtype=F32):
    B, nq, hq = q.shape
    nc = kc.shape[1]
    H = hq // ATT_DK
    assert nq % tq == 0
    with_latent = kl is not None
    in_specs = [pl.BlockSpec((1, tq, ATT_DK), lambda b, h, i, j: (b, i, h)),
                pl.BlockSpec((1, nc, ATT_DK), lambda b, h, i, j: (b, 0, h)),
                pl.BlockSpec((1, 2 * ATT_DV, nc), lambda b, h, i, j: (b, h, 0))]
    args = [q, kc, transpose_values(vc)]
    steps = 1
    if with_latent:
        assert kl.shape[1] % tk == 0
        steps += kl.shape[1] // tk
        in_specs += [pl.BlockSpec((1, tk, ATT_DK), lambda b, h, i, j: (b, jnp.maximum(j - 1, 0), h)),
                     pl.BlockSpec((1, 2 * ATT_DV, tk), lambda b, h, i, j: (b, h, jnp.maximum(j - 1, 0)))]
        args += [kl, transpose_values(vl)]
    return pl.pallas_call(
        functools.partial(_flash_kernel, with_latent=with_latent),
        grid=(B, H, nq // tq, steps),
        in_specs=in_specs,
        out_specs=pl.BlockSpec((1, tq, ATT_DV), lambda b, h, i, j: (b, i, h)),
        out_shape=jax.ShapeDtypeStruct((B, nq, H * ATT_DV), out_dtype),
        scratch_shapes=[pltpu.VMEM((1, tq), F32), pltpu.VMEM((2 * ATT_DV, tq), F32)],
        compiler_params=pltpu.CompilerParams(
            dimension_semantics=("parallel", "parallel", "parallel", "arbitrary")),
        name="mla_flash",
    )(*args)


PROJ = {}
_off = 0
for _name, _w in (('gate', 4 * D_MODEL), ('hy_x0', HY_WIDTH), ('hy_x1', HY_WIDTH), ('hy_v', HY_WIDTH),
                  ('rw_r', RW_WIDTH), ('rw_k', RW_WIDTH), ('rw_v', RW_WIDTH), ('ssm_z', SSM_INNER),
                  ('ssm_x', SSM_INNER), ('mla_cq', MLA_Q_LORA), ('mla_ckv', MLA_KV_LORA),
                  ('ssm_b', 256), ('ssm_c', 256), ('rw_gd', 256), ('rw_wd', 128), ('rw_ad', 128),
                  ('mla_kpe', 128), ('mla_kpe_sw', 128), ('ssm_dt', 128)):
    assert _off % _w == 0, (_name, _off, _w)
    PROJ[_name] = (_off, _w)
    _off += _w
PROJ_TN = 512
PROJ_N = -(-_off // PROJ_TN) * PROJ_TN


def _rope_swap_matrix():
    p = np.zeros((QK_ROPE, QK_ROPE), np.float32)
    q = QK_ROPE // 4
    for g in range(2):
        for i in range(q):
            p[g * 2 * q + q + i, g * 2 * q + i] = -1.0
            p[g * 2 * q + i, g * 2 * q + q + i] = 1.0
    return jnp.asarray(p)


def build_w_in(w_in):
    cuts = np.cumsum([0, MLA_COLS, SSM_COLS, HY_COLS, RW_COLS, 4 * D_MODEL]).tolist()
    mla, ssm, hy, rw, gate = (w_in[:, cuts[i]:cuts[i + 1]] for i in range(5))
    kpe = mla[:, MLA_Q_LORA + MLA_KV_LORA:]
    src = {
        'gate': gate, 'hy_x0': hy[:, :HY_WIDTH], 'hy_x1': hy[:, HY_WIDTH:2 * HY_WIDTH], 'hy_v': hy[:, 2 * HY_WIDTH:],
        'rw_r': rw[:, :RW_WIDTH], 'rw_k': rw[:, RW_WIDTH:2 * RW_WIDTH], 'rw_v': rw[:, 2 * RW_WIDTH:3 * RW_WIDTH],
        'rw_wd': rw[:, 3 * RW_WIDTH:3 * RW_WIDTH + 128], 'rw_ad': rw[:, 3 * RW_WIDTH + 128:3 * RW_WIDTH + 256],
        'rw_gd': rw[:, 3 * RW_WIDTH + 256:],
        'ssm_z': ssm[:, :SSM_INNER], 'ssm_x': ssm[:, SSM_INNER:2 * SSM_INNER],
        'ssm_b': ssm[:, 2 * SSM_INNER:2 * SSM_INNER + 256], 'ssm_c': ssm[:, 2 * SSM_INNER + 256:2 * SSM_INNER + 512],
        'ssm_dt': ssm[:, 2 * SSM_INNER + 512:],
        'mla_cq': mla[:, :MLA_Q_LORA], 'mla_ckv': mla[:, MLA_Q_LORA:MLA_Q_LORA + MLA_KV_LORA],
        'mla_kpe': kpe, 'mla_kpe_sw': kpe @ _rope_swap_matrix(),
    }
    cols, pos = [], 0
    for name, (off, w) in PROJ.items():
        assert off == pos
        piece = src[name]
        cols.append(jnp.pad(piece, ((0, 0), (0, w - piece.shape[1]))))
        pos += w
    cols.append(jnp.zeros((w_in.shape[0], PROJ_N - pos), w_in.dtype))
    return jnp.concatenate(cols, 1).astype(BF16)


def _inproj_kernel(x_ref, sh_ref, sc_ref, w_ref, o_ref, h_scr):
    @pl.when(pl.program_id(1) == 0)
    def _():
        h_scr[...] = (_ln(x_ref[...]) * (1.0 + sc_ref[0]) + sh_ref[0]).astype(BF16)

    o_ref[...] = jnp.dot(h_scr[...], w_ref[...], preferred_element_type=F32).astype(o_ref.dtype)


def inproj(x2d, shift, scale, w_p, rows_per_mod):
    R, D = x2d.shape
    tm = min(TM, R)
    bpm = max(rows_per_mod // tm, 1)
    mod = pl.BlockSpec((1, 1, D), lambda i, j: (i // bpm, 0, 0))
    return pl.pallas_call(
        _inproj_kernel,
        grid=(R // tm, PROJ_N // PROJ_TN),
        in_specs=[pl.BlockSpec((tm, D), lambda i, j: (i, 0)), mod, mod,
                  pl.BlockSpec((D, PROJ_TN), lambda i, j: (0, j))],
        out_specs=pl.BlockSpec((tm, PROJ_TN), lambda i, j: (i, j)),
        out_shape=jax.ShapeDtypeStruct((R, PROJ_N), BF16),
        scratch_shapes=[pltpu.VMEM((tm, D), BF16)],
        compiler_params=pltpu.CompilerParams(dimension_semantics=("parallel", "arbitrary"),
                                             vmem_limit_bytes=VMEM_LIMIT),
        name="inproj",
    )(x2d, shift, scale, w_p)


def build_mla_weights(p):
    wq = p['mla_w_q_up'].reshape(MLA_Q_LORA, MLA_HEADS, QK_NOPE + QK_ROPE)
    wkv = p['mla_w_kv_up'].reshape(MLA_KV_LORA, MLA_HEADS, QK_NOPE + V_HEAD)
    wq_p = jnp.pad(wq, ((0, 0), (0, 0), (0, ATT_DK - QK_NOPE - QK_ROPE))).reshape(MLA_Q_LORA, MLA_HEADS * ATT_DK)
    wsw = jnp.einsum('chd,de->che', wq[:, :, QK_NOPE:], _rope_swap_matrix())
    wsw_p = jnp.pad(wsw, ((0, 0), (0, 0), (0, 128 - QK_ROPE))).reshape(MLA_Q_LORA, MLA_HEADS * 128)
    wkn = wkv[:, :, :QK_NOPE].reshape(MLA_KV_LORA, MLA_HEADS * QK_NOPE)
    wv = wkv[:, :, QK_NOPE:].reshape(MLA_KV_LORA, MLA_HEADS * V_HEAD)
    return wq_p.astype(BF16), wsw_p.astype(BF16), wkn.astype(BF16), wv.astype(BF16)


def axial_rope_tables(n_tokens):
    rows = n_tokens // GRID_W
    row = jnp.repeat(jnp.arange(rows), GRID_W).astype(F32)
    col = jnp.tile(jnp.arange(GRID_W), rows).astype(F32)
    half = QK_ROPE // 2
    inv = ROPE_BASE ** (-jnp.arange(0, half, 2, dtype=F32) / half)
    ang = jnp.stack([row[:, None] * inv, col[:, None] * inv], 1)
    return jnp.cos(ang), jnp.sin(ang)


def rope_tables_128(n_tokens):
    cos, sin = axial_rope_tables(n_tokens)
    full = lambda t: jnp.concatenate([t[:, 0], t[:, 0], t[:, 1], t[:, 1]], -1)
    pad = lambda t: jnp.pad(t, ((0, 0), (0, 128 - QK_ROPE)))
    return pad(full(cos)), pad(full(sin))


def _mla_proj_kernel(cq_ref, ckv_ref, kpe_ref, ksw_ref, cos_ref, sin_ref, gq_ref, gkv_ref, wq_ref, wsw_ref, wkn_ref,
                     wv_ref, q_ref, k_ref, v_ref, *, q_scale):
    def rms(x, g):
        return (x * lax.rsqrt(jnp.mean(x * x, -1, keepdims=True) + 1e-6) * g).astype(BF16)

    cqn = rms(cq_ref[...].astype(F32), gq_ref[...])
    ckvn = rms(ckv_ref[...].astype(F32), gkv_ref[...])
    cos = cos_ref[...]
    sin = sin_ref[...]
    q = jnp.dot(cqn, wq_ref[...], preferred_element_type=F32)
    qsw = jnp.dot(cqn, wsw_ref[...], preferred_element_type=F32)
    kn = jnp.dot(ckvn, wkn_ref[...], preferred_element_type=F32)
    v_ref[...] = jnp.dot(ckvn, wv_ref[...], preferred_element_type=F32).astype(v_ref.dtype)
    kpe = (kpe_ref[...].astype(F32) * cos + ksw_ref[...].astype(F32) * sin).astype(k_ref.dtype)
    for h in range(MLA_HEADS):
        lo = h * ATT_DK
        q_ref[:, lo:lo + 128] = (q[:, lo:lo + 128] * q_scale).astype(q_ref.dtype)
        pe = q[:, lo + 128:lo + 256] * cos + qsw[:, h * 128:(h + 1) * 128] * sin
        q_ref[:, lo + 128:lo + 256] = (pe * q_scale).astype(q_ref.dtype)
        k_ref[:, lo:lo + 128] = kn[:, h * 128:(h + 1) * 128].astype(k_ref.dtype)
        k_ref[:, lo + 128:lo + 256] = kpe


def mla_proj(proj, cos, sin, gq, gkv, weights):
    R = proj.shape[0]
    tm = min(TM, R)
    wq, wsw, wkn, wv = weights
    col = lambda name: pl.BlockSpec((tm, PROJ[name][1]), lambda i, o=PROJ[name][0] // PROJ[name][1]: (i, o))
    row = lambda w: pl.BlockSpec((tm, w), lambda i: (i, 0))
    full = lambda a: pl.BlockSpec(a.shape, lambda i: (0,) * a.ndim)
    gq2, gkv2 = gq.reshape(1, -1), gkv.reshape(1, -1)
    return pl.pallas_call(
        functools.partial(_mla_proj_kernel, q_scale=float((QK_NOPE + QK_ROPE) ** -0.5)),
        grid=(R // tm,),
        in_specs=[col('mla_cq'), col('mla_ckv'), col('mla_kpe'), col('mla_kpe_sw'), row(128), row(128),
                  full(gq2), full(gkv2), full(wq), full(wsw), full(wkn), full(wv)],
        out_specs=[row(MLA_HEADS * ATT_DK), row(MLA_HEADS * ATT_DK), row(MLA_HEADS * ATT_DV)],
        out_shape=[jax.ShapeDtypeStruct((R, MLA_HEADS * ATT_DK), BF16),
                   jax.ShapeDtypeStruct((R, MLA_HEADS * ATT_DK), BF16),
                   jax.ShapeDtypeStruct((R, MLA_HEADS * ATT_DV), BF16)],
        compiler_params=pltpu.CompilerParams(dimension_semantics=("parallel",), vmem_limit_bytes=VMEM_LIMIT),
        name="mla_proj",
    )(proj, proj, proj, proj, cos, sin, gq2, gkv2, wq, wsw, wkn, wv)


MERGE_TN = 512


def _merge_kernel(b0, b1, b2, b3, g0, g1, g2, g3, w_ref, o_ref):
    acc = None
    for i, (b, g) in enumerate(((b0, g0), (b1, g1), (b2, g2), (b3, g3))):
        t = jax.nn.sigmoid(g[...].astype(F32)) * jnp.dot(b[...], w_ref[i], preferred_element_type=F32)
        acc = t if acc is None else acc + t
    o_ref[...] = acc.astype(o_ref.dtype)


def merge_branches_pallas(branches, proj, w_branch):
    R = proj.shape[0]
    tm = min(TM, R)
    nj = D_MODEL // MERGE_TN
    g0 = PROJ['gate'][0] // MERGE_TN
    br = pl.BlockSpec((tm, BRANCH_WIDTH), lambda i, j: (i, 0))
    gates = [pl.BlockSpec((tm, MERGE_TN), lambda i, j, k=k: (i, g0 + k * nj + j)) for k in range(N_BRANCH)]
    return pl.pallas_call(
        _merge_kernel,
        grid=(R // tm, nj),
        in_specs=[br] * 4 + gates + [pl.BlockSpec((N_BRANCH, BRANCH_WIDTH, MERGE_TN), lambda i, j: (0, 0, j))],
        out_specs=pl.BlockSpec((tm, MERGE_TN), lambda i, j: (i, j)),
        out_shape=jax.ShapeDtypeStruct((R, D_MODEL), BF16),
        compiler_params=pltpu.CompilerParams(dimension_semantics=("parallel", "arbitrary"),
                                             vmem_limit_bytes=VMEM_LIMIT),
        name="branch_merge",
    )(*branches, proj, proj, proj, proj, w_branch)


ROUTER_PAD = 128


def _mm3_f32(a, b_hi, b_lo):
    ah = a.astype(BF16)
    al = (a - ah.astype(F32)).astype(BF16)
    d = lambda x, y: jnp.dot(x, y, preferred_element_type=F32)
    return d(ah, b_hi) + d(ah, b_lo) + d(al, b_hi)


def _outproj_kernel(m_ref, x_ref, g1_ref, sh2_ref, sc2_ref, lng_ref, lnb_ref, w_ref, wr_hi, wr_lo,
                    x_out, h2_out, logit_out):
    y = jnp.dot(m_ref[...], w_ref[...], preferred_element_type=F32)
    xn = _ln(ALPHA * x_ref[...] + g1_ref[0] * y) * lng_ref[...] + lnb_ref[...]
    x_out[...] = xn
    h2 = _ln(xn) * (1.0 + sc2_ref[0]) + sh2_ref[0]
    h2_out[...] = h2.astype(h2_out.dtype)
    logit_out[...] = _mm3_f32(h2, wr_hi[...], wr_lo[...])


def outproj_postnorm(merged, x2d, g1, sh2, sc2, ln_g, ln_b, w_out, w_router, rows_per_mod):
    R, D = x2d.shape
    tm = min(TM, R)
    bpm = max(rows_per_mod // tm, 1)
    mod = pl.BlockSpec((1, 1, D), lambda i: (i // bpm, 0, 0))
    row = lambda w: pl.BlockSpec((tm, w), lambda i: (i, 0))
    full = lambda a: pl.BlockSpec(a.shape, lambda i: (0,) * a.ndim)
    wr = jnp.pad(w_router, ((0, 0), (0, ROUTER_PAD - w_router.shape[1])))
    wr_hi = wr.astype(BF16)
    wr_lo = (wr - wr_hi.astype(F32)).astype(BF16)
    lg, lb = ln_g.reshape(1, D), ln_b.reshape(1, D)
    return pl.pallas_call(
        _outproj_kernel,
        grid=(R // tm,),
        in_specs=[row(D), row(D), mod, mod, mod, full(lg), full(lb), full(w_out), full(wr_hi), full(wr_lo)],
        out_specs=[row(D), row(D), row(ROUTER_PAD)],
        out_shape=[jax.ShapeDtypeStruct((R, D), F32), jax.ShapeDtypeStruct((R, D), BF16),
                   jax.ShapeDtypeStruct((R, ROUTER_PAD), F32)],
        compiler_params=pltpu.CompilerParams(dimension_semantics=("parallel",), vmem_limit_bytes=VMEM_LIMIT),
        name="outproj_postnorm",
    )(merged, x2d, g1, sh2, sc2, lg, lb, w_out, wr_hi, wr_lo)


def _postnorm_kernel(x_ref, y_ref, g_ref, lng_ref, lnb_ref, o_ref):
    o_ref[...] = _ln(ALPHA * x_ref[...] + g_ref[0] * y_ref[...]) * lng_ref[...] + lnb_ref[...]


def postnorm(x2d, y2d, g, ln_g, ln_b, rows_per_mod):
    R, D = x2d.shape
    tm = min(TM, R)
    bpm = max(rows_per_mod // tm, 1)
    row = pl.BlockSpec((tm, D), lambda i: (i, 0))
    vec = pl.BlockSpec((1, D), lambda i: (0, 0))
    return pl.pallas_call(
        _postnorm_kernel,
        grid=(R // tm,),
        in_specs=[row, row, pl.BlockSpec((1, 1, D), lambda i: (i // bpm, 0, 0)), vec, vec],
        out_specs=row,
        out_shape=jax.ShapeDtypeStruct((R, D), F32),
        compiler_params=pltpu.CompilerParams(dimension_semantics=("parallel",), vmem_limit_bytes=VMEM_LIMIT),
        name="postnorm",
    )(x2d, y2d, g, ln_g.reshape(1, D), ln_b.reshape(1, D))


FF_KT = 256
FF_NT = 512


def _expert_hidden_kernel(x_ref, wg_ref, wu_ref, h_ref, g_scr, u_scr):
    kt = pl.program_id(2)
    x = x_ref[0, 0]
    g = jnp.dot(x, wg_ref[0].astype(BF16), preferred_element_type=F32)
    u = jnp.dot(x, wu_ref[0].astype(BF16), preferred_element_type=F32)

    @pl.when(kt == 0)
    def _():
        g_scr[...] = g
        u_scr[...] = u

    @pl.when(kt > 0)
    def _():
        g_scr[...] += g
        u_scr[...] += u

    @pl.when(kt == pl.num_programs(2) - 1)
    def _():
        h_ref[0, 0] = (jax.nn.silu(g_scr[...]) * u_scr[...]).astype(h_ref.dtype)


def _expert_down_kernel(h_ref, gate_ref, wd_ref, o_ref):
    o_ref[0, 0] = jnp.dot(h_ref[0, 0], wd_ref[0].astype(BF16), preferred_element_type=F32) * gate_ref[0, 0]


def expert_ffn(xs, gate, wg, wu, wd):
    S, E, C, D = xs.shape
    F = wg.shape[-1]
    hid = pl.pallas_call(
        _expert_hidden_kernel,
        grid=(E, S, D // FF_KT),
        in_specs=[pl.BlockSpec((1, 1, C, FF_KT), lambda e, s, k: (s, e, 0, k)),
                  pl.BlockSpec((1, FF_KT, F), lambda e, s, k: (e, k, 0)),
                  pl.BlockSpec((1, FF_KT, F), lambda e, s, k: (e, k, 0))],
        out_specs=pl.BlockSpec((1, 1, C, F), lambda e, s, k: (s, e, 0, 0)),
        out_shape=jax.ShapeDtypeStruct((S, E, C, F), BF16),
        scratch_shapes=[pltpu.VMEM((C, F), F32), pltpu.VMEM((C, F), F32)],
        compiler_params=pltpu.CompilerParams(dimension_semantics=("parallel", "parallel", "arbitrary"),
                                             vmem_limit_bytes=VMEM_LIMIT),
        name="expert_hidden",
    )(xs, wg, wu)
    return pl.pallas_call(
        _expert_down_kernel,
        grid=(E, S, D // FF_NT),
        in_specs=[pl.BlockSpec((1, 1, C, F), lambda e, s, n: (s, e, 0, 0)),
                  pl.BlockSpec((1, 1, C, 1), lambda e, s, n: (s, e, 0, 0)),
                  pl.BlockSpec((1, F, FF_NT), lambda e, s, n: (e, 0, n))],
        out_specs=pl.BlockSpec((1, 1, C, FF_NT), lambda e, s, n: (s, e, 0, n)),
        out_shape=jax.ShapeDtypeStruct((S, E, C, D), F32),
        compiler_params=pltpu.CompilerParams(dimension_semantics=("parallel", "parallel", "parallel"),
                                             vmem_limit_bytes=VMEM_LIMIT),
        name="expert_down",
    )(hid, gate, wd)


HALO = 8


def _halo_specs(tm, w, col_block, nrows):
    nb8 = nrows // HALO
    prev = pl.BlockSpec((HALO, w), lambda i: (jnp.maximum(i * (tm // HALO) - 1, 0), col_block))
    nxt = pl.BlockSpec((HALO, w), lambda i: (jnp.minimum((i + 1) * (tm // HALO), nb8 - 1), col_block))
    return prev, nxt


def _neighbours(cur, prev8, next8, seq_len):
    tm = cur.shape[0]
    row = lax.broadcasted_iota(jnp.int32, cur.shape, 0)
    g = row + pl.program_id(0) * tm
    before = jnp.where(row == 0, prev8[HALO - 1:HALO, :], pltpu.roll(cur, 1, 0))
    after = jnp.where(row == tm - 1, next8[0:1, :], pltpu.roll(cur, tm - 1, 0))
    pos = g & (seq_len - 1)
    return jnp.where(pos == 0, 0.0, before), jnp.where(pos == seq_len - 1, 0.0, after)


RW_PIECES = ('rw_r', 'rw_k', 'rw_v', 'rw_wd', 'rw_ad', 'rw_gd')


def _head_sum_matrix(scale=1.0):
    r = np.arange(RW_WIDTH)
    return jnp.asarray(((r[:, None] // RW_HEAD_DIM) == (r[None, :] // RW_HEAD_DIM)).astype(np.float32) * scale, BF16)


def _seg_sum(x, m):
    xh = x.astype(BF16)
    xl = (x - xh.astype(F32)).astype(BF16)
    return jnp.dot(xh, m, preferred_element_type=F32) + jnp.dot(xl, m, preferred_element_type=F32)


def _rwkv_prep_kernel(*refs, seq_len):
    cur = refs[0:6]
    prv = refs[6:12]
    nxt = refs[12:18]
    mu = refs[18:24]
    (w0_ref, a0_ref, wup_ref, aup_ref, gup_ref, kkw_ref, ka_ref, rk_ref, hs_ref) = refs[24:33]
    (r_o, v_o, kk_o, lw0_o, lw1_o, kd0_o, kd1_o, bd0_o, bd1_o, g_o, bonus_o) = refs[33:]
    us = []
    for c, p, n, m in zip(cur, prv, nxt, mu):
        x = c[...].astype(F32)
        before, after = _neighbours(x, p[...].astype(F32), n[...].astype(F32), seq_len)
        us.append(x + m[0:1, :] * (before - x) + m[1:2, :] * (after - x))
    r, k, v, wd, ad, gd = us
    hs = hs_ref[...]
    g_o[...] = jnp.dot(jax.nn.sigmoid(gd).astype(BF16), gup_ref[...], preferred_element_type=F32).astype(g_o.dtype)
    kk = k * kkw_ref[...]
    kk = kk / jnp.maximum(jnp.sqrt(_seg_sum(kk * kk, hs)), 1e-12)
    wl = jnp.dot(jnp.tanh(wd).astype(BF16), wup_ref[...], preferred_element_type=F32) + w0_ref[...]
    al = jnp.dot(ad.astype(BF16), aup_ref[...], preferred_element_type=F32) + a0_ref[...]
    r_o[...] = r.astype(r_o.dtype)
    v_o[...] = v.astype(v_o.dtype)
    kk_o[...] = kk.astype(kk_o.dtype)
    rk = rk_ref[...]
    acc = None
    for d, (lw_o, kd_o, bd_o) in enumerate(((lw0_o, kd0_o, bd0_o), (lw1_o, kd1_o, bd1_o))):
        sl = slice(d * RW_WIDTH, (d + 1) * RW_WIDTH)
        w_log = -jax.nn.softplus(-wl[:, sl]) - 0.5
        lw_o[...] = -jnp.exp(w_log)
        iclr = jax.nn.sigmoid(al[:, sl])
        kd = k * (1.0 + (iclr - 1.0) * ka_ref[...])
        kd_o[...] = kd.astype(kd_o.dtype)
        bd_o[...] = (kk * iclr).astype(bd_o.dtype)
        t = r * kd * rk
        acc = t if acc is None else acc + t
    bonus_o[...] = (_seg_sum(acc, hs) * v).astype(bonus_o.dtype)


def build_rwkv_weights(p):
    W = RW_WIDTH
    mu = p['rw_mu']
    cuts = np.cumsum([0, W, W, W, 128, 128, RW_GATE_LORA]).tolist()
    mus = [mu[:, cuts[i]:cuts[i + 1]] for i in range(6)]
    mus[5] = jnp.pad(mus[5], ((0, 0), (0, 256 - RW_GATE_LORA)))
    z = jnp.zeros((RW_DECAY_LORA, W), F32)
    wup = jnp.concatenate([jnp.concatenate([p['rw_w_up'][0], z], 1), jnp.concatenate([z, p['rw_w_up'][1]], 1)], 0)
    aup = jnp.concatenate([jnp.concatenate([p['rw_a_up'][0], z], 1), jnp.concatenate([z, p['rw_a_up'][1]], 1)], 0)
    gup = jnp.pad(p['rw_g_up'], ((0, 256 - RW_GATE_LORA), (0, 0)))
    return dict(mu=mus, w0=p['rw_w0'].reshape(1, 2 * W), a0=p['rw_a0'].reshape(1, 2 * W), wup=wup.astype(BF16),
                aup=aup.astype(BF16), gup=gup.astype(BF16), kkw=p['rw_kk'].reshape(1, W), ka=p['rw_ka'].reshape(1, W),
                rk=p['rw_rk'].reshape(1, W), hs=_head_sum_matrix(), hmean=_head_sum_matrix(1.0 / RW_HEAD_DIM),
                ln_g=p['rw_ln_g'].reshape(1, W), ln_b=p['rw_ln_b'].reshape(1, W))


def rwkv_prep_pallas(proj, wts, seq_len):
    R = proj.shape[0]
    tm = min(TM, R)
    W = RW_WIDTH
    cur, prv, nxt = [], [], []
    for name in RW_PIECES:
        off, w = PROJ[name]
        cur.append(pl.BlockSpec((tm, w), lambda i, o=off // w: (i, o)))
        a, b = _halo_specs(tm, w, off // w, R)
        prv.append(a)
        nxt.append(b)
    full = lambda a: pl.BlockSpec(a.shape, lambda i: (0,) * a.ndim)
    row = pl.BlockSpec((tm, W), lambda i: (i, 0))
    consts = [wts[k] for k in ('w0', 'a0', 'wup', 'aup', 'gup', 'kkw', 'ka', 'rk', 'hs')]
    bf = jax.ShapeDtypeStruct((R, W), BF16)
    f32 = jax.ShapeDtypeStruct((R, W), F32)
    return pl.pallas_call(
        functools.partial(_rwkv_prep_kernel, seq_len=seq_len),
        grid=(R // tm,),
        in_specs=cur + prv + nxt + [full(m) for m in wts['mu']] + [full(c) for c in consts],
        out_specs=[row] * 11,
        out_shape=[bf, bf, bf, f32, f32, bf, bf, bf, bf, bf, bf],
        compiler_params=pltpu.CompilerParams(dimension_semantics=("parallel",), vmem_limit_bytes=VMEM_LIMIT),
        name="rwkv_prep",
    )(*([proj] * 18), *wts['mu'], *consts)


def _rwkv_out_kernel(yf_ref, yb_ref, bonus_ref, g_ref, hm_ref, lng_ref, lnb_ref, o_ref):
    y = yf_ref[...] + yb_ref[...]
    hm = hm_ref[...]
    yc = y - _seg_sum(y, hm)
    yn = yc * lax.rsqrt(_seg_sum(yc * yc, hm) + RW_GN_EPS) * lng_ref[...] + lnb_ref[...]
    o_ref[...] = ((yn + bonus_ref[...].astype(F32)) * g_ref[...].astype(F32)).astype(o_ref.dtype)


def rwkv_out_pallas(yf, yb, bonus, g, wts):
    R, W = yf.shape
    tm = min(TM, R)
    row = pl.BlockSpec((tm, W), lambda i: (i, 0))
    full = lambda a: pl.BlockSpec(a.shape, lambda i: (0,) * a.ndim)
    return pl.pallas_call(
        _rwkv_out_kernel,
        grid=(R // tm,),
        in_specs=[row] * 4 + [full(wts['hmean']), full(wts['ln_g']), full(wts['ln_b'])],
        out_specs=row,
        out_shape=jax.ShapeDtypeStruct((R, W), BF16),
        compiler_params=pltpu.CompilerParams(dimension_semantics=("parallel",), vmem_limit_bytes=VMEM_LIMIT),
        name="rwkv_out",
    )(yf, yb, bonus, g, wts['hmean'], wts['ln_g'], wts['ln_b'])


SSD_T = SSM_CHUNK
SSD_GW = SSM_INNER // SSM_GROUPS
SSD_HG = SSM_HEADS // SSM_GROUPS
SSM_PIECES = ('ssm_x', 'ssm_b', 'ssm_c')


def _split_dot(x, m):
    xh = x.astype(BF16)
    xl = (x - xh.astype(F32)).astype(BF16)
    return jnp.dot(xh, m, preferred_element_type=F32) + jnp.dot(xl, m, preferred_element_type=F32)


def _ssd_prep_kernel(*refs, seq_len):
    cur, prv, nxt = refs[0:3], refs[3:6], refs[6:9]
    cw, cb = refs[9:12], refs[12:15]
    dt_ref, dtb_ref, perm_ref = refs[15:18]
    x_o, b_o, c_o, dt_o = refs[18:]
    for c, p, n, w, bias, o in zip(cur, prv, nxt, cw, cb, (x_o, b_o, c_o)):
        x = c[...].astype(F32)
        before, after = _neighbours(x, p[...].astype(F32), n[...].astype(F32), seq_len)
        y = before * w[0:1, :] + x * w[1:2, :] + after * w[2:3, :] + bias[...]
        o[...] = jax.nn.silu(y).astype(o.dtype)
    dt = jax.nn.softplus(dt_ref[...].astype(F32) + dtb_ref[...])
    dt_o[...] = _split_dot(dt, perm_ref[...])


def build_ssd_weights(p):
    w, bias = p['ssm_conv_w'], p['ssm_conv_b']
    cuts = [0, SSM_INNER, SSM_INNER + 256, SSM_INNER + 512]
    cw = [w[:, cuts[i]:cuts[i + 1]] for i in range(3)]
    cb = [bias[cuts[i]:cuts[i + 1]].reshape(1, -1) for i in range(3)]
    dtb = jnp.pad(p['ssm_dt_bias'].reshape(1, 2 * SSM_HEADS), ((0, 0), (0, 128 - 2 * SSM_HEADS)))
    perm = np.zeros((128, SSM_GROUPS * 128), np.float32)
    for d in range(2):
        for g in range(SSM_GROUPS):
            for j in range(SSD_HG):
                perm[d * SSM_HEADS + g * SSD_HG + j, g * 128 + d * SSD_HG + j] = 1.0
    a = -jnp.exp(p['ssm_a_log'].astype(F32))
    a_g = jnp.stack([jnp.pad(jnp.concatenate([a[0, g * SSD_HG:(g + 1) * SSD_HG], a[1, g * SSD_HG:(g + 1) * SSD_HG]]),
                             (0, 128 - 2 * SSD_HG)) for g in range(SSM_GROUPS)]).reshape(SSM_GROUPS, 1, 128)
    d_exp = jnp.repeat(p['ssm_d'], SSM_HEAD_DIM).reshape(1, SSM_INNER)
    return dict(cw=cw, cb=cb, dtb=dtb, perm=jnp.asarray(perm, BF16), a_g=a_g, d_exp=d_exp,
                norm=p['ssm_norm'].reshape(1, SSM_INNER))


def ssd_prep_pallas(proj, wts, seq_len):
    R = proj.shape[0]
    tm = min(TM, R)
    cur, prv, nxt = [], [], []
    for name in SSM_PIECES:
        off, w = PROJ[name]
        cur.append(pl.BlockSpec((tm, w), lambda i, o=off // w: (i, o)))
        a, b = _halo_specs(tm, w, off // w, R)
        prv.append(a)
        nxt.append(b)
    full = lambda a: pl.BlockSpec(a.shape, lambda i: (0,) * a.ndim)
    off, w = PROJ['ssm_dt']
    consts = wts['cw'] + wts['cb']
    row = lambda width: pl.BlockSpec((tm, width), lambda i: (i, 0))
    return pl.pallas_call(
        functools.partial(_ssd_prep_kernel, seq_len=seq_len),
        grid=(R // tm,),
        in_specs=cur + prv + nxt + [full(c) for c in consts]
        + [pl.BlockSpec((tm, w), lambda i: (i, off // w)), full(wts['dtb']), full(wts['perm'])],
        out_specs=[row(SSM_INNER), row(256), row(256), row(SSM_GROUPS * 128)],
        out_shape=[jax.ShapeDtypeStruct((R, SSM_INNER), BF16), jax.ShapeDtypeStruct((R, 256), BF16),
                   jax.ShapeDtypeStruct((R, 256), BF16), jax.ShapeDtypeStruct((R, SSM_GROUPS * 128), F32)],
        compiler_params=pltpu.CompilerParams(dimension_semantics=("parallel",), vmem_limit_bytes=VMEM_LIMIT),
        name="ssd_prep",
    )(*([proj] * 9), *consts, proj, wts['dtb'], wts['perm'])


def _ssd_chunks(chains, consts):
    T = SSD_T
    each = lambda f, *ls: [f(*xs) for xs in zip(*ls)]
    eye, cums, tris, expand, lane_head = consts
    S, xs, bm, cm, dt, a, dirs = (list(t) for t in zip(*chains))
    cum = [cums[d] for d in dirs]
    tri = [tris[d] for d in dirs]
    ex = [expand[d] for d in dirs]
    acs = each(lambda m, x, av: jnp.dot(m, x * av, precision=HI, preferred_element_type=F32), cum, dt, a)
    acs_t = each(lambda x: lax.dot_general(eye, x, (((1,), (1,)), ((), ())), precision=HI,
                                           preferred_element_type=F32), acs)
    dt_e = each(_split_dot, dt, ex)
    acs_e = each(_split_dot, acs, ex)
    atot_e = each(lambda x, d: x[0:1, :] if d else x[T - 1:T, :], acs_e, dirs)
    X = each(lambda x, y: x * y, xs, dt_e)
    cb = each(_mm_nt, cm, bm)
    y = []
    for c in range(len(chains)):
        parts = []
        for pr in range(SSD_HG // 2):
            acc = None
            for hh in range(2):
                i = dirs[c] * SSD_HG + 2 * pr + hh
                diff = acs[c][:, i:i + 1] - acs_t[c][i:i + 1, :]
                m = cb[c] * jnp.exp(jnp.where(tri[c], diff, -jnp.inf))
                xh = jnp.where(lane_head == hh, X[c][:, pr * 128:(pr + 1) * 128], 0.0)
                t = _mm(m, xh)
                acc = t if acc is None else acc + t
            parts.append(acc)
        y.append(jnp.concatenate(parts, axis=1))
    states = each(lambda b_, x, at, ac: _mm_tn(b_, x * jnp.exp(at - ac)), bm, X, atot_e, acs_e)
    y_off = each(lambda c_, s, ac: _mm(c_, s) * jnp.exp(ac), cm, S, acs_e)
    S_new = each(lambda s, at, st: s * jnp.exp(at) + st, S, atot_e, states)
    return [(yd + yo, sn) for yd, yo, sn in zip(y, y_off, S_new)]


def _ssd_consts():
    T = SSD_T
    row = lax.broadcasted_iota(jnp.int32, (T, T), 0)
    col = lax.broadcasted_iota(jnp.int32, (T, T), 1)
    eye = (row == col).astype(F32)
    cums = ((row >= col).astype(F32), (row <= col).astype(F32))
    tris = (row >= col, row <= col)
    er = lax.broadcasted_iota(jnp.int32, (128, SSD_GW), 0)
    ec = lax.broadcasted_iota(jnp.int32, (128, SSD_GW), 1)
    expand = tuple((er == d * SSD_HG + ec // SSM_HEAD_DIM).astype(BF16) for d in range(2))
    lane_head = lax.broadcasted_iota(jnp.int32, (T, 128), 1) // SSM_HEAD_DIM
    return eye, cums, tris, expand, lane_head


def _ssd_kernel(xf, bf, cf, dtf, xb, bb, cb_, dtb, a_ref, s0_ref, yf_ref, yb_ref, sT_ref, s_scr, *, nsub):
    j = pl.program_id(2)

    @pl.when(j == 0)
    def _():
        s_scr[...] = s0_ref[:, 0, 0]

    T = SSD_T
    consts = _ssd_consts()
    a = a_ref[0]

    def body(i, carry):
        sf = pl.ds(pl.multiple_of(i * T, T), T)
        sb = pl.ds(pl.multiple_of((nsub - 1 - i) * T, T), T)
        ld = lambda ref, sl: ref[0, sl, :].astype(F32)
        res = _ssd_chunks([(carry[0], ld(xf, sf), ld(bf, sf), ld(cf, sf), ld(dtf, sf), a, 0),
                           (carry[1], ld(xb, sb), ld(bb, sb), ld(cb_, sb), ld(dtb, sb), a, 1)], consts)
        yf_ref[0, sf, :] = res[0][0]
        yb_ref[0, sb, :] = res[1][0]
        return res[0][1], res[1][1]

    fin = lax.fori_loop(0, nsub, body, (s_scr[0], s_scr[1]))
    s_scr[0] = fin[0]
    s_scr[1] = fin[1]

    @pl.when(j == pl.num_programs(2) - 1)
    def _():
        sT_ref[0, 0, 0] = fin[0]
        sT_ref[1, 0, 0] = fin[1]


def ssd_scan_pallas(xs, bm, cm, dt, a_g, s0, *, tb=256):
    B, n, _ = xs.shape
    tb = min(tb, n)
    nb = n // tb
    G = SSM_GROUPS
    fwd = lambda w: pl.BlockSpec((1, tb, w), lambda b, g, j: (b, j, g))
    bwd = lambda w: pl.BlockSpec((1, tb, w), lambda b, g, j: (b, nb - 1 - j, g))
    st = pl.BlockSpec((2, 1, 1, SSM_STATE, SSD_GW), lambda b, g, j: (0, b, g, 0, 0))
    return pl.pallas_call(
        functools.partial(_ssd_kernel, nsub=tb // SSD_T),
        grid=(B, G, nb),
        in_specs=[fwd(SSD_GW), fwd(128), fwd(128), fwd(128), bwd(SSD_GW), bwd(128), bwd(128), bwd(128),
                  pl.BlockSpec((1, 1, 128), lambda b, g, j: (g, 0, 0)), st],
        out_specs=[fwd(SSD_GW), bwd(SSD_GW), st],
        out_shape=[jax.ShapeDtypeStruct((B, n, SSM_INNER), F32), jax.ShapeDtypeStruct((B, n, SSM_INNER), F32),
                   jax.ShapeDtypeStruct(s0.shape, F32)],
        scratch_shapes=[pltpu.VMEM((2, SSM_STATE, SSD_GW), F32)],
        compiler_params=pltpu.CompilerParams(dimension_semantics=("parallel", "parallel", "arbitrary"),
                                             vmem_limit_bytes=VMEM_LIMIT),
        name="ssd_scan",
    )(xs, bm, cm, dt, xs, bm, cm, dt, a_g, s0)


def _ssd_out_kernel(yf_ref, yb_ref, xs_ref, z_ref, d_ref, nw_ref, o_ref):
    y = (yf_ref[...] + yb_ref[...] + xs_ref[...].astype(F32) * d_ref[...]) * jax.nn.silu(z_ref[...].astype(F32))
    outs = []
    for g in range(SSM_GROUPS):
        yg = y[:, g * SSD_GW:(g + 1) * SSD_GW]
        outs.append(yg * lax.rsqrt(jnp.mean(yg * yg, -1, keepdims=True) + 1e-5))
    o_ref[...] = (jnp.concatenate(outs, axis=1) * nw_ref[...]).astype(o_ref.dtype)


def ssd_out_pallas(yf, yb, xs, proj, wts):
    R = yf.shape[0]
    tm = min(TM, R)
    row = pl.BlockSpec((tm, SSM_INNER), lambda i: (i, 0))
    vec = pl.BlockSpec((1, SSM_INNER), lambda i: (0, 0))
    zoff = PROJ['ssm_z'][0] // SSM_INNER
    return pl.pallas_call(
        _ssd_out_kernel,
        grid=(R // tm,),
        in_specs=[row, row, row, pl.BlockSpec((tm, SSM_INNER), lambda i: (i, zoff)), vec, vec],
        out_specs=row,
        out_shape=jax.ShapeDtypeStruct((R, SSM_INNER), BF16),
        compiler_params=pltpu.CompilerParams(dimension_semantics=("parallel",), vmem_limit_bytes=VMEM_LIMIT),
        name="ssd_out",
    )(yf, yb, xs, proj, wts['d_exp'], wts['norm'])


HY_PIECES = ('hy_x0', 'hy_x1', 'hy_v')
FFT_R = 128


def _hilo(x):
    hi = x.astype(BF16)
    return hi, (x - hi.astype(F32)).astype(BF16)


def _dot3(a_hi, a_lo, x):
    xh, xl = _hilo(x)
    d = lambda p, q: jnp.dot(p, q, preferred_element_type=F32)
    return d(a_hi, xh) + d(a_hi, xl) + d(a_lo, xh)


def _hilo_tables(pairs):
    out = {}
    for name, x in pairs:
        hi, lo = _hilo(jnp.asarray(x, F32))
        out[name + '_hi'], out[name + '_lo'] = hi, lo
    return out


def _hyena_prep_kernel(*refs, seq_len):
    cur, prv, nxt = refs[0:3], refs[3:6], refs[6:9]
    cw, cb = refs[9:12], refs[12:15]
    x0_o, v_o = refs[15:]
    ys = []
    for c, p, n, w, bias in zip(cur, prv, nxt, cw, cb):
        x = c[...].astype(F32)
        before, after = _neighbours(x, p[...].astype(F32), n[...].astype(F32), seq_len)
        ys.append(before * w[0:1, :] + x * w[1:2, :] + after * w[2:3, :] + bias[...])
    x0_o[...] = ys[0].astype(x0_o.dtype)
    v_o[...] = ys[2] * ys[1]


def hyena_prep_pallas(proj, cw, cb, seq_len):
    R = proj.shape[0]
    tm = min(TM, R)
    cur, prv, nxt = [], [], []
    for name in HY_PIECES:
        off, w = PROJ[name]
        cur.append(pl.BlockSpec((tm, w), lambda i, o=off // w: (i, o)))
        a, b = _halo_specs(tm, w, off // w, R)
        prv.append(a)
        nxt.append(b)
    full = lambda a: pl.BlockSpec(a.shape, lambda i: (0,) * a.ndim)
    row = pl.BlockSpec((tm, HY_WIDTH), lambda i: (i, 0))
    return pl.pallas_call(
        functools.partial(_hyena_prep_kernel, seq_len=seq_len),
        grid=(R // tm,),
        in_specs=cur + prv + nxt + [full(c) for c in cw + cb],
        out_specs=[row, row],
        out_shape=[jax.ShapeDtypeStruct((R, HY_WIDTH), BF16), jax.ShapeDtypeStruct((R, HY_WIDTH), F32)],
        compiler_params=pltpu.CompilerParams(dimension_semantics=("parallel",), vmem_limit_bytes=VMEM_LIMIT),
        name="hyena_prep",
    )(*([proj] * 9), *cw, *cb)


def filter_features(n):
    t = jnp.linspace(0.0, 1.0, n, dtype=F32)[:, None]
    wpos = 2.0 * math.pi * jnp.arange(n, dtype=F32)[:, None] / n
    f = jnp.linspace(1e-4, HY_BANDS - 1, HY_BANDS, dtype=F32)[None]
    z = jnp.concatenate([t, jnp.cos(f * wpos), -jnp.sin(f * wpos), t, jnp.ones_like(t)], -1)
    back = jnp.concatenate([jnp.zeros((1, z.shape[1]), F32), z[:0:-1]], 0)
    return jnp.pad(jnp.concatenate([z, back], 0), ((0, 0), (0, 128 - z.shape[1])))


def _filter_kernel(z_ref, w1_ref, b1_ref, w2_ref, b2_ref, fr_ref, w3_ref, dl_ref, o_ref):
    hp = lambda a, b: jnp.dot(a, b, precision=HI, preferred_element_type=F32)
    z = z_ref[...]
    fr = fr_ref[...]
    h = jnp.sin(fr * (hp(z, w1_ref[...]) + b1_ref[...]))
    h = jnp.sin(fr * (hp(h, w2_ref[...]) + b2_ref[...]))
    filt = hp(h, w3_ref[0])
    t = z[:, HY_EMB:HY_EMB + 1]
    o_ref[...] = filt * jnp.exp(-t * dl_ref[...]) * z[:, HY_EMB + 1:HY_EMB + 2]


def hyena_filter_pallas(n, p):
    z = filter_features(n)
    tr = min(512, n)
    nb = 2 * n // tr
    w1 = jnp.pad(p['hy_w1'], ((0, 128 - HY_EMB), (0, 0)))
    w3 = jnp.moveaxis(p['hy_w3'].reshape(HY_FILTER_HIDDEN, 2, HY_WIDTH), 1, 0)
    deltas = jnp.abs(jnp.linspace(HY_MIN_DECAY, HY_MAX_DECAY, HY_WIDTH, dtype=F32)).reshape(1, HY_WIDTH)
    vec = lambda a: a.reshape(1, -1)
    full = lambda a: pl.BlockSpec(a.shape, lambda i: (0,) * a.ndim)
    args = (w1, vec(p['hy_b1']), p['hy_w2'], vec(p['hy_b2']), vec(p['hy_freq']))
    return pl.pallas_call(
        _filter_kernel,
        grid=(nb,),
        in_specs=[pl.BlockSpec((tr, 128), lambda i: (i, 0))] + [full(a) for a in args]
        + [pl.BlockSpec((1, HY_FILTER_HIDDEN, HY_WIDTH), lambda i: (i // (nb // 2), 0, 0)), full(deltas)],
        out_specs=pl.BlockSpec((tr, HY_WIDTH), lambda i: (i, 0)),
        out_shape=jax.ShapeDtypeStruct((2 * n, HY_WIDTH), F32),
        compiler_params=pltpu.CompilerParams(dimension_semantics=("parallel",), vmem_limit_bytes=VMEM_LIMIT),
        name="hyena_filter",
    )(z, *args, w3, deltas)


FFT_TN = 4096
FFT_KB = 8
FFT_TC = 256


def dft_tables():
    r = np.arange(FFT_R, dtype=np.float64)
    th1 = 2.0 * np.pi * np.outer(r, r) / FFT_R
    k = r[:, None, None] + FFT_R * r[None, :, None]
    phi = 2.0 * np.pi * k * r[None, None, :] / (FFT_R * FFT_R)
    return _hilo_tables((('c1', np.cos(th1)), ('s1', np.sin(th1)), ('gc', np.cos(phi)), ('gs', np.sin(phi)),
                         ('gct', np.swapaxes(np.cos(phi), 1, 2)), ('gst', np.swapaxes(np.sin(phi), 1, 2))))


def _dft_l1_kernel(x_ref, ch, cl, sh, sl, ar_ref, ai_ref):
    x = x_ref[0]
    ar_ref[0] = _dot3(ch[...], cl[...], x)
    ai_ref[0] = -_dot3(sh[...], sl[...], x)


def dft_level1(x2, tabs):
    B, nb, L = x2.shape
    cut = lambda a: a[:, :nb]
    consts = [cut(tabs['c1_hi']), cut(tabs['c1_lo']), cut(tabs['s1_hi']), cut(tabs['s1_lo'])]
    full = lambda a: pl.BlockSpec(a.shape, lambda b, j: (0,) * a.ndim)
    out = pl.BlockSpec((1, FFT_R, FFT_TN), lambda b, j: (b, 0, j))
    return pl.pallas_call(
        _dft_l1_kernel,
        grid=(B, L // FFT_TN),
        in_specs=[pl.BlockSpec((1, nb, FFT_TN), lambda b, j: (b, 0, j))] + [full(c) for c in consts],
        out_specs=[out, out],
        out_shape=[jax.ShapeDtypeStruct((B, FFT_R, L), F32)] * 2,
        compiler_params=pltpu.CompilerParams(dimension_semantics=("parallel", "parallel"),
                                             vmem_limit_bytes=VMEM_LIMIT),
        name="dft_level1",
    )(x2, *consts)


def _level2_fwd(ar, ai, gch, gcl, gsh, gsl):
    xr = _dot3(gch, gcl, ar) + _dot3(gsh, gsl, ai)
    xi = _dot3(gch, gcl, ai) - _dot3(gsh, gsl, ar)
    return xr, xi


def _spectrum_kernel(ar_ref, ai_ref, gch, gcl, gsh, gsl, hr_ref, hi_ref):
    for q in range(FFT_KB):
        xr, xi = _level2_fwd(ar_ref[0, q], ai_ref[0, q], gch[q], gcl[q], gsh[q], gsl[q])
        hr_ref[0, q] = xr
        hi_ref[0, q] = xi


def _conv_l2_kernel(ar_ref, ai_ref, hr_ref, hi_ref, gc, gs, tc, ts, br_ref, bi_ref):
    d1 = lambda a, x: jnp.dot(a, x.astype(BF16), preferred_element_type=F32)
    for q in range(FFT_KB):
        ar, ai = ar_ref[0, q], ai_ref[0, q]
        xr = d1(gc[q], ar) + d1(gs[q], ai)
        xi = d1(gc[q], ai) - d1(gs[q], ar)
        hr, hi = hr_ref[0, q], hi_ref[0, q]
        yr = xr * hr - xi * hi
        yi = xr * hi + xi * hr
        br_ref[0, q] = d1(tc[q], yr) - d1(ts[q], yi)
        bi_ref[0, q] = d1(tc[q], yi) + d1(ts[q], yr)


def _l2_specs():
    data = pl.BlockSpec((1, FFT_KB, FFT_R, FFT_TC), lambda b, c, k: (b, k, 0, c))
    filt = pl.BlockSpec((1, FFT_KB, FFT_R, FFT_TC), lambda b, c, k: (0, k, 0, c))
    tab = pl.BlockSpec((FFT_KB, FFT_R, FFT_R), lambda b, c, k: (k, 0, 0))
    return data, filt, tab


def dft_spectrum(ar, ai, tabs):
    B, _, _, C = ar.shape
    data, _, tab = _l2_specs()
    return pl.pallas_call(
        _spectrum_kernel,
        grid=(B, C // FFT_TC, FFT_R // FFT_KB),
        in_specs=[data, data] + [tab] * 4,
        out_specs=[data, data],
        out_shape=[jax.ShapeDtypeStruct(ar.shape, F32)] * 2,
        compiler_params=pltpu.CompilerParams(dimension_semantics=("parallel", "parallel", "parallel"),
                                             vmem_limit_bytes=VMEM_LIMIT),
        name="dft_spectrum",
    )(ar, ai, tabs['gc_hi'], tabs['gc_lo'], tabs['gs_hi'], tabs['gs_lo'])


def conv_level2(ar, ai, hr, hi, tabs):
    B, _, _, C = ar.shape
    data, filt, tab = _l2_specs()
    names = ('gc_hi', 'gs_hi', 'gct_hi', 'gst_hi')
    return pl.pallas_call(
        _conv_l2_kernel,
        grid=(B, C // FFT_TC, FFT_R // FFT_KB),
        in_specs=[data, data, filt, filt] + [tab] * 4,
        out_specs=[data, data],
        out_shape=[jax.ShapeDtypeStruct(ar.shape, F32)] * 2,
        compiler_params=pltpu.CompilerParams(dimension_semantics=("parallel", "parallel", "parallel"),
                                             vmem_limit_bytes=VMEM_LIMIT),
        name="conv_level2",
    )(ar, ai, hr, hi, *[tabs[k] for k in names])


def _idft_l1_kernel(br_ref, bi_ref, v_ref, x0_ref, d_ref, ch, cl, sh, sl, o_ref, *, inv_n):
    y = (_dot3(ch[...], cl[...], br_ref[0]) - _dot3(sh[...], sl[...], bi_ref[0])) * inv_n
    o_ref[0] = ((y + v_ref[0] * d_ref[...]) * x0_ref[0].astype(F32)).astype(o_ref.dtype)


def idft_level1_epilogue(br, bi, v2, x02, d_t, tabs):
    B, nb, L = v2.shape
    cut = lambda a: a[:nb, :]
    consts = [cut(tabs['c1_hi']), cut(tabs['c1_lo']), cut(tabs['s1_hi']), cut(tabs['s1_lo'])]
    full = lambda a: pl.BlockSpec(a.shape, lambda b, j: (0,) * a.ndim)
    big = pl.BlockSpec((1, FFT_R, FFT_TN), lambda b, j: (b, 0, j))
    half = pl.BlockSpec((1, nb, FFT_TN), lambda b, j: (b, 0, j))
    return pl.pallas_call(
        functools.partial(_idft_l1_kernel, inv_n=1.0 / (FFT_R * FFT_R)),
        grid=(B, L // FFT_TN),
        in_specs=[big, big, half, half, pl.BlockSpec((1, FFT_TN), lambda b, j: (0, j))] + [full(c) for c in consts],
        out_specs=half,
        out_shape=jax.ShapeDtypeStruct((B, nb, L), BF16),
        compiler_params=pltpu.CompilerParams(dimension_semantics=("parallel", "parallel"),
                                             vmem_limit_bytes=VMEM_LIMIT),
        name="idft_level1",
    )(br, bi, v2, x02, d_t, *consts)


def hyena_long_conv_fft(v, x0, kfull, hy_d, tabs):
    B, n, C = v.shape
    nb = n // FFT_R
    L = FFT_R * C
    kr, ki = dft_level1(kfull.reshape(1, 2 * nb, L), tabs)
    hr, hi = dft_spectrum(kr.reshape(1, FFT_R, FFT_R, C), ki.reshape(1, FFT_R, FFT_R, C), tabs)
    ar, ai = dft_level1(v.reshape(B, nb, L), tabs)
    br, bi = conv_level2(ar.reshape(B, FFT_R, FFT_R, C), ai.reshape(B, FFT_R, FFT_R, C), hr, hi, tabs)
    d_t = jnp.tile(hy_d.reshape(1, C), (1, FFT_R))
    out = idft_level1_epilogue(br.reshape(B, FFT_R, L), bi.reshape(B, FFT_R, L), v.reshape(B, nb, L),
                               x0.reshape(B, nb, L), d_t, tabs)
    return out.reshape(B, n, C)


def dense_tables(n):
    t = np.arange(2 * n, dtype=np.float64)
    th = 2.0 * np.pi * np.outer(t, t) / (2 * n)
    return _hilo_tables((('c', np.cos(th)), ('s', np.sin(th))))


def _dense_conv_kernel(v_ref, x0_ref, k_ref, d_ref, ch, cl, sh, sl, o_ref, *, n):
    c_hi, c_lo, s_hi, s_lo = ch[...], cl[...], sh[...], sl[...]
    k = k_ref[...]
    hr = _dot3(c_hi, c_lo, k)
    hi = -_dot3(s_hi, s_lo, k)
    v = v_ref[0]
    xr = _dot3(c_hi[:, :n], c_lo[:, :n], v)
    xi = -_dot3(s_hi[:, :n], s_lo[:, :n], v)
    yr = xr * hr - xi * hi
    yi = xr * hi + xi * hr
    y = (_dot3(c_hi[:n, :], c_lo[:n, :], yr) - _dot3(s_hi[:n, :], s_lo[:n, :], yi)) * (0.5 / n)
    o_ref[0] = ((y + v * d_ref[...]) * x0_ref[0].astype(F32)).astype(o_ref.dtype)


def hyena_long_conv_dense(v, x0, kfull, hy_d, tabs):
    B, n, C = v.shape
    tc = 256
    seq = pl.BlockSpec((1, n, tc), lambda b, c: (b, 0, c))
    full = lambda a: pl.BlockSpec(a.shape, lambda b, c: (0,) * a.ndim)
    consts = [tabs[k] for k in ('c_hi', 'c_lo', 's_hi', 's_lo')]
    return pl.pallas_call(
        functools.partial(_dense_conv_kernel, n=n),
        grid=(B, C // tc),
        in_specs=[seq, seq, pl.BlockSpec((2 * n, tc), lambda b, c: (0, c)), pl.BlockSpec((1, tc), lambda b, c: (0, c))]
        + [full(c) for c in consts],
        out_specs=seq,
        out_shape=jax.ShapeDtypeStruct((B, n, C), BF16),
        compiler_params=pltpu.CompilerParams(dimension_semantics=("parallel", "parallel"),
                                             vmem_limit_bytes=VMEM_LIMIT),
        name="hyena_dense_conv",
    )(v, x0, kfull, hy_d.reshape(1, C), *consts)


def build_hyena_weights(p):
    w, bias = p['hy_conv_w'], p['hy_conv_b']
    cw = [w[:, i * HY_WIDTH:(i + 1) * HY_WIDTH] for i in range(3)]
    cb = [bias[i * HY_WIDTH:(i + 1) * HY_WIDTH].reshape(1, -1) for i in range(3)]
    return cw, cb


def hyena_branch_pallas(proj, p, tabs, b, n):
    x0, v = hyena_prep_pallas(proj, p['hy_w'][0], p['hy_w'][1], n)
    kfull = hyena_filter_pallas(n, p)
    r3 = lambda a: a.reshape(b, n, HY_WIDTH)
    if 2 * n == FFT_R * FFT_R:
        y = hyena_long_conv_fft(r3(v), r3(x0), kfull, p['hy_d'], tabs['fft'])
    else:
        y = hyena_long_conv_dense(r3(v), r3(x0), kfull, p['hy_d'], dense_tables(n))
    return y.reshape(b * n, HY_WIDTH)


ADA_TN = 1024


def _ada_kernel(c_ref, w_ref, b_ref, o_ref):
    o_ref[...] = jnp.dot(jax.nn.silu(c_ref[...]).astype(BF16), w_ref[...].astype(BF16),
                         preferred_element_type=F32) + b_ref[...]


def ada_modulation(cond, w_ada, b_ada):
    m, d = cond.shape
    n = w_ada.shape[1]
    return pl.pallas_call(
        _ada_kernel,
        grid=(n // ADA_TN,),
        in_specs=[pl.BlockSpec((m, d), lambda j: (0, 0)), pl.BlockSpec((d, ADA_TN), lambda j: (0, j)),
                  pl.BlockSpec((1, ADA_TN), lambda j: (0, j))],
        out_specs=pl.BlockSpec((m, ADA_TN), lambda j: (0, j)),
        out_shape=jax.ShapeDtypeStruct((m, n), F32),
        compiler_params=pltpu.CompilerParams(dimension_semantics=("parallel",), vmem_limit_bytes=VMEM_LIMIT),
        name="ada_modulation",
    )(cond, w_ada, b_ada.reshape(1, n))


def expert_choice_ffn(h2, logits, ffn_w, b, n):
    cap = EC_CAPACITY * n // N_EXPERTS
    aff = jax.nn.softmax(logits[:, :N_EXPERTS].reshape(b, n, N_EXPERTS), -1)
    gate, idx = lax.top_k(jnp.swapaxes(aff, 1, 2), cap)
    bidx = jnp.arange(b)[:, None, None]
    xs = h2.reshape(b, n, D_MODEL)[bidx, idx]
    ye = expert_ffn(xs, gate[..., None], *ffn_w)
    return jnp.zeros((b, n, D_MODEL), F32).at[bidx, idx].add(ye).reshape(b * n, D_MODEL)


def _proj_piece(proj, name, width=None):
    off, w = PROJ[name]
    return proj[:, off:off + (width or w)].astype(F32)


def trunk_layer(xc, xl, mods, p, tabs, need_ctx, b, n_ctx, n_lat):
    D = D_MODEL
    piece = lambda rows, k: mods[rows, None, k * D:(k + 1) * D]
    lat = dict(x=xl, n=n_lat, rpm=n_lat, rows=slice(0, b), cos=tabs['cos_l'], sin=tabs['sin_l'])
    ctx = dict(x=xc, n=n_ctx, rpm=b * n_ctx, rows=slice(b, b + 1), cos=tabs['cos_c'], sin=tabs['sin_c'])
    for t in (ctx, lat):
        t['proj'] = inproj(t['x'], piece(t['rows'], 0), piece(t['rows'], 1), p['w_in_p'], t['rpm'])
        t['q'], t['k'], t['v'] = mla_proj(t['proj'], t['cos'], t['sin'], p['mla_q_norm'], p['mla_kv_norm'], p['mla_w'])
        t['rw'] = rwkv_prep_pallas(t['proj'], p['rw_w'], t['n'])
    sets = (ctx, lat) if need_ctx else (lat,)

    r3 = lambda a, n: a.reshape(b, n, a.shape[-1])
    kc, vc = r3(ctx['k'], n_ctx), r3(ctx['v'], n_ctx)
    lat['mla'] = flash_attention(r3(lat['q'], n_lat), kc, vc, r3(lat['k'], n_lat), r3(lat['v'], n_lat),
                                 tq=min(n_lat, 1024), tk=min(n_lat, 1024), out_dtype=BF16)
    if need_ctx:
        ctx['mla'] = flash_attention(r3(ctx['q'], n_ctx), kc, vc, tq=n_ctx, out_dtype=BF16)

    s0 = jnp.zeros((2, b, SSM_GROUPS, SSM_STATE, SSD_GW), F32)
    for t in (ctx, lat):
        xs, bm, cm, dt = ssd_prep_pallas(t['proj'], p['ssd_w'], t['n'])
        yf, yb, s0 = ssd_scan_pallas(r3(xs, t['n']), r3(bm, t['n']), r3(cm, t['n']), r3(dt, t['n']),
                                     p['ssd_w']['a_g'], s0)
        if t in sets:
            flat = lambda a: a.reshape(b * t['n'], SSM_INNER)
            t['ssm'] = ssd_out_pallas(flat(yf), flat(yb), xs, t['proj'], p['ssd_w'])

    for t in sets:
        t['hy'] = hyena_branch_pallas(t['proj'], p, tabs, b, t['n'])

    s0 = jnp.zeros((2, b, RW_WIDTH // RW_P, RW_P, RW_P), F32)
    for t in (ctx, lat):
        r_, v_, kk_, lw0, lw1, kd0, kd1, bd0, bd1, g_, bonus = (r3(a, t['n']) for a in t['rw'])
        yf, yb, s0 = rwkv_scan_pallas(r_, v_, kk_, lw0, lw1, kd0, kd1, bd0, bd1, s0)
        if t in sets:
            flat = lambda a: a.reshape(b * t['n'], RW_WIDTH)
            t['rwkv'] = rwkv_out_pallas(flat(yf), flat(yb), flat(bonus), flat(g_), p['rw_w'])

    outs = []
    for t in sets:
        rows = b * t['n']
        branches = [t['mla'].reshape(rows, BRANCH_WIDTH), t['ssm'], t['hy'], t['rwkv']]
        merged = merge_branches_pallas(branches, t['proj'], p['w_branch_b'])
        x1, h2, logits = outproj_postnorm(merged, t['x'], piece(t['rows'], 2), piece(t['rows'], 3), piece(t['rows'], 4),
                                          p['ln1_g'], p['ln1_b'], p['w_out_b'], p['w_router'], t['rpm'])
        moe = expert_choice_ffn(h2, logits, p['ffn_w'], b, t['n'])
        outs.append(postnorm(x1, moe, piece(t['rows'], 5), p['ln2_g'], p['ln2_b'], t['rpm']))
    return (outs[0], outs[1]) if need_ctx else (None, outs[0])


def kernel(x, c, ctx, c_ctx, w_ada, b_ada, w_in, mla_q_norm, mla_w_q_up, mla_kv_norm, mla_w_kv_up,
           ssm_conv_w, ssm_conv_b, ssm_dt_bias, ssm_a_log, ssm_d, ssm_norm,
           hy_conv_w, hy_conv_b, hy_w1, hy_b1, hy_w2, hy_b2, hy_w3, hy_freq, hy_d,
           rw_mu, rw_w0, rw_w_up, rw_a0, rw_a_up, rw_g_up, rw_kk, rw_ka, rw_rk, rw_ln_g, rw_ln_b,
           w_branch, w_out, ln1_g, ln1_b, w_router, w_gate_e, w_up_e, w_down_e, ln2_g, ln2_b):
    stacked = dict(
        w_in=w_in, mla_q_norm=mla_q_norm, mla_w_q_up=mla_w_q_up, mla_kv_norm=mla_kv_norm, mla_w_kv_up=mla_w_kv_up,
        ssm_conv_w=ssm_conv_w, ssm_conv_b=ssm_conv_b, ssm_dt_bias=ssm_dt_bias, ssm_a_log=ssm_a_log,
        ssm_d=ssm_d, ssm_norm=ssm_norm,
        hy_conv_w=hy_conv_w, hy_conv_b=hy_conv_b, hy_w1=hy_w1, hy_b1=hy_b1, hy_w2=hy_w2, hy_b2=hy_b2,
        hy_w3=hy_w3, hy_freq=hy_freq, hy_d=hy_d,
        rw_mu=rw_mu, rw_w0=rw_w0, rw_w_up=rw_w_up, rw_a0=rw_a0, rw_a_up=rw_a_up, rw_g_up=rw_g_up,
        rw_kk=rw_kk, rw_ka=rw_ka, rw_rk=rw_rk, rw_ln_g=rw_ln_g, rw_ln_b=rw_ln_b,
        w_branch=w_branch, w_out=w_out, ln1_g=ln1_g, ln1_b=ln1_b,
        w_router=w_router, w_gate_e=w_gate_e, w_up_e=w_up_e, w_down_e=w_down_e, ln2_g=ln2_g, ln2_b=ln2_b)
    b, n_lat, D = x.shape
    n_ctx = ctx.shape[1]
    cos, sin = rope_tables_128(n_lat)
    ones = jnp.pad(jnp.ones((b * n_ctx, QK_ROPE), F32), ((0, 0), (0, 128 - QK_ROPE)))
    tabs = dict(cos_l=jnp.tile(cos, (b, 1)), sin_l=jnp.tile(sin, (b, 1)), cos_c=ones, sin_c=jnp.zeros_like(ones),
                fft=dft_tables())
    cond = jnp.concatenate([c, c_ctx[None], jnp.zeros((8 - b - 1, D), F32)], 0)
    xc, xl = ctx.reshape(b * n_ctx, D), x.reshape(b * n_lat, D)
    for i in range(DEPTH):
        p = {name: arr[i] for name, arr in stacked.items()}
        p['w_in_p'] = build_w_in(p['w_in'])
        p['mla_w'] = build_mla_weights(p)
        p['rw_w'] = build_rwkv_weights(p)
        p['ssd_w'] = build_ssd_weights(p)
        p['hy_w'] = build_hyena_weights(p)
        p['w_branch_b'] = p['w_branch'].astype(BF16)
        p['w_out_b'] = p['w_out'].astype(BF16)
        p['ffn_w'] = (p['w_gate_e'], p['w_up_e'], p['w_down_e'])
        mods = ada_modulation(cond, w_ada[i], b_ada[i])
        xc, xl = trunk_layer(xc, xl, mods, p, tabs, i < DEPTH - 1, b, n_ctx, n_lat)
    return xl.reshape(b, n_lat, D)
```

```python
import functools
import math

import jax
import jax.numpy as jnp
from jax import lax
import numpy as np
from jax.experimental import pallas as pl
from jax.experimental.pallas import tpu as pltpu

D_MODEL = 2048
DEPTH = 2

GRID_W = 64
N_BRANCH = 4
BRANCH_WIDTH = D_MODEL // 2

MLA_Q_LORA = D_MODEL // 4
MLA_KV_LORA = D_MODEL // 4
QK_NOPE = 128
QK_ROPE = 64
V_HEAD = 128
MLA_HEADS = BRANCH_WIDTH // V_HEAD
ROPE_BASE = 10000.0

SSM_HEAD_DIM = 64
SSM_HEADS = BRANCH_WIDTH // SSM_HEAD_DIM
SSM_INNER = SSM_HEADS * SSM_HEAD_DIM
SSM_GROUPS = 2
SSM_STATE = 128
SSM_CHUNK = 128
SSM_XBC = SSM_INNER + 2 * SSM_GROUPS * SSM_STATE

HY_WIDTH = BRANCH_WIDTH
HY_EMB = 33
HY_BANDS = (HY_EMB - 1) // 2
HY_FILTER_HIDDEN = 64
HY_TARGET = 1e-2
HY_SLOW_FRAC = 1.5
HY_QUICK_FRAC = 0.3
HY_MIN_DECAY = math.log(HY_TARGET) / HY_SLOW_FRAC
HY_MAX_DECAY = math.log(HY_TARGET) / HY_QUICK_FRAC

RW_HEAD_DIM = 64
RW_WIDTH = BRANCH_WIDTH
RW_HEADS = RW_WIDTH // RW_HEAD_DIM
RW_DECAY_LORA = 64
RW_ICLR_LORA = 64
RW_GATE_LORA = 160
RW_GN_EPS = 64e-5

N_EXPERTS = 16
EXPERT_FF = 1408
EC_CAPACITY = 2

ALPHA = (2 * DEPTH) ** 0.25

MLA_COLS = MLA_Q_LORA + MLA_KV_LORA + QK_ROPE
SSM_COLS = SSM_INNER + SSM_XBC + 2 * SSM_HEADS
HY_COLS = 3 * HY_WIDTH
RW_COLS = 3 * RW_WIDTH + 2 * RW_DECAY_LORA + 2 * RW_ICLR_LORA + RW_GATE_LORA

F32 = jnp.float32
BF16 = jnp.bfloat16
HI = lax.Precision.HIGHEST

RW_T = 64
RW_P = 128
RW_PAIRS_PER_STEP = 4
RW_INV_BASE = 8
ATT_DK = 256
ATT_DV = 128
TM = 512
VMEM_LIMIT = 48 * 1024 * 1024


def _mm(a, b):
    return jnp.dot(a.astype(BF16), b.astype(BF16), preferred_element_type=F32)


def _mm_nt(a, b):
    return lax.dot_general(a.astype(BF16), b.astype(BF16), (((1,), (1,)), ((), ())), preferred_element_type=F32)


def _mm_tn(a, b):
    return lax.dot_general(a.astype(BF16), b.astype(BF16), (((0,), (0,)), ((), ())), preferred_element_type=F32)


def _ln(x, eps=1e-6):
    mu = jnp.mean(x, -1, keepdims=True)
    xc = x - mu
    return xc * lax.rsqrt(jnp.mean(xc * xc, -1, keepdims=True) + eps)


def _rwkv_pair_chunks(chains, inv_masks):
    T = RW_T
    each = lambda f, *ls: [f(*xs) for xs in zip(*ls)]
    S = [ch[0] for ch in chains]
    r, lw, k, v, kk, b = ([ch[1][i] for ch in chains] for i in range(6))
    a = [-x for x in kk]
    rev = [ch[2] for ch in chains]
    cum, head_lo, strict_bd, incl_bd, eye_bd = ([ch[3][i] for ch in chains] for i in range(5))

    def stack(x, lo):
        return jnp.concatenate([jnp.where(lo, x, 0.0), jnp.where(lo, 0.0, x)], axis=0)

    c = each(lambda m, x: jnp.dot(m, x, precision=HI, preferred_element_type=F32), cum, lw)
    cT = each(lambda x, rv: x[0:1, :] if rv else x[T - 1:T, :], c, rev)
    e_neg = each(lambda x: jnp.exp(-x), c)
    e_end = each(lambda x, y: jnp.exp(x - y), cT, c)
    a_s = each(lambda x, cc, l, lo: stack(x * jnp.exp(cc - l), lo), a, c, lw, head_lo)
    r_s = each(lambda x, cc, lo: stack(x * jnp.exp(cc), lo), r, c, head_lo)
    b_s = each(lambda x, e, lo: stack(x * e, lo), b, e_neg, head_lo)
    k_s = each(lambda x, e, lo: stack(x * e, lo), k, e_neg, head_lo)
    bh_s = each(lambda x, e, lo: stack(x * e, lo), b, e_end, head_lo)
    kh_s = each(lambda x, e, lo: stack(x * e, lo), k, e_end, head_lo)
    v_s = each(stack, v, head_lo)
    g_T = each(jnp.exp, cT)

    ar = each(lambda x, y: jnp.concatenate([x, y], axis=0), a_s, r_s)
    gbk = each(lambda x, y, z: _mm_nt(x, jnp.concatenate([y, z], axis=0)), ar, b_s, k_s)
    gb = [x[:, :2 * T] for x in gbk]
    gk = [x[:, 2 * T:] for x in gbk]
    n = each(lambda m, g: jnp.where(m, g[:2 * T], 0.0), strict_bd, gb)
    l_ak = each(lambda m, g: jnp.where(m, g[:2 * T], 0.0), strict_bd, gk)
    m_rb = each(lambda m, g: jnp.where(m, g[2 * T:], 0.0), incl_bd, gb)
    m_rk = each(lambda m, g: jnp.where(m, g[2 * T:], 0.0), incl_bd, gk)
    lv = each(_mm, l_ak, v_s)
    n8 = each(lambda x: jnp.where(inv_masks[0], x, 0.0), n)
    p = each(lambda e, x: e + x, eye_bd, n8)
    m = each(_mm, n8, n8)
    p = each(lambda x, y: x + _mm(y, x), p, m)
    m = each(_mm, m, m)
    p = each(lambda x, y: x + _mm(y, x), p, m)
    for merge in inv_masks[1:]:
        n_off = each(lambda x: jnp.where(merge, x, 0.0), n)
        p = each(lambda x, y: x + _mm(_mm(x, y), x), p, n_off)
    wu = each(lambda x, y, z: _mm(x, jnp.concatenate([y, z], axis=1)), p, a_s, lv)
    w = [x[:, :RW_P] for x in wu]
    u0 = [x[:, RW_P:] for x in wu]
    wr = each(lambda x, y, s: _mm_nt(jnp.concatenate([x, y], axis=0), s), w, r_s, S)
    uv = each(lambda x, y, z: jnp.concatenate([x[:2 * T] + y, z], axis=0), wr, u0, v_s)
    y_s = each(lambda x, p1, p2, z: x[2 * T:] + _mm(jnp.concatenate([p1, p2], axis=1), z), wr, m_rb, m_rk, uv)
    y = each(lambda x: x[:T] + x[T:], y_s)
    S_new = each(lambda s, g, z, p1, p2: s * g + _mm_tn(z, jnp.concatenate([p1, p2], axis=0)),
                 S, g_T, uv, bh_s, kh_s)
    return list(zip(y, S_new))


def _rwkv_consts():
    T = RW_T
    row = lax.broadcasted_iota(jnp.int32, (T, T), 0)
    col = lax.broadcasted_iota(jnp.int32, (T, T), 1)
    cum_f = (row >= col).astype(F32)
    cum_b = (row <= col).astype(F32)
    head_lo = lax.broadcasted_iota(jnp.int32, (T, RW_P), 1) < RW_P // 2
    r2 = lax.broadcasted_iota(jnp.int32, (2 * T, 2 * T), 0)
    c2 = lax.broadcasted_iota(jnp.int32, (2 * T, 2 * T), 1)
    same = (r2 >= T) == (c2 >= T)
    eye_bd = (r2 == c2).astype(F32)
    fwd = (cum_f, head_lo, same & (r2 > c2), same & (r2 >= c2), eye_bd)
    bwd = (cum_b, head_lo, same & (r2 < c2), same & (r2 <= c2), eye_bd)
    blk = lambda s: (r2 // s) == (c2 // s)
    inv_masks = [blk(RW_INV_BASE)]
    s = RW_INV_BASE
    while s < T:
        inv_masks.append(blk(2 * s) & jnp.logical_not(blk(s)))
        s *= 2
    return fwd, bwd, inv_masks


def _rwkv_kernel(rf, vf, af, lwf, kf, bf, rb, vb, ab, lwb, kb, bb, s0_ref, yf_ref, yb_ref, sT_ref, s_scr, *, nsub):
    j = pl.program_id(2)

    @pl.when(j == 0)
    def _():
        s_scr[...] = s0_ref[:, 0]

    T = RW_T
    cf, cb, inv_masks = _rwkv_consts()

    def body(i, carry):
        sf = pl.ds(pl.multiple_of(i * T, T), T)
        sb = pl.ds(pl.multiple_of((nsub - 1 - i) * T, T), T)
        lanes = [slice(q * RW_P, (q + 1) * RW_P) for q in range(RW_PAIRS_PER_STEP)]
        ins = []
        for ln in lanes:
            ins.append(tuple(ref[0, sf, ln].astype(F32) for ref in (rf, lwf, kf, vf, af, bf)))
            ins.append(tuple(ref[0, sb, ln].astype(F32) for ref in (rb, lwb, kb, vb, ab, bb)))
        res = _rwkv_pair_chunks([(carry[c], ins[c], bool(c % 2), cb if c % 2 else cf)
                                 for c in range(2 * RW_PAIRS_PER_STEP)], inv_masks)
        for q, ln in enumerate(lanes):
            yf_ref[0, sf, ln] = res[2 * q][0]
            yb_ref[0, sb, ln] = res[2 * q + 1][0]
        return tuple(s for _, s in res)

    init = tuple(s_scr[d, q] for q in range(RW_PAIRS_PER_STEP) for d in range(2))
    fin = lax.fori_loop(0, nsub, body, init)
    for q in range(RW_PAIRS_PER_STEP):
        for d in range(2):
            s_scr[d, q] = fin[2 * q + d]

    @pl.when(j == pl.num_programs(2) - 1)
    def _():
        for q in range(RW_PAIRS_PER_STEP):
            for d in range(2):
                sT_ref[d, 0, q] = fin[2 * q + d]


def rwkv_scan_pallas(r, v, kk, lw0, lw1, kd0, kd1, bd0, bd1, s0, *, tb=256):
    B, n, C = r.shape
    tb = min(tb, n)
    nb = n // tb
    pp = RW_PAIRS_PER_STEP
    assert n % tb == 0 and tb % RW_T == 0 and C % (pp * RW_P) == 0
    fw = pl.BlockSpec((1, tb, pp * RW_P), lambda b, p, j: (b, j, p))
    bw = pl.BlockSpec((1, tb, pp * RW_P), lambda b, p, j: (b, nb - 1 - j, p))
    st = pl.BlockSpec((2, 1, pp, RW_P, RW_P), lambda b, p, j: (0, b, p, 0, 0))
    return pl.pallas_call(
        functools.partial(_rwkv_kernel, nsub=tb // RW_T),
        grid=(B, C // (pp * RW_P), nb),
        in_specs=[fw] * 6 + [bw] * 6 + [st],
        out_specs=[fw, bw, st],
        out_shape=[jax.ShapeDtypeStruct((B, n, C), F32), jax.ShapeDtypeStruct((B, n, C), F32),
                   jax.ShapeDtypeStruct(s0.shape, F32)],
        scratch_shapes=[pltpu.VMEM((2, pp, RW_P, RW_P), F32)],
        compiler_params=pltpu.CompilerParams(dimension_semantics=("parallel", "parallel", "arbitrary")),
        name="rwkv_scan",
    )(r, v, kk, lw0, kd0, bd0, r, v, kk, lw1, kd1, bd1, s0)


FLASH_STREAMS = 2


def _flash_update(q_ref, k, vt, m_scr, acc_scr):
    tq = q_ref.shape[1]
    cols = [slice(i * tq // FLASH_STREAMS, (i + 1) * tq // FLASH_STREAMS) for i in range(FLASH_STREAMS)]
    s = [lax.dot_general(k, q_ref[0, c, :], (((1,), (1,)), ((), ())), preferred_element_type=F32) for c in cols]
    m_prev = [m_scr[:, c] for c in cols]
    m_new = [jnp.maximum(mp, jnp.max(x, axis=0, keepdims=True)) for mp, x in zip(m_prev, s)]
    p = [jnp.exp((x - mn).astype(BF16)) for x, mn in zip(s, m_new)]
    pv = [jnp.dot(vt, x, preferred_element_type=F32) for x in p]
    for c, mp, mn, x in zip(cols, m_prev, m_new, pv):
        acc_scr[:, c] = jnp.exp(mp - mn) * acc_scr[:, c] + x
        m_scr[:, c] = mn


def _flash_kernel(q_ref, kc_ref, vtc_ref, *rest, with_latent):
    if with_latent:
        kl_ref, vtl_ref, o_ref, m_scr, acc_scr = rest
    else:
        o_ref, m_scr, acc_scr = rest
    j = pl.program_id(3)

    @pl.when(j == 0)
    def _():
        m_scr[...] = jnp.full(m_scr.shape, -jnp.inf, F32)
        acc_scr[...] = jnp.zeros(acc_scr.shape, F32)
        _flash_update(q_ref, kc_ref[0], vtc_ref[0], m_scr, acc_scr)

    if with_latent:
        @pl.when(j > 0)
        def _():
            _flash_update(q_ref, kl_ref[0], vtl_ref[0], m_scr, acc_scr)

    @pl.when(j == pl.num_programs(3) - 1)
    def _():
        acc = acc_scr[...]
        o_ref[0] = jnp.transpose(acc[:ATT_DV] / acc[ATT_DV:ATT_DV + 1]).astype(o_ref.dtype)


def transpose_values(v):
    B, nk, hv = v.shape
    H = hv // ATT_DV
    ext = jnp.zeros((B, nk, H, ATT_DV), v.dtype).at[..., 0].set(1)
    va = jnp.concatenate([v.reshape(B, nk, H, ATT_DV), ext], -1)
    return jnp.transpose(va, (0, 2, 3, 1)).reshape(B, H * 2 * ATT_DV, nk)


def flash_attention(q, kc, vc, kl=None, vl=None, *, tq, tk=None, out_dtype=F32):
    B, nq, hq = q.shape
    nc = kc.shape[1]
    H = hq // ATT_DK
    assert nq % tq == 0
    with_latent = kl is not None
    in_specs = [pl.BlockSpec((1, tq, ATT_DK), lambda b, h, i, j: (b, i, h)),
                pl.BlockSpec((1, nc, ATT_DK), lambda b, h, i, j: (b, 0, h)),
                pl.BlockSpec((1, 2 * ATT_DV, nc), lambda b, h, i, j: (b, h, 0))]
    args = [q, kc, transpose_values(vc)]
    steps = 1
    if with_latent:
        assert kl.shape[1] % tk == 0
        steps += kl.shape[1] // tk
        in_specs += [pl.BlockSpec((1, tk, ATT_DK), lambda b, h, i, j: (b, jnp.maximum(j - 1, 0), h)),
                     pl.BlockSpec((1, 2 * ATT_DV, tk), lambda b, h, i, j: (b, h, jnp.maximum(j - 1, 0)))]
        args += [kl, transpose_values(vl)]
    return pl.pallas_call(
        functools.partial(_flash_kernel, with_latent=with_latent),
        grid=(B, H, nq // tq, steps),
        in_specs=in_specs,
        out_specs=pl.BlockSpec((1, tq, ATT_DV), lambda b, h, i, j: (b, i, h)),
        out_shape=jax.ShapeDtypeStruct((B, nq, H * ATT_DV), out_dtype),
        scratch_shapes=[pltpu.VMEM((1, tq), F32), pltpu.VMEM((2 * ATT_DV, tq), F32)],
        compiler_params=pltpu.CompilerParams(
            dimension_semantics=("parallel", "parallel", "parallel", "arbitrary")),
        name="mla_flash",
    )(*args)


PROJ = {}
_off = 0
for _name, _w in (('gate', 4 * D_MODEL), ('hy_x0', HY_WIDTH), ('hy_x1', HY_WIDTH), ('hy_v', HY_WIDTH),
                  ('rw_r', RW_WIDTH), ('rw_k', RW_WIDTH), ('rw_v', RW_WIDTH), ('ssm_z', SSM_INNER),
                  ('ssm_x', SSM_INNER), ('mla_cq', MLA_Q_LORA), ('mla_ckv', MLA_KV_LORA),
                  ('ssm_b', 256), ('ssm_c', 256), ('rw_gd', 256), ('rw_wd', 128), ('rw_ad', 128),
                  ('mla_kpe', 128), ('mla_kpe_sw', 128), ('ssm_dt', 128)):
    assert _off % _w == 0, (_name, _off, _w)
    PROJ[_name] = (_off, _w)
    _off += _w
PROJ_TN = 512
PROJ_N = -(-_off // PROJ_TN) * PROJ_TN


def _rope_swap_matrix():
    p = np.zeros((QK_ROPE, QK_ROPE), np.float32)
    q = QK_ROPE // 4
    for g in range(2):
        for i in range(q):
            p[g * 2 * q + q + i, g * 2 * q + i] = -1.0
            p[g * 2 * q + i, g * 2 * q + q + i] = 1.0
    return jnp.asarray(p)


def build_w_in(w_in):
    cuts = np.cumsum([0, MLA_COLS, SSM_COLS, HY_COLS, RW_COLS, 4 * D_MODEL]).tolist()
    mla, ssm, hy, rw, gate = (w_in[:, cuts[i]:cuts[i + 1]] for i in range(5))
    kpe = mla[:, MLA_Q_LORA + MLA_KV_LORA:]
    src = {
        'gate': gate, 'hy_x0': hy[:, :HY_WIDTH], 'hy_x1': hy[:, HY_WIDTH:2 * HY_WIDTH], 'hy_v': hy[:, 2 * HY_WIDTH:],
        'rw_r': rw[:, :RW_WIDTH], 'rw_k': rw[:, RW_WIDTH:2 * RW_WIDTH], 'rw_v': rw[:, 2 * RW_WIDTH:3 * RW_WIDTH],
        'rw_wd': rw[:, 3 * RW_WIDTH:3 * RW_WIDTH + 128], 'rw_ad': rw[:, 3 * RW_WIDTH + 128:3 * RW_WIDTH + 256],
        'rw_gd': rw[:, 3 * RW_WIDTH + 256:],
        'ssm_z': ssm[:, :SSM_INNER], 'ssm_x': ssm[:, SSM_INNER:2 * SSM_INNER],
        'ssm_b': ssm[:, 2 * SSM_INNER:2 * SSM_INNER + 256], 'ssm_c': ssm[:, 2 * SSM_INNER + 256:2 * SSM_INNER + 512],
        'ssm_dt': ssm[:, 2 * SSM_INNER + 512:],
        'mla_cq': mla[:, :MLA_Q_LORA], 'mla_ckv': mla[:, MLA_Q_LORA:MLA_Q_LORA + MLA_KV_LORA],
        'mla_kpe': kpe, 'mla_kpe_sw': kpe @ _rope_swap_matrix(),
    }
    cols, pos = [], 0
    for name, (off, w) in PROJ.items():
        assert off == pos
        piece = src[name]
        cols.append(jnp.pad(piece, ((0, 0), (0, w - piece.shape[1]))))
        pos += w
    cols.append(jnp.zeros((w_in.shape[0], PROJ_N - pos), w_in.dtype))
    return jnp.concatenate(cols, 1).astype(BF16)


def _inproj_kernel(x_ref, sh_ref, sc_ref, w_ref, o_ref, h_scr):
    @pl.when(pl.program_id(1) == 0)
    def _():
        h_scr[...] = (_ln(x_ref[...]) * (1.0 + sc_ref[0]) + sh_ref[0]).astype(BF16)

    o_ref[...] = jnp.dot(h_scr[...], w_ref[...], preferred_element_type=F32).astype(o_ref.dtype)


def inproj(x2d, shift, scale, w_p, rows_per_mod):
    R, D = x2d.shape
    tm = min(TM, R)
    bpm = max(rows_per_mod // tm, 1)
    mod = pl.BlockSpec((1, 1, D), lambda i, j: (i // bpm, 0, 0))
    return pl.pallas_call(
        _inproj_kernel,
        grid=(R // tm, PROJ_N // PROJ_TN),
        in_specs=[pl.BlockSpec((tm, D), lambda i, j: (i, 0)), mod, mod,
                  pl.BlockSpec((D, PROJ_TN), lambda i, j: (0, j))],
        out_specs=pl.BlockSpec((tm, PROJ_TN), lambda i, j: (i, j)),
        out_shape=jax.ShapeDtypeStruct((R, PROJ_N), BF16),
        scratch_shapes=[pltpu.VMEM((tm, D), BF16)],
        compiler_params=pltpu.CompilerParams(dimension_semantics=("parallel", "arbitrary"),
                                             vmem_limit_bytes=VMEM_LIMIT),
        name="inproj",
    )(x2d, shift, scale, w_p)


def build_mla_weights(p):
    wq = p['mla_w_q_up'].reshape(MLA_Q_LORA, MLA_HEADS, QK_NOPE + QK_ROPE)
    wkv = p['mla_w_kv_up'].reshape(MLA_KV_LORA, MLA_HEADS, QK_NOPE + V_HEAD)
    wq_p = jnp.pad(wq, ((0, 0), (0, 0), (0, ATT_DK - QK_NOPE - QK_ROPE))).reshape(MLA_Q_LORA, MLA_HEADS * ATT_DK)
    wsw = jnp.einsum('chd,de->che', wq[:, :, QK_NOPE:], _rope_swap_matrix())
    wsw_p = jnp.pad(wsw, ((0, 0), (0, 0), (0, 128 - QK_ROPE))).reshape(MLA_Q_LORA, MLA_HEADS * 128)
    wkn = wkv[:, :, :QK_NOPE].reshape(MLA_KV_LORA, MLA_HEADS * QK_NOPE)
    wv = wkv[:, :, QK_NOPE:].reshape(MLA_KV_LORA, MLA_HEADS * V_HEAD)
    return wq_p.astype(BF16), wsw_p.astype(BF16), wkn.astype(BF16), wv.astype(BF16)


def axial_rope_tables(n_tokens):
    rows = n_tokens // GRID_W
    row = jnp.repeat(jnp.arange(rows), GRID_W).astype(F32)
    col = jnp.tile(jnp.arange(GRID_W), rows).astype(F32)
    half = QK_ROPE // 2
    inv = ROPE_BASE ** (-jnp.arange(0, half, 2, dtype=F32) / half)
    ang = jnp.stack([row[:, None] * inv, col[:, None] * inv], 1)
    return jnp.cos(ang), jnp.sin(ang)


def rope_tables_128(n_tokens):
    cos, sin = axial_rope_tables(n_tokens)
    full = lambda t: jnp.concatenate([t[:, 0], t[:, 0], t[:, 1], t[:, 1]], -1)
    pad = lambda t: jnp.pad(t, ((0, 0), (0, 128 - QK_ROPE)))
    return pad(full(cos)), pad(full(sin))


def _mla_proj_kernel(cq_ref, ckv_ref, kpe_ref, ksw_ref, cos_ref, sin_ref, gq_ref, gkv_ref, wq_ref, wsw_ref, wkn_ref,
                     wv_ref, q_ref, k_ref, v_ref, *, q_scale):
    def rms(x, g):
        return (x * lax.rsqrt(jnp.mean(x * x, -1, keepdims=True) + 1e-6) * g).astype(BF16)

    cqn = rms(cq_ref[...].astype(F32), gq_ref[...])
    ckvn = rms(ckv_ref[...].astype(F32), gkv_ref[...])
    cos = cos_ref[...]
    sin = sin_ref[...]
    q = jnp.dot(cqn, wq_ref[...], preferred_element_type=F32)
    qsw = jnp.dot(cqn, wsw_ref[...], preferred_element_type=F32)
    kn = jnp.dot(ckvn, wkn_ref[...], preferred_element_type=F32)
    v_ref[...] = jnp.dot(ckvn, wv_ref[...], preferred_element_type=F32).astype(v_ref.dtype)
    kpe = (kpe_ref[...].astype(F32) * cos + ksw_ref[...].astype(F32) * sin).astype(k_ref.dtype)
    for h in range(MLA_HEADS):
        lo = h * ATT_DK
        q_ref[:, lo:lo + 128] = (q[:, lo:lo + 128] * q_scale).astype(q_ref.dtype)
        pe = q[:, lo + 128:lo + 256] * cos + qsw[:, h * 128:(h + 1) * 128] * sin
        q_ref[:, lo + 128:lo + 256] = (pe * q_scale).astype(q_ref.dtype)
        k_ref[:, lo:lo + 128] = kn[:, h * 128:(h + 1) * 128].astype(k_ref.dtype)
        k_ref[:, lo + 128:lo + 256] = kpe


def mla_proj(proj, cos, sin, gq, gkv, weights):
    R = proj.shape[0]
    tm = min(TM, R)
    wq, wsw, wkn, wv = weights
    col = lambda name: pl.BlockSpec((tm, PROJ[name][1]), lambda i, o=PROJ[name][0] // PROJ[name][1]: (i, o))
    row = lambda w: pl.BlockSpec((tm, w), lambda i: (i, 0))
    full = lambda a: pl.BlockSpec(a.shape, lambda i: (0,) * a.ndim)
    gq2, gkv2 = gq.reshape(1, -1), gkv.reshape(1, -1)
    return pl.pallas_call(
        functools.partial(_mla_proj_kernel, q_scale=float((QK_NOPE + QK_ROPE) ** -0.5)),
        grid=(R // tm,),
        in_specs=[col('mla_cq'), col('mla_ckv'), col('mla_kpe'), col('mla_kpe_sw'), row(128), row(128),
                  full(gq2), full(gkv2), full(wq), full(wsw), full(wkn), full(wv)],
        out_specs=[row(MLA_HEADS * ATT_DK), row(MLA_HEADS * ATT_DK), row(MLA_HEADS * ATT_DV)],
        out_shape=[jax.ShapeDtypeStruct((R, MLA_HEADS * ATT_DK), BF16),
                   jax.ShapeDtypeStruct((R, MLA_HEADS * ATT_DK), BF16),
                   jax.ShapeDtypeStruct((R, MLA_HEADS * ATT_DV), BF16)],
        compiler_params=pltpu.CompilerParams(dimension_semantics=("parallel",), vmem_limit_bytes=VMEM_LIMIT),
        name="mla_proj",
    )(proj, proj, proj, proj, cos, sin, gq2, gkv2, wq, wsw, wkn, wv)


MERGE_TN = 512


def _merge_kernel(b0, b1, b2, b3, g0, g1, g2, g3, w_ref, o_ref):
    acc = None
    for i, (b, g) in enumerate(((b0, g0), (b1, g1), (b2, g2), (b3, g3))):
        t = jax.nn.sigmoid(g[...].astype(F32)) * jnp.dot(b[...], w_ref[i], preferred_element_type=F32)
        acc = t if acc is None else acc + t
    o_ref[...] = acc.astype(o_ref.dtype)


def merge_branches_pallas(branches, proj, w_branch):
    R = proj.shape[0]
    tm = min(TM, R)
    nj = D_MODEL // MERGE_TN
    g0 = PROJ['gate'][0] // MERGE_TN
    br = pl.BlockSpec((tm, BRANCH_WIDTH), lambda i, j: (i, 0))
    gates = [pl.BlockSpec((tm, MERGE_TN), lambda i, j, k=k: (i, g0 + k * nj + j)) for k in range(N_BRANCH)]
    return pl.pallas_call(
        _merge_kernel,
        grid=(R // tm, nj),
        in_specs=[br] * 4 + gates + [pl.BlockSpec((N_BRANCH, BRANCH_WIDTH, MERGE_TN), lambda i, j: (0, 0, j))],
        out_specs=pl.BlockSpec((tm, MERGE_TN), lambda i, j: (i, j)),
        out_shape=jax.ShapeDtypeStruct((R, D_MODEL), BF16),
        compiler_params=pltpu.CompilerParams(dimension_semantics=("parallel", "arbitrary"),
                                             vmem_limit_bytes=VMEM_LIMIT),
        name="branch_merge",
    )(*branches, proj, proj, proj, proj, w_branch)


ROUTER_PAD = 128


def _mm3_f32(a, b_hi, b_lo):
    ah = a.astype(BF16)
    al = (a - ah.astype(F32)).astype(BF16)
    d = lambda x, y: jnp.dot(x, y, preferred_element_type=F32)
    return d(ah, b_hi) + d(ah, b_lo) + d(al, b_hi)


def _outproj_kernel(m_ref, x_ref, g1_ref, sh2_ref, sc2_ref, lng_ref, lnb_ref, w_ref, wr_hi, wr_lo,
                    x_out, h2_out, logit_out):
    y = jnp.dot(m_ref[...], w_ref[...], preferred_element_type=F32)
    xn = _ln(ALPHA * x_ref[...] + g1_ref[0] * y) * lng_ref[...] + lnb_ref[...]
    x_out[...] = xn
    h2 = _ln(xn) * (1.0 + sc2_ref[0]) + sh2_ref[0]
    h2_out[...] = h2.astype(h2_out.dtype)
    logit_out[...] = _mm3_f32(h2, wr_hi[...], wr_lo[...])


def outproj_postnorm(merged, x2d, g1, sh2, sc2, ln_g, ln_b, w_out, w_router, rows_per_mod):
    R, D = x2d.shape
    tm = min(TM, R)
    bpm = max(rows_per_mod // tm, 1)
    mod = pl.BlockSpec((1, 1, D), lambda i: (i // bpm, 0, 0))
    row = lambda w: pl.BlockSpec((tm, w), lambda i: (i, 0))
    full = lambda a: pl.BlockSpec(a.shape, lambda i: (0,) * a.ndim)
    wr = jnp.pad(w_router, ((0, 0), (0, ROUTER_PAD - w_router.shape[1])))
    wr_hi = wr.astype(BF16)
    wr_lo = (wr - wr_hi.astype(F32)).astype(BF16)
    lg, lb = ln_g.reshape(1, D), ln_b.reshape(1, D)
    return pl.pallas_call(
        _outproj_kernel,
        grid=(R // tm,),
        in_specs=[row(D), row(D), mod, mod, mod, full(lg), full(lb), full(w_out), full(wr_hi), full(wr_lo)],
        out_specs=[row(D), row(D), row(ROUTER_PAD)],
        out_shape=[jax.ShapeDtypeStruct((R, D), F32), jax.ShapeDtypeStruct((R, D), BF16),
                   jax.ShapeDtypeStruct((R, ROUTER_PAD), F32)],
        compiler_params=pltpu.CompilerParams(dimension_semantics=("parallel",), vmem_limit_bytes=VMEM_LIMIT),
        name="outproj_postnorm",
    )(merged, x2d, g1, sh2, sc2, lg, lb, w_out, wr_hi, wr_lo)


def _postnorm_kernel(x_ref, y_ref, g_ref, lng_ref, lnb_ref, o_ref):
    o_ref[...] = _ln(ALPHA * x_ref[...] + g_ref[0] * y_ref[...]) * lng_ref[...] + lnb_ref[...]


def postnorm(x2d, y2d, g, ln_g, ln_b, rows_per_mod):
    R, D = x2d.shape
    tm = min(TM, R)
    bpm = max(rows_per_mod // tm, 1)
    row = pl.BlockSpec((tm, D), lambda i: (i, 0))
    vec = pl.BlockSpec((1, D), lambda i: (0, 0))
    return pl.pallas_call(
        _postnorm_kernel,
        grid=(R // tm,),
        in_specs=[row, row, pl.BlockSpec((1, 1, D), lambda i: (i // bpm, 0, 0)), vec, vec],
        out_specs=row,
        out_shape=jax.ShapeDtypeStruct((R, D), F32),
        compiler_params=pltpu.CompilerParams(dimension_semantics=("parallel",), vmem_limit_bytes=VMEM_LIMIT),
        name="postnorm",
    )(x2d, y2d, g, ln_g.reshape(1, D), ln_b.reshape(1, D))


FF_PAD = 1536
FF_TILE = 512


def _expert_ffn_kernel(x_ref, gate_ref, wg_ref, wu_ref, wd_ref, o_ref):
    f = pl.program_id(2)
    x = x_ref[0, 0]
    hid = (jax.nn.silu(jnp.dot(x, wg_ref[0], preferred_element_type=F32))
           * jnp.dot(x, wu_ref[0], preferred_element_type=F32))
    part = jnp.dot(hid.astype(BF16), wd_ref[0], preferred_element_type=F32)

    @pl.when(f == 0)
    def _():
        o_ref[0, 0] = part

    @pl.when(f > 0)
    def _():
        o_ref[0, 0] += part

    @pl.when(f == pl.num_programs(2) - 1)
    def _():
        o_ref[0, 0] = o_ref[0, 0] * gate_ref[0, 0]


def expert_ffn(xs, gate, wg, wu, wd):
    S, E, C, D = xs.shape
    nf = FF_PAD // FF_TILE
    return pl.pallas_call(
        _expert_ffn_kernel,
        grid=(E, S, nf),
        in_specs=[pl.BlockSpec((1, 1, C, D), lambda e, s, f: (s, e, 0, 0)),
                  pl.BlockSpec((1, 1, C, 1), lambda e, s, f: (s, e, 0, 0)),
                  pl.BlockSpec((1, D, FF_TILE), lambda e, s, f: (e, 0, f)),
                  pl.BlockSpec((1, D, FF_TILE), lambda e, s, f: (e, 0, f)),
                  pl.BlockSpec((1, FF_TILE, D), lambda e, s, f: (e, f, 0))],
        out_specs=pl.BlockSpec((1, 1, C, D), lambda e, s, f: (s, e, 0, 0)),
        out_shape=jax.ShapeDtypeStruct((S, E, C, D), F32),
        compiler_params=pltpu.CompilerParams(dimension_semantics=("parallel", "parallel", "arbitrary"),
                                             vmem_limit_bytes=VMEM_LIMIT),
        name="expert_ffn",
    )(xs, gate, wg, wu, wd)


def build_expert_weights(p):
    padf = lambda w: jnp.pad(w, ((0, 0), (0, 0), (0, FF_PAD - EXPERT_FF))).astype(BF16)
    wd = jnp.pad(p['w_down_e'], ((0, 0), (0, FF_PAD - EXPERT_FF), (0, 0))).astype(BF16)
    return padf(p['w_gate_e']), padf(p['w_up_e']), wd


HALO = 8


def _halo_specs(tm, w, col_block, nrows):
    nb8 = nrows // HALO
    prev = pl.BlockSpec((HALO, w), lambda i: (jnp.maximum(i * (tm // HALO) - 1, 0), col_block))
    nxt = pl.BlockSpec((HALO, w), lambda i: (jnp.minimum((i + 1) * (tm // HALO), nb8 - 1), col_block))
    return prev, nxt


def _neighbours(cur, prev8, next8, seq_len):
    tm = cur.shape[0]
    row = lax.broadcasted_iota(jnp.int32, cur.shape, 0)
    g = row + pl.program_id(0) * tm
    before = jnp.where(row == 0, prev8[HALO - 1:HALO, :], pltpu.roll(cur, 1, 0))
    after = jnp.where(row == tm - 1, next8[0:1, :], pltpu.roll(cur, tm - 1, 0))
    pos = g & (seq_len - 1)
    return jnp.where(pos == 0, 0.0, before), jnp.where(pos == seq_len - 1, 0.0, after)


RW_PIECES = ('rw_r', 'rw_k', 'rw_v', 'rw_wd', 'rw_ad', 'rw_gd')


def _head_sum_matrix(scale=1.0):
    r = np.arange(RW_WIDTH)
    return jnp.asarray(((r[:, None] // RW_HEAD_DIM) == (r[None, :] // RW_HEAD_DIM)).astype(np.float32) * scale, BF16)


def _seg_sum(x, m):
    xh = x.astype(BF16)
    xl = (x - xh.astype(F32)).astype(BF16)
    return jnp.dot(xh, m, preferred_element_type=F32) + jnp.dot(xl, m, preferred_element_type=F32)


def _rwkv_prep_kernel(*refs, seq_len):
    cur = refs[0:6]
    prv = refs[6:12]
    nxt = refs[12:18]
    mu = refs[18:24]
    (w0_ref, a0_ref, wup_ref, aup_ref, gup_ref, kkw_ref, ka_ref, rk_ref, hs_ref) = refs[24:33]
    (r_o, v_o, kk_o, lw0_o, lw1_o, kd0_o, kd1_o, bd0_o, bd1_o, g_o, bonus_o) = refs[33:]
    us = []
    for c, p, n, m in zip(cur, prv, nxt, mu):
        x = c[...].astype(F32)
        before, after = _neighbours(x, p[...].astype(F32), n[...].astype(F32), seq_len)
        us.append(x + m[0:1, :] * (before - x) + m[1:2, :] * (after - x))
    r, k, v, wd, ad, gd = us
    hs = hs_ref[...]
    g_o[...] = jnp.dot(jax.nn.sigmoid(gd).astype(BF16), gup_ref[...], preferred_element_type=F32).astype(g_o.dtype)
    kk = k * kkw_ref[...]
    kk = kk / jnp.maximum(jnp.sqrt(_seg_sum(kk * kk, hs)), 1e-12)
    wl = jnp.dot(jnp.tanh(wd).astype(BF16), wup_ref[...], preferred_element_type=F32) + w0_ref[...]
    al = jnp.dot(ad.astype(BF16), aup_ref[...], preferred_element_type=F32) + a0_ref[...]
    r_o[...] = r.astype(r_o.dtype)
    v_o[...] = v.astype(v_o.dtype)
    kk_o[...] = kk.astype(kk_o.dtype)
    rk = rk_ref[...]
    acc = None
    for d, (lw_o, kd_o, bd_o) in enumerate(((lw0_o, kd0_o, bd0_o), (lw1_o, kd1_o, bd1_o))):
        sl = slice(d * RW_WIDTH, (d + 1) * RW_WIDTH)
        w_log = -jax.nn.softplus(-wl[:, sl]) - 0.5
        lw_o[...] = -jnp.exp(w_log)
        iclr = jax.nn.sigmoid(al[:, sl])
        kd = k * (1.0 + (iclr - 1.0) * ka_ref[...])
        kd_o[...] = kd.astype(kd_o.dtype)
        bd_o[...] = (kk * iclr).astype(bd_o.dtype)
        t = r * kd * rk
        acc = t if acc is None else acc + t
    bonus_o[...] = (_seg_sum(acc, hs) * v).astype(bonus_o.dtype)


def build_rwkv_weights(p):
    W = RW_WIDTH
    mu = p['rw_mu']
    cuts = np.cumsum([0, W, W, W, 128, 128, RW_GATE_LORA]).tolist()
    mus = [mu[:, cuts[i]:cuts[i + 1]] for i in range(6)]
    mus[5] = jnp.pad(mus[5], ((0, 0), (0, 256 - RW_GATE_LORA)))
    z = jnp.zeros((RW_DECAY_LORA, W), F32)
    wup = jnp.concatenate([jnp.concatenate([p['rw_w_up'][0], z], 1), jnp.concatenate([z, p['rw_w_up'][1]], 1)], 0)
    aup = jnp.concatenate([jnp.concatenate([p['rw_a_up'][0], z], 1), jnp.concatenate([z, p['rw_a_up'][1]], 1)], 0)
    gup = jnp.pad(p['rw_g_up'], ((0, 256 - RW_GATE_LORA), (0, 0)))
    return dict(mu=mus, w0=p['rw_w0'].reshape(1, 2 * W), a0=p['rw_a0'].reshape(1, 2 * W), wup=wup.astype(BF16),
                aup=aup.astype(BF16), gup=gup.astype(BF16), kkw=p['rw_kk'].reshape(1, W), ka=p['rw_ka'].reshape(1, W),
                rk=p['rw_rk'].reshape(1, W), hs=_head_sum_matrix(), hmean=_head_sum_matrix(1.0 / RW_HEAD_DIM),
                ln_g=p['rw_ln_g'].reshape(1, W), ln_b=p['rw_ln_b'].reshape(1, W))


def rwkv_prep_pallas(proj, wts, seq_len):
    R = proj.shape[0]
    tm = min(TM, R)
    W = RW_WIDTH
    cur, prv, nxt = [], [], []
    for name in RW_PIECES:
        off, w = PROJ[name]
        cur.append(pl.BlockSpec((tm, w), lambda i, o=off // w: (i, o)))
        a, b = _halo_specs(tm, w, off // w, R)
        prv.append(a)
        nxt.append(b)
    full = lambda a: pl.BlockSpec(a.shape, lambda i: (0,) * a.ndim)
    row = pl.BlockSpec((tm, W), lambda i: (i, 0))
    consts = [wts[k] for k in ('w0', 'a0', 'wup', 'aup', 'gup', 'kkw', 'ka', 'rk', 'hs')]
    bf = jax.ShapeDtypeStruct((R, W), BF16)
    f32 = jax.ShapeDtypeStruct((R, W), F32)
    return pl.pallas_call(
        functools.partial(_rwkv_prep_kernel, seq_len=seq_len),
        grid=(R // tm,),
        in_specs=cur + prv + nxt + [full(m) for m in wts['mu']] + [full(c) for c in consts],
        out_specs=[row] * 11,
        out_shape=[bf, bf, bf, f32, f32, bf, bf, bf, bf, bf, bf],
        compiler_params=pltpu.CompilerParams(dimension_semantics=("parallel",), vmem_limit_bytes=VMEM_LIMIT),
        name="rwkv_prep",
    )(*([proj] * 18), *wts['mu'], *consts)


def _rwkv_out_kernel(yf_ref, yb_ref, bonus_ref, g_ref, hm_ref, lng_ref, lnb_ref, o_ref):
    y = yf_ref[...] + yb_ref[...]
    hm = hm_ref[...]
    yc = y - _seg_sum(y, hm)
    yn = yc * lax.rsqrt(_seg_sum(yc * yc, hm) + RW_GN_EPS) * lng_ref[...] + lnb_ref[...]
    o_ref[...] = ((yn + bonus_ref[...].astype(F32)) * g_ref[...].astype(F32)).astype(o_ref.dtype)


def rwkv_out_pallas(yf, yb, bonus, g, wts):
    R, W = yf.shape
    tm = min(TM, R)
    row = pl.BlockSpec((tm, W), lambda i: (i, 0))
    full = lambda a: pl.BlockSpec(a.shape, lambda i: (0,) * a.ndim)
    return pl.pallas_call(
        _rwkv_out_kernel,
        grid=(R // tm,),
        in_specs=[row] * 4 + [full(wts['hmean']), full(wts['ln_g']), full(wts['ln_b'])],
        out_specs=row,
        out_shape=jax.ShapeDtypeStruct((R, W), BF16),
        compiler_params=pltpu.CompilerParams(dimension_semantics=("parallel",), vmem_limit_bytes=VMEM_LIMIT),
        name="rwkv_out",
    )(yf, yb, bonus, g, wts['hmean'], wts['ln_g'], wts['ln_b'])


SSD_T = SSM_CHUNK
SSD_GW = SSM_INNER // SSM_GROUPS
SSD_HG = SSM_HEADS // SSM_GROUPS
SSM_PIECES = ('ssm_x', 'ssm_b', 'ssm_c')


def _split_dot(x, m):
    xh = x.astype(BF16)
    xl = (x - xh.astype(F32)).astype(BF16)
    return jnp.dot(xh, m, preferred_element_type=F32) + jnp.dot(xl, m, preferred_element_type=F32)


def _ssd_prep_kernel(*refs, seq_len):
    cur, prv, nxt = refs[0:3], refs[3:6], refs[6:9]
    cw, cb = refs[9:12], refs[12:15]
    dt_ref, dtb_ref, perm_ref = refs[15:18]
    x_o, b_o, c_o, dt_o = refs[18:]
    for c, p, n, w, bias, o in zip(cur, prv, nxt, cw, cb, (x_o, b_o, c_o)):
        x = c[...].astype(F32)
        before, after = _neighbours(x, p[...].astype(F32), n[...].astype(F32), seq_len)
        y = before * w[0:1, :] + x * w[1:2, :] + after * w[2:3, :] + bias[...]
        o[...] = jax.nn.silu(y).astype(o.dtype)
    dt = jax.nn.softplus(dt_ref[...].astype(F32) + dtb_ref[...])
    dt_o[...] = _split_dot(dt, perm_ref[...])


def build_ssd_weights(p):
    w, bias = p['ssm_conv_w'], p['ssm_conv_b']
    cuts = [0, SSM_INNER, SSM_INNER + 256, SSM_INNER + 512]
    cw = [w[:, cuts[i]:cuts[i + 1]] for i in range(3)]
    cb = [bias[cuts[i]:cuts[i + 1]].reshape(1, -1) for i in range(3)]
    dtb = jnp.pad(p['ssm_dt_bias'].reshape(1, 2 * SSM_HEADS), ((0, 0), (0, 128 - 2 * SSM_HEADS)))
    perm = np.zeros((128, SSM_GROUPS * 128), np.float32)
    for d in range(2):
        for g in range(SSM_GROUPS):
            for j in range(SSD_HG):
                perm[d * SSM_HEADS + g * SSD_HG + j, g * 128 + d * SSD_HG + j] = 1.0
    a = -jnp.exp(p['ssm_a_log'].astype(F32))
    a_g = jnp.stack([jnp.pad(jnp.concatenate([a[0, g * SSD_HG:(g + 1) * SSD_HG], a[1, g * SSD_HG:(g + 1) * SSD_HG]]),
                             (0, 128 - 2 * SSD_HG)) for g in range(SSM_GROUPS)]).reshape(SSM_GROUPS, 1, 128)
    d_exp = jnp.repeat(p['ssm_d'], SSM_HEAD_DIM).reshape(1, SSM_INNER)
    return dict(cw=cw, cb=cb, dtb=dtb, perm=jnp.asarray(perm, BF16), a_g=a_g, d_exp=d_exp,
                norm=p['ssm_norm'].reshape(1, SSM_INNER))


def ssd_prep_pallas(proj, wts, seq_len):
    R = proj.shape[0]
    tm = min(TM, R)
    cur, prv, nxt = [], [], []
    for name in SSM_PIECES:
        off, w = PROJ[name]
        cur.append(pl.BlockSpec((tm, w), lambda i, o=off // w: (i, o)))
        a, b = _halo_specs(tm, w, off // w, R)
        prv.append(a)
        nxt.append(b)
    full = lambda a: pl.BlockSpec(a.shape, lambda i: (0,) * a.ndim)
    off, w = PROJ['ssm_dt']
    consts = wts['cw'] + wts['cb']
    row = lambda width: pl.BlockSpec((tm, width), lambda i: (i, 0))
    return pl.pallas_call(
        functools.partial(_ssd_prep_kernel, seq_len=seq_len),
        grid=(R // tm,),
        in_specs=cur + prv + nxt + [full(c) for c in consts]
        + [pl.BlockSpec((tm, w), lambda i: (i, off // w)), full(wts['dtb']), full(wts['perm'])],
        out_specs=[row(SSM_INNER), row(256), row(256), row(SSM_GROUPS * 128)],
        out_shape=[jax.ShapeDtypeStruct((R, SSM_INNER), BF16), jax.ShapeDtypeStruct((R, 256), BF16),
                   jax.ShapeDtypeStruct((R, 256), BF16), jax.ShapeDtypeStruct((R, SSM_GROUPS * 128), F32)],
        compiler_params=pltpu.CompilerParams(dimension_semantics=("parallel",), vmem_limit_bytes=VMEM_LIMIT),
        name="ssd_prep",
    )(*([proj] * 9), *consts, proj, wts['dtb'], wts['perm'])


def _ssd_chunks(chains, consts):
    T = SSD_T
    each = lambda f, *ls: [f(*xs) for xs in zip(*ls)]
    eye, cums, tris, expand, lane_head = consts
    S, xs, bm, cm, dt, a, dirs = (list(t) for t in zip(*chains))
    cum = [cums[d] for d in dirs]
    tri = [tris[d] for d in dirs]
    ex = [expand[d] for d in dirs]
    acs = each(lambda m, x, av: jnp.dot(m, x * av, precision=HI, preferred_element_type=F32), cum, dt, a)
    acs_t = each(lambda x: lax.dot_general(eye, x, (((1,), (1,)), ((), ())), precision=HI,
                                           preferred_element_type=F32), acs)
    dt_e = each(_split_dot, dt, ex)
    acs_e = each(_split_dot, acs, ex)
    atot_e = each(lambda x, d: x[0:1, :] if d else x[T - 1:T, :], acs_e, dirs)
    X = each(lambda x, y: x * y, xs, dt_e)
    cb = each(_mm_nt, cm, bm)
    y = []
    for c in range(len(chains)):
        parts = []
        for pr in range(SSD_HG // 2):
            acc = None
            for hh in range(2):
                i = dirs[c] * SSD_HG + 2 * pr + hh
                diff = acs[c][:, i:i + 1] - acs_t[c][i:i + 1, :]
                m = cb[c] * jnp.exp(jnp.where(tri[c], diff, -jnp.inf))
                xh = jnp.where(lane_head == hh, X[c][:, pr * 128:(pr + 1) * 128], 0.0)
                t = _mm(m, xh)
                acc = t if acc is None else acc + t
            parts.append(acc)
        y.append(jnp.concatenate(parts, axis=1))
    states = each(lambda b_, x, at, ac: _mm_tn(b_, x * jnp.exp(at - ac)), bm, X, atot_e, acs_e)
    y_off = each(lambda c_, s, ac: _mm(c_, s) * jnp.exp(ac), cm, S, acs_e)
    S_new = each(lambda s, at, st: s * jnp.exp(at) + st, S, atot_e, states)
    return [(yd + yo, sn) for yd, yo, sn in zip(y, y_off, S_new)]


def _ssd_consts():
    T = SSD_T
    row = lax.broadcasted_iota(jnp.int32, (T, T), 0)
    col = lax.broadcasted_iota(jnp.int32, (T, T), 1)
    eye = (row == col).astype(F32)
    cums = ((row >= col).astype(F32), (row <= col).astype(F32))
    tris = (row >= col, row <= col)
    er = lax.broadcasted_iota(jnp.int32, (128, SSD_GW), 0)
    ec = lax.broadcasted_iota(jnp.int32, (128, SSD_GW), 1)
    expand = tuple((er == d * SSD_HG + ec // SSM_HEAD_DIM).astype(BF16) for d in range(2))
    lane_head = lax.broadcasted_iota(jnp.int32, (T, 128), 1) // SSM_HEAD_DIM
    return eye, cums, tris, expand, lane_head


def _ssd_kernel(xf, bf, cf, dtf, xb, bb, cb_, dtb, a_ref, s0_ref, yf_ref, yb_ref, sT_ref, s_scr, *, nsub):
    j = pl.program_id(2)

    @pl.when(j == 0)
    def _():
        s_scr[...] = s0_ref[:, 0, 0]

    T = SSD_T
    consts = _ssd_consts()
    a = a_ref[0]

    def body(i, carry):
        sf = pl.ds(pl.multiple_of(i * T, T), T)
        sb = pl.ds(pl.multiple_of((nsub - 1 - i) * T, T), T)
        ld = lambda ref, sl: ref[0, sl, :].astype(F32)
        res = _ssd_chunks([(carry[0], ld(xf, sf), ld(bf, sf), ld(cf, sf), ld(dtf, sf), a, 0),
                           (carry[1], ld(xb, sb), ld(bb, sb), ld(cb_, sb), ld(dtb, sb), a, 1)], consts)
        yf_ref[0, sf, :] = res[0][0]
        yb_ref[0, sb, :] = res[1][0]
        return res[0][1], res[1][1]

    fin = lax.fori_loop(0, nsub, body, (s_scr[0], s_scr[1]))
    s_scr[0] = fin[0]
    s_scr[1] = fin[1]

    @pl.when(j == pl.num_programs(2) - 1)
    def _():
        sT_ref[0, 0, 0] = fin[0]
        sT_ref[1, 0, 0] = fin[1]


def ssd_scan_pallas(xs, bm, cm, dt, a_g, s0, *, tb=256):
    B, n, _ = xs.shape
    tb = min(tb, n)
    nb = n // tb
    G = SSM_GROUPS
    fwd = lambda w: pl.BlockSpec((1, tb, w), lambda b, g, j: (b, j, g))
    bwd = lambda w: pl.BlockSpec((1, tb, w), lambda b, g, j: (b, nb - 1 - j, g))
    st = pl.BlockSpec((2, 1, 1, SSM_STATE, SSD_GW), lambda b, g, j: (0, b, g, 0, 0))
    return pl.pallas_call(
        functools.partial(_ssd_kernel, nsub=tb // SSD_T),
        grid=(B, G, nb),
        in_specs=[fwd(SSD_GW), fwd(128), fwd(128), fwd(128), bwd(SSD_GW), bwd(128), bwd(128), bwd(128),
                  pl.BlockSpec((1, 1, 128), lambda b, g, j: (g, 0, 0)), st],
        out_specs=[fwd(SSD_GW), bwd(SSD_GW), st],
        out_shape=[jax.ShapeDtypeStruct((B, n, SSM_INNER), F32), jax.ShapeDtypeStruct((B, n, SSM_INNER), F32),
                   jax.ShapeDtypeStruct(s0.shape, F32)],
        scratch_shapes=[pltpu.VMEM((2, SSM_STATE, SSD_GW), F32)],
        compiler_params=pltpu.CompilerParams(dimension_semantics=("parallel", "parallel", "arbitrary"),
                                             vmem_limit_bytes=VMEM_LIMIT),
        name="ssd_scan",
    )(xs, bm, cm, dt, xs, bm, cm, dt, a_g, s0)


def _ssd_out_kernel(yf_ref, yb_ref, xs_ref, z_ref, d_ref, nw_ref, o_ref):
    y = (yf_ref[...] + yb_ref[...] + xs_ref[...].astype(F32) * d_ref[...]) * jax.nn.silu(z_ref[...].astype(F32))
    outs = []
    for g in range(SSM_GROUPS):
        yg = y[:, g * SSD_GW:(g + 1) * SSD_GW]
        outs.append(yg * lax.rsqrt(jnp.mean(yg * yg, -1, keepdims=True) + 1e-5))
    o_ref[...] = (jnp.concatenate(outs, axis=1) * nw_ref[...]).astype(o_ref.dtype)


def ssd_out_pallas(yf, yb, xs, proj, wts):
    R = yf.shape[0]
    tm = min(TM, R)
    row = pl.BlockSpec((tm, SSM_INNER), lambda i: (i, 0))
    vec = pl.BlockSpec((1, SSM_INNER), lambda i: (0, 0))
    zoff = PROJ['ssm_z'][0] // SSM_INNER
    return pl.pallas_call(
        _ssd_out_kernel,
        grid=(R // tm,),
        in_specs=[row, row, row, pl.BlockSpec((tm, SSM_INNER), lambda i: (i, zoff)), vec, vec],
        out_specs=row,
        out_shape=jax.ShapeDtypeStruct((R, SSM_INNER), BF16),
        compiler_params=pltpu.CompilerParams(dimension_semantics=("parallel",), vmem_limit_bytes=VMEM_LIMIT),
        name="ssd_out",
    )(yf, yb, xs, proj, wts['d_exp'], wts['norm'])


HY_PIECES = ('hy_x0', 'hy_x1', 'hy_v')
FFT_R = 128


def _hilo(x):
    hi = x.astype(BF16)
    return hi, (x - hi.astype(F32)).astype(BF16)


def _dot3(a_hi, a_lo, x):
    xh, xl = _hilo(x)
    d = lambda p, q: jnp.dot(p, q, preferred_element_type=F32)
    return d(a_hi, xh) + d(a_hi, xl) + d(a_lo, xh)


def _hilo_tables(pairs):
    out = {}
    for name, x in pairs:
        hi, lo = _hilo(jnp.asarray(x, F32))
        out[name + '_hi'], out[name + '_lo'] = hi, lo
    return out


def _hyena_prep_kernel(*refs, seq_len):
    cur, prv, nxt = refs[0:3], refs[3:6], refs[6:9]
    cw, cb = refs[9:12], refs[12:15]
    x0_o, v_o = refs[15:]
    ys = []
    for c, p, n, w, bias in zip(cur, prv, nxt, cw, cb):
        x = c[...].astype(F32)
        before, after = _neighbours(x, p[...].astype(F32), n[...].astype(F32), seq_len)
        ys.append(before * w[0:1, :] + x * w[1:2, :] + after * w[2:3, :] + bias[...])
    x0_o[...] = ys[0].astype(x0_o.dtype)
    v_o[...] = ys[2] * ys[1]


def hyena_prep_pallas(proj, cw, cb, seq_len):
    R = proj.shape[0]
    tm = min(TM, R)
    cur, prv, nxt = [], [], []
    for name in HY_PIECES:
        off, w = PROJ[name]
        cur.append(pl.BlockSpec((tm, w), lambda i, o=off // w: (i, o)))
        a, b = _halo_specs(tm, w, off // w, R)
        prv.append(a)
        nxt.append(b)
    full = lambda a: pl.BlockSpec(a.shape, lambda i: (0,) * a.ndim)
    row = pl.BlockSpec((tm, HY_WIDTH), lambda i: (i, 0))
    return pl.pallas_call(
        functools.partial(_hyena_prep_kernel, seq_len=seq_len),
        grid=(R // tm,),
        in_specs=cur + prv + nxt + [full(c) for c in cw + cb],
        out_specs=[row, row],
        out_shape=[jax.ShapeDtypeStruct((R, HY_WIDTH), BF16), jax.ShapeDtypeStruct((R, HY_WIDTH), F32)],
        compiler_params=pltpu.CompilerParams(dimension_semantics=("parallel",), vmem_limit_bytes=VMEM_LIMIT),
        name="hyena_prep",
    )(*([proj] * 9), *cw, *cb)


def filter_features(n):
    t = jnp.linspace(0.0, 1.0, n, dtype=F32)[:, None]
    wpos = 2.0 * math.pi * jnp.arange(n, dtype=F32)[:, None] / n
    f = jnp.linspace(1e-4, HY_BANDS - 1, HY_BANDS, dtype=F32)[None]
    z = jnp.concatenate([t, jnp.cos(f * wpos), -jnp.sin(f * wpos), t, jnp.ones_like(t)], -1)
    back = jnp.concatenate([jnp.zeros((1, z.shape[1]), F32), z[:0:-1]], 0)
    return jnp.pad(jnp.concatenate([z, back], 0), ((0, 0), (0, 128 - z.shape[1])))


def _filter_kernel(z_ref, w1_ref, b1_ref, w2_ref, b2_ref, fr_ref, w3_ref, dl_ref, o_ref):
    hp = lambda a, b: jnp.dot(a, b, precision=HI, preferred_element_type=F32)
    z = z_ref[...]
    fr = fr_ref[...]
    h = jnp.sin(fr * (hp(z, w1_ref[...]) + b1_ref[...]))
    h = jnp.sin(fr * (hp(h, w2_ref[...]) + b2_ref[...]))
    filt = hp(h, w3_ref[0])
    t = z[:, HY_EMB:HY_EMB + 1]
    o_ref[...] = filt * jnp.exp(-t * dl_ref[...]) * z[:, HY_EMB + 1:HY_EMB + 2]


def hyena_filter_pallas(n, p):
    z = filter_features(n)
    tr = min(512, n)
    nb = 2 * n // tr
    w1 = jnp.pad(p['hy_w1'], ((0, 128 - HY_EMB), (0, 0)))
    w3 = jnp.moveaxis(p['hy_w3'].reshape(HY_FILTER_HIDDEN, 2, HY_WIDTH), 1, 0)
    deltas = jnp.abs(jnp.linspace(HY_MIN_DECAY, HY_MAX_DECAY, HY_WIDTH, dtype=F32)).reshape(1, HY_WIDTH)
    vec = lambda a: a.reshape(1, -1)
    full = lambda a: pl.BlockSpec(a.shape, lambda i: (0,) * a.ndim)
    args = (w1, vec(p['hy_b1']), p['hy_w2'], vec(p['hy_b2']), vec(p['hy_freq']))
    return pl.pallas_call(
        _filter_kernel,
        grid=(nb,),
        in_specs=[pl.BlockSpec((tr, 128), lambda i: (i, 0))] + [full(a) for a in args]
        + [pl.BlockSpec((1, HY_FILTER_HIDDEN, HY_WIDTH), lambda i: (i // (nb // 2), 0, 0)), full(deltas)],
        out_specs=pl.BlockSpec((tr, HY_WIDTH), lambda i: (i, 0)),
        out_shape=jax.ShapeDtypeStruct((2 * n, HY_WIDTH), F32),
        compiler_params=pltpu.CompilerParams(dimension_semantics=("parallel",), vmem_limit_bytes=VMEM_LIMIT),
        name="hyena_filter",
    )(z, *args, w3, deltas)


FFT_TN = 4096
FFT_KB = 8
FFT_TC = 256


def dft_tables():
    r = np.arange(FFT_R, dtype=np.float64)
    th1 = 2.0 * np.pi * np.outer(r, r) / FFT_R
    k = r[:, None, None] + FFT_R * r[None, :, None]
    phi = 2.0 * np.pi * k * r[None, None, :] / (FFT_R * FFT_R)
    return _hilo_tables((('c1', np.cos(th1)), ('s1', np.sin(th1)), ('gc', np.cos(phi)), ('gs', np.sin(phi)),
                         ('gct', np.swapaxes(np.cos(phi), 1, 2)), ('gst', np.swapaxes(np.sin(phi), 1, 2))))


def _dft_l1_kernel(x_ref, ch, cl, sh, sl, ar_ref, ai_ref):
    x = x_ref[0]
    ar_ref[0] = _dot3(ch[...], cl[...], x)
    ai_ref[0] = -_dot3(sh[...], sl[...], x)


def dft_level1(x2, tabs):
    B, nb, L = x2.shape
    cut = lambda a: a[:, :nb]
    consts = [cut(tabs['c1_hi']), cut(tabs['c1_lo']), cut(tabs['s1_hi']), cut(tabs['s1_lo'])]
    full = lambda a: pl.BlockSpec(a.shape, lambda b, j: (0,) * a.ndim)
    out = pl.BlockSpec((1, FFT_R, FFT_TN), lambda b, j: (b, 0, j))
    return pl.pallas_call(
        _dft_l1_kernel,
        grid=(B, L // FFT_TN),
        in_specs=[pl.BlockSpec((1, nb, FFT_TN), lambda b, j: (b, 0, j))] + [full(c) for c in consts],
        out_specs=[out, out],
        out_shape=[jax.ShapeDtypeStruct((B, FFT_R, L), F32)] * 2,
        compiler_params=pltpu.CompilerParams(dimension_semantics=("parallel", "parallel"),
                                             vmem_limit_bytes=VMEM_LIMIT),
        name="dft_level1",
    )(x2, *consts)


def _level2_fwd(ar, ai, gch, gcl, gsh, gsl):
    xr = _dot3(gch, gcl, ar) + _dot3(gsh, gsl, ai)
    xi = _dot3(gch, gcl, ai) - _dot3(gsh, gsl, ar)
    return xr, xi


def _spectrum_kernel(ar_ref, ai_ref, gch, gcl, gsh, gsl, hr_ref, hi_ref):
    for q in range(FFT_KB):
        xr, xi = _level2_fwd(ar_ref[0, q], ai_ref[0, q], gch[q], gcl[q], gsh[q], gsl[q])
        hr_ref[0, q] = xr
        hi_ref[0, q] = xi


def _conv_l2_kernel(ar_ref, ai_ref, hr_ref, hi_ref, gc, gs, tc, ts, br_ref, bi_ref):
    d1 = lambda a, x: jnp.dot(a, x.astype(BF16), preferred_element_type=F32)
    for q in range(FFT_KB):
        ar, ai = ar_ref[0, q], ai_ref[0, q]
        xr = d1(gc[q], ar) + d1(gs[q], ai)
        xi = d1(gc[q], ai) - d1(gs[q], ar)
        hr, hi = hr_ref[0, q], hi_ref[0, q]
        yr = xr * hr - xi * hi
        yi = xr * hi + xi * hr
        br_ref[0, q] = d1(tc[q], yr) - d1(ts[q], yi)
        bi_ref[0, q] = d1(tc[q], yi) + d1(ts[q], yr)


def _l2_specs():
    data = pl.BlockSpec((1, FFT_KB, FFT_R, FFT_TC), lambda b, c, k: (b, k, 0, c))
    filt = pl.BlockSpec((1, FFT_KB, FFT_R, FFT_TC), lambda b, c, k: (0, k, 0, c))
    tab = pl.BlockSpec((FFT_KB, FFT_R, FFT_R), lambda b, c, k: (k, 0, 0))
    return data, filt, tab


def dft_spectrum(ar, ai, tabs):
    B, _, _, C = ar.shape
    data, _, tab = _l2_specs()
    return pl.pallas_call(
        _spectrum_kernel,
        grid=(B, C // FFT_TC, FFT_R // FFT_KB),
        in_specs=[data, data] + [tab] * 4,
        out_specs=[data, data],
        out_shape=[jax.ShapeDtypeStruct(ar.shape, F32)] * 2,
        compiler_params=pltpu.CompilerParams(dimension_semantics=("parallel", "parallel", "parallel"),
                                             vmem_limit_bytes=VMEM_LIMIT),
        name="dft_spectrum",
    )(ar, ai, tabs['gc_hi'], tabs['gc_lo'], tabs['gs_hi'], tabs['gs_lo'])


def conv_level2(ar, ai, hr, hi, tabs):
    B, _, _, C = ar.shape
    data, filt, tab = _l2_specs()
    names = ('gc_hi', 'gs_hi', 'gct_hi', 'gst_hi')
    return pl.pallas_call(
        _conv_l2_kernel,
        grid=(B, C // FFT_TC, FFT_R // FFT_KB),
        in_specs=[data, data, filt, filt] + [tab] * 4,
        out_specs=[data, data],
        out_shape=[jax.ShapeDtypeStruct(ar.shape, F32)] * 2,
        compiler_params=pltpu.CompilerParams(dimension_semantics=("parallel", "parallel", "parallel"),
                                             vmem_limit_bytes=VMEM_LIMIT),
        name="conv_level2",
    )(ar, ai, hr, hi, *[tabs[k] for k in names])


def _idft_l1_kernel(br_ref, bi_ref, v_ref, x0_ref, d_ref, ch, cl, sh, sl, o_ref, *, inv_n):
    y = (_dot3(ch[...], cl[...], br_ref[0]) - _dot3(sh[...], sl[...], bi_ref[0])) * inv_n
    o_ref[0] = ((y + v_ref[0] * d_ref[...]) * x0_ref[0].astype(F32)).astype(o_ref.dtype)


def idft_level1_epilogue(br, bi, v2, x02, d_t, tabs):
    B, nb, L = v2.shape
    cut = lambda a: a[:nb, :]
    consts = [cut(tabs['c1_hi']), cut(tabs['c1_lo']), cut(tabs['s1_hi']), cut(tabs['s1_lo'])]
    full = lambda a: pl.BlockSpec(a.shape, lambda b, j: (0,) * a.ndim)
    big = pl.BlockSpec((1, FFT_R, FFT_TN), lambda b, j: (b, 0, j))
    half = pl.BlockSpec((1, nb, FFT_TN), lambda b, j: (b, 0, j))
    return pl.pallas_call(
        functools.partial(_idft_l1_kernel, inv_n=1.0 / (FFT_R * FFT_R)),
        grid=(B, L // FFT_TN),
        in_specs=[big, big, half, half, pl.BlockSpec((1, FFT_TN), lambda b, j: (0, j))] + [full(c) for c in consts],
        out_specs=half,
        out_shape=jax.ShapeDtypeStruct((B, nb, L), BF16),
        compiler_params=pltpu.CompilerParams(dimension_semantics=("parallel", "parallel"),
                                             vmem_limit_bytes=VMEM_LIMIT),
        name="idft_level1",
    )(br, bi, v2, x02, d_t, *consts)


def hyena_long_conv_fft(v, x0, kfull, hy_d, tabs):
    B, n, C = v.shape
    nb = n // FFT_R
    L = FFT_R * C
    kr, ki = dft_level1(kfull.reshape(1, 2 * nb, L), tabs)
    hr, hi = dft_spectrum(kr.reshape(1, FFT_R, FFT_R, C), ki.reshape(1, FFT_R, FFT_R, C), tabs)
    ar, ai = dft_level1(v.reshape(B, nb, L), tabs)
    br, bi = conv_level2(ar.reshape(B, FFT_R, FFT_R, C), ai.reshape(B, FFT_R, FFT_R, C), hr, hi, tabs)
    d_t = jnp.tile(hy_d.reshape(1, C), (1, FFT_R))
    out = idft_level1_epilogue(br.reshape(B, FFT_R, L), bi.reshape(B, FFT_R, L), v.reshape(B, nb, L),
                               x0.reshape(B, nb, L), d_t, tabs)
    return out.reshape(B, n, C)


def dense_tables(n):
    t = np.arange(2 * n, dtype=np.float64)
    th = 2.0 * np.pi * np.outer(t, t) / (2 * n)
    return _hilo_tables((('c', np.cos(th)), ('s', np.sin(th))))


def _dense_conv_kernel(v_ref, x0_ref, k_ref, d_ref, ch, cl, sh, sl, o_ref, *, n):
    c_hi, c_lo, s_hi, s_lo = ch[...], cl[...], sh[...], sl[...]
    k = k_ref[...]
    hr = _dot3(c_hi, c_lo, k)
    hi = -_dot3(s_hi, s_lo, k)
    v = v_ref[0]
    xr = _dot3(c_hi[:, :n], c_lo[:, :n], v)
    xi = -_dot3(s_hi[:, :n], s_lo[:, :n], v)
    yr = xr * hr - xi * hi
    yi = xr * hi + xi * hr
    y = (_dot3(c_hi[:n, :], c_lo[:n, :], yr) - _dot3(s_hi[:n, :], s_lo[:n, :], yi)) * (0.5 / n)
    o_ref[0] = ((y + v * d_ref[...]) * x0_ref[0].astype(F32)).astype(o_ref.dtype)


def hyena_long_conv_dense(v, x0, kfull, hy_d, tabs):
    B, n, C = v.shape
    tc = 256
    seq = pl.BlockSpec((1, n, tc), lambda b, c: (b, 0, c))
    full = lambda a: pl.BlockSpec(a.shape, lambda b, c: (0,) * a.ndim)
    consts = [tabs[k] for k in ('c_hi', 'c_lo', 's_hi', 's_lo')]
    return pl.pallas_call(
        functools.partial(_dense_conv_kernel, n=n),
        grid=(B, C // tc),
        in_specs=[seq, seq, pl.BlockSpec((2 * n, tc), lambda b, c: (0, c)), pl.BlockSpec((1, tc), lambda b, c: (0, c))]
        + [full(c) for c in consts],
        out_specs=seq,
        out_shape=jax.ShapeDtypeStruct((B, n, C), BF16),
        compiler_params=pltpu.CompilerParams(dimension_semantics=("parallel", "parallel"),
                                             vmem_limit_bytes=VMEM_LIMIT),
        name="hyena_dense_conv",
    )(v, x0, kfull, hy_d.reshape(1, C), *consts)


def build_hyena_weights(p):
    w, bias = p['hy_conv_w'], p['hy_conv_b']
    cw = [w[:, i * HY_WIDTH:(i + 1) * HY_WIDTH] for i in range(3)]
    cb = [bias[i * HY_WIDTH:(i + 1) * HY_WIDTH].reshape(1, -1) for i in range(3)]
    return cw, cb


def hyena_branch_pallas(proj, p, tabs, b, n):
    x0, v = hyena_prep_pallas(proj, p['hy_w'][0], p['hy_w'][1], n)
    kfull = hyena_filter_pallas(n, p)
    r3 = lambda a: a.reshape(b, n, HY_WIDTH)
    if 2 * n == FFT_R * FFT_R:
        y = hyena_long_conv_fft(r3(v), r3(x0), kfull, p['hy_d'], tabs['fft'])
    else:
        y = hyena_long_conv_dense(r3(v), r3(x0), kfull, p['hy_d'], dense_tables(n))
    return y.reshape(b * n, HY_WIDTH)


ADA_TN = 1024


def _ada_kernel(c_ref, w_ref, b_ref, o_ref):
    o_ref[...] = jnp.dot(jax.nn.silu(c_ref[...]).astype(BF16), w_ref[...].astype(BF16),
                         preferred_element_type=F32) + b_ref[...]


def ada_modulation(cond, w_ada, b_ada):
    m, d = cond.shape
    n = w_ada.shape[1]
    return pl.pallas_call(
        _ada_kernel,
        grid=(n // ADA_TN,),
        in_specs=[pl.BlockSpec((m, d), lambda j: (0, 0)), pl.BlockSpec((d, ADA_TN), lambda j: (0, j)),
                  pl.BlockSpec((1, ADA_TN), lambda j: (0, j))],
        out_specs=pl.BlockSpec((m, ADA_TN), lambda j: (0, j)),
        out_shape=jax.ShapeDtypeStruct((m, n), F32),
        compiler_params=pltpu.CompilerParams(dimension_semantics=("parallel",), vmem_limit_bytes=VMEM_LIMIT),
        name="ada_modulation",
    )(cond, w_ada, b_ada.reshape(1, n))


def expert_choice_ffn(h2, logits, ffn_w, b, n):
    cap = EC_CAPACITY * n // N_EXPERTS
    aff = jax.nn.softmax(logits[:, :N_EXPERTS].reshape(b, n, N_EXPERTS), -1)
    gate, idx = lax.top_k(jnp.swapaxes(aff, 1, 2), cap)
    bidx = jnp.arange(b)[:, None, None]
    xs = h2.reshape(b, n, D_MODEL)[bidx, idx]
    ye = expert_ffn(xs, gate[..., None], *ffn_w)
    return jnp.zeros((b, n, D_MODEL), F32).at[bidx, idx].add(ye).reshape(b * n, D_MODEL)


def _proj_piece(proj, name, width=None):
    off, w = PROJ[name]
    return proj[:, off:off + (width or w)].astype(F32)


def trunk_layer(xc, xl, mods, p, tabs, need_ctx, b, n_ctx, n_lat):
    D = D_MODEL
    piece = lambda rows, k: mods[rows, None, k * D:(k + 1) * D]
    lat = dict(x=xl, n=n_lat, rpm=n_lat, rows=slice(0, b), cos=tabs['cos_l'], sin=tabs['sin_l'])
    ctx = dict(x=xc, n=n_ctx, rpm=b * n_ctx, rows=slice(b, b + 1), cos=tabs['cos_c'], sin=tabs['sin_c'])
    for t in (ctx, lat):
        t['proj'] = inproj(t['x'], piece(t['rows'], 0), piece(t['rows'], 1), p['w_in_p'], t['rpm'])
        t['q'], t['k'], t['v'] = mla_proj(t['proj'], t['cos'], t['sin'], p['mla_q_norm'], p['mla_kv_norm'], p['mla_w'])
        t['rw'] = rwkv_prep_pallas(t['proj'], p['rw_w'], t['n'])
    sets = (ctx, lat) if need_ctx else (lat,)

    r3 = lambda a, n: a.reshape(b, n, a.shape[-1])
    kc, vc = r3(ctx['k'], n_ctx), r3(ctx['v'], n_ctx)
    lat['mla'] = flash_attention(r3(lat['q'], n_lat), kc, vc, r3(lat['k'], n_lat), r3(lat['v'], n_lat),
                                 tq=min(n_lat, 1024), tk=min(n_lat, 1024), out_dtype=BF16)
    if need_ctx:
        ctx['mla'] = flash_attention(r3(ctx['q'], n_ctx), kc, vc, tq=n_ctx, out_dtype=BF16)

    s0 = jnp.zeros((2, b, SSM_GROUPS, SSM_STATE, SSD_GW), F32)
    for t in (ctx, lat):
        xs, bm, cm, dt = ssd_prep_pallas(t['proj'], p['ssd_w'], t['n'])
        yf, yb, s0 = ssd_scan_pallas(r3(xs, t['n']), r3(bm, t['n']), r3(cm, t['n']), r3(dt, t['n']),
                                     p['ssd_w']['a_g'], s0)
        if t in sets:
            flat = lambda a: a.reshape(b * t['n'], SSM_INNER)
            t['ssm'] = ssd_out_pallas(flat(yf), flat(yb), xs, t['proj'], p['ssd_w'])

    for t in sets:
        t['hy'] = hyena_branch_pallas(t['proj'], p, tabs, b, t['n'])

    s0 = jnp.zeros((2, b, RW_WIDTH // RW_P, RW_P, RW_P), F32)
    for t in (ctx, lat):
        r_, v_, kk_, lw0, lw1, kd0, kd1, bd0, bd1, g_, bonus = (r3(a, t['n']) for a in t['rw'])
        yf, yb, s0 = rwkv_scan_pallas(r_, v_, kk_, lw0, lw1, kd0, kd1, bd0, bd1, s0)
        if t in sets:
            flat = lambda a: a.reshape(b * t['n'], RW_WIDTH)
            t['rwkv'] = rwkv_out_pallas(flat(yf), flat(yb), flat(bonus), flat(g_), p['rw_w'])

    outs = []
    for t in sets:
        rows = b * t['n']
        branches = [t['mla'].reshape(rows, BRANCH_WIDTH), t['ssm'], t['hy'], t['rwkv']]
        merged = merge_branches_pallas(branches, t['proj'], p['w_branch_b'])
        x1, h2, logits = outproj_postnorm(merged, t['x'], piece(t['rows'], 2), piece(t['rows'], 3), piece(t['rows'], 4),
                                          p['ln1_g'], p['ln1_b'], p['w_out_b'], p['w_router'], t['rpm'])
        moe = expert_choice_ffn(h2, logits, p['ffn_w'], b, t['n'])
        outs.append(postnorm(x1, moe, piece(t['rows'], 5), p['ln2_g'], p['ln2_b'], t['rpm']))
    return (outs[0], outs[1]) if need_ctx else (None, outs[0])


def kernel(x, c, ctx, c_ctx, w_ada, b_ada, w_in, mla_q_norm, mla_w_q_up, mla_kv_norm, mla_w_kv_up,
           ssm_conv_w, ssm_conv_b, ssm_dt_bias, ssm_a_log, ssm_d, ssm_norm,
           hy_conv_w, hy_conv_b, hy_w1, hy_b1, hy_w2, hy_b2, hy_w3, hy_freq, hy_d,
           rw_mu, rw_w0, rw_w_up, rw_a0, rw_a_up, rw_g_up, rw_kk, rw_ka, rw_rk, rw_ln_g, rw_ln_b,
           w_branch, w_out, ln1_g, ln1_b, w_router, w_gate_e, w_up_e, w_down_e, ln2_g, ln2_b):
    stacked = dict(
        w_in=w_in, mla_q_norm=mla_q_norm, mla_w_q_up=mla_w_q_up, mla_kv_norm=mla_kv_norm, mla_w_kv_up=mla_w_kv_up,
        ssm_conv_w=ssm_conv_w, ssm_conv_b=ssm_conv_b, ssm_dt_bias=ssm_dt_bias, ssm_a_log=ssm_a_log,
        ssm_d=ssm_d, ssm_norm=ssm_norm,
        hy_conv_w=hy_conv_w, hy_conv_b=hy_conv_b, hy_w1=hy_w1, hy_b1=hy_b1, hy_w2=hy_w2, hy_b2=hy_b2,
        hy_w3=hy_w3, hy_freq=hy_freq, hy_d=hy_d,
        rw_mu=rw_mu, rw_w0=rw_w0, rw_w_up=rw_w_up, rw_a0=rw_a0, rw_a_up=rw_a_up, rw_g_up=rw_g_up,
        rw_kk=rw_kk, rw_ka=rw_ka, rw_rk=rw_rk, rw_ln_g=rw_ln_g, rw_ln_b=rw_ln_b,
        w_branch=w_branch, w_out=w_out, ln1_g=ln1_g, ln1_b=ln1_b,
        w_router=w_router, w_gate_e=w_gate_e, w_up_e=w_up_e, w_down_e=w_down_e, ln2_g=ln2_g, ln2_b=ln2_b)
    b, n_lat, D = x.shape
    n_ctx = ctx.shape[1]
    cos, sin = rope_tables_128(n_lat)
    ones = jnp.pad(jnp.ones((b * n_ctx, QK_ROPE), F32), ((0, 0), (0, 128 - QK_ROPE)))
    tabs = dict(cos_l=jnp.tile(cos, (b, 1)), sin_l=jnp.tile(sin, (b, 1)), cos_c=ones, sin_c=jnp.zeros_like(ones),
                fft=dft_tables())
    cond = jnp.concatenate([c, c_ctx[None], jnp.zeros((8 - b - 1, D), F32)], 0)
    xc, xl = ctx.reshape(b * n_ctx, D), x.reshape(b * n_lat, D)
    for i in range(DEPTH):
        p = {name: arr[i] for name, arr in stacked.items()}
        p['w_in_p'] = build_w_in(p['w_in'])
        p['mla_w'] = build_mla_weights(p)
        p['rw_w'] = build_rwkv_weights(p)
        p['ssd_w'] = build_ssd_weights(p)
        p['hy_w'] = build_hyena_weights(p)
        p['w_branch_b'] = p['w_branch'].astype(BF16)
        p['w_out_b'] = p['w_out'].astype(BF16)
        p['ffn_w'] = build_expert_weights(p)
        mods = ada_modulation(cond, w_ada[i], b_ada[i])
        xc, xl = trunk_layer(xc, xl, mods, p, tabs, i < DEPTH - 1, b, n_ctx, n_lat)
    return xl.reshape(b, n_lat, D)
```
